```python
import math
import jax
import jax.numpy as jnp
from jax import lax
import numpy as np

D_MODEL = 2048
BATCH = 32
SEQ = 256
DEPTH = 1
DEC_BATCH = 2
DEC_SEQ = 4096
PAST_LEN = 512

GRID_W = 64
D_MIX = D_MODEL
D_RG = D_MIX // 2
D_HY = D_MIX - D_RG
RG_HEADS = 8
RG_HEAD_DIM = D_RG // RG_HEADS
RG_CONV_W = 4
RG_C = 8.0
HY_ORDER = 2
HY_CONV_W = 3
HY_BANDS = 16
HY_FEAT = 1 + 2 * HY_BANDS
HY_FILT_HIDDEN = 64
HY_MIN_DECAY = math.log(100.0) / 1.5
HY_MAX_DECAY = math.log(100.0) / 0.3
D_IN = 2 * D_RG + (HY_ORDER + 1) * D_HY
N_EXPERTS = 256
TOP_K = 8
D_EXPERT = 512
D_SHARED = 512
ROUTED_SCALE = 2.5
MOE_BLOCK = 128
N_MOD = 6
EPS = 1e-6

kernel_name = 'hymba_rglru_hyena_moe_dit_step'


def rmsnorm(x, g):
    xf = x.astype(jnp.float32)
    y = xf * lax.rsqrt(jnp.mean(xf * xf, axis=-1, keepdims=True) + EPS)
    return (y * g.astype(jnp.float32)).astype(x.dtype)


def grid_pos_emb(rows, d):
    quarter = d // 4
    omega = 1.0 / (10000.0 ** (jnp.arange(quarter, dtype=jnp.float32) / quarter))
    r = jnp.repeat(jnp.arange(rows, dtype=jnp.float32), GRID_W)[:, None] * omega
    cc = jnp.tile(jnp.arange(GRID_W, dtype=jnp.float32), rows)[:, None] * omega
    return jnp.concatenate([jnp.sin(r), jnp.cos(r), jnp.sin(cc), jnp.cos(cc)], axis=-1)


def dwconv(x, w, b, left):
    width = w.shape[0]
    n_l = x.shape[1]
    xp = jnp.pad(x, ((0, 0), (left, width - 1 - left), (0, 0)))
    y = b
    for k in range(width):
        y = y + w[k] * xp[:, k:k + n_l]
    return y


def _combine(e1, e2):
    a1, b1 = e1
    a2, b2 = e2
    return a1 * a2, a2 * b1 + b2


def rglru_scan(xr, wa, ba, wx, bx, lam, h0, reverse):
    n_b, n_l, _ = xr.shape
    xf = xr.astype(jnp.float32)
    xh = xf.reshape(n_b, n_l, RG_HEADS, RG_HEAD_DIM)
    r = jax.nn.sigmoid(jnp.einsum('blhi,hij->blhj', xh, wa.astype(jnp.float32)).reshape(n_b, n_l, D_RG) + ba.astype(jnp.float32))
    i = jax.nn.sigmoid(jnp.einsum('blhi,hij->blhj', xh, wx.astype(jnp.float32)).reshape(n_b, n_l, D_RG) + bx.astype(jnp.float32))
    log_a = -RG_C * r * jax.nn.softplus(-lam.astype(jnp.float32))
    a = jnp.exp(log_a)
    b = jnp.sqrt(-jnp.expm1(2.0 * log_a)) * (i * xf)
    edge = n_l - 1 if reverse else 0
    b = b.at[:, edge].add(a[:, edge] * h0.astype(jnp.float32))
    _, h = lax.associative_scan(_combine, (a, b), axis=1, reverse=reverse)
    final = h[:, 0] if reverse else h[:, -1]
    return h, final


def hyena_filters(n_l, w1, b1, freq, w2, b2, w3, decay):
    f32 = jnp.float32
    pos = jnp.arange(n_l, dtype=f32)
    t = pos / n_l
    ang = (2.0 * math.pi) * pos[:, None] / n_l * jnp.arange(1, HY_BANDS + 1, dtype=f32)
    feats = jnp.concatenate([t[:, None], jnp.cos(ang), -jnp.sin(ang)], axis=-1)
    fr = freq.astype(f32)
    hid = jnp.sin(fr * (feats @ w1.astype(f32) + b1.astype(f32)))
    hid = jnp.sin(fr * (hid @ w2.astype(f32) + b2.astype(f32)))
    k = (hid @ w3.astype(f32)).reshape(n_l, HY_ORDER, 2, D_HY)
    k = k * jnp.exp(-t[:, None, None, None] * jnp.abs(decay.astype(f32))[None])
    kf, kb = k[:, :, 0], k[:, :, 1]
    full = jnp.concatenate([kf, jnp.zeros((1, HY_ORDER, D_HY), f32), jnp.flip(kb[1:], axis=0)], axis=0)
    full = full * lax.rsqrt(jnp.sum(full * full, axis=0, keepdims=True) + EPS)
    return jnp.fft.rfft(full, axis=0)


def fftconv(u, kf, bias):
    n_l = u.shape[1]
    uf = jnp.fft.rfft(u.astype(jnp.float32), n=2 * n_l, axis=1)
    y = jnp.fft.irfft(uf * kf[None], n=2 * n_l, axis=1)[:, :n_l]
    return y + u.astype(jnp.float32) * bias.astype(jnp.float32)


def mixer(h, h0, filt, p):
    u = h @ p['w_in']
    xr, gr, hv = jnp.split(u, [D_RG, 2 * D_RG], axis=-1)
    xr = dwconv(xr, p['rg_conv_w'], p['rg_conv_b'], RG_CONV_W // 2)
    h_f, s_f = rglru_scan(xr, p['rg_wa'][0], p['rg_ba'][0], p['rg_wx'][0], p['rg_bx'][0], p['rg_lam'][0], h0[:, 0], False)
    h_b, s_b = rglru_scan(xr, p['rg_wa'][1], p['rg_ba'][1], p['rg_wx'][1], p['rg_bx'][1], p['rg_lam'][1], h0[:, 1], True)
    y_rg = (jax.nn.gelu(gr.astype(jnp.float32)) * (h_f + h_b)).astype(h.dtype)
    hv = dwconv(hv, p['hy_conv_w'], p['hy_conv_b'], HY_CONV_W // 2)
    v, x1, x2 = jnp.split(hv, 3, axis=-1)
    z = fftconv(v, filt[:, 0], p['hy_bias'][0]) * x1.astype(jnp.float32)
    z = fftconv(z, filt[:, 1], p['hy_bias'][1]) * x2.astype(jnp.float32)
    y_hy = z.astype(h.dtype)
    y = jnp.concatenate([rmsnorm(y_rg, p['gn_rg']), rmsnorm(y_hy, p['gn_hy'])], axis=-1) @ p['w_out']
    return y, jnp.stack([s_f, s_b], axis=1)


def moe(h, p):
    n_b, n_l, d = h.shape
    n_tok = n_b * n_l
    xt = h.reshape(n_tok, d)
    scores = jax.nn.sigmoid((xt @ p['router_w']).astype(jnp.float32))
    _, idx = lax.top_k(scores + p['router_b'].astype(jnp.float32), TOP_K)
    gw = jnp.take_along_axis(scores, idx, axis=1)
    gw = gw / jnp.sum(gw, axis=1, keepdims=True) * ROUTED_SCALE
    n_asg = n_tok * TOP_K
    flat_e = idx.reshape(n_asg)
    flat_tok = jnp.arange(n_asg, dtype=jnp.int32) // TOP_K
    order = jnp.argsort(flat_e)
    e_sorted = flat_e[order]
    counts = jnp.bincount(flat_e, length=N_EXPERTS)
    padded = (counts + MOE_BLOCK - 1) // MOE_BLOCK * MOE_BLOCK
    start = jnp.cumsum(counts) - counts
    pend = jnp.cumsum(padded)
    pstart = pend - padded
    dest = pstart[e_sorted] + jnp.arange(n_asg, dtype=jnp.int32) - start[e_sorted]
    n_blk = -(-n_asg // MOE_BLOCK) + N_EXPERTS
    n_rows = n_blk * MOE_BLOCK
    tok_buf = jnp.full((n_rows,), n_tok, jnp.int32).at[dest].set(flat_tok[order])
    w_buf = jnp.zeros((n_rows,), jnp.float32).at[dest].set(gw.reshape(n_asg)[order])
    blk_e = jnp.minimum(jnp.searchsorted(pend, jnp.arange(n_blk, dtype=jnp.int32) * MOE_BLOCK, side='right'), N_EXPERTS - 1)
    x_pad = jnp.concatenate([xt, jnp.zeros((1, d), xt.dtype)], axis=0)
    xb = x_pad[tok_buf].reshape(n_blk, MOE_BLOCK, d)
    wg, wu, wd = p['exp_w_gate'], p['exp_w_up'], p['exp_w_down']

    def expert_block(args):
        xblk, e = args
        return (jax.nn.silu(xblk @ wg[e]) * (xblk @ wu[e])) @ wd[e]

    yb = lax.map(expert_block, (xb, blk_e)).reshape(n_rows, d)
    routed = jnp.zeros((n_tok + 1, d), yb.dtype).at[tok_buf].add(yb * w_buf[:, None].astype(yb.dtype))[:n_tok]
    shared = (jax.nn.silu(xt @ p['sh_w_gate']) * (xt @ p['sh_w_up'])) @ p['sh_w_down']
    return (routed + shared.astype(routed.dtype)).reshape(n_b, n_l, d).astype(h.dtype)


def trunk_layer(x, mod, h0, p):
    n_l = x.shape[1]
    sh1, sc1, g1, sh2, sc2, g2 = jnp.split(mod, N_MOD, axis=-1)
    filt = hyena_filters(n_l, p['hy_w1'], p['hy_b1'], p['hy_freq'], p['hy_w2'], p['hy_b2'], p['hy_w3'], p['hy_decay'])
    h = rmsnorm(x, p['norm1_g']) * (1.0 + sc1[:, None]) + sh1[:, None]
    y, st = mixer(h, h0, filt, p)
    x = x + g1[:, None] * y
    h = rmsnorm(x, p['norm2_g']) * (1.0 + sc2[:, None]) + sh2[:, None]
    x = x + g2[:, None] * moe(h, p)
    return x, st


def setup_inputs(seed: int = 0) -> dict:
    key = jax.random.key(seed)
    ks = iter(jax.random.split(key, 48))
    f32 = jnp.float32

    def nrm(shape, s):
        return jax.random.normal(next(ks), shape, f32) * s

    lam_u = jax.random.uniform(next(ks), (DEPTH, 2, D_RG), f32, 0.9, 0.999)
    a_base = lam_u ** (1.0 / RG_C)
    rg_lam = jnp.log(a_base) - jnp.log1p(-a_base)
    decay = jnp.broadcast_to(jnp.linspace(HY_MIN_DECAY, HY_MAX_DECAY, D_HY, dtype=f32), (DEPTH, HY_ORDER, 2, D_HY))
    return {
        'x_prompt': nrm((BATCH, SEQ, D_MODEL), 1.0),
        'x_sample': nrm((DEC_BATCH, DEC_SEQ, D_MODEL), 1.0),
        'state_rglru': nrm((DEC_BATCH, DEPTH, 2, D_RG), 0.5),
        'c': nrm((DEC_BATCH, D_MODEL), 1.0),
        'c_ctx': nrm((D_MODEL,), 1.0),
        'ada_w': nrm((DEPTH, D_MODEL, N_MOD * D_MODEL), 0.5 * D_MODEL ** -0.5),
        'ada_b': nrm((DEPTH, N_MOD * D_MODEL), 0.01),
        'norm1_g': 1.0 + nrm((DEPTH, D_MODEL), 0.01),
        'norm2_g': 1.0 + nrm((DEPTH, D_MODEL), 0.01),
        'w_in': nrm((DEPTH, D_MODEL, D_IN), D_MODEL ** -0.5),
        'rg_conv_w': nrm((DEPTH, RG_CONV_W, D_RG), RG_CONV_W ** -0.5),
        'rg_conv_b': nrm((DEPTH, D_RG), 0.01),
        'rg_wa': nrm((DEPTH, 2, RG_HEADS, RG_HEAD_DIM, RG_HEAD_DIM), RG_HEAD_DIM ** -0.5),
        'rg_ba': nrm((DEPTH, 2, D_RG), 0.01),
        'rg_wx': nrm((DEPTH, 2, RG_HEADS, RG_HEAD_DIM, RG_HEAD_DIM), RG_HEAD_DIM ** -0.5),
        'rg_bx': nrm((DEPTH, 2, D_RG), 0.01),
        'rg_lam': rg_lam,
        'hy_conv_w': nrm((DEPTH, HY_CONV_W, (HY_ORDER + 1) * D_HY), HY_CONV_W ** -0.5),
        'hy_conv_b': nrm((DEPTH, (HY_ORDER + 1) * D_HY), 0.01),
        'hy_w1': nrm((DEPTH, HY_FEAT, HY_FILT_HIDDEN), HY_FEAT ** -0.5),
        'hy_b1': nrm((DEPTH, HY_FILT_HIDDEN), 0.01),
        'hy_freq': 1.0 + nrm((DEPTH, HY_FILT_HIDDEN), 0.01),
        'hy_w2': nrm((DEPTH, HY_FILT_HIDDEN, HY_FILT_HIDDEN), HY_FILT_HIDDEN ** -0.5),
        'hy_b2': nrm((DEPTH, HY_FILT_HIDDEN), 0.01),
        'hy_w3': nrm((DEPTH, HY_FILT_HIDDEN, HY_ORDER * 2 * D_HY), HY_FILT_HIDDEN ** -0.5),
        'hy_decay': decay + nrm((DEPTH, HY_ORDER, 2, D_HY), 0.1),
        'hy_bias': nrm((DEPTH, HY_ORDER, D_HY), 0.1),
        'gn_rg': 1.0 + nrm((DEPTH, D_RG), 0.01),
        'gn_hy': 1.0 + nrm((DEPTH, D_HY), 0.01),
        'w_out': nrm((DEPTH, D_MIX, D_MODEL), D_MIX ** -0.5),
        'router_w': nrm((DEPTH, D_MODEL, N_EXPERTS), D_MODEL ** -0.5),
        'router_b': nrm((DEPTH, N_EXPERTS), 0.01),
        'exp_w_gate': nrm((DEPTH, N_EXPERTS, D_MODEL, D_EXPERT), D_MODEL ** -0.5),
        'exp_w_up': nrm((DEPTH, N_EXPERTS, D_MODEL, D_EXPERT), D_MODEL ** -0.5),
        'exp_w_down': nrm((DEPTH, N_EXPERTS, D_EXPERT, D_MODEL), D_EXPERT ** -0.5),
        'sh_w_gate': nrm((DEPTH, D_MODEL, D_SHARED), D_MODEL ** -0.5),
        'sh_w_up': nrm((DEPTH, D_MODEL, D_SHARED), D_MODEL ** -0.5),
        'sh_w_down': nrm((DEPTH, D_SHARED, D_MODEL), D_SHARED ** -0.5),
        'final_g': 1.0 + nrm((D_MODEL,), 0.01),
    }


def reference(x_prompt, x_sample, state_rglru, c, c_ctx, ada_w, ada_b, norm1_g, norm2_g, w_in,
              rg_conv_w, rg_conv_b, rg_wa, rg_ba, rg_wx, rg_bx, rg_lam,
              hy_conv_w, hy_conv_b, hy_w1, hy_b1, hy_freq, hy_w2, hy_b2, hy_w3, hy_decay, hy_bias,
              gn_rg, gn_hy, w_out, router_w, router_b, exp_w_gate, exp_w_up, exp_w_down,
              sh_w_gate, sh_w_up, sh_w_down, final_g):
    n_ctx_b = x_prompt.shape[0]
    n_lat = x_sample.shape[1]
    rows = n_lat // GRID_W
    xc = x_prompt
    xs = x_sample + grid_pos_emb(rows, x_sample.shape[-1]).astype(x_sample.dtype)[None]
    ctx_states = []
    for l in range(DEPTH):
        p = {
            'norm1_g': norm1_g[l], 'norm2_g': norm2_g[l], 'w_in': w_in[l],
            'rg_conv_w': rg_conv_w[l], 'rg_conv_b': rg_conv_b[l], 'rg_wa': rg_wa[l], 'rg_ba': rg_ba[l],
            'rg_wx': rg_wx[l], 'rg_bx': rg_bx[l], 'rg_lam': rg_lam[l],
            'hy_conv_w': hy_conv_w[l], 'hy_conv_b': hy_conv_b[l], 'hy_w1': hy_w1[l], 'hy_b1': hy_b1[l],
            'hy_freq': hy_freq[l], 'hy_w2': hy_w2[l], 'hy_b2': hy_b2[l], 'hy_w3': hy_w3[l],
            'hy_decay': hy_decay[l], 'hy_bias': hy_bias[l], 'gn_rg': gn_rg[l], 'gn_hy': gn_hy[l],
            'w_out': w_out[l], 'router_w': router_w[l], 'router_b': router_b[l],
            'exp_w_gate': exp_w_gate[l], 'exp_w_up': exp_w_up[l], 'exp_w_down': exp_w_down[l],
            'sh_w_gate': sh_w_gate[l], 'sh_w_up': sh_w_up[l], 'sh_w_down': sh_w_down[l],
        }
        mod_ctx = jnp.broadcast_to(jax.nn.silu(c_ctx) @ ada_w[l] + ada_b[l], (n_ctx_b, N_MOD * D_MODEL))
        mod_lat = jax.nn.silu(c) @ ada_w[l] + ada_b[l]
        xc, st = trunk_layer(xc, mod_ctx, jnp.zeros((n_ctx_b, 2, D_RG), x_prompt.dtype), p)
        ctx_states.append(st)
        xs, _ = trunk_layer(xs, mod_lat, state_rglru[:, l], p)
    new_state_rglru = jnp.stack(ctx_states, axis=1).astype(x_prompt.dtype)
    y_prompt = rmsnorm(xc, final_g)
    y_sample = rmsnorm(xs, final_g)
    return (y_prompt, y_sample, new_state_rglru)
```

```python
import functools
import math

import jax
import jax.numpy as jnp
from jax import lax
from jax.experimental import pallas as pl
from jax.experimental.pallas import tpu as pltpu

F32 = jnp.float32
BF16 = jnp.bfloat16

GRID_W = 64
RG_HEADS = 8
RG_CONV_W = 4
RG_C = 8.0
HY_CONV_W = 3
HY_BANDS = 16
TOP_K = 8
ROUTED_SCALE = 2.5
N_MOD = 6
EPS = 1e-6

LANES = 128
SUBLANES = 8
VMEM_LIMIT_BYTES = 56 * 1024 * 1024

EXPERT_ROWS = 256


def _params(n_axes, vmem=VMEM_LIMIT_BYTES):
    return pltpu.CompilerParams(dimension_semantics=("arbitrary",) * n_axes, vmem_limit_bytes=vmem)


def _rms(x, g):
    return x * lax.rsqrt(jnp.mean(x * x, axis=-1, keepdims=True) + EPS) * g


def _mod_kernel(c_ref, w_ref, b_ref, o_ref):
    c = c_ref[...]
    s = (c * jax.nn.sigmoid(c)).astype(BF16)
    o_ref[...] = jnp.dot(s, w_ref[...].astype(BF16), preferred_element_type=F32) + b_ref[...]


def _modulation(cvec, ada_w, ada_b):
    d, n = ada_w.shape
    tn = 1536
    return pl.pallas_call(
        _mod_kernel,
        grid=(n // tn,),
        in_specs=[pl.BlockSpec((SUBLANES, d), lambda j: (0, 0)),
                  pl.BlockSpec((d, tn), lambda j: (0, j)),
                  pl.BlockSpec((1, tn), lambda j: (0, j))],
        out_specs=pl.BlockSpec((SUBLANES, tn), lambda j: (0, j)),
        out_shape=jax.ShapeDtypeStruct((SUBLANES, n), F32),
        compiler_params=_params(1),
        name="modulation",
    )(cvec, ada_w, ada_b)


def _inproj_kernel(*refs, has_pos):
    if has_pos:
        x_ref, pos_ref, mod_ref, g_ref, w_ref, u_ref, xo_ref, h_scr = refs
    else:
        x_ref, mod_ref, g_ref, w_ref, u_ref, h_scr = refs

    @pl.when(pl.program_id(1) == 0)
    def _():
        x = x_ref[...]
        if has_pos:
            x = x + pos_ref[...]
            xo_ref[...] = x
        h = _rms(x, g_ref[...]) * (1.0 + mod_ref[0, 1:2, :]) + mod_ref[0, 0:1, :]
        h_scr[...] = h.astype(BF16)

    u_ref[...] = jnp.dot(h_scr[...], w_ref[...], preferred_element_type=F32)


def _in_proj(x2d, pos, mod, norm_g, w_bf, group_of_tile, tm=512, tn=1024):
    t, d = x2d.shape
    n = w_bf.shape[1]
    has_pos = pos is not None
    in_specs = [pl.BlockSpec((tm, d), lambda i, j: (i, 0))]
    args = [x2d]
    if has_pos:
        pos_tiles = pos.shape[0] // tm
        in_specs.append(pl.BlockSpec((tm, d), lambda i, j: (i % pos_tiles, 0)))
        args.append(pos)
    in_specs += [pl.BlockSpec((1, N_MOD, d), lambda i, j: (group_of_tile(i, tm), 0, 0)),
                 pl.BlockSpec((1, d), lambda i, j: (0, 0)),
                 pl.BlockSpec((d, tn), lambda i, j: (0, j))]
    args += [mod, norm_g, w_bf]
    out_specs = [pl.BlockSpec((tm, tn), lambda i, j: (i, j))]
    out_shape = [jax.ShapeDtypeStruct((t, n), F32)]
    if has_pos:
        out_specs.append(pl.BlockSpec((tm, d), lambda i, j: (i, 0)))
        out_shape.append(jax.ShapeDtypeStruct((t, d), F32))
    res = pl.pallas_call(
        functools.partial(_inproj_kernel, has_pos=has_pos),
        grid=(t // tm, n // tn),
        in_specs=in_specs, out_specs=out_specs, out_shape=out_shape,
        scratch_shapes=[pltpu.VMEM((tm, d), BF16)],
        compiler_params=_params(2),
        name="in_proj",
    )(*args)
    return (res[0], res[1]) if has_pos else (res[0], x2d)


def _shift_rows(win, off):
    if off == 0:
        return win
    n = win.shape[0]
    return pltpu.roll(win, (-off) % n, axis=0)


def _scan_chunk(a, b, reverse):
    n = a.shape[0]
    row = lax.broadcasted_iota(jnp.int32, a.shape, 0)
    dist = 1
    while dist < n:
        if reverse:
            a_s = pltpu.roll(a, n - dist, axis=0)
            b_s = pltpu.roll(b, n - dist, axis=0)
            m = row < n - dist
        else:
            a_s = pltpu.roll(a, dist, axis=0)
            b_s = pltpu.roll(b, dist, axis=0)
            m = row >= dist
        b = jnp.where(m, a * b_s + b, b)
        a = jnp.where(m, a * a_s, a)
        dist *= 2
    return a, b


def _rglru_kernel(xr_ref, gr_ref, cw_ref, cb_ref, w4_ref, b4_ref, lam_ref, h0_ref,
                  y_ref, st_ref, af, bf, ab, bb, hf, *, seq_len, n_seq, t1, tc):
    hd = xr_ref.shape[-1]
    nlam = -lam_ref[...]
    sp = jnp.maximum(nlam, 0.0) + jnp.log1p(jnp.exp(-jnp.abs(nlam)))
    cw = cw_ref[...]
    cb = cb_ref[...]
    b4 = b4_ref[0]
    nc1 = seq_len // t1
    ncs = seq_len // tc

    def per_seq(s, carry0):
        def gates(c, carry):
            r0 = pl.multiple_of(c * t1, t1)
            cur = xr_ref[s, pl.ds(r0, t1), :]
            p0 = pl.multiple_of(jnp.maximum(r0 - SUBLANES, 0), SUBLANES)
            n0 = pl.multiple_of(jnp.minimum(r0 + t1, seq_len - SUBLANES), SUBLANES)
            prev = jnp.where(c > 0, xr_ref[s, pl.ds(p0, SUBLANES), :], 0.0)
            nxt = jnp.where(c < nc1 - 1, xr_ref[s, pl.ds(n0, SUBLANES), :], 0.0)
            win = jnp.concatenate([prev, cur, nxt], axis=0)
            xr = cb
            for k in range(RG_CONV_W):
                xr = xr + cw[k:k + 1, :] * _shift_rows(win, k - RG_CONV_W // 2)[SUBLANES:SUBLANES + t1]
            z = jnp.dot(xr.astype(BF16), w4_ref[0], preferred_element_type=F32) + b4
            for d_i, (a_scr, b_scr) in enumerate(((af, bf), (ab, bb))):
                r = jax.nn.sigmoid(z[:, (2 * d_i) * hd:(2 * d_i + 1) * hd])
                gi = jax.nn.sigmoid(z[:, (2 * d_i + 1) * hd:(2 * d_i + 2) * hd])
                log_a = (-RG_C) * r * sp[d_i:d_i + 1, :]
                a = jnp.exp(log_a)
                a_scr[pl.ds(r0, t1), :] = a
                b_scr[pl.ds(r0, t1), :] = jnp.sqrt(-jnp.tanh(log_a) * (a * a + 1.0)) * (gi * xr)
            return carry

        lax.fori_loop(0, nc1, gates, 0)
        h0 = h0_ref[s]

        def fwd(c, carry):
            r0 = pl.multiple_of(c * tc, tc)
            a, h = _scan_chunk(af[pl.ds(r0, tc), :], bf[pl.ds(r0, tc), :], False)
            h = a * carry + h
            hf[pl.ds(r0, tc), :] = h
            return h[tc - 1:tc, :]

        s_f = lax.fori_loop(0, ncs, fwd, h0[0:1, :])

        def bwd(cc, carry):
            r0 = pl.multiple_of((ncs - 1 - cc) * tc, tc)
            a, h = _scan_chunk(ab[pl.ds(r0, tc), :], bb[pl.ds(r0, tc), :], True)
            h = a * carry + h
            g = gr_ref[s, pl.ds(r0, tc), :]
            y_ref[s, pl.ds(r0, tc), :] = jax.nn.gelu(g) * (hf[pl.ds(r0, tc), :] + h)
            return h[0:1, :]

        s_b = lax.fori_loop(0, ncs, bwd, h0[1:2, :])
        st_ref[s] = jnp.concatenate([s_f, s_b], axis=0)
        return carry0

    lax.fori_loop(0, n_seq, per_seq, 0)


def _rglru(u3, h0, conv_w, conv_b, w4, b4, lam, n_seq):
    b, seq_len, _ = u3.shape
    hd = w4.shape[1]
    d_rg = hd * RG_HEADS
    t1 = min(seq_len, 256)
    tc = 64
    kern = functools.partial(_rglru_kernel, seq_len=seq_len, n_seq=n_seq, t1=t1, tc=tc)
    return pl.pallas_call(
        kern,
        grid=(b // n_seq, RG_HEADS),
        in_specs=[pl.BlockSpec((n_seq, seq_len, hd), lambda i, h: (i, 0, h)),
                  pl.BlockSpec((n_seq, seq_len, hd), lambda i, h: (i, 0, RG_HEADS + h)),
                  pl.BlockSpec((RG_CONV_W, hd), lambda i, h: (0, h)),
                  pl.BlockSpec((1, hd), lambda i, h: (0, h)),
                  pl.BlockSpec((1, hd, 4 * hd), lambda i, h: (h, 0, 0)),
                  pl.BlockSpec((1, 1, 4 * hd), lambda i, h: (h, 0, 0)),
                  pl.BlockSpec((2, hd), lambda i, h: (0, h)),
                  pl.BlockSpec((n_seq, 2, hd), lambda i, h: (i, 0, h))],
        out_specs=[pl.BlockSpec((n_seq, seq_len, hd), lambda i, h: (i, 0, h)),
                   pl.BlockSpec((n_seq, 2, hd), lambda i, h: (i, 0, h))],
        out_shape=[jax.ShapeDtypeStruct((b, seq_len, d_rg), F32),
                   jax.ShapeDtypeStruct((b, 2, d_rg), F32)],
        scratch_shapes=[pltpu.VMEM((seq_len, hd), F32)] * 5,
        compiler_params=_params(2),
        name="rglru",
    )(u3, u3, conv_w, conv_b, w4, b4, lam, h0)


def _filt_time_kernel(w1_ref, b1_ref, fr_ref, w2_ref, b2_ref, w3_ref, dec_ref,
                      g_ref, d_ref, st_ref, *, seq_len, tl):
    i = pl.program_id(0)
    hi = lax.Precision.HIGHEST
    posi = i * tl + lax.broadcasted_iota(jnp.int32, (tl, LANES), 0)
    pos = posi.astype(F32)
    lane = lax.broadcasted_iota(jnp.int32, (tl, LANES), 1)
    band = jnp.where(lane <= HY_BANDS, lane, lane - HY_BANDS).astype(F32)
    ang = (2.0 * math.pi) * pos / seq_len * band
    t = pos / seq_len
    feats = jnp.where(lane == 0, t,
                      jnp.where(lane <= HY_BANDS, jnp.cos(ang),
                                jnp.where(lane <= 2 * HY_BANDS, -jnp.sin(ang), 0.0)))
    fr = fr_ref[...]
    hid = jnp.sin(fr * (jnp.dot(feats, w1_ref[...], precision=hi, preferred_element_type=F32) + b1_ref[...]))
    hid = jnp.sin(fr * (jnp.dot(hid, w2_ref[...], precision=hi, preferred_element_type=F32) + b2_ref[...]))
    k = jnp.dot(hid, w3_ref[...], precision=hi, preferred_element_type=F32)
    k = k * jnp.exp(-t[:, 0:1] * jnp.abs(dec_ref[...]))
    c = k.shape[1] // 4
    first = posi[:, 0:1] == 0
    sign = jnp.where(posi[:, 0:1] % 2 == 0, 1.0, -1.0)

    @pl.when(i == 0)
    def _():
        st_ref[...] = jnp.zeros_like(st_ref)

    for o in range(2):
        kf = k[:, (2 * o) * c:(2 * o + 1) * c]
        kb = jnp.where(first, 0.0, k[:, (2 * o + 1) * c:(2 * o + 2) * c])
        g = kf + kb
        g_ref[:, o * c:(o + 1) * c] = g.astype(BF16)
        d_ref[:, o * c:(o + 1) * c] = (kf - kb).astype(BF16)
        st_ref[0:1, o * c:(o + 1) * c] += jnp.sum(kf * kf + kb * kb, axis=0, keepdims=True)
        st_ref[1:2, o * c:(o + 1) * c] += jnp.sum(sign * g, axis=0, keepdims=True)


def _filt_dft_kernel(c_ref, s_ref, g_ref, d_ref, st_ref, kre_ref, q_ref):
    scale = lax.rsqrt(st_ref[0:1, :] + EPS)
    kre_ref[...] = jnp.dot(c_ref[...], g_ref[...], preferred_element_type=F32) * scale
    q_ref[...] = jnp.dot(s_ref[...], d_ref[...], preferred_element_type=F32) * scale


def _hyena_filters(seq_len, cmat, smat, w1, b1, freq, w2, b2, w3, decay):
    n_hid = w1.shape[1]
    n_out = w3.shape[1]
    c2 = n_out // 2
    tl = min(seq_len, 512)
    w1p = jnp.zeros((LANES, n_hid), F32).at[:w1.shape[0]].set(w1)
    g, d, stats = pl.pallas_call(
        functools.partial(_filt_time_kernel, seq_len=seq_len, tl=tl),
        grid=(seq_len // tl,),
        in_specs=[pl.BlockSpec((LANES, n_hid), lambda i: (0, 0)),
                  pl.BlockSpec((1, n_hid), lambda i: (0, 0)),
                  pl.BlockSpec((1, n_hid), lambda i: (0, 0)),
                  pl.BlockSpec((n_hid, n_hid), lambda i: (0, 0)),
                  pl.BlockSpec((1, n_hid), lambda i: (0, 0)),
                  pl.BlockSpec((n_hid, n_out), lambda i: (0, 0)),
                  pl.BlockSpec((1, n_out), lambda i: (0, 0))],
        out_specs=[pl.BlockSpec((tl, c2), lambda i: (i, 0)),
                   pl.BlockSpec((tl, c2), lambda i: (i, 0)),
                   pl.BlockSpec((SUBLANES, c2), lambda i: (0, 0))],
        out_shape=[jax.ShapeDtypeStruct((seq_len, c2), BF16),
                   jax.ShapeDtypeStruct((seq_len, c2), BF16),
                   jax.ShapeDtypeStruct((SUBLANES, c2), F32)],
        compiler_params=_params(1),
        name="hyena_filter_taps",
    )(w1p, b1, freq, w2, b2, w3, decay)
    tm = min(seq_len, 512)
    tn = 512
    kre, q = pl.pallas_call(
        _filt_dft_kernel,
        grid=(c2 // tn, seq_len // tm),
        in_specs=[pl.BlockSpec((tm, seq_len), lambda n, m: (m, 0)),
                  pl.BlockSpec((tm, seq_len), lambda n, m: (m, 0)),
                  pl.BlockSpec((seq_len, tn), lambda n, m: (0, n)),
                  pl.BlockSpec((seq_len, tn), lambda n, m: (0, n)),
                  pl.BlockSpec((SUBLANES, tn), lambda n, m: (0, n))],
        out_specs=[pl.BlockSpec((tm, tn), lambda n, m: (m, n)),
                   pl.BlockSpec((tm, tn), lambda n, m: (m, n))],
        out_shape=[jax.ShapeDtypeStruct((seq_len, c2), F32)] * 2,
        compiler_params=_params(2),
        name="hyena_filter_dft",
    )(cmat, smat, g, d, stats)
    return kre, q, stats


def _dft_matrices(seq_len):
    idx = jnp.arange(seq_len, dtype=jnp.int32)
    kn = (idx[:, None] * idx[None, :]) % (2 * seq_len)
    ang = kn.astype(F32) * (math.pi / seq_len)
    cmat = jnp.cos(ang)
    alt = jnp.where(idx % 2 == 0, 1.0, -1.0).astype(F32)
    smat = jnp.where(idx[:, None] == 0, alt[None, :], jnp.sin(ang))
    return cmat.astype(BF16), smat.astype(BF16), smat.T.astype(BF16)


def _conv3_kernel(cur_ref, prev_ref, nxt_ref, w_ref, b_ref, o_ref, *, n_row_tiles):
    r = pl.program_id(1)
    cur = cur_ref[0]
    tr = cur.shape[0]
    prev = jnp.where(r > 0, prev_ref[0], 0.0)
    nxt = jnp.where(r < n_row_tiles - 1, nxt_ref[0], 0.0)
    win = jnp.concatenate([prev, cur, nxt], axis=0)
    w = w_ref[...]
    acc = b_ref[...]
    for k in range(HY_CONV_W):
        acc = acc + w[k:k + 1, :] * _shift_rows(win, k - HY_CONV_W // 2)[SUBLANES:SUBLANES + tr]
    o_ref[0] = acc.astype(BF16)


def _hyena_short_conv(u3, conv_w, conv_b, col0):
    b, seq_len, _ = u3.shape
    n_cols = conv_w.shape[1]
    tcol = 1024
    tr = min(seq_len, 512)
    nrt = seq_len // tr
    cb0 = col0 // tcol
    rb = tr // SUBLANES
    last = seq_len // SUBLANES - 1
    return pl.pallas_call(
        functools.partial(_conv3_kernel, n_row_tiles=nrt),
        grid=(b, nrt, n_cols // tcol),
        in_specs=[pl.BlockSpec((1, tr, tcol), lambda i, r, j: (i, r, cb0 + j)),
                  pl.BlockSpec((1, SUBLANES, tcol), lambda i, r, j: (i, jnp.maximum(r * rb - 1, 0), cb0 + j)),
                  pl.BlockSpec((1, SUBLANES, tcol), lambda i, r, j: (i, jnp.minimum((r + 1) * rb, last), cb0 + j)),
                  pl.BlockSpec((HY_CONV_W, tcol), lambda i, r, j: (0, j)),
                  pl.BlockSpec((1, tcol), lambda i, r, j: (0, j))],
        out_specs=pl.BlockSpec((1, tr, tcol), lambda i, r, j: (i, r, j)),
        out_shape=jax.ShapeDtypeStruct((b, seq_len, n_cols), BF16),
        compiler_params=_params(3),
        name="hyena_short_conv",
    )(u3, u3, u3, conv_w, conv_b)


def _hy_fwd_kernel(c_ref, s_ref, u_ref, kre_ref, q_ref, st_ref, pre_ref, pm_ref, *, seq_len):
    m = pl.program_id(2)
    u = u_ref[0]
    a = jnp.dot(c_ref[...], u, preferred_element_type=F32)
    bv = jnp.dot(s_ref[...], u, preferred_element_type=F32)
    tm = a.shape[0]
    is0 = (m * tm + lax.broadcasted_iota(jnp.int32, a.shape, 0)) == 0
    k_nyq = st_ref[1:2, :] * lax.rsqrt(st_ref[0:1, :] + EPS)
    kre = kre_ref[...]
    q = jnp.where(is0, 0.0, q_ref[...])
    kre_b = jnp.where(is0, k_nyq, kre)
    wk = jnp.where(is0, 0.5 / seq_len, 1.0 / seq_len)
    pre_ref[0] = ((a * kre - bv * q) * wk).astype(BF16)
    pm_ref[0] = ((a * q + bv * kre_b) * wk).astype(BF16)


def _hy_inv_kernel(c_ref, st_ref, pre_ref, pm_ref, u_ref, x_ref, bias_ref, z_ref):
    y = jnp.dot(c_ref[...], pre_ref[0], preferred_element_type=F32)
    y = y + jnp.dot(st_ref[...], pm_ref[0], preferred_element_type=F32)
    y = y + u_ref[0].astype(F32) * bias_ref[...]
    z_ref[0] = (y * x_ref[0].astype(F32)).astype(BF16)


def _hyena_order(order, u_arr, u_cb, hvc, gate_cb, mats, filt, bias, tm, tc):
    cmat, smat, smat_t = mats
    kre, q, stats = filt
    b, seq_len, _ = hvc.shape
    c = kre.shape[1] // 2
    nct = c // tc
    grid = (b, nct, seq_len // tm)
    pre, pm = pl.pallas_call(
        functools.partial(_hy_fwd_kernel, seq_len=seq_len),
        grid=grid,
        in_specs=[pl.BlockSpec((tm, seq_len), lambda i, n, m: (m, 0)),
                  pl.BlockSpec((tm, seq_len), lambda i, n, m: (m, 0)),
                  pl.BlockSpec((1, seq_len, tc), lambda i, n, m: (i, 0, u_cb * nct + n)),
                  pl.BlockSpec((tm, tc), lambda i, n, m: (m, order * nct + n)),
                  pl.BlockSpec((tm, tc), lambda i, n, m: (m, order * nct + n)),
                  pl.BlockSpec((SUBLANES, tc), lambda i, n, m: (0, order * nct + n))],
        out_specs=[pl.BlockSpec((1, tm, tc), lambda i, n, m: (i, m, n))] * 2,
        out_shape=[jax.ShapeDtypeStruct((b, seq_len, c), BF16)] * 2,
        compiler_params=_params(3),
        name="hyena_fwd_dft",
    )(cmat, smat, u_arr, kre, q, stats)
    return pl.pallas_call(
        _hy_inv_kernel,
        grid=grid,
        in_specs=[pl.BlockSpec((tm, seq_len), lambda i, n, m: (m, 0)),
                  pl.BlockSpec((tm, seq_len), lambda i, n, m: (m, 0)),
                  pl.BlockSpec((1, seq_len, tc), lambda i, n, m: (i, 0, n)),
                  pl.BlockSpec((1, seq_len, tc), lambda i, n, m: (i, 0, n)),
                  pl.BlockSpec((1, tm, tc), lambda i, n, m: (i, m, u_cb * nct + n)),
                  pl.BlockSpec((1, tm, tc), lambda i, n, m: (i, m, gate_cb * nct + n)),
                  pl.BlockSpec((1, tc), lambda i, n, m: (0, n))],
        out_specs=pl.BlockSpec((1, tm, tc), lambda i, n, m: (i, m, n)),
        out_shape=jax.ShapeDtypeStruct((b, seq_len, c), BF16),
        compiler_params=_params(3),
        name="hyena_inv_dft",
    )(cmat, smat_t, pre, pm, u_arr, hvc, bias[order][None])


def _post_mixer_kernel(yrg_ref, yhy_ref, x_ref, mod_ref, gnr_ref, gnh_ref, wo_ref, n2_ref,
                       rw_ref, rb_ref, x1_ref, h2_ref, idx_ref, gw_ref):
    d_rg = yrg_ref.shape[1]
    na = _rms(yrg_ref[...], gnr_ref[...]).astype(BF16)
    nb = _rms(yhy_ref[...].astype(F32), gnh_ref[...]).astype(BF16)
    y = jnp.dot(na, wo_ref[0:d_rg, :], preferred_element_type=F32)
    y = y + jnp.dot(nb, wo_ref[d_rg:, :], preferred_element_type=F32)
    x1 = x_ref[...] + mod_ref[0, 2:3, :] * y
    x1_ref[...] = x1
    h2 = (_rms(x1, n2_ref[...]) * (1.0 + mod_ref[0, 4:5, :]) + mod_ref[0, 3:4, :]).astype(BF16)
    h2_ref[...] = h2
    scores = jax.nn.sigmoid(jnp.dot(h2, rw_ref[...], preferred_element_type=F32))
    sel = scores + rb_ref[...]
    n_exp = scores.shape[1]
    lane = lax.broadcasted_iota(jnp.int32, scores.shape, 1)
    col = lax.broadcasted_iota(jnp.int32, idx_ref.shape, 1)
    idx_acc = jnp.zeros(idx_ref.shape, jnp.int32)
    gw_acc = jnp.zeros(gw_ref.shape, F32)
    for k in range(TOP_K):
        mx = jnp.max(sel, axis=1, keepdims=True)
        pick = jnp.min(jnp.where(sel == mx, lane, n_exp), axis=1, keepdims=True)
        hit = lane == pick
        val = jnp.sum(jnp.where(hit, scores, 0.0), axis=1, keepdims=True)
        sel = jnp.where(hit, -jnp.inf, sel)
        idx_acc = jnp.where(col == k, pick, idx_acc)
        gw_acc = jnp.where(col == k, val, gw_acc)
    idx_ref[...] = idx_acc
    gw_ref[...] = gw_acc / jnp.sum(gw_acc, axis=1, keepdims=True) * ROUTED_SCALE


def _post_mixer(y_rg, y_hy, x2d, mod, gn_rg, gn_hy, w_out_bf, norm2_g, router_w_bf, router_b,
                group_of_tile, tm=256):
    t, d = x2d.shape
    d_rg = y_rg.shape[1]
    d_hy = y_hy.shape[1]
    n_exp = router_w_bf.shape[1]
    return pl.pallas_call(
        _post_mixer_kernel,
        grid=(t // tm,),
        in_specs=[pl.BlockSpec((tm, d_rg), lambda i: (i, 0)),
                  pl.BlockSpec((tm, d_hy), lambda i: (i, 0)),
                  pl.BlockSpec((tm, d), lambda i: (i, 0)),
                  pl.BlockSpec((1, N_MOD, d), lambda i: (group_of_tile(i, tm), 0, 0)),
                  pl.BlockSpec((1, d_rg), lambda i: (0, 0)),
                  pl.BlockSpec((1, d_hy), lambda i: (0, 0)),
                  pl.BlockSpec((d_rg + d_hy, d), lambda i: (0, 0)),
                  pl.BlockSpec((1, d), lambda i: (0, 0)),
                  pl.BlockSpec((d, n_exp), lambda i: (0, 0)),
                  pl.BlockSpec((1, n_exp), lambda i: (0, 0))],
        out_specs=[pl.BlockSpec((tm, d), lambda i: (i, 0)),
                   pl.BlockSpec((tm, d), lambda i: (i, 0)),
                   pl.BlockSpec((tm, TOP_K), lambda i: (i, 0)),
                   pl.BlockSpec((tm, TOP_K), lambda i: (i, 0))],
        out_shape=[jax.ShapeDtypeStruct((t, d), F32),
                   jax.ShapeDtypeStruct((t, d), BF16),
                   jax.ShapeDtypeStruct((t, TOP_K), jnp.int32),
                   jax.ShapeDtypeStruct((t, TOP_K), F32)],
        compiler_params=_params(1),
        name="post_mixer",
    )(y_rg, y_hy, x2d, mod, gn_rg, gn_hy, w_out_bf, norm2_g, router_w_bf, router_b)


def _expert_kernel(be_ref, nu_ref, x_ref, w_ref, wg_ref, wu_ref, wd_ref, o_ref, wg_bf, wu_bf, wd_bf):
    i = pl.program_id(0)
    changed = jnp.logical_or(i == 0, be_ref[i] != be_ref[jnp.maximum(i - 1, 0)])

    @pl.when(changed)
    def _():
        wg_bf[...] = wg_ref[0].astype(BF16)
        wu_bf[...] = wu_ref[0].astype(BF16)
        wd_bf[...] = wd_ref[0].astype(BF16)

    @pl.when(i < nu_ref[0])
    def _():
        x = x_ref[...]
        g = jnp.dot(x, wg_bf[...], preferred_element_type=F32)
        u = jnp.dot(x, wu_bf[...], preferred_element_type=F32)
        hmid = (g * jax.nn.sigmoid(g) * u).astype(BF16)
        y = jnp.dot(hmid, wd_bf[...], preferred_element_type=F32)
        o_ref[...] = (y * w_ref[...]).astype(BF16)

    @pl.when(i >= nu_ref[0])
    def _():
        o_ref[...] = jnp.zeros_like(o_ref)


def _experts(blk_e, n_used, xb, w_buf, wg, wu, wd):
    n_rows, d = xb.shape
    n_exp, _, de = wg.shape
    tm = EXPERT_ROWS
    grid_spec = pltpu.PrefetchScalarGridSpec(
        num_scalar_prefetch=2,
        grid=(n_rows // tm,),
        in_specs=[pl.BlockSpec((tm, d), lambda i, be, nu: (i, 0)),
                  pl.BlockSpec((tm, 1), lambda i, be, nu: (i, 0)),
                  pl.BlockSpec((1, d, de), lambda i, be, nu: (be[i], 0, 0)),
                  pl.BlockSpec((1, d, de), lambda i, be, nu: (be[i], 0, 0)),
                  pl.BlockSpec((1, de, d), lambda i, be, nu: (be[i], 0, 0))],
        out_specs=pl.BlockSpec((tm, d), lambda i, be, nu: (i, 0)),
        scratch_shapes=[pltpu.VMEM((d, de), BF16), pltpu.VMEM((d, de), BF16), pltpu.VMEM((de, d), BF16)],
    )
    return pl.pallas_call(
        _expert_kernel,
        grid_spec=grid_spec,
        out_shape=jax.ShapeDtypeStruct((n_rows, d), BF16),
        compiler_params=_params(1),
        name="routed_experts",
    )(blk_e, n_used, xb, w_buf, wg, wu, wd)


def _route(idx, gw, blk, n_exp_static):
    n_tok, top_k = idx.shape
    n_asg = n_tok * top_k
    flat_e = idx.reshape(n_asg)
    order = jnp.argsort(flat_e)
    e_sorted = flat_e[order]
    counts = jnp.bincount(flat_e, length=n_exp_static)
    padded = (counts + blk - 1) // blk * blk
    start = jnp.cumsum(counts) - counts
    pend = jnp.cumsum(padded)
    pstart = pend - padded
    dest_sorted = (pstart[e_sorted] + jnp.arange(n_asg, dtype=jnp.int32) - start[e_sorted]).astype(jnp.int32)
    n_blk = n_asg // blk + n_exp_static
    n_rows = n_blk * blk
    tok_buf = jnp.full((n_rows,), n_tok, jnp.int32).at[dest_sorted].set((order // top_k).astype(jnp.int32))
    w_buf = jnp.zeros((n_rows,), F32).at[dest_sorted].set(gw.reshape(n_asg)[order])
    dest = jnp.zeros((n_asg,), jnp.int32).at[order].set(dest_sorted).reshape(n_tok, top_k)
    blk_e = jnp.minimum(jnp.searchsorted(pend, jnp.arange(n_blk, dtype=jnp.int32) * blk, side='right'),
                        n_exp_static - 1).astype(jnp.int32)
    n_used = (pend[-1] // blk).astype(jnp.int32).reshape(1)
    return tok_buf, w_buf, dest, blk_e, n_used


def _finish_kernel(h2_ref, x1_ref, yg_ref, mod_ref, sg_ref, su_ref, sd_ref, fg_ref, o_ref, *, final_norm):
    d = x1_ref.shape[1]
    h2 = h2_ref[...]
    g = jnp.dot(h2, sg_ref[...], preferred_element_type=F32)
    u = jnp.dot(h2, su_ref[...], preferred_element_type=F32)
    hmid = (g * jax.nn.sigmoid(g) * u).astype(BF16)
    acc = jnp.dot(hmid, sd_ref[...], preferred_element_type=F32)
    for k in range(TOP_K):
        acc = acc + yg_ref[:, k * d:(k + 1) * d].astype(F32)
    x2 = x1_ref[...] + mod_ref[0, 5:6, :] * acc
    o_ref[...] = _rms(x2, fg_ref[...]) if final_norm else x2


def _finish(h2_all, x1_all, yg, mod, sg_bf, su_bf, sd_bf, final_g, row0, n_rows, group_of_tile,
            final_norm, tm=256):
    d = x1_all.shape[1]
    ds_ = sg_bf.shape[1]
    t0 = row0 // tm
    return pl.pallas_call(
        functools.partial(_finish_kernel, final_norm=final_norm),
        grid=(n_rows // tm,),
        in_specs=[pl.BlockSpec((tm, d), lambda i: (t0 + i, 0)),
                  pl.BlockSpec((tm, d), lambda i: (t0 + i, 0)),
                  pl.BlockSpec((tm, TOP_K * d), lambda i: (t0 + i, 0)),
                  pl.BlockSpec((1, N_MOD, d), lambda i: (group_of_tile(i, tm), 0, 0)),
                  pl.BlockSpec((d, ds_), lambda i: (0, 0)),
                  pl.BlockSpec((d, ds_), lambda i: (0, 0)),
                  pl.BlockSpec((ds_, d), lambda i: (0, 0)),
                  pl.BlockSpec((1, d), lambda i: (0, 0))],
        out_specs=pl.BlockSpec((tm, d), lambda i: (i, 0)),
        out_shape=jax.ShapeDtypeStruct((n_rows, d), F32),
        compiler_params=_params(1),
        name="finish",
    )(h2_all, x1_all, yg, mod, sg_bf, su_bf, sd_bf, final_g)


def _grid_pos_emb(rows, d):
    quarter = d // 4
    omega = 1.0 / (10000.0 ** (jnp.arange(quarter, dtype=F32) / quarter))
    r = jnp.repeat(jnp.arange(rows, dtype=F32), GRID_W)[:, None] * omega
    cc = jnp.tile(jnp.arange(GRID_W, dtype=F32), rows)[:, None] * omega
    return jnp.concatenate([jnp.sin(r), jnp.cos(r), jnp.sin(cc), jnp.cos(cc)], axis=-1)


def _mixer_path(x2d, pos, n_b, seq_len, h0, mod, group_of_tile, p, mats, filt, n_seq):
    d = x2d.shape[1]
    d_rg = p['gn_rg'].shape[1]
    c = p['gn_hy'].shape[1]
    u, x_in = _in_proj(x2d, pos, mod, p['norm1_g'], p['w_in'], group_of_tile)
    u3 = u.reshape(n_b, seq_len, u.shape[1])
    y_rg, st = _rglru(u3, h0, p['rg_conv_w'], p['rg_conv_b'], p['rg_w4'], p['rg_b4'], p['rg_lam'], n_seq)
    hvc = _hyena_short_conv(u3, p['hy_conv_w'], p['hy_conv_b'], 2 * d_rg)
    tm = min(seq_len, 512)
    tc = 512 if seq_len > 512 else c
    z1 = _hyena_order(0, hvc, 0, hvc, 1, mats, filt, p['hy_bias'], tm, tc)
    y_hy = _hyena_order(1, z1, 0, hvc, 2, mats, filt, p['hy_bias'], tm, tc)
    x1, h2, idx, gw = _post_mixer(y_rg.reshape(-1, d_rg), y_hy.reshape(-1, c), x_in, mod,
                                  p['gn_rg'], p['gn_hy'], p['w_out'], p['norm2_g'],
                                  p['router_w'], p['router_b'], group_of_tile)
    return x1, h2, idx, gw, st


def kernel(x_prompt, x_sample, state_rglru, c, c_ctx, ada_w, ada_b, norm1_g, norm2_g, w_in, rg_conv_w, rg_conv_b, rg_wa, rg_ba, rg_wx, rg_bx, rg_lam, hy_conv_w, hy_conv_b, hy_w1, hy_b1, hy_freq, hy_w2, hy_b2, hy_w3, hy_decay, hy_bias, gn_rg, gn_hy, w_out, router_w, router_b, exp_w_gate, exp_w_up, exp_w_down, sh_w_gate, sh_w_up, sh_w_down, final_g):
    n_cb, seq_c, d = x_prompt.shape
    n_lb, seq_l, _ = x_sample.shape
    depth = ada_w.shape[0]
    d_rg = gn_rg.shape[1]
    hd = d_rg // RG_HEADS
    t_c = n_cb * seq_c
    t_l = n_lb * seq_l
    assert n_lb + 1 <= SUBLANES

    pos = _grid_pos_emb(seq_l // GRID_W, d)
    cvec = jnp.zeros((SUBLANES, d), F32).at[0].set(c_ctx).at[1:1 + n_lb].set(c)
    mats_c = _dft_matrices(seq_c)
    mats_l = _dft_matrices(seq_l)

    def group_ctx(i, tm):
        return 0

    def group_lat(i, tm):
        return 1 + (i * tm) // seq_l

    xc = x_prompt.reshape(t_c, d)
    xs = x_sample.reshape(t_l, d)
    ctx_states = []
    for l in range(depth):
        last = l == depth - 1
        mod = _modulation(cvec, ada_w[l], ada_b[l][None]).reshape(SUBLANES, N_MOD, d)

        w4 = jnp.concatenate([rg_wa[l, 0], rg_wx[l, 0], rg_wa[l, 1], rg_wx[l, 1]], axis=-1).astype(BF16)
        b4 = jnp.concatenate([rg_ba[l, 0].reshape(RG_HEADS, 1, hd), rg_bx[l, 0].reshape(RG_HEADS, 1, hd),
                              rg_ba[l, 1].reshape(RG_HEADS, 1, hd), rg_bx[l, 1].reshape(RG_HEADS, 1, hd)], axis=-1)
        p = {
            'norm1_g': norm1_g[l][None], 'norm2_g': norm2_g[l][None], 'w_in': w_in[l].astype(BF16),
            'rg_conv_w': rg_conv_w[l], 'rg_conv_b': rg_conv_b[l][None], 'rg_w4': w4, 'rg_b4': b4,
            'rg_lam': rg_lam[l], 'hy_conv_w': hy_conv_w[l], 'hy_conv_b': hy_conv_b[l][None],
            'hy_bias': hy_bias[l], 'gn_rg': gn_rg[l][None], 'gn_hy': gn_hy[l][None],
            'w_out': w_out[l].astype(BF16), 'router_w': router_w[l].astype(BF16), 'router_b': router_b[l][None],
        }
        filt_args = (hy_w1[l], hy_b1[l][None], hy_freq[l][None], hy_w2[l], hy_b2[l][None], hy_w3[l],
                     hy_decay[l].reshape(1, -1))
        filt_c = _hyena_filters(seq_c, mats_c[0], mats_c[1], *filt_args)
        filt_l = _hyena_filters(seq_l, mats_l[0], mats_l[1], *filt_args)

        h0_c = jnp.zeros((n_cb, 2, d_rg), F32)
        x1_c, h2_c, idx_c, gw_c, st_c = _mixer_path(xc, None, n_cb, seq_c, h0_c, mod, group_ctx, p,
                                                    mats_c, filt_c, n_seq=min(8, n_cb))
        ctx_states.append(st_c)
        x1_l, h2_l, idx_l, gw_l, _ = _mixer_path(xs, pos if l == 0 else None, n_lb, seq_l,
                                                 state_rglru[:, l], mod, group_lat, p,
                                                 mats_l, filt_l, n_seq=1)

        h2_all = jnp.concatenate([h2_c, h2_l], axis=0)
        x1_all = jnp.concatenate([x1_c, x1_l], axis=0)
        idx_all = jnp.concatenate([idx_c, idx_l], axis=0)
        gw_all = jnp.concatenate([gw_c, gw_l], axis=0)
        tok_buf, w_buf, dest, blk_e, n_used = _route(idx_all, gw_all, EXPERT_ROWS, router_w.shape[-1])
        h2_pad = jnp.concatenate([h2_all, jnp.zeros((1, d), BF16)], axis=0)
        xb = h2_pad[tok_buf]
        yb = _experts(blk_e, n_used, xb, w_buf[:, None], exp_w_gate[l], exp_w_up[l], exp_w_down[l])
        yg = yb[dest.reshape(-1)].reshape(t_c + t_l, TOP_K * d)
        sh = (sh_w_gate[l].astype(BF16), sh_w_up[l].astype(BF16), sh_w_down[l].astype(BF16))
        xc = _finish(h2_all, x1_all, yg, mod, *sh, final_g[None], 0, t_c, group_ctx, last)
        xs = _finish(h2_all, x1_all, yg, mod, *sh, final_g[None], t_c, t_l, group_lat, last)

    new_state = jnp.stack(ctx_states, axis=1).astype(x_prompt.dtype)
    return (xc.reshape(n_cb, seq_c, d), xs.reshape(n_lb, seq_l, d), new_state)
```

```python
import functools
import math

import jax
import jax.numpy as jnp
from jax import lax
from jax.experimental import pallas as pl
from jax.experimental.pallas import tpu as pltpu

F32 = jnp.float32
BF16 = jnp.bfloat16

GRID_W = 64
RG_HEADS = 8
RG_CONV_W = 4
RG_C = 8.0
HY_CONV_W = 3
HY_BANDS = 16
TOP_K = 8
ROUTED_SCALE = 2.5
N_MOD = 6
EPS = 1e-6

LANES = 128
SUBLANES = 8
VMEM_LIMIT_BYTES = 56 * 1024 * 1024

EXPERT_ROWS = 256


def _params(n_axes, vmem=VMEM_LIMIT_BYTES):
    return pltpu.CompilerParams(dimension_semantics=("arbitrary",) * n_axes, vmem_limit_bytes=vmem)


def _rms(x, g):
    return x * lax.rsqrt(jnp.mean(x * x, axis=-1, keepdims=True) + EPS) * g


HI_HALF = -65536


def _pack_pair(lo, hi):
    lo_b = lax.bitcast_convert_type(lo.astype(BF16).astype(F32), jnp.int32)
    hi_b = lax.bitcast_convert_type(hi.astype(BF16).astype(F32), jnp.int32)
    return hi_b | lax.shift_right_logical(lo_b, 16)


def _unpack_pair(p):
    lo = lax.bitcast_convert_type(lax.shift_left(p, 16), F32)
    hi = lax.bitcast_convert_type(p & HI_HALF, F32)
    return lo, hi


def _mod_kernel(c_ref, w_ref, b_ref, o_ref):
    c = c_ref[...]
    s = (c * jax.nn.sigmoid(c)).astype(BF16)
    o_ref[...] = jnp.dot(s, w_ref[...].astype(BF16), preferred_element_type=F32) + b_ref[...]


def _modulation(cvec, ada_w, ada_b):
    d, n = ada_w.shape
    tn = 1536
    return pl.pallas_call(
        _mod_kernel,
        grid=(n // tn,),
        in_specs=[pl.BlockSpec((SUBLANES, d), lambda j: (0, 0)),
                  pl.BlockSpec((d, tn), lambda j: (0, j)),
                  pl.BlockSpec((1, tn), lambda j: (0, j))],
        out_specs=pl.BlockSpec((SUBLANES, tn), lambda j: (0, j)),
        out_shape=jax.ShapeDtypeStruct((SUBLANES, n), F32),
        compiler_params=_params(1),
        name="modulation",
    )(cvec, ada_w, ada_b)


def _inproj_kernel(*refs, has_pos):
    if has_pos:
        x_ref, pos_ref, mod_ref, g_ref, w_ref, u_ref, xo_ref, h_scr = refs
    else:
        x_ref, mod_ref, g_ref, w_ref, u_ref, h_scr = refs

    @pl.when(pl.program_id(1) == 0)
    def _():
        x = x_ref[...]
        if has_pos:
            x = x + pos_ref[...]
            xo_ref[...] = x
        h = _rms(x, g_ref[...]) * (1.0 + mod_ref[0, 1:2, :]) + mod_ref[0, 0:1, :]
        h_scr[...] = h.astype(BF16)

    u_ref[...] = jnp.dot(h_scr[...], w_ref[...], preferred_element_type=F32)


def _in_proj(x2d, pos, mod, norm_g, w_bf, group_of_tile, tm=512, tn=1024):
    t, d = x2d.shape
    n = w_bf.shape[1]
    has_pos = pos is not None
    in_specs = [pl.BlockSpec((tm, d), lambda i, j: (i, 0))]
    args = [x2d]
    if has_pos:
        pos_tiles = pos.shape[0] // tm
        in_specs.append(pl.BlockSpec((tm, d), lambda i, j: (i % pos_tiles, 0)))
        args.append(pos)
    in_specs += [pl.BlockSpec((1, N_MOD, d), lambda i, j: (group_of_tile(i, tm), 0, 0)),
                 pl.BlockSpec((1, d), lambda i, j: (0, 0)),
                 pl.BlockSpec((d, tn), lambda i, j: (0, j))]
    args += [mod, norm_g, w_bf]
    out_specs = [pl.BlockSpec((tm, tn), lambda i, j: (i, j))]
    out_shape = [jax.ShapeDtypeStruct((t, n), F32)]
    if has_pos:
        out_specs.append(pl.BlockSpec((tm, d), lambda i, j: (i, 0)))
        out_shape.append(jax.ShapeDtypeStruct((t, d), F32))
    res = pl.pallas_call(
        functools.partial(_inproj_kernel, has_pos=has_pos),
        grid=(t // tm, n // tn),
        in_specs=in_specs, out_specs=out_specs, out_shape=out_shape,
        scratch_shapes=[pltpu.VMEM((tm, d), BF16)],
        compiler_params=_params(2),
        name="in_proj",
    )(*args)
    return (res[0], res[1]) if has_pos else (res[0], x2d)


def _shift_rows(win, off):
    if off == 0:
        return win
    n = win.shape[0]
    return pltpu.roll(win, (-off) % n, axis=0)


def _scan_chunk(a, b, reverse):
    n = a.shape[0]
    row = lax.broadcasted_iota(jnp.int32, a.shape, 0)
    dist = 1
    while dist < n:
        if reverse:
            a_s = pltpu.roll(a, n - dist, axis=0)
            b_s = pltpu.roll(b, n - dist, axis=0)
            m = row < n - dist
        else:
            a_s = pltpu.roll(a, dist, axis=0)
            b_s = pltpu.roll(b, dist, axis=0)
            m = row >= dist
        b = jnp.where(m, a * b_s + b, b)
        a = jnp.where(m, a * a_s, a)
        dist *= 2
    return a, b


def _rglru_kernel(xr_ref, gr_ref, cw_ref, cb_ref, w4_ref, b4_ref, lam_ref, h0_ref,
                  y_ref, st_ref, af, bf, ab, bb, hf, *, seq_len, n_seq, t1, tc):
    hd = xr_ref.shape[-1]
    nlam = -lam_ref[...]
    sp = jnp.maximum(nlam, 0.0) + jnp.log1p(jnp.exp(-jnp.abs(nlam)))
    cw = cw_ref[...]
    cb = cb_ref[...]
    b4 = b4_ref[0]
    nc1 = seq_len // t1
    ncs = seq_len // tc

    def per_seq(s, carry0):
        def gates(c, carry):
            r0 = pl.multiple_of(c * t1, t1)
            cur = xr_ref[s, pl.ds(r0, t1), :]
            p0 = pl.multiple_of(jnp.maximum(r0 - SUBLANES, 0), SUBLANES)
            n0 = pl.multiple_of(jnp.minimum(r0 + t1, seq_len - SUBLANES), SUBLANES)
            prev = jnp.where(c > 0, xr_ref[s, pl.ds(p0, SUBLANES), :], 0.0)
            nxt = jnp.where(c < nc1 - 1, xr_ref[s, pl.ds(n0, SUBLANES), :], 0.0)
            win = jnp.concatenate([prev, cur, nxt], axis=0)
            xr = cb
            for k in range(RG_CONV_W):
                xr = xr + cw[k:k + 1, :] * _shift_rows(win, k - RG_CONV_W // 2)[SUBLANES:SUBLANES + t1]
            z = jnp.dot(xr.astype(BF16), w4_ref[0], preferred_element_type=F32) + b4
            for d_i, (a_scr, b_scr) in enumerate(((af, bf), (ab, bb))):
                r = jax.nn.sigmoid(z[:, (2 * d_i) * hd:(2 * d_i + 1) * hd])
                gi = jax.nn.sigmoid(z[:, (2 * d_i + 1) * hd:(2 * d_i + 2) * hd])
                log_a = (-RG_C) * r * sp[d_i:d_i + 1, :]
                a = jnp.exp(log_a)
                a_scr[pl.ds(r0, t1), :] = a
                b_scr[pl.ds(r0, t1), :] = jnp.sqrt(-jnp.tanh(log_a) * (a * a + 1.0)) * (gi * xr)
            return carry

        lax.fori_loop(0, nc1, gates, 0)
        h0 = h0_ref[s]

        def fwd(c, carry):
            r0 = pl.multiple_of(c * tc, tc)
            a, h = _scan_chunk(af[pl.ds(r0, tc), :], bf[pl.ds(r0, tc), :], False)
            h = a * carry + h
            hf[pl.ds(r0, tc), :] = h
            return h[tc - 1:tc, :]

        s_f = lax.fori_loop(0, ncs, fwd, h0[0:1, :])

        def bwd(cc, carry):
            r0 = pl.multiple_of((ncs - 1 - cc) * tc, tc)
            a, h = _scan_chunk(ab[pl.ds(r0, tc), :], bb[pl.ds(r0, tc), :], True)
            h = a * carry + h
            g = gr_ref[s, pl.ds(r0, tc), :]
            y_ref[s, pl.ds(r0, tc), :] = jax.nn.gelu(g) * (hf[pl.ds(r0, tc), :] + h)
            return h[0:1, :]

        s_b = lax.fori_loop(0, ncs, bwd, h0[1:2, :])
        st_ref[s] = jnp.concatenate([s_f, s_b], axis=0)
        return carry0

    lax.fori_loop(0, n_seq, per_seq, 0)


def _rglru(u3, h0, conv_w, conv_b, w4, b4, lam, n_seq):
    b, seq_len, _ = u3.shape
    hd = w4.shape[1]
    d_rg = hd * RG_HEADS
    t1 = min(seq_len, 256)
    tc = 64
    kern = functools.partial(_rglru_kernel, seq_len=seq_len, n_seq=n_seq, t1=t1, tc=tc)
    return pl.pallas_call(
        kern,
        grid=(b // n_seq, RG_HEADS),
        in_specs=[pl.BlockSpec((n_seq, seq_len, hd), lambda i, h: (i, 0, h)),
                  pl.BlockSpec((n_seq, seq_len, hd), lambda i, h: (i, 0, RG_HEADS + h)),
                  pl.BlockSpec((RG_CONV_W, hd), lambda i, h: (0, h)),
                  pl.BlockSpec((1, hd), lambda i, h: (0, h)),
                  pl.BlockSpec((1, hd, 4 * hd), lambda i, h: (h, 0, 0)),
                  pl.BlockSpec((1, 1, 4 * hd), lambda i, h: (h, 0, 0)),
                  pl.BlockSpec((2, hd), lambda i, h: (0, h)),
                  pl.BlockSpec((n_seq, 2, hd), lambda i, h: (i, 0, h))],
        out_specs=[pl.BlockSpec((n_seq, seq_len, hd), lambda i, h: (i, 0, h)),
                   pl.BlockSpec((n_seq, 2, hd), lambda i, h: (i, 0, h))],
        out_shape=[jax.ShapeDtypeStruct((b, seq_len, d_rg), F32),
                   jax.ShapeDtypeStruct((b, 2, d_rg), F32)],
        scratch_shapes=[pltpu.VMEM((seq_len, hd), F32)] * 5,
        compiler_params=_params(2),
        name="rglru",
    )(u3, u3, conv_w, conv_b, w4, b4, lam, h0)


def _filt_time_kernel(w1_ref, b1_ref, fr_ref, w2_ref, b2_ref, w3_ref, dec_ref,
                      g_ref, d_ref, st_ref, *, seq_len, tl):
    i = pl.program_id(0)
    hi = lax.Precision.HIGHEST
    posi = i * tl + lax.broadcasted_iota(jnp.int32, (tl, LANES), 0)
    pos = posi.astype(F32)
    lane = lax.broadcasted_iota(jnp.int32, (tl, LANES), 1)
    band = jnp.where(lane <= HY_BANDS, lane, lane - HY_BANDS).astype(F32)
    ang = (2.0 * math.pi) * pos / seq_len * band
    t = pos / seq_len
    feats = jnp.where(lane == 0, t,
                      jnp.where(lane <= HY_BANDS, jnp.cos(ang),
                                jnp.where(lane <= 2 * HY_BANDS, -jnp.sin(ang), 0.0)))
    fr = fr_ref[...]
    hid = jnp.sin(fr * (jnp.dot(feats, w1_ref[...], precision=hi, preferred_element_type=F32) + b1_ref[...]))
    hid = jnp.sin(fr * (jnp.dot(hid, w2_ref[...], precision=hi, preferred_element_type=F32) + b2_ref[...]))
    k = jnp.dot(hid, w3_ref[...], precision=hi, preferred_element_type=F32)
    k = k * jnp.exp(-t[:, 0:1] * jnp.abs(dec_ref[...]))
    c = k.shape[1] // 4
    first = posi[:, 0:1] == 0
    sign = jnp.where(posi[:, 0:1] % 2 == 0, 1.0, -1.0)

    @pl.when(i == 0)
    def _():
        st_ref[...] = jnp.zeros_like(st_ref)

    for o in range(2):
        kf = k[:, (2 * o) * c:(2 * o + 1) * c]
        kb = jnp.where(first, 0.0, k[:, (2 * o + 1) * c:(2 * o + 2) * c])
        g = kf + kb
        g_ref[:, o * c:(o + 1) * c] = g.astype(BF16)
        d_ref[:, o * c:(o + 1) * c] = (kf - kb).astype(BF16)
        st_ref[0:1, o * c:(o + 1) * c] += jnp.sum(kf * kf + kb * kb, axis=0, keepdims=True)
        st_ref[1:2, o * c:(o + 1) * c] += jnp.sum(sign * g, axis=0, keepdims=True)


def _filt_dft_kernel(c_ref, s_ref, g_ref, d_ref, st_ref, kre_ref, q_ref):
    scale = lax.rsqrt(st_ref[0:1, :] + EPS)
    kre_ref[...] = jnp.dot(c_ref[...], g_ref[...], preferred_element_type=F32) * scale
    q_ref[...] = jnp.dot(s_ref[...], d_ref[...], preferred_element_type=F32) * scale


def _hyena_filters(seq_len, cmat, smat, w1, b1, freq, w2, b2, w3, decay):
    n_hid = w1.shape[1]
    n_out = w3.shape[1]
    c2 = n_out // 2
    tl = min(seq_len, 512)
    w1p = jnp.zeros((LANES, n_hid), F32).at[:w1.shape[0]].set(w1)
    g, d, stats = pl.pallas_call(
        functools.partial(_filt_time_kernel, seq_len=seq_len, tl=tl),
        grid=(seq_len // tl,),
        in_specs=[pl.BlockSpec((LANES, n_hid), lambda i: (0, 0)),
                  pl.BlockSpec((1, n_hid), lambda i: (0, 0)),
                  pl.BlockSpec((1, n_hid), lambda i: (0, 0)),
                  pl.BlockSpec((n_hid, n_hid), lambda i: (0, 0)),
                  pl.BlockSpec((1, n_hid), lambda i: (0, 0)),
                  pl.BlockSpec((n_hid, n_out), lambda i: (0, 0)),
                  pl.BlockSpec((1, n_out), lambda i: (0, 0))],
        out_specs=[pl.BlockSpec((tl, c2), lambda i: (i, 0)),
                   pl.BlockSpec((tl, c2), lambda i: (i, 0)),
                   pl.BlockSpec((SUBLANES, c2), lambda i: (0, 0))],
        out_shape=[jax.ShapeDtypeStruct((seq_len, c2), BF16),
                   jax.ShapeDtypeStruct((seq_len, c2), BF16),
                   jax.ShapeDtypeStruct((SUBLANES, c2), F32)],
        compiler_params=_params(1),
        name="hyena_filter_taps",
    )(w1p, b1, freq, w2, b2, w3, decay)
    tm = min(seq_len, 512)
    tn = 512
    kre, q = pl.pallas_call(
        _filt_dft_kernel,
        grid=(c2 // tn, seq_len // tm),
        in_specs=[pl.BlockSpec((tm, seq_len), lambda n, m: (m, 0)),
                  pl.BlockSpec((tm, seq_len), lambda n, m: (m, 0)),
                  pl.BlockSpec((seq_len, tn), lambda n, m: (0, n)),
                  pl.BlockSpec((seq_len, tn), lambda n, m: (0, n)),
                  pl.BlockSpec((SUBLANES, tn), lambda n, m: (0, n))],
        out_specs=[pl.BlockSpec((tm, tn), lambda n, m: (m, n)),
                   pl.BlockSpec((tm, tn), lambda n, m: (m, n))],
        out_shape=[jax.ShapeDtypeStruct((seq_len, c2), F32)] * 2,
        compiler_params=_params(2),
        name="hyena_filter_dft",
    )(cmat, smat, g, d, stats)
    return kre, q, stats


def _dft_matrices(seq_len):
    assert seq_len & (seq_len - 1) == 0
    tm = min(seq_len, 256)
    out = jax.ShapeDtypeStruct((seq_len, seq_len), BF16)
    return pl.pallas_call(
        functools.partial(_dft_table_kernel, seq_len=seq_len),
        grid=(seq_len // tm,),
        in_specs=[],
        out_specs=[pl.BlockSpec((tm, seq_len), lambda i: (i, 0))] * 3,
        out_shape=[out] * 3,
        scratch_shapes=[pltpu.VMEM((tm, seq_len), F32)] * 2,
        compiler_params=_params(1),
        name="dft_tables",
    )()


def _dft_table_kernel(c_ref, s_ref, st_ref, c0, s0, *, seq_len):
    i = pl.program_id(0)
    tm = c_ref.shape[0]
    wrap = 2 * seq_len - 1
    row = lax.broadcasted_iota(jnp.int32, (tm, seq_len), 0)
    col = lax.broadcasted_iota(jnp.int32, (tm, seq_len), 1)

    @pl.when(i == 0)
    def _():
        ang = ((row * col) & wrap).astype(F32) * (math.pi / seq_len)
        c0[...] = jnp.cos(ang)
        s0[...] = jnp.sin(ang)

    k0 = i * tm
    ang0 = ((k0 * col[0:1, :]) & wrap).astype(F32) * (math.pi / seq_len)
    cn = jnp.cos(ang0)
    sn = jnp.sin(ang0)
    cmat = c0[...] * cn - s0[...] * sn
    smat = s0[...] * cn + c0[...] * sn
    alt_col = jnp.where((col & 1) == 0, 1.0, -1.0)
    alt_row = jnp.where(((row + k0) & 1) == 0, 1.0, -1.0)
    c_ref[...] = cmat.astype(BF16)
    s_ref[...] = jnp.where(row + k0 == 0, alt_col, smat).astype(BF16)
    st_ref[...] = jnp.where(col == 0, alt_row, smat).astype(BF16)


def _conv3_kernel(cur_ref, prev_ref, nxt_ref, w_ref, b_ref, o_ref, *, n_row_tiles):
    r = pl.program_id(1)
    cur = cur_ref[0]
    tr = cur.shape[0]
    prev = jnp.where(r > 0, prev_ref[0], 0.0)
    nxt = jnp.where(r < n_row_tiles - 1, nxt_ref[0], 0.0)
    win = jnp.concatenate([prev, cur, nxt], axis=0)
    w = w_ref[...]
    acc = b_ref[...]
    for k in range(HY_CONV_W):
        acc = acc + w[k:k + 1, :] * _shift_rows(win, k - HY_CONV_W // 2)[SUBLANES:SUBLANES + tr]
    o_ref[0] = acc.astype(BF16)


def _hyena_short_conv(u3, conv_w, conv_b, col0):
    b, seq_len, _ = u3.shape
    n_cols = conv_w.shape[1]
    tcol = 1024
    tr = min(seq_len, 512)
    nrt = seq_len // tr
    cb0 = col0 // tcol
    rb = tr // SUBLANES
    last = seq_len // SUBLANES - 1
    return pl.pallas_call(
        functools.partial(_conv3_kernel, n_row_tiles=nrt),
        grid=(b, nrt, n_cols // tcol),
        in_specs=[pl.BlockSpec((1, tr, tcol), lambda i, r, j: (i, r, cb0 + j)),
                  pl.BlockSpec((1, SUBLANES, tcol), lambda i, r, j: (i, jnp.maximum(r * rb - 1, 0), cb0 + j)),
                  pl.BlockSpec((1, SUBLANES, tcol), lambda i, r, j: (i, jnp.minimum((r + 1) * rb, last), cb0 + j)),
                  pl.BlockSpec((HY_CONV_W, tcol), lambda i, r, j: (0, j)),
                  pl.BlockSpec((1, tcol), lambda i, r, j: (0, j))],
        out_specs=pl.BlockSpec((1, tr, tcol), lambda i, r, j: (i, r, j)),
        out_shape=jax.ShapeDtypeStruct((b, seq_len, n_cols), BF16),
        compiler_params=_params(3),
        name="hyena_short_conv",
    )(u3, u3, u3, conv_w, conv_b)


def _hy_fwd_kernel(c_ref, s_ref, u_ref, kre_ref, q_ref, st_ref, pre_ref, pm_ref, *, seq_len):
    m = pl.program_id(2)
    u = u_ref[0]
    a = jnp.dot(c_ref[...], u, preferred_element_type=F32)
    bv = jnp.dot(s_ref[...], u, preferred_element_type=F32)
    tm = a.shape[0]
    is0 = (m * tm + lax.broadcasted_iota(jnp.int32, a.shape, 0)) == 0
    k_nyq = st_ref[1:2, :] * lax.rsqrt(st_ref[0:1, :] + EPS)
    kre = kre_ref[...]
    q = jnp.where(is0, 0.0, q_ref[...])
    kre_b = jnp.where(is0, k_nyq, kre)
    wk = jnp.where(is0, 0.5 / seq_len, 1.0 / seq_len)
    pre_ref[0] = ((a * kre - bv * q) * wk).astype(BF16)
    pm_ref[0] = ((a * q + bv * kre_b) * wk).astype(BF16)


def _hy_inv_kernel(c_ref, st_ref, pre_ref, pm_ref, u_ref, x_ref, bias_ref, z_ref):
    y = jnp.dot(c_ref[...], pre_ref[0], preferred_element_type=F32)
    y = y + jnp.dot(st_ref[...], pm_ref[0], preferred_element_type=F32)
    y = y + u_ref[0].astype(F32) * bias_ref[...]
    z_ref[0] = (y * x_ref[0].astype(F32)).astype(BF16)


def _hyena_order(order, u_arr, u_cb, hvc, gate_cb, mats, filt, bias, tm, tc):
    cmat, smat, smat_t = mats
    kre, q, stats = filt
    b, seq_len, _ = hvc.shape
    c = kre.shape[1] // 2
    nct = c // tc
    grid = (b, nct, seq_len // tm)
    pre, pm = pl.pallas_call(
        functools.partial(_hy_fwd_kernel, seq_len=seq_len),
        grid=grid,
        in_specs=[pl.BlockSpec((tm, seq_len), lambda i, n, m: (m, 0)),
                  pl.BlockSpec((tm, seq_len), lambda i, n, m: (m, 0)),
                  pl.BlockSpec((1, seq_len, tc), lambda i, n, m: (i, 0, u_cb * nct + n)),
                  pl.BlockSpec((tm, tc), lambda i, n, m: (m, order * nct + n)),
                  pl.BlockSpec((tm, tc), lambda i, n, m: (m, order * nct + n)),
                  pl.BlockSpec((SUBLANES, tc), lambda i, n, m: (0, order * nct + n))],
        out_specs=[pl.BlockSpec((1, tm, tc), lambda i, n, m: (i, m, n))] * 2,
        out_shape=[jax.ShapeDtypeStruct((b, seq_len, c), BF16)] * 2,
        compiler_params=_params(3),
        name="hyena_fwd_dft",
    )(cmat, smat, u_arr, kre, q, stats)
    return pl.pallas_call(
        _hy_inv_kernel,
        grid=grid,
        in_specs=[pl.BlockSpec((tm, seq_len), lambda i, n, m: (m, 0)),
                  pl.BlockSpec((tm, seq_len), lambda i, n, m: (m, 0)),
                  pl.BlockSpec((1, seq_len, tc), lambda i, n, m: (i, 0, n)),
                  pl.BlockSpec((1, seq_len, tc), lambda i, n, m: (i, 0, n)),
                  pl.BlockSpec((1, tm, tc), lambda i, n, m: (i, m, u_cb * nct + n)),
                  pl.BlockSpec((1, tm, tc), lambda i, n, m: (i, m, gate_cb * nct + n)),
                  pl.BlockSpec((1, tc), lambda i, n, m: (0, n))],
        out_specs=pl.BlockSpec((1, tm, tc), lambda i, n, m: (i, m, n)),
        out_shape=jax.ShapeDtypeStruct((b, seq_len, c), BF16),
        compiler_params=_params(3),
        name="hyena_inv_dft",
    )(cmat, smat_t, pre, pm, u_arr, hvc, bias[order][None])


def _post_mixer_kernel(yrg_ref, yhy_ref, x_ref, mod_ref, gnr_ref, gnh_ref, wo_ref, n2_ref,
                       rw_ref, rb_ref, x1_ref, h2_ref, idx_ref, gw_ref):
    d_rg = yrg_ref.shape[1]
    na = _rms(yrg_ref[...], gnr_ref[...]).astype(BF16)
    nb = _rms(yhy_ref[...].astype(F32), gnh_ref[...]).astype(BF16)
    y = jnp.dot(na, wo_ref[0:d_rg, :], preferred_element_type=F32)
    y = y + jnp.dot(nb, wo_ref[d_rg:, :], preferred_element_type=F32)
    x1 = x_ref[...] + mod_ref[0, 2:3, :] * y
    x1_ref[...] = x1
    h2f = _rms(x1, n2_ref[...]) * (1.0 + mod_ref[0, 4:5, :]) + mod_ref[0, 3:4, :]
    half = h2f.shape[1] // 2
    h2_ref[...] = _pack_pair(h2f[:, :half], h2f[:, half:])
    h2 = h2f.astype(BF16)
    scores = jax.nn.sigmoid(jnp.dot(h2, rw_ref[...], preferred_element_type=F32))
    sel = scores + rb_ref[...]
    n_exp = scores.shape[1]
    lane = lax.broadcasted_iota(jnp.int32, scores.shape, 1)
    col = lax.broadcasted_iota(jnp.int32, idx_ref.shape, 1)
    idx_acc = jnp.zeros(idx_ref.shape, jnp.int32)
    gw_acc = jnp.zeros(gw_ref.shape, F32)
    for k in range(TOP_K):
        mx = jnp.max(sel, axis=1, keepdims=True)
        pick = jnp.min(jnp.where(sel == mx, lane, n_exp), axis=1, keepdims=True)
        hit = lane == pick
        val = jnp.sum(jnp.where(hit, scores, 0.0), axis=1, keepdims=True)
        sel = jnp.where(hit, -jnp.inf, sel)
        idx_acc = jnp.where(col == k, pick, idx_acc)
        gw_acc = jnp.where(col == k, val, gw_acc)
    idx_ref[...] = idx_acc
    gw_ref[...] = gw_acc / jnp.sum(gw_acc, axis=1, keepdims=True) * ROUTED_SCALE


def _post_mixer(y_rg, y_hy, x2d, mod, gn_rg, gn_hy, w_out_bf, norm2_g, router_w_bf, router_b,
                group_of_tile, tm=256):
    t, d = x2d.shape
    d_rg = y_rg.shape[1]
    d_hy = y_hy.shape[1]
    n_exp = router_w_bf.shape[1]
    return pl.pallas_call(
        _post_mixer_kernel,
        grid=(t // tm,),
        in_specs=[pl.BlockSpec((tm, d_rg), lambda i: (i, 0)),
                  pl.BlockSpec((tm, d_hy), lambda i: (i, 0)),
                  pl.BlockSpec((tm, d), lambda i: (i, 0)),
                  pl.BlockSpec((1, N_MOD, d), lambda i: (group_of_tile(i, tm), 0, 0)),
                  pl.BlockSpec((1, d_rg), lambda i: (0, 0)),
                  pl.BlockSpec((1, d_hy), lambda i: (0, 0)),
                  pl.BlockSpec((d_rg + d_hy, d), lambda i: (0, 0)),
                  pl.BlockSpec((1, d), lambda i: (0, 0)),
                  pl.BlockSpec((d, n_exp), lambda i: (0, 0)),
                  pl.BlockSpec((1, n_exp), lambda i: (0, 0))],
        out_specs=[pl.BlockSpec((tm, d), lambda i: (i, 0)),
                   pl.BlockSpec((tm, d // 2), lambda i: (i, 0)),
                   pl.BlockSpec((tm, TOP_K), lambda i: (i, 0)),
                   pl.BlockSpec((tm, TOP_K), lambda i: (i, 0))],
        out_shape=[jax.ShapeDtypeStruct((t, d), F32),
                   jax.ShapeDtypeStruct((t, d // 2), jnp.int32),
                   jax.ShapeDtypeStruct((t, TOP_K), jnp.int32),
                   jax.ShapeDtypeStruct((t, TOP_K), F32)],
        compiler_params=_params(1),
        name="post_mixer",
    )(y_rg, y_hy, x2d, mod, gn_rg, gn_hy, w_out_bf, norm2_g, router_w_bf, router_b)


def _route_kernel(idx_ref, dest_ref, be_ref, bv_ref, nu_ref, tri, cnt, base, pst, *, blk, n_blk):
    p = pl.program_id(0)
    i = pl.program_id(1)
    tm, top_k = idx_ref.shape
    ne = cnt.shape[1]
    shift = blk.bit_length() - 1
    assert blk == 1 << shift

    @pl.when(jnp.logical_and(p == 0, i == 0))
    def _():
        r = lax.broadcasted_iota(jnp.int32, (tm, tm), 0)
        c = lax.broadcasted_iota(jnp.int32, (tm, tm), 1)
        tri[...] = jnp.where(r > c, 1.0, 0.0).astype(BF16)
        cnt[...] = jnp.zeros_like(cnt)

    idx = idx_ref[...]
    lane = lax.broadcasted_iota(jnp.int32, (tm, ne), 1)
    hits = [lane == idx[:, k:k + 1] for k in range(top_k)]
    occ = jnp.zeros((tm, ne), F32)
    for h in hits:
        occ = occ + jnp.where(h, 1.0, 0.0)
    col_sum = jnp.sum(occ, axis=0, keepdims=True)

    @pl.when(p == 0)
    def _():
        cnt[0:1, :] += col_sum

    @pl.when(jnp.logical_and(p == 1, i == 0))
    def _():
        counts = cnt[...]
        c_i = counts.astype(jnp.int32)
        padded = lax.shift_left(lax.shift_right_logical(c_i + (blk - 1), shift), shift).astype(F32)
        r = lax.broadcasted_iota(jnp.int32, (ne, ne), 0)
        c = lax.broadcasted_iota(jnp.int32, (ne, ne), 1)
        upper = jnp.where(r <= c, 1.0, 0.0)
        pend = jnp.dot(padded, upper, precision=lax.Precision.HIGHEST, preferred_element_type=F32)
        pstart = pend - padded
        pst[...] = pstart
        base[...] = jnp.zeros_like(base)
        b0 = (lax.broadcasted_iota(jnp.int32, (n_blk, ne), 0) * blk).astype(F32)
        be = jnp.sum(jnp.where(pend[0:1, :] <= b0, 1.0, 0.0), axis=1, keepdims=True)
        be = jnp.minimum(be, ne - 1.0)
        own = lax.broadcasted_iota(jnp.int32, (n_blk, ne), 1).astype(F32) == be
        pst_b = jnp.sum(jnp.where(own, pstart[0:1, :], 0.0), axis=1, keepdims=True)
        cnt_b = jnp.sum(jnp.where(own, counts[0:1, :], 0.0), axis=1, keepdims=True)
        valid = jnp.clip(cnt_b - (b0[:, 0:1] - pst_b), 0.0, float(blk))
        be_ref[...] = be.astype(jnp.int32)
        bv_ref[...] = valid.astype(jnp.int32)
        total = jnp.max(pend[0:1, :], axis=1, keepdims=True).astype(jnp.int32)
        nu_ref[...] = jnp.broadcast_to(lax.shift_right_logical(total, shift), nu_ref.shape)

    @pl.when(p == 1)
    def _():
        cum = jnp.dot(tri[...], occ.astype(BF16), preferred_element_type=F32) + base[0:1, :] + pst[0:1, :]
        col = lax.broadcasted_iota(jnp.int32, (tm, top_k), 1)
        acc = jnp.zeros((tm, top_k), F32)
        for k in range(top_k):
            v = jnp.sum(jnp.where(hits[k], cum, 0.0), axis=1, keepdims=True)
            acc = jnp.where(col == k, v, acc)
        dest_ref[...] = acc.astype(jnp.int32)
        base[0:1, :] += col_sum


def _route(idx, n_exp, blk, tm=512):
    n_tok, top_k = idx.shape
    n_blk = n_tok * top_k // blk + n_exp
    dest, be, bv, nu = pl.pallas_call(
        functools.partial(_route_kernel, blk=blk, n_blk=n_blk),
        grid=(2, n_tok // tm),
        in_specs=[pl.BlockSpec((tm, top_k), lambda p, i: (i, 0))],
        out_specs=[pl.BlockSpec((tm, top_k), lambda p, i: (i * p, 0)),
                   pl.BlockSpec((n_blk, 1), lambda p, i: (0, 0)),
                   pl.BlockSpec((n_blk, 1), lambda p, i: (0, 0)),
                   pl.BlockSpec((SUBLANES, LANES), lambda p, i: (0, 0))],
        out_shape=[jax.ShapeDtypeStruct((n_tok, top_k), jnp.int32),
                   jax.ShapeDtypeStruct((n_blk, 1), jnp.int32),
                   jax.ShapeDtypeStruct((n_blk, 1), jnp.int32),
                   jax.ShapeDtypeStruct((SUBLANES, LANES), jnp.int32)],
        scratch_shapes=[pltpu.VMEM((tm, tm), BF16), pltpu.VMEM((SUBLANES, n_exp), F32),
                        pltpu.VMEM((SUBLANES, n_exp), F32), pltpu.VMEM((SUBLANES, n_exp), F32)],
        compiler_params=_params(2),
        name="route",
    )(idx)
    return dest, be.reshape(n_blk), bv.reshape(n_blk), nu[0, 0:1]


def _dispatch_kernel(dest_ref, xa_ref, xl_ref, out_ref, sem, *, n_a):
    tm = xa_ref.shape[0]

    def run(src):
        def issue(t, c):
            for k in range(TOP_K):
                row = dest_ref[t * TOP_K + k]
                pltpu.make_async_copy(src.at[pl.ds(t, 1), :], out_ref.at[pl.ds(row, 1), :], sem).start()
            return c

        lax.fori_loop(0, tm, issue, 0)

        def drain(t, c):
            for k in range(TOP_K):
                pltpu.make_async_copy(src.at[pl.ds(0, 1), :], out_ref.at[pl.ds(0, 1), :], sem).wait()
            return c

        lax.fori_loop(0, tm, drain, 0, unroll=8)

    @pl.when(pl.program_id(0) < n_a)
    def _():
        run(xa_ref)

    @pl.when(pl.program_id(0) >= n_a)
    def _():
        run(xl_ref)


def _dispatch(dest_flat, h2p_a, h2p_l, n_rows, tm=256):
    dh = h2p_a.shape[1]
    n_a = h2p_a.shape[0] // tm
    n_l = h2p_l.shape[0] // tm
    return pl.pallas_call(
        functools.partial(_dispatch_kernel, n_a=n_a),
        grid=(n_a + n_l,),
        in_specs=[pl.BlockSpec((tm * TOP_K,), lambda i: (i,), memory_space=pltpu.SMEM),
                  pl.BlockSpec((tm, dh), lambda i: (jnp.minimum(i, n_a - 1), 0)),
                  pl.BlockSpec((tm, dh), lambda i: (jnp.maximum(i - n_a, 0), 0))],
        out_specs=pl.BlockSpec(memory_space=pl.ANY),
        out_shape=jax.ShapeDtypeStruct((n_rows, dh), jnp.int32),
        scratch_shapes=[pltpu.SemaphoreType.DMA(())],
        compiler_params=_params(1),
        name="dispatch",
    )(dest_flat, h2p_a, h2p_l)


def _expert_kernel(be_ref, bv_ref, nu_ref, x_ref, wg_hbm, wu_hbm, wd_hbm, o_ref,
                   wg_f, wu_f, wd_f, wg_b, wu_b, wd_b, sem, grp, *, n_blk):
    i = pl.program_id(0)
    n_used = nu_ref[0]
    e = be_ref[i]
    active = i < n_used
    changed = jnp.logical_or(i == 0, e != be_ref[jnp.maximum(i - 1, 0)])

    def copies(ex, slot):
        return (pltpu.make_async_copy(wg_hbm.at[ex], wg_f.at[slot], sem.at[slot, 0]),
                pltpu.make_async_copy(wu_hbm.at[ex], wu_f.at[slot], sem.at[slot, 1]),
                pltpu.make_async_copy(wd_hbm.at[ex], wd_f.at[slot], sem.at[slot, 2]))

    @pl.when(jnp.logical_and(active, i == 0))
    def _():
        grp[0] = 0
        for cp in copies(e, 0):
            cp.start()

    @pl.when(jnp.logical_and(active, changed))
    def _():
        slot = grp[0] % 2
        for cp in copies(e, slot):
            cp.wait()
        nxt = lax.while_loop(
            lambda j: jnp.logical_and(j < n_used, be_ref[jnp.minimum(j, n_blk - 1)] == e),
            lambda j: j + 1, i + 1)

        @pl.when(nxt < n_used)
        def _():
            for cp in copies(be_ref[jnp.minimum(nxt, n_blk - 1)], 1 - slot):
                cp.start()

        wg_b[...] = wg_f[slot].astype(BF16)
        wu_b[...] = wu_f[slot].astype(BF16)
        wd_b[...] = wd_f[slot].astype(BF16)
        grp[0] = grp[0] + 1

    @pl.when(active)
    def _():
        tm, dh = x_ref.shape
        live = lax.broadcasted_iota(jnp.int32, (tm, dh), 0) < bv_ref[i]
        lo, hi = _unpack_pair(x_ref[...])
        lo = jnp.where(live, lo, 0.0).astype(BF16)
        hi = jnp.where(live, hi, 0.0).astype(BF16)
        g = jnp.dot(lo, wg_b[0:dh, :], preferred_element_type=F32)
        g = g + jnp.dot(hi, wg_b[dh:, :], preferred_element_type=F32)
        u = jnp.dot(lo, wu_b[0:dh, :], preferred_element_type=F32)
        u = u + jnp.dot(hi, wu_b[dh:, :], preferred_element_type=F32)
        hmid = (g * jax.nn.sigmoid(g) * u).astype(BF16)
        y = jnp.dot(hmid, wd_b[...], preferred_element_type=F32)
        o_ref[...] = _pack_pair(y[:, :dh], y[:, dh:])

    @pl.when(jnp.logical_not(active))
    def _():
        o_ref[...] = jnp.zeros_like(o_ref)


def _experts(blk_e, blk_valid, n_used, xb, wg, wu, wd):
    n_rows, dh = xb.shape
    n_exp, d, de = wg.shape
    tm = EXPERT_ROWS
    n_blk = n_rows // tm
    grid_spec = pltpu.PrefetchScalarGridSpec(
        num_scalar_prefetch=3,
        grid=(n_blk,),
        in_specs=[pl.BlockSpec((tm, dh), lambda i, be, bv, nu: (i, 0)),
                  pl.BlockSpec(memory_space=pl.ANY),
                  pl.BlockSpec(memory_space=pl.ANY),
                  pl.BlockSpec(memory_space=pl.ANY)],
        out_specs=pl.BlockSpec((tm, dh), lambda i, be, bv, nu: (i, 0)),
        scratch_shapes=[pltpu.VMEM((2, d, de), F32), pltpu.VMEM((2, d, de), F32), pltpu.VMEM((2, de, d), F32),
                        pltpu.VMEM((d, de), BF16), pltpu.VMEM((d, de), BF16), pltpu.VMEM((de, d), BF16),
                        pltpu.SemaphoreType.DMA((2, 3)), pltpu.SMEM((1,), jnp.int32)],
    )
    return pl.pallas_call(
        functools.partial(_expert_kernel, n_blk=n_blk),
        grid_spec=grid_spec,
        out_shape=jax.ShapeDtypeStruct((n_rows, dh), jnp.int32),
        compiler_params=_params(1),
        name="routed_experts",
    )(blk_e, blk_valid, n_used, xb, wg, wu, wd)


def _finish_kernel(dcur_ref, dnxt_ref, h2_ref, x1_ref, gw_ref, mod_ref, sg_ref, su_ref, sd_ref, fg_ref,
                   yb_hbm, o_ref, gbuf, sem, *, final_norm):
    i = pl.program_id(0)
    n_tiles = pl.num_programs(0)
    tm, dh = h2_ref.shape
    slot = i % 2

    def row_copy(row, t, k, s):
        return pltpu.make_async_copy(yb_hbm.at[pl.ds(row, 1), :], gbuf.at[s, pl.ds(k * tm + t, 1), :], sem.at[s])

    def gather(d_ref, s):
        def issue(t, c):
            for k in range(TOP_K):
                row_copy(d_ref[t * TOP_K + k], t, k, s).start()
            return c

        lax.fori_loop(0, tm, issue, 0)

    @pl.when(i == 0)
    def _():
        gather(dcur_ref, 0)

    @pl.when(i + 1 < n_tiles)
    def _():
        gather(dnxt_ref, 1 - slot)

    lo, hi = _unpack_pair(h2_ref[...])
    lo = lo.astype(BF16)
    hi = hi.astype(BF16)
    g = jnp.dot(lo, sg_ref[0:dh, :], preferred_element_type=F32) + jnp.dot(hi, sg_ref[dh:, :], preferred_element_type=F32)
    u = jnp.dot(lo, su_ref[0:dh, :], preferred_element_type=F32) + jnp.dot(hi, su_ref[dh:, :], preferred_element_type=F32)
    hmid = (g * jax.nn.sigmoid(g) * u).astype(BF16)
    shared = jnp.dot(hmid, sd_ref[...], preferred_element_type=F32)
    acc_lo = shared[:, :dh]
    acc_hi = shared[:, dh:]

    def drain(t, c):
        for k in range(TOP_K):
            row_copy(0, 0, 0, slot).wait()
        return c

    lax.fori_loop(0, tm, drain, 0, unroll=8)

    gw = gw_ref[...]
    for k in range(TOP_K):
        r_lo, r_hi = _unpack_pair(gbuf[slot, k * tm:(k + 1) * tm, :])
        acc_lo = acc_lo + gw[:, k:k + 1] * r_lo
        acc_hi = acc_hi + gw[:, k:k + 1] * r_hi
    x_lo = x1_ref[:, :dh] + mod_ref[0, 5:6, :dh] * acc_lo
    x_hi = x1_ref[:, dh:] + mod_ref[0, 5:6, dh:] * acc_hi
    if final_norm:
        ms = (jnp.sum(x_lo * x_lo, axis=-1, keepdims=True) + jnp.sum(x_hi * x_hi, axis=-1, keepdims=True)) / (2 * dh)
        inv = lax.rsqrt(ms + EPS)
        x_lo = x_lo * inv * fg_ref[:, :dh]
        x_hi = x_hi * inv * fg_ref[:, dh:]
    o_ref[:, :dh] = x_lo
    o_ref[:, dh:] = x_hi


def _finish(dest_flat, h2p, x1, gw, yb, mod, sg_bf, su_bf, sd_bf, final_g, row0, group_of_tile,
            final_norm, tm=256):
    n_rows, d = x1.shape
    dh = d // 2
    ds_ = sg_bf.shape[1]
    t0 = row0 // tm
    n_tiles = n_rows // tm
    return pl.pallas_call(
        functools.partial(_finish_kernel, final_norm=final_norm),
        grid=(n_tiles,),
        in_specs=[pl.BlockSpec((tm * TOP_K,), lambda i: (t0 + i,), memory_space=pltpu.SMEM),
                  pl.BlockSpec((tm * TOP_K,), lambda i: (t0 + jnp.minimum(i + 1, n_tiles - 1),),
                               memory_space=pltpu.SMEM),
                  pl.BlockSpec((tm, dh), lambda i: (i, 0)),
                  pl.BlockSpec((tm, d), lambda i: (i, 0)),
                  pl.BlockSpec((tm, TOP_K), lambda i: (i, 0)),
                  pl.BlockSpec((1, N_MOD, d), lambda i: (group_of_tile(i, tm), 0, 0)),
                  pl.BlockSpec((d, ds_), lambda i: (0, 0)),
                  pl.BlockSpec((d, ds_), lambda i: (0, 0)),
                  pl.BlockSpec((ds_, d), lambda i: (0, 0)),
                  pl.BlockSpec((1, d), lambda i: (0, 0)),
                  pl.BlockSpec(memory_space=pl.ANY)],
        out_specs=pl.BlockSpec((tm, d), lambda i: (i, 0)),
        out_shape=jax.ShapeDtypeStruct((n_rows, d), F32),
        scratch_shapes=[pltpu.VMEM((2, TOP_K * tm, dh), jnp.int32), pltpu.SemaphoreType.DMA((2,))],
        compiler_params=_params(1),
        name="finish",
    )(dest_flat, dest_flat, h2p, x1, gw, mod, sg_bf, su_bf, sd_bf, final_g, yb)


def _grid_pos_emb(rows, d):
    quarter = d // 4
    omega = 1.0 / (10000.0 ** (jnp.arange(quarter, dtype=F32) / quarter))
    r = jnp.repeat(jnp.arange(rows, dtype=F32), GRID_W)[:, None] * omega
    cc = jnp.tile(jnp.arange(GRID_W, dtype=F32), rows)[:, None] * omega
    return jnp.concatenate([jnp.sin(r), jnp.cos(r), jnp.sin(cc), jnp.cos(cc)], axis=-1)


def _mixer_path(x2d, pos, n_b, seq_len, h0, mod, group_of_tile, p, mats, filt, n_seq):
    d = x2d.shape[1]
    d_rg = p['gn_rg'].shape[1]
    c = p['gn_hy'].shape[1]
    u, x_in = _in_proj(x2d, pos, mod, p['norm1_g'], p['w_in'], group_of_tile)
    u3 = u.reshape(n_b, seq_len, u.shape[1])
    y_rg, st = _rglru(u3, h0, p['rg_conv_w'], p['rg_conv_b'], p['rg_w4'], p['rg_b4'], p['rg_lam'], n_seq)
    hvc = _hyena_short_conv(u3, p['hy_conv_w'], p['hy_conv_b'], 2 * d_rg)
    tm = min(seq_len, 512)
    tc = 512 if seq_len > 512 else c
    z1 = _hyena_order(0, hvc, 0, hvc, 1, mats, filt, p['hy_bias'], tm, tc)
    y_hy = _hyena_order(1, z1, 0, hvc, 2, mats, filt, p['hy_bias'], tm, tc)
    x1, h2, idx, gw = _post_mixer(y_rg.reshape(-1, d_rg), y_hy.reshape(-1, c), x_in, mod,
                                  p['gn_rg'], p['gn_hy'], p['w_out'], p['norm2_g'],
                                  p['router_w'], p['router_b'], group_of_tile)
    return x1, h2, idx, gw, st


def kernel(x_prompt, x_sample, state_rglru, c, c_ctx, ada_w, ada_b, norm1_g, norm2_g, w_in, rg_conv_w, rg_conv_b, rg_wa, rg_ba, rg_wx, rg_bx, rg_lam, hy_conv_w, hy_conv_b, hy_w1, hy_b1, hy_freq, hy_w2, hy_b2, hy_w3, hy_decay, hy_bias, gn_rg, gn_hy, w_out, router_w, router_b, exp_w_gate, exp_w_up, exp_w_down, sh_w_gate, sh_w_up, sh_w_down, final_g):
    n_cb, seq_c, d = x_prompt.shape
    n_lb, seq_l, _ = x_sample.shape
    depth = ada_w.shape[0]
    d_rg = gn_rg.shape[1]
    hd = d_rg // RG_HEADS
    t_c = n_cb * seq_c
    t_l = n_lb * seq_l
    assert n_lb + 1 <= SUBLANES

    pos = _grid_pos_emb(seq_l // GRID_W, d)
    cvec = jnp.zeros((SUBLANES, d), F32).at[0].set(c_ctx).at[1:1 + n_lb].set(c)
    mats_c = _dft_matrices(seq_c)
    mats_l = _dft_matrices(seq_l)

    def group_ctx(i, tm):
        return 0

    def group_lat(i, tm):
        return 1 + (i * tm) // seq_l

    xc = x_prompt.reshape(t_c, d)
    xs = x_sample.reshape(t_l, d)
    ctx_states = []
    for l in range(depth):
        last = l == depth - 1
        mod = _modulation(cvec, ada_w[l], ada_b[l][None]).reshape(SUBLANES, N_MOD, d)

        w4 = jnp.concatenate([rg_wa[l, 0], rg_wx[l, 0], rg_wa[l, 1], rg_wx[l, 1]], axis=-1).astype(BF16)
        b4 = jnp.concatenate([rg_ba[l, 0].reshape(RG_HEADS, 1, hd), rg_bx[l, 0].reshape(RG_HEADS, 1, hd),
                              rg_ba[l, 1].reshape(RG_HEADS, 1, hd), rg_bx[l, 1].reshape(RG_HEADS, 1, hd)], axis=-1)
        p = {
            'norm1_g': norm1_g[l][None], 'norm2_g': norm2_g[l][None], 'w_in': w_in[l].astype(BF16),
            'rg_conv_w': rg_conv_w[l], 'rg_conv_b': rg_conv_b[l][None], 'rg_w4': w4, 'rg_b4': b4,
            'rg_lam': rg_lam[l], 'hy_conv_w': hy_conv_w[l], 'hy_conv_b': hy_conv_b[l][None],
            'hy_bias': hy_bias[l], 'gn_rg': gn_rg[l][None], 'gn_hy': gn_hy[l][None],
            'w_out': w_out[l].astype(BF16), 'router_w': router_w[l].astype(BF16), 'router_b': router_b[l][None],
        }
        filt_args = (hy_w1[l], hy_b1[l][None], hy_freq[l][None], hy_w2[l], hy_b2[l][None], hy_w3[l],
                     hy_decay[l].reshape(1, -1))
        filt_c = _hyena_filters(seq_c, mats_c[0], mats_c[1], *filt_args)
        filt_l = _hyena_filters(seq_l, mats_l[0], mats_l[1], *filt_args)

        h0_c = jnp.zeros((n_cb, 2, d_rg), F32)
        x1_c, h2_c, idx_c, gw_c, st_c = _mixer_path(xc, None, n_cb, seq_c, h0_c, mod, group_ctx, p,
                                                    mats_c, filt_c, n_seq=min(8, n_cb))
        ctx_states.append(st_c)
        x1_l, h2_l, idx_l, gw_l, _ = _mixer_path(xs, pos if l == 0 else None, n_lb, seq_l,
                                                 state_rglru[:, l], mod, group_lat, p,
                                                 mats_l, filt_l, n_seq=1)

        idx_all = jnp.concatenate([idx_c, idx_l], axis=0)
        dest, blk_e, blk_valid, n_used = _route(idx_all, router_w.shape[-1], EXPERT_ROWS)
        dest_flat = dest.reshape(-1)
        xb = _dispatch(dest_flat, h2_c, h2_l, blk_e.shape[0] * EXPERT_ROWS)
        yb = _experts(blk_e, blk_valid, n_used, xb, exp_w_gate[l], exp_w_up[l], exp_w_down[l])
        sh = (sh_w_gate[l].astype(BF16), sh_w_up[l].astype(BF16), sh_w_down[l].astype(BF16))
        xc = _finish(dest_flat, h2_c, x1_c, gw_c, yb, mod, *sh, final_g[None], 0, group_ctx, last)
        xs = _finish(dest_flat, h2_l, x1_l, gw_l, yb, mod, *sh, final_g[None], t_c, group_lat, last)

    new_state = jnp.stack(ctx_states, axis=1).astype(x_prompt.dtype)
    return (xc.reshape(n_cb, seq_c, d), xs.reshape(n_lb, seq_l, d), new_state)
```

```python
import functools
import math

import jax
import jax.numpy as jnp
from jax import lax
from jax.experimental import pallas as pl
from jax.experimental.pallas import tpu as pltpu

F32 = jnp.float32
BF16 = jnp.bfloat16

GRID_W = 64
RG_HEADS = 8
RG_CONV_W = 4
RG_C = 8.0
HY_CONV_W = 3
HY_BANDS = 16
TOP_K = 8
ROUTED_SCALE = 2.5
N_MOD = 6
EPS = 1e-6

LANES = 128
SUBLANES = 8
VMEM_LIMIT_BYTES = 56 * 1024 * 1024

EXPERT_ROWS = 288


def _params(n_axes, vmem=VMEM_LIMIT_BYTES):
    return pltpu.CompilerParams(dimension_semantics=("arbitrary",) * n_axes, vmem_limit_bytes=vmem)


def _rms(x, g):
    return x * lax.rsqrt(jnp.mean(x * x, axis=-1, keepdims=True) + EPS) * g


def _sigmoid(x):
    return 0.5 * jnp.tanh(0.5 * x) + 0.5


HI_HALF = -65536


def _pack_pair(lo, hi):
    lo_b = lax.bitcast_convert_type(lo.astype(BF16).astype(F32), jnp.int32)
    hi_b = lax.bitcast_convert_type(hi.astype(BF16).astype(F32), jnp.int32)
    return hi_b | lax.shift_right_logical(lo_b, 16)


def _unpack_pair(p):
    lo = lax.bitcast_convert_type(lax.shift_left(p, 16), F32)
    hi = lax.bitcast_convert_type(p & HI_HALF, F32)
    return lo, hi


def _store_token_tiles(ref, packed):
    m = packed.shape[0]
    for j in range(SUBLANES):
        ref[pl.ds(j, m, stride=SUBLANES), :] = packed[:, j * LANES:(j + 1) * LANES]


def _load_token_tiles(ref, row0, m):
    return jnp.concatenate(
        [ref[pl.ds(row0 * SUBLANES + j, m, stride=SUBLANES), :] for j in range(SUBLANES)], axis=1)


def _mod_kernel(c_ref, w_ref, b_ref, o_ref):
    c = c_ref[...]
    s = (c * jax.nn.sigmoid(c)).astype(BF16)
    o_ref[...] = jnp.dot(s, w_ref[...].astype(BF16), preferred_element_type=F32) + b_ref[...]


def _modulation(cvec, ada_w, ada_b):
    d, n = ada_w.shape
    tn = 1536
    return pl.pallas_call(
        _mod_kernel,
        grid=(n // tn,),
        in_specs=[pl.BlockSpec((SUBLANES, d), lambda j: (0, 0)),
                  pl.BlockSpec((d, tn), lambda j: (0, j)),
                  pl.BlockSpec((1, tn), lambda j: (0, j))],
        out_specs=pl.BlockSpec((SUBLANES, tn), lambda j: (0, j)),
        out_shape=jax.ShapeDtypeStruct((SUBLANES, n), F32),
        compiler_params=_params(1),
        name="modulation",
    )(cvec, ada_w, ada_b)


def _inproj_kernel(*refs, has_pos, tn):
    if has_pos:
        x_ref, pos_ref, mod_ref, g_ref, w_hbm, u_ref, w_vmem, h_scr, sem = refs
    else:
        x_ref, mod_ref, g_ref, w_hbm, u_ref, w_vmem, h_scr, sem = refs
    i = pl.program_id(0)
    j = pl.program_id(1)

    @pl.when(jnp.logical_and(i == 0, j == 0))
    def _():
        cp = pltpu.make_async_copy(w_hbm, w_vmem, sem)
        cp.start()
        cp.wait()

    @pl.when(j == 0)
    def _():
        x = x_ref[...]
        if has_pos:
            x = x + pos_ref[...]
        h = _rms(x, g_ref[...]) * (1.0 + mod_ref[0, 1:2, :]) + mod_ref[0, 0:1, :]
        h_scr[...] = h.astype(BF16)

    w = w_vmem[:, pl.ds(pl.multiple_of(j * tn, tn), tn)]
    u_ref[...] = jnp.dot(h_scr[...], w, preferred_element_type=F32)


def _in_proj(x2d, pos, mod, norm_g, w_bf, group_of_tile, tm=512, tn=1024):
    t, d = x2d.shape
    n = w_bf.shape[1]
    has_pos = pos is not None
    in_specs = [pl.BlockSpec((tm, d), lambda i, j: (i, 0))]
    args = [x2d]
    if has_pos:
        pos_tiles = pos.shape[0] // tm
        in_specs.append(pl.BlockSpec((tm, d), lambda i, j: (i % pos_tiles, 0)))
        args.append(pos)
    in_specs += [pl.BlockSpec((1, N_MOD, d), lambda i, j: (group_of_tile(i, tm), 0, 0)),
                 pl.BlockSpec((1, d), lambda i, j: (0, 0)),
                 pl.BlockSpec(memory_space=pl.ANY)]
    args += [mod, norm_g, w_bf]
    return pl.pallas_call(
        functools.partial(_inproj_kernel, has_pos=has_pos, tn=tn),
        grid=(t // tm, n // tn),
        in_specs=in_specs,
        out_specs=pl.BlockSpec((tm, tn), lambda i, j: (i, j)),
        out_shape=jax.ShapeDtypeStruct((t, n), F32),
        scratch_shapes=[pltpu.VMEM((d, n), BF16), pltpu.VMEM((tm, d), BF16), pltpu.SemaphoreType.DMA(())],
        compiler_params=_params(2),
        name="in_proj",
    )(*args)


def _shift_rows(win, off):
    if off == 0:
        return win
    n = win.shape[0]
    return pltpu.roll(win, (-off) % n, axis=0)


def _scan_chunk(a, b, reverse):
    n = a.shape[0]
    row = lax.broadcasted_iota(jnp.int32, a.shape, 0)
    dist = 1
    while dist < n:
        if reverse:
            a_s = pltpu.roll(a, n - dist, axis=0)
            b_s = pltpu.roll(b, n - dist, axis=0)
            m = row < n - dist
        else:
            a_s = pltpu.roll(a, dist, axis=0)
            b_s = pltpu.roll(b, dist, axis=0)
            m = row >= dist
        b = jnp.where(m, a * b_s + b, b)
        a = jnp.where(m, a * a_s, a)
        dist *= 2
    return a, b


def _rglru_kernel(xr_ref, gr_ref, cw_ref, cb_ref, w4_ref, b4_ref, lam_ref, h0_ref,
                  y_ref, st_ref, af, bf, ab, bb, hf, *, seq_len, n_seq, t1, tc):
    hd = xr_ref.shape[-1]
    nlam = -lam_ref[...]
    sp = jnp.maximum(nlam, 0.0) + jnp.log1p(jnp.exp(-jnp.abs(nlam)))
    cw = cw_ref[...]
    cb = cb_ref[...]
    b4 = b4_ref[0]
    nc1 = seq_len // t1
    ncs = seq_len // tc

    def per_seq(s, carry0):
        def gates(c, carry):
            r0 = pl.multiple_of(c * t1, t1)
            cur = xr_ref[s, pl.ds(r0, t1), :]
            p0 = pl.multiple_of(jnp.maximum(r0 - SUBLANES, 0), SUBLANES)
            n0 = pl.multiple_of(jnp.minimum(r0 + t1, seq_len - SUBLANES), SUBLANES)
            prev = jnp.where(c > 0, xr_ref[s, pl.ds(p0, SUBLANES), :], 0.0)
            nxt = jnp.where(c < nc1 - 1, xr_ref[s, pl.ds(n0, SUBLANES), :], 0.0)
            win = jnp.concatenate([prev, cur, nxt], axis=0)
            xr = cb
            for k in range(RG_CONV_W):
                xr = xr + cw[k:k + 1, :] * _shift_rows(win, k - RG_CONV_W // 2)[SUBLANES:SUBLANES + t1]
            z = jnp.dot(xr.astype(BF16), w4_ref[0], preferred_element_type=F32) + b4
            for d_i, (a_scr, b_scr) in enumerate(((af, bf), (ab, bb))):
                r = _sigmoid(z[:, (2 * d_i) * hd:(2 * d_i + 1) * hd])
                gi = _sigmoid(z[:, (2 * d_i + 1) * hd:(2 * d_i + 2) * hd])
                log_a = (-RG_C) * r * sp[d_i:d_i + 1, :]
                a = jnp.exp(log_a)
                a_scr[pl.ds(r0, t1), :] = a
                b_scr[pl.ds(r0, t1), :] = jnp.sqrt(-jnp.tanh(log_a) * (a * a + 1.0)) * (gi * xr)
            return carry

        lax.fori_loop(0, nc1, gates, 0)
        h0 = h0_ref[s]

        def fwd(c, carry):
            r0 = pl.multiple_of(c * tc, tc)
            a, h = _scan_chunk(af[pl.ds(r0, tc), :], bf[pl.ds(r0, tc), :], False)
            h = a * carry + h
            hf[pl.ds(r0, tc), :] = h
            return h[tc - 1:tc, :]

        s_f = lax.fori_loop(0, ncs, fwd, h0[0:1, :])

        def bwd(cc, carry):
            r0 = pl.multiple_of((ncs - 1 - cc) * tc, tc)
            a, h = _scan_chunk(ab[pl.ds(r0, tc), :], bb[pl.ds(r0, tc), :], True)
            h = a * carry + h
            g = gr_ref[s, pl.ds(r0, tc), :]
            y_ref[s, pl.ds(r0, tc), :] = jax.nn.gelu(g) * (hf[pl.ds(r0, tc), :] + h)
            return h[0:1, :]

        s_b = lax.fori_loop(0, ncs, bwd, h0[1:2, :])
        st_ref[s] = jnp.concatenate([s_f, s_b], axis=0)
        return carry0

    lax.fori_loop(0, n_seq, per_seq, 0)


def _rglru(u3, h0, conv_w, conv_b, w4, b4, lam, n_seq):
    b, seq_len, _ = u3.shape
    hd = w4.shape[1]
    d_rg = hd * RG_HEADS
    t1 = min(seq_len, 256)
    tc = 64
    kern = functools.partial(_rglru_kernel, seq_len=seq_len, n_seq=n_seq, t1=t1, tc=tc)
    return pl.pallas_call(
        kern,
        grid=(b // n_seq, RG_HEADS),
        in_specs=[pl.BlockSpec((n_seq, seq_len, hd), lambda i, h: (i, 0, h)),
                  pl.BlockSpec((n_seq, seq_len, hd), lambda i, h: (i, 0, RG_HEADS + h)),
                  pl.BlockSpec((RG_CONV_W, hd), lambda i, h: (0, h)),
                  pl.BlockSpec((1, hd), lambda i, h: (0, h)),
                  pl.BlockSpec((1, hd, 4 * hd), lambda i, h: (h, 0, 0)),
                  pl.BlockSpec((1, 1, 4 * hd), lambda i, h: (h, 0, 0)),
                  pl.BlockSpec((2, hd), lambda i, h: (0, h)),
                  pl.BlockSpec((n_seq, 2, hd), lambda i, h: (i, 0, h))],
        out_specs=[pl.BlockSpec((n_seq, seq_len, hd), lambda i, h: (i, 0, h)),
                   pl.BlockSpec((n_seq, 2, hd), lambda i, h: (i, 0, h))],
        out_shape=[jax.ShapeDtypeStruct((b, seq_len, d_rg), F32),
                   jax.ShapeDtypeStruct((b, 2, d_rg), F32)],
        scratch_shapes=[pltpu.VMEM((seq_len, hd), F32)] * 5,
        compiler_params=_params(2),
        name="rglru",
    )(u3, u3, conv_w, conv_b, w4, b4, lam, h0)


def _filt_time_kernel(w1_ref, b1_ref, fr_ref, w2_ref, b2_ref, w3_ref, dec_ref,
                      g_ref, d_ref, st_ref, *, seq_len, tl):
    i = pl.program_id(0)
    hi = lax.Precision.HIGHEST
    posi = i * tl + lax.broadcasted_iota(jnp.int32, (tl, LANES), 0)
    pos = posi.astype(F32)
    lane = lax.broadcasted_iota(jnp.int32, (tl, LANES), 1)
    band = jnp.where(lane <= HY_BANDS, lane, lane - HY_BANDS).astype(F32)
    ang = (2.0 * math.pi) * pos / seq_len * band
    t = pos / seq_len
    feats = jnp.where(lane == 0, t,
                      jnp.where(lane <= HY_BANDS, jnp.cos(ang),
                                jnp.where(lane <= 2 * HY_BANDS, -jnp.sin(ang), 0.0)))
    fr = fr_ref[...]
    hid = jnp.sin(fr * (jnp.dot(feats, w1_ref[...], precision=hi, preferred_element_type=F32) + b1_ref[...]))
    hid = jnp.sin(fr * (jnp.dot(hid, w2_ref[...], precision=hi, preferred_element_type=F32) + b2_ref[...]))
    k = jnp.dot(hid, w3_ref[...], precision=hi, preferred_element_type=F32)
    k = k * jnp.exp(-t[:, 0:1] * jnp.abs(dec_ref[...]))
    c = k.shape[1] // 4
    first = posi[:, 0:1] == 0
    sign = jnp.where(posi[:, 0:1] % 2 == 0, 1.0, -1.0)

    @pl.when(i == 0)
    def _():
        st_ref[...] = jnp.zeros_like(st_ref)

    for o in range(2):
        kf = k[:, (2 * o) * c:(2 * o + 1) * c]
        kb = jnp.where(first, 0.0, k[:, (2 * o + 1) * c:(2 * o + 2) * c])
        g = kf + kb
        g_ref[:, o * c:(o + 1) * c] = g.astype(BF16)
        d_ref[:, o * c:(o + 1) * c] = (kf - kb).astype(BF16)
        st_ref[0:1, o * c:(o + 1) * c] += jnp.sum(kf * kf + kb * kb, axis=0, keepdims=True)
        st_ref[1:2, o * c:(o + 1) * c] += jnp.sum(sign * g, axis=0, keepdims=True)


def _filt_dft_kernel(c_ref, s_ref, g_ref, d_ref, st_ref, kre_ref, q_ref):
    scale = lax.rsqrt(st_ref[0:1, :] + EPS)
    kre_ref[...] = jnp.dot(c_ref[...], g_ref[...], preferred_element_type=F32) * scale
    q_ref[...] = jnp.dot(s_ref[...], d_ref[...], preferred_element_type=F32) * scale


def _hyena_filters(seq_len, cmat, smat, w1, b1, freq, w2, b2, w3, decay):
    n_hid = w1.shape[1]
    n_out = w3.shape[1]
    c2 = n_out // 2
    tl = min(seq_len, 512)
    w1p = jnp.zeros((LANES, n_hid), F32).at[:w1.shape[0]].set(w1)
    g, d, stats = pl.pallas_call(
        functools.partial(_filt_time_kernel, seq_len=seq_len, tl=tl),
        grid=(seq_len // tl,),
        in_specs=[pl.BlockSpec((LANES, n_hid), lambda i: (0, 0)),
                  pl.BlockSpec((1, n_hid), lambda i: (0, 0)),
                  pl.BlockSpec((1, n_hid), lambda i: (0, 0)),
                  pl.BlockSpec((n_hid, n_hid), lambda i: (0, 0)),
                  pl.BlockSpec((1, n_hid), lambda i: (0, 0)),
                  pl.BlockSpec((n_hid, n_out), lambda i: (0, 0)),
                  pl.BlockSpec((1, n_out), lambda i: (0, 0))],
        out_specs=[pl.BlockSpec((tl, c2), lambda i: (i, 0)),
                   pl.BlockSpec((tl, c2), lambda i: (i, 0)),
                   pl.BlockSpec((SUBLANES, c2), lambda i: (0, 0))],
        out_shape=[jax.ShapeDtypeStruct((seq_len, c2), BF16),
                   jax.ShapeDtypeStruct((seq_len, c2), BF16),
                   jax.ShapeDtypeStruct((SUBLANES, c2), F32)],
        compiler_params=_params(1),
        name="hyena_filter_taps",
    )(w1p, b1, freq, w2, b2, w3, decay)
    tm = min(seq_len, 512)
    tn = 512
    kre, q = pl.pallas_call(
        _filt_dft_kernel,
        grid=(c2 // tn, seq_len // tm),
        in_specs=[pl.BlockSpec((tm, seq_len), lambda n, m: (m, 0)),
                  pl.BlockSpec((tm, seq_len), lambda n, m: (m, 0)),
                  pl.BlockSpec((seq_len, tn), lambda n, m: (0, n)),
                  pl.BlockSpec((seq_len, tn), lambda n, m: (0, n)),
                  pl.BlockSpec((SUBLANES, tn), lambda n, m: (0, n))],
        out_specs=[pl.BlockSpec((tm, tn), lambda n, m: (m, n)),
                   pl.BlockSpec((tm, tn), lambda n, m: (m, n))],
        out_shape=[jax.ShapeDtypeStruct((seq_len, c2), F32)] * 2,
        compiler_params=_params(2),
        name="hyena_filter_dft",
    )(cmat, smat, g, d, stats)
    return kre, q, stats


def _dft_matrices(seq_len):
    assert seq_len & (seq_len - 1) == 0
    tm = min(seq_len, 256)
    out = jax.ShapeDtypeStruct((seq_len, seq_len), BF16)
    return pl.pallas_call(
        functools.partial(_dft_table_kernel, seq_len=seq_len),
        grid=(seq_len // tm,),
        in_specs=[],
        out_specs=[pl.BlockSpec((tm, seq_len), lambda i: (i, 0))] * 3,
        out_shape=[out] * 3,
        scratch_shapes=[pltpu.VMEM((tm, seq_len), F32)] * 2,
        compiler_params=_params(1),
        name="dft_tables",
    )()


def _dft_table_kernel(c_ref, s_ref, st_ref, c0, s0, *, seq_len):
    i = pl.program_id(0)
    tm = c_ref.shape[0]
    wrap = 2 * seq_len - 1
    row = lax.broadcasted_iota(jnp.int32, (tm, seq_len), 0)
    col = lax.broadcasted_iota(jnp.int32, (tm, seq_len), 1)

    @pl.when(i == 0)
    def _():
        ang = ((row * col) & wrap).astype(F32) * (math.pi / seq_len)
        c0[...] = jnp.cos(ang)
        s0[...] = jnp.sin(ang)

    k0 = i * tm
    ang0 = ((k0 * col[0:1, :]) & wrap).astype(F32) * (math.pi / seq_len)
    cn = jnp.cos(ang0)
    sn = jnp.sin(ang0)
    cmat = c0[...] * cn - s0[...] * sn
    smat = s0[...] * cn + c0[...] * sn
    alt_col = jnp.where((col & 1) == 0, 1.0, -1.0)
    alt_row = jnp.where(((row + k0) & 1) == 0, 1.0, -1.0)
    c_ref[...] = cmat.astype(BF16)
    s_ref[...] = jnp.where(row + k0 == 0, alt_col, smat).astype(BF16)
    st_ref[...] = jnp.where(col == 0, alt_row, smat).astype(BF16)


def _conv3_kernel(cur_ref, prev_ref, nxt_ref, w_ref, b_ref, o_ref, *, n_row_tiles):
    r = pl.program_id(1)
    cur = cur_ref[0]
    tr = cur.shape[0]
    prev = jnp.where(r > 0, prev_ref[0], 0.0)
    nxt = jnp.where(r < n_row_tiles - 1, nxt_ref[0], 0.0)
    win = jnp.concatenate([prev, cur, nxt], axis=0)
    w = w_ref[...]
    acc = b_ref[...]
    for k in range(HY_CONV_W):
        acc = acc + w[k:k + 1, :] * _shift_rows(win, k - HY_CONV_W // 2)[SUBLANES:SUBLANES + tr]
    o_ref[0] = acc.astype(BF16)


def _hyena_short_conv(u3, conv_w, conv_b, col0):
    b, seq_len, _ = u3.shape
    n_cols = conv_w.shape[1]
    tcol = 1024
    tr = min(seq_len, 512)
    nrt = seq_len // tr
    cb0 = col0 // tcol
    rb = tr // SUBLANES
    last = seq_len // SUBLANES - 1
    return pl.pallas_call(
        functools.partial(_conv3_kernel, n_row_tiles=nrt),
        grid=(b, nrt, n_cols // tcol),
        in_specs=[pl.BlockSpec((1, tr, tcol), lambda i, r, j: (i, r, cb0 + j)),
                  pl.BlockSpec((1, SUBLANES, tcol), lambda i, r, j: (i, jnp.maximum(r * rb - 1, 0), cb0 + j)),
                  pl.BlockSpec((1, SUBLANES, tcol), lambda i, r, j: (i, jnp.minimum((r + 1) * rb, last), cb0 + j)),
                  pl.BlockSpec((HY_CONV_W, tcol), lambda i, r, j: (0, j)),
                  pl.BlockSpec((1, tcol), lambda i, r, j: (0, j))],
        out_specs=pl.BlockSpec((1, tr, tcol), lambda i, r, j: (i, r, j)),
        out_shape=jax.ShapeDtypeStruct((b, seq_len, n_cols), BF16),
        compiler_params=_params(3),
        name="hyena_short_conv",
    )(u3, u3, u3, conv_w, conv_b)


def _hy_fwd_kernel(c_ref, s_ref, u_ref, kre_ref, q_ref, st_ref, pre_ref, pm_ref, *, seq_len):
    m = pl.program_id(2)
    u = u_ref[0]
    a = jnp.dot(c_ref[...], u, preferred_element_type=F32)
    bv = jnp.dot(s_ref[...], u, preferred_element_type=F32)
    tm = a.shape[0]
    is0 = (m * tm + lax.broadcasted_iota(jnp.int32, a.shape, 0)) == 0
    k_nyq = st_ref[1:2, :] * lax.rsqrt(st_ref[0:1, :] + EPS)
    kre = kre_ref[...]
    q = jnp.where(is0, 0.0, q_ref[...])
    kre_b = jnp.where(is0, k_nyq, kre)
    wk = jnp.where(is0, 0.5 / seq_len, 1.0 / seq_len)
    pre_ref[0] = ((a * kre - bv * q) * wk).astype(BF16)
    pm_ref[0] = ((a * q + bv * kre_b) * wk).astype(BF16)


def _hy_inv_kernel(c_ref, st_ref, pre_ref, pm_ref, u_ref, x_ref, bias_ref, z_ref):
    y = jnp.dot(c_ref[...], pre_ref[0], preferred_element_type=F32)
    y = y + jnp.dot(st_ref[...], pm_ref[0], preferred_element_type=F32)
    y = y + u_ref[0].astype(F32) * bias_ref[...]
    z_ref[0] = (y * x_ref[0].astype(F32)).astype(BF16)


def _hyena_order(order, u_arr, u_cb, hvc, gate_cb, mats, filt, bias, tm, tc):
    cmat, smat, smat_t = mats
    kre, q, stats = filt
    b, seq_len, _ = hvc.shape
    c = kre.shape[1] // 2
    nct = c // tc
    grid = (b, nct, seq_len // tm)
    pre, pm = pl.pallas_call(
        functools.partial(_hy_fwd_kernel, seq_len=seq_len),
        grid=grid,
        in_specs=[pl.BlockSpec((tm, seq_len), lambda i, n, m: (m, 0)),
                  pl.BlockSpec((tm, seq_len), lambda i, n, m: (m, 0)),
                  pl.BlockSpec((1, seq_len, tc), lambda i, n, m: (i, 0, u_cb * nct + n)),
                  pl.BlockSpec((tm, tc), lambda i, n, m: (m, order * nct + n)),
                  pl.BlockSpec((tm, tc), lambda i, n, m: (m, order * nct + n)),
                  pl.BlockSpec((SUBLANES, tc), lambda i, n, m: (0, order * nct + n))],
        out_specs=[pl.BlockSpec((1, tm, tc), lambda i, n, m: (i, m, n))] * 2,
        out_shape=[jax.ShapeDtypeStruct((b, seq_len, c), BF16)] * 2,
        compiler_params=_params(3),
        name="hyena_fwd_dft",
    )(cmat, smat, u_arr, kre, q, stats)
    return pl.pallas_call(
        _hy_inv_kernel,
        grid=grid,
        in_specs=[pl.BlockSpec((tm, seq_len), lambda i, n, m: (m, 0)),
                  pl.BlockSpec((tm, seq_len), lambda i, n, m: (m, 0)),
                  pl.BlockSpec((1, seq_len, tc), lambda i, n, m: (i, 0, n)),
                  pl.BlockSpec((1, seq_len, tc), lambda i, n, m: (i, 0, n)),
                  pl.BlockSpec((1, tm, tc), lambda i, n, m: (i, m, u_cb * nct + n)),
                  pl.BlockSpec((1, tm, tc), lambda i, n, m: (i, m, gate_cb * nct + n)),
                  pl.BlockSpec((1, tc), lambda i, n, m: (0, n))],
        out_specs=pl.BlockSpec((1, tm, tc), lambda i, n, m: (i, m, n)),
        out_shape=jax.ShapeDtypeStruct((b, seq_len, c), BF16),
        compiler_params=_params(3),
        name="hyena_inv_dft",
    )(cmat, smat_t, pre, pm, u_arr, hvc, bias[order][None])


def _post_mixer_kernel(*refs, has_pos):
    if has_pos:
        (yrg_ref, yhy_ref, x_ref, pos_ref, mod_ref, gnr_ref, gnh_ref, wo_ref, n2_ref,
         rw_ref, rb_ref, x1_ref, h2_ref, idx_ref, gw_ref) = refs
    else:
        (yrg_ref, yhy_ref, x_ref, mod_ref, gnr_ref, gnh_ref, wo_ref, n2_ref,
         rw_ref, rb_ref, x1_ref, h2_ref, idx_ref, gw_ref) = refs
    d_rg = yrg_ref.shape[1]
    na = _rms(yrg_ref[...], gnr_ref[...]).astype(BF16)
    nb = _rms(yhy_ref[...].astype(F32), gnh_ref[...]).astype(BF16)
    y = jnp.dot(na, wo_ref[0:d_rg, :], preferred_element_type=F32)
    y = y + jnp.dot(nb, wo_ref[d_rg:, :], preferred_element_type=F32)
    x = x_ref[...]
    if has_pos:
        x = x + pos_ref[...]
    x1 = x + mod_ref[0, 2:3, :] * y
    x1_ref[...] = x1
    h2f = _rms(x1, n2_ref[...]) * (1.0 + mod_ref[0, 4:5, :]) + mod_ref[0, 3:4, :]
    half = h2f.shape[1] // 2
    _store_token_tiles(h2_ref, _pack_pair(h2f[:, :half], h2f[:, half:]))
    h2 = h2f.astype(BF16)
    scores = jax.nn.sigmoid(jnp.dot(h2, rw_ref[...], preferred_element_type=F32))
    sel = scores + rb_ref[...]
    n_exp = scores.shape[1]
    lane = lax.broadcasted_iota(jnp.int32, scores.shape, 1)
    col = lax.broadcasted_iota(jnp.int32, idx_ref.shape, 1)
    idx_acc = jnp.zeros(idx_ref.shape, jnp.int32)
    gw_acc = jnp.zeros(gw_ref.shape, F32)
    for k in range(TOP_K):
        mx = jnp.max(sel, axis=1, keepdims=True)
        pick = jnp.min(jnp.where(sel == mx, lane, n_exp), axis=1, keepdims=True)
        hit = lane == pick
        val = jnp.sum(jnp.where(hit, scores, 0.0), axis=1, keepdims=True)
        sel = jnp.where(hit, -jnp.inf, sel)
        idx_acc = jnp.where(col == k, pick, idx_acc)
        gw_acc = jnp.where(col == k, val, gw_acc)
    idx_ref[...] = idx_acc
    gw_ref[...] = gw_acc / jnp.sum(gw_acc, axis=1, keepdims=True) * ROUTED_SCALE


def _post_mixer(y_rg, y_hy, x2d, pos, mod, gn_rg, gn_hy, w_out_bf, norm2_g, router_w_bf, router_b,
                group_of_tile, tm=256):
    t, d = x2d.shape
    d_rg = y_rg.shape[1]
    d_hy = y_hy.shape[1]
    n_exp = router_w_bf.shape[1]
    has_pos = pos is not None
    in_specs = [pl.BlockSpec((tm, d_rg), lambda i: (i, 0)),
                pl.BlockSpec((tm, d_hy), lambda i: (i, 0)),
                pl.BlockSpec((tm, d), lambda i: (i, 0))]
    args = [y_rg, y_hy, x2d]
    if has_pos:
        pos_tiles = pos.shape[0] // tm
        in_specs.append(pl.BlockSpec((tm, d), lambda i: (i % pos_tiles, 0)))
        args.append(pos)
    in_specs += [pl.BlockSpec((1, N_MOD, d), lambda i: (group_of_tile(i, tm), 0, 0)),
                 pl.BlockSpec((1, d_rg), lambda i: (0, 0)),
                 pl.BlockSpec((1, d_hy), lambda i: (0, 0)),
                 pl.BlockSpec((d_rg + d_hy, d), lambda i: (0, 0)),
                 pl.BlockSpec((1, d), lambda i: (0, 0)),
                 pl.BlockSpec((d, n_exp), lambda i: (0, 0)),
                 pl.BlockSpec((1, n_exp), lambda i: (0, 0))]
    args += [mod, gn_rg, gn_hy, w_out_bf, norm2_g, router_w_bf, router_b]
    return pl.pallas_call(
        functools.partial(_post_mixer_kernel, has_pos=has_pos),
        grid=(t // tm,),
        in_specs=in_specs,
        out_specs=[pl.BlockSpec((tm, d), lambda i: (i, 0)),
                   pl.BlockSpec((tm * SUBLANES, LANES), lambda i: (i, 0)),
                   pl.BlockSpec((tm, TOP_K), lambda i: (i, 0)),
                   pl.BlockSpec((tm, TOP_K), lambda i: (i, 0))],
        out_shape=[jax.ShapeDtypeStruct((t, d), F32),
                   jax.ShapeDtypeStruct((t * SUBLANES, LANES), jnp.int32),
                   jax.ShapeDtypeStruct((t, TOP_K), jnp.int32),
                   jax.ShapeDtypeStruct((t, TOP_K), F32)],
        compiler_params=_params(1),
        name="post_mixer",
    )(*args)


def _route_kernel(idx_ref, dest_ref, be_ref, bv_ref, nu_ref, tri, cnt, base, pst, *, blk, n_blk):
    p = pl.program_id(0)
    i = pl.program_id(1)
    tm, top_k = idx_ref.shape
    ne = cnt.shape[1]

    def div_blk(n):
        return jnp.floor((n + 0.5) / blk)

    @pl.when(jnp.logical_and(p == 0, i == 0))
    def _():
        r = lax.broadcasted_iota(jnp.int32, (tm, tm), 0)
        c = lax.broadcasted_iota(jnp.int32, (tm, tm), 1)
        tri[...] = jnp.where(r > c, 1.0, 0.0).astype(BF16)
        cnt[...] = jnp.zeros_like(cnt)

    idx = idx_ref[...]
    lane = lax.broadcasted_iota(jnp.int32, (tm, ne), 1)
    hits = [lane == idx[:, k:k + 1] for k in range(top_k)]
    occ = jnp.zeros((tm, ne), F32)
    for h in hits:
        occ = occ + jnp.where(h, 1.0, 0.0)
    col_sum = jnp.sum(occ, axis=0, keepdims=True)

    @pl.when(p == 0)
    def _():
        cnt[0:1, :] += col_sum

    @pl.when(jnp.logical_and(p == 1, i == 0))
    def _():
        counts = cnt[...]
        padded = div_blk(counts + (blk - 1.0)) * blk
        r = lax.broadcasted_iota(jnp.int32, (ne, ne), 0)
        c = lax.broadcasted_iota(jnp.int32, (ne, ne), 1)
        upper = jnp.where(r <= c, 1.0, 0.0)
        pend = jnp.dot(padded, upper, precision=lax.Precision.HIGHEST, preferred_element_type=F32)
        pstart = pend - padded
        pst[...] = pstart
        base[...] = jnp.zeros_like(base)
        b0 = (lax.broadcasted_iota(jnp.int32, (n_blk, ne), 0) * blk).astype(F32)
        be = jnp.sum(jnp.where(pend[0:1, :] <= b0, 1.0, 0.0), axis=1, keepdims=True)
        be = jnp.minimum(be, ne - 1.0)
        own = lax.broadcasted_iota(jnp.int32, (n_blk, ne), 1).astype(F32) == be
        pst_b = jnp.sum(jnp.where(own, pstart[0:1, :], 0.0), axis=1, keepdims=True)
        cnt_b = jnp.sum(jnp.where(own, counts[0:1, :], 0.0), axis=1, keepdims=True)
        valid = jnp.clip(cnt_b - (b0[:, 0:1] - pst_b), 0.0, float(blk))
        be_ref[...] = be.astype(jnp.int32)
        bv_ref[...] = valid.astype(jnp.int32)
        total = jnp.max(pend[0:1, :], axis=1, keepdims=True)
        nu_ref[...] = jnp.broadcast_to(div_blk(total).astype(jnp.int32), nu_ref.shape)

    @pl.when(p == 1)
    def _():
        cum = jnp.dot(tri[...], occ.astype(BF16), preferred_element_type=F32) + base[0:1, :] + pst[0:1, :]
        col = lax.broadcasted_iota(jnp.int32, (tm, top_k), 1)
        acc = jnp.zeros((tm, top_k), F32)
        for k in range(top_k):
            v = jnp.sum(jnp.where(hits[k], cum, 0.0), axis=1, keepdims=True)
            acc = jnp.where(col == k, v, acc)
        dest_ref[...] = acc.astype(jnp.int32)
        base[0:1, :] += col_sum


def _route(idx, n_exp, blk, tm=512):
    n_tok, top_k = idx.shape
    n_blk = -(-n_tok * top_k // blk) + n_exp
    dest, be, bv, nu = pl.pallas_call(
        functools.partial(_route_kernel, blk=blk, n_blk=n_blk),
        grid=(2, n_tok // tm),
        in_specs=[pl.BlockSpec((tm, top_k), lambda p, i: (i, 0))],
        out_specs=[pl.BlockSpec((tm, top_k), lambda p, i: (i * p, 0)),
                   pl.BlockSpec((n_blk, 1), lambda p, i: (0, 0)),
                   pl.BlockSpec((n_blk, 1), lambda p, i: (0, 0)),
                   pl.BlockSpec((SUBLANES, LANES), lambda p, i: (0, 0))],
        out_shape=[jax.ShapeDtypeStruct((n_tok, top_k), jnp.int32),
                   jax.ShapeDtypeStruct((n_blk, 1), jnp.int32),
                   jax.ShapeDtypeStruct((n_blk, 1), jnp.int32),
                   jax.ShapeDtypeStruct((SUBLANES, LANES), jnp.int32)],
        scratch_shapes=[pltpu.VMEM((tm, tm), BF16), pltpu.VMEM((SUBLANES, n_exp), F32),
                        pltpu.VMEM((SUBLANES, n_exp), F32), pltpu.VMEM((SUBLANES, n_exp), F32)],
        compiler_params=_params(2),
        name="route",
    )(idx)
    return dest, be.reshape(n_blk), bv.reshape(n_blk), nu[0, 0:1]


def _dispatch_kernel(dest_ref, xa_ref, xl_ref, out_ref, sem, *, n_a):
    tm = xa_ref.shape[0] // SUBLANES

    def tile(ref, r):
        return ref.at[pl.ds(pl.multiple_of(r * SUBLANES, SUBLANES), SUBLANES), :]

    def run(src):
        def issue(t, c):
            for k in range(TOP_K):
                pltpu.make_async_copy(tile(src, t), tile(out_ref, dest_ref[t * TOP_K + k]), sem).start(priority=k % 2)
            return c

        lax.fori_loop(0, tm, issue, 0)

        def drain(t, c):
            for k in range(TOP_K):
                pltpu.make_async_copy(tile(src, 0), tile(out_ref, 0), sem).wait()
            return c

        lax.fori_loop(0, tm, drain, 0, unroll=8)

    @pl.when(pl.program_id(0) < n_a)
    def _():
        run(xa_ref)

    @pl.when(pl.program_id(0) >= n_a)
    def _():
        run(xl_ref)


def _dispatch(dest_flat, h2p_a, h2p_l, n_rows, tm=256):
    n_a = h2p_a.shape[0] // (tm * SUBLANES)
    n_l = h2p_l.shape[0] // (tm * SUBLANES)
    return pl.pallas_call(
        functools.partial(_dispatch_kernel, n_a=n_a),
        grid=(n_a + n_l,),
        in_specs=[pl.BlockSpec((tm * TOP_K,), lambda i: (i,), memory_space=pltpu.SMEM),
                  pl.BlockSpec((tm * SUBLANES, LANES), lambda i: (jnp.minimum(i, n_a - 1), 0)),
                  pl.BlockSpec((tm * SUBLANES, LANES), lambda i: (jnp.maximum(i - n_a, 0), 0))],
        out_specs=pl.BlockSpec(memory_space=pl.ANY),
        out_shape=jax.ShapeDtypeStruct((n_rows * SUBLANES, LANES), jnp.int32),
        scratch_shapes=[pltpu.SemaphoreType.DMA(())],
        compiler_params=_params(1),
        name="dispatch",
    )(dest_flat, h2p_a, h2p_l)


def _expert_kernel(be_ref, bv_ref, nu_ref, x_ref, wg_hbm, wu_hbm, wd_hbm, o_ref,
                   wg_f, wu_f, wd_f, wg_b, wu_b, wd_b, sem, grp, *, n_blk):
    i = pl.program_id(0)
    n_used = nu_ref[0]
    e = be_ref[i]
    active = i < n_used
    changed = jnp.logical_or(i == 0, e != be_ref[jnp.maximum(i - 1, 0)])

    def expert_at(j):
        return be_ref[jnp.minimum(j, n_blk - 1)]

    def next_group(j):
        ej = expert_at(j)
        return lax.while_loop(lambda q: jnp.logical_and(q < n_used, expert_at(q) == ej), lambda q: q + 1, j + 1)

    def copies(ex, slot):
        return (pltpu.make_async_copy(wg_hbm.at[ex], wg_f.at[slot], sem.at[slot, 0]),
                pltpu.make_async_copy(wu_hbm.at[ex], wu_f.at[slot], sem.at[slot, 1]),
                pltpu.make_async_copy(wd_hbm.at[ex], wd_f.at[slot], sem.at[slot, 2]))

    @pl.when(jnp.logical_and(active, i == 0))
    def _():
        grp[0] = 0
        for cp in copies(e, 0):
            cp.start()
        n1 = next_group(i)

        @pl.when(n1 < n_used)
        def _():
            for cp in copies(expert_at(n1), 1):
                cp.start()

    @pl.when(jnp.logical_and(active, changed))
    def _():
        slot = grp[0] % 2
        for cp in copies(e, slot):
            cp.wait()
        wg_b[...] = wg_f[slot].astype(BF16)
        wu_b[...] = wu_f[slot].astype(BF16)
        wd_b[...] = wd_f[slot].astype(BF16)
        n2 = next_group(next_group(i))

        @pl.when(n2 < n_used)
        def _():
            for cp in copies(expert_at(n2), slot):
                cp.start()

        grp[0] = grp[0] + 1

    @pl.when(active)
    def _():
        tm = x_ref.shape[0] // SUBLANES
        dh = SUBLANES * LANES
        live = lax.broadcasted_iota(jnp.int32, (tm, dh), 0) < bv_ref[i]
        lo, hi = _unpack_pair(_load_token_tiles(x_ref, 0, tm))
        lo = jnp.where(live, lo, 0.0).astype(BF16)
        hi = jnp.where(live, hi, 0.0).astype(BF16)
        g = jnp.dot(lo, wg_b[0:dh, :], preferred_element_type=F32)
        g = g + jnp.dot(hi, wg_b[dh:, :], preferred_element_type=F32)
        u = jnp.dot(lo, wu_b[0:dh, :], preferred_element_type=F32)
        u = u + jnp.dot(hi, wu_b[dh:, :], preferred_element_type=F32)
        hmid = (g * _sigmoid(g) * u).astype(BF16)
        y = jnp.dot(hmid, wd_b[...], preferred_element_type=F32)
        _store_token_tiles(o_ref, _pack_pair(y[:, :dh], y[:, dh:]))

    @pl.when(jnp.logical_not(active))
    def _():
        o_ref[...] = jnp.zeros_like(o_ref)


def _experts(blk_e, blk_valid, n_used, xb, wg, wu, wd):
    n_exp, d, de = wg.shape
    tm = EXPERT_ROWS
    n_blk = xb.shape[0] // (tm * SUBLANES)
    grid_spec = pltpu.PrefetchScalarGridSpec(
        num_scalar_prefetch=3,
        grid=(n_blk,),
        in_specs=[pl.BlockSpec((tm * SUBLANES, LANES), lambda i, be, bv, nu: (i, 0)),
                  pl.BlockSpec(memory_space=pl.ANY),
                  pl.BlockSpec(memory_space=pl.ANY),
                  pl.BlockSpec(memory_space=pl.ANY)],
        out_specs=pl.BlockSpec((tm * SUBLANES, LANES), lambda i, be, bv, nu: (i, 0)),
        scratch_shapes=[pltpu.VMEM((2, d, de), F32), pltpu.VMEM((2, d, de), F32), pltpu.VMEM((2, de, d), F32),
                        pltpu.VMEM((d, de), BF16), pltpu.VMEM((d, de), BF16), pltpu.VMEM((de, d), BF16),
                        pltpu.SemaphoreType.DMA((2, 3)), pltpu.SMEM((1,), jnp.int32)],
    )
    return pl.pallas_call(
        functools.partial(_expert_kernel, n_blk=n_blk),
        grid_spec=grid_spec,
        out_shape=jax.ShapeDtypeStruct(xb.shape, jnp.int32),
        compiler_params=_params(1),
        name="routed_experts",
    )(blk_e, blk_valid, n_used, xb, wg, wu, wd)


def _finish_kernel(dcur_ref, dnxt_ref, h2_ref, x1_ref, gw_ref, mod_ref, sg_ref, su_ref, sd_ref, fg_ref,
                   yb_hbm, o_ref, gbuf, sem, *, final_norm):
    i = pl.program_id(0)
    n_tiles = pl.num_programs(0)
    tm = h2_ref.shape[0] // SUBLANES
    dh = SUBLANES * LANES
    slot = i % 2

    def row_copy(row, t, k, s):
        src = yb_hbm.at[pl.ds(pl.multiple_of(row * SUBLANES, SUBLANES), SUBLANES), :]
        dst = gbuf.at[s, pl.ds(pl.multiple_of((k * tm + t) * SUBLANES, SUBLANES), SUBLANES), :]
        return pltpu.make_async_copy(src, dst, sem.at[s])

    def gather(d_ref, s):
        def issue(t, c):
            for k in range(TOP_K):
                row_copy(d_ref[t * TOP_K + k], t, k, s).start(priority=k % 2)
            return c

        lax.fori_loop(0, tm, issue, 0)

    @pl.when(i == 0)
    def _():
        gather(dcur_ref, 0)

    @pl.when(i + 1 < n_tiles)
    def _():
        gather(dnxt_ref, 1 - slot)

    lo, hi = _unpack_pair(_load_token_tiles(h2_ref, 0, tm))
    lo = lo.astype(BF16)
    hi = hi.astype(BF16)
    g = jnp.dot(lo, sg_ref[0:dh, :], preferred_element_type=F32) + jnp.dot(hi, sg_ref[dh:, :], preferred_element_type=F32)
    u = jnp.dot(lo, su_ref[0:dh, :], preferred_element_type=F32) + jnp.dot(hi, su_ref[dh:, :], preferred_element_type=F32)
    hmid = (g * _sigmoid(g) * u).astype(BF16)
    shared = jnp.dot(hmid, sd_ref[...], preferred_element_type=F32)
    acc_lo = shared[:, :dh]
    acc_hi = shared[:, dh:]

    def drain(t, c):
        for k in range(TOP_K):
            row_copy(0, 0, 0, slot).wait()
        return c

    lax.fori_loop(0, tm, drain, 0, unroll=8)

    gw = gw_ref[...]
    for k in range(TOP_K):
        r_lo, r_hi = _unpack_pair(_load_token_tiles(gbuf.at[slot], k * tm, tm))
        acc_lo = acc_lo + gw[:, k:k + 1] * r_lo
        acc_hi = acc_hi + gw[:, k:k + 1] * r_hi
    x_lo = x1_ref[:, :dh] + mod_ref[0, 5:6, :dh] * acc_lo
    x_hi = x1_ref[:, dh:] + mod_ref[0, 5:6, dh:] * acc_hi
    if final_norm:
        ms = (jnp.sum(x_lo * x_lo, axis=-1, keepdims=True) + jnp.sum(x_hi * x_hi, axis=-1, keepdims=True)) / (2 * dh)
        inv = lax.rsqrt(ms + EPS)
        x_lo = x_lo * inv * fg_ref[:, :dh]
        x_hi = x_hi * inv * fg_ref[:, dh:]
    o_ref[:, :dh] = x_lo
    o_ref[:, dh:] = x_hi


def _finish(dest_flat, h2p, x1, gw, yb, mod, sg_bf, su_bf, sd_bf, final_g, row0, group_of_tile,
            final_norm, tm=256):
    n_rows, d = x1.shape
    ds_ = sg_bf.shape[1]
    t0 = row0 // tm
    n_tiles = n_rows // tm
    return pl.pallas_call(
        functools.partial(_finish_kernel, final_norm=final_norm),
        grid=(n_tiles,),
        in_specs=[pl.BlockSpec((tm * TOP_K,), lambda i: (t0 + i,), memory_space=pltpu.SMEM),
                  pl.BlockSpec((tm * TOP_K,), lambda i: (t0 + jnp.minimum(i + 1, n_tiles - 1),),
                               memory_space=pltpu.SMEM),
                  pl.BlockSpec((tm * SUBLANES, LANES), lambda i: (i, 0)),
                  pl.BlockSpec((tm, d), lambda i: (i, 0)),
                  pl.BlockSpec((tm, TOP_K), lambda i: (i, 0)),
                  pl.BlockSpec((1, N_MOD, d), lambda i: (group_of_tile(i, tm), 0, 0)),
                  pl.BlockSpec((d, ds_), lambda i: (0, 0)),
                  pl.BlockSpec((d, ds_), lambda i: (0, 0)),
                  pl.BlockSpec((ds_, d), lambda i: (0, 0)),
                  pl.BlockSpec((1, d), lambda i: (0, 0)),
                  pl.BlockSpec(memory_space=pl.ANY)],
        out_specs=pl.BlockSpec((tm, d), lambda i: (i, 0)),
        out_shape=jax.ShapeDtypeStruct((n_rows, d), F32),
        scratch_shapes=[pltpu.VMEM((2, TOP_K * tm * SUBLANES, LANES), jnp.int32), pltpu.SemaphoreType.DMA((2,))],
        compiler_params=_params(1),
        name="finish",
    )(dest_flat, dest_flat, h2p, x1, gw, mod, sg_bf, su_bf, sd_bf, final_g, yb)


def _grid_pos_emb(rows, d):
    quarter = d // 4
    omega = 1.0 / (10000.0 ** (jnp.arange(quarter, dtype=F32) / quarter))
    r = jnp.arange(rows, dtype=F32)[:, None] * omega
    cc = jnp.arange(GRID_W, dtype=F32)[:, None] * omega
    by_row = jnp.concatenate([jnp.sin(r), jnp.cos(r)], axis=-1)
    by_col = jnp.concatenate([jnp.sin(cc), jnp.cos(cc)], axis=-1)
    full = jnp.concatenate([jnp.broadcast_to(by_row[:, None, :], (rows, GRID_W, d // 2)),
                            jnp.broadcast_to(by_col[None, :, :], (rows, GRID_W, d // 2))], axis=-1)
    return full.reshape(rows * GRID_W, d)


def _mixer_path(x2d, pos, n_b, seq_len, h0, mod, group_of_tile, p, mats, filt, n_seq):
    d_rg = p['gn_rg'].shape[1]
    c = p['gn_hy'].shape[1]
    u = _in_proj(x2d, pos, mod, p['norm1_g'], p['w_in'], group_of_tile)
    u3 = u.reshape(n_b, seq_len, u.shape[1])
    y_rg, st = _rglru(u3, h0, p['rg_conv_w'], p['rg_conv_b'], p['rg_w4'], p['rg_b4'], p['rg_lam'], n_seq)
    hvc = _hyena_short_conv(u3, p['hy_conv_w'], p['hy_conv_b'], 2 * d_rg)
    tm = min(seq_len, 512)
    tc = 512 if seq_len > 512 else c
    z1 = _hyena_order(0, hvc, 0, hvc, 1, mats, filt, p['hy_bias'], tm, tc)
    y_hy = _hyena_order(1, z1, 0, hvc, 2, mats, filt, p['hy_bias'], tm, tc)
    x1, h2, idx, gw = _post_mixer(y_rg.reshape(-1, d_rg), y_hy.reshape(-1, c), x2d, pos, mod,
                                  p['gn_rg'], p['gn_hy'], p['w_out'], p['norm2_g'],
                                  p['router_w'], p['router_b'], group_of_tile)
    return x1, h2, idx, gw, st


def kernel(x_prompt, x_sample, state_rglru, c, c_ctx, ada_w, ada_b, norm1_g, norm2_g, w_in, rg_conv_w, rg_conv_b, rg_wa, rg_ba, rg_wx, rg_bx, rg_lam, hy_conv_w, hy_conv_b, hy_w1, hy_b1, hy_freq, hy_w2, hy_b2, hy_w3, hy_decay, hy_bias, gn_rg, gn_hy, w_out, router_w, router_b, exp_w_gate, exp_w_up, exp_w_down, sh_w_gate, sh_w_up, sh_w_down, final_g):
    n_cb, seq_c, d = x_prompt.shape
    n_lb, seq_l, _ = x_sample.shape
    depth = ada_w.shape[0]
    d_rg = gn_rg.shape[1]
    hd = d_rg // RG_HEADS
    t_c = n_cb * seq_c
    t_l = n_lb * seq_l
    assert n_lb + 1 <= SUBLANES
    assert d == 2 * SUBLANES * LANES

    pos = _grid_pos_emb(seq_l // GRID_W, d)
    cvec = jnp.zeros((SUBLANES, d), F32).at[0].set(c_ctx).at[1:1 + n_lb].set(c)
    mats_c = _dft_matrices(seq_c)
    mats_l = _dft_matrices(seq_l)

    def group_ctx(i, tm):
        return 0

    def group_lat(i, tm):
        return 1 + (i * tm) // seq_l

    xc = x_prompt.reshape(t_c, d)
    xs = x_sample.reshape(t_l, d)
    ctx_states = []
    for l in range(depth):
        last = l == depth - 1
        mod = _modulation(cvec, ada_w[l], ada_b[l][None]).reshape(SUBLANES, N_MOD, d)

        w4 = jnp.concatenate([rg_wa[l, 0], rg_wx[l, 0], rg_wa[l, 1], rg_wx[l, 1]], axis=-1).astype(BF16)
        b4 = jnp.concatenate([rg_ba[l, 0].reshape(RG_HEADS, 1, hd), rg_bx[l, 0].reshape(RG_HEADS, 1, hd),
                              rg_ba[l, 1].reshape(RG_HEADS, 1, hd), rg_bx[l, 1].reshape(RG_HEADS, 1, hd)], axis=-1)
        p = {
            'norm1_g': norm1_g[l][None], 'norm2_g': norm2_g[l][None], 'w_in': w_in[l].astype(BF16),
            'rg_conv_w': rg_conv_w[l], 'rg_conv_b': rg_conv_b[l][None], 'rg_w4': w4, 'rg_b4': b4,
            'rg_lam': rg_lam[l], 'hy_conv_w': hy_conv_w[l], 'hy_conv_b': hy_conv_b[l][None],
            'hy_bias': hy_bias[l], 'gn_rg': gn_rg[l][None], 'gn_hy': gn_hy[l][None],
            'w_out': w_out[l].astype(BF16), 'router_w': router_w[l].astype(BF16), 'router_b': router_b[l][None],
        }
        filt_args = (hy_w1[l], hy_b1[l][None], hy_freq[l][None], hy_w2[l], hy_b2[l][None], hy_w3[l],
                     hy_decay[l].reshape(1, -1))
        filt_c = _hyena_filters(seq_c, mats_c[0], mats_c[1], *filt_args)
        filt_l = _hyena_filters(seq_l, mats_l[0], mats_l[1], *filt_args)

        h0_c = jnp.zeros((n_cb, 2, d_rg), F32)
        x1_c, h2_c, idx_c, gw_c, st_c = _mixer_path(xc, None, n_cb, seq_c, h0_c, mod, group_ctx, p,
                                                    mats_c, filt_c, n_seq=min(8, n_cb))
        ctx_states.append(st_c)
        x1_l, h2_l, idx_l, gw_l, _ = _mixer_path(xs, pos if l == 0 else None, n_lb, seq_l,
                                                 state_rglru[:, l], mod, group_lat, p,
                                                 mats_l, filt_l, n_seq=1)

        idx_all = jnp.concatenate([idx_c, idx_l], axis=0)
        dest, blk_e, blk_valid, n_used = _route(idx_all, router_w.shape[-1], EXPERT_ROWS)
        dest_flat = dest.reshape(-1)
        xb = _dispatch(dest_flat, h2_c, h2_l, blk_e.shape[0] * EXPERT_ROWS)
        yb = _experts(blk_e, blk_valid, n_used, xb, exp_w_gate[l], exp_w_up[l], exp_w_down[l])
        sh = (sh_w_gate[l].astype(BF16), sh_w_up[l].astype(BF16), sh_w_down[l].astype(BF16))
        xc = _finish(dest_flat, h2_c, x1_c, gw_c, yb, mod, *sh, final_g[None], 0, group_ctx, last)
        xs = _finish(dest_flat, h2_l, x1_l, gw_l, yb, mod, *sh, final_g[None], t_c, group_lat, last)

    new_state = jnp.stack(ctx_states, axis=1).astype(x_prompt.dtype)
    return (xc.reshape(n_cb, seq_c, d), xs.reshape(n_lb, seq_l, d), new_state)
```

```python
import functools
import math

import jax
import jax.numpy as jnp
from jax import lax
from jax.experimental import pallas as pl
from jax.experimental.pallas import tpu as pltpu

F32 = jnp.float32
BF16 = jnp.bfloat16

GRID_W = 64
RG_HEADS = 8
RG_CONV_W = 4
RG_C = 8.0
HY_CONV_W = 3
HY_BANDS = 16
TOP_K = 8
ROUTED_SCALE = 2.5
N_MOD = 6
EPS = 1e-6

LANES = 128
SUBLANES = 8
VMEM_LIMIT_BYTES = 56 * 1024 * 1024

EXPERT_ROWS = 288


def _params(n_axes, vmem=VMEM_LIMIT_BYTES):
    return pltpu.CompilerParams(dimension_semantics=("arbitrary",) * n_axes, vmem_limit_bytes=vmem)


def _rms(x, g):
    return x * lax.rsqrt(jnp.mean(x * x, axis=-1, keepdims=True) + EPS) * g


def _sigmoid(x):
    return 0.5 * jnp.tanh(0.5 * x) + 0.5


HI_HALF = -65536


def _pack_pair(lo, hi):
    lo_b = lax.bitcast_convert_type(lo.astype(BF16).astype(F32), jnp.int32)
    hi_b = lax.bitcast_convert_type(hi.astype(BF16).astype(F32), jnp.int32)
    return hi_b | lax.shift_right_logical(lo_b, 16)


def _unpack_pair(p):
    lo = lax.bitcast_convert_type(lax.shift_left(p, 16), F32)
    hi = lax.bitcast_convert_type(p & HI_HALF, F32)
    return lo, hi


def _store_token_tiles(ref, packed):
    m = packed.shape[0]
    for j in range(SUBLANES):
        ref[pl.ds(j, m, stride=SUBLANES), :] = packed[:, j * LANES:(j + 1) * LANES]


def _load_token_tiles(ref, row0, m):
    return jnp.concatenate(
        [ref[pl.ds(row0 * SUBLANES + j, m, stride=SUBLANES), :] for j in range(SUBLANES)], axis=1)


def _mod_kernel(c_ref, w_ref, b_ref, o_ref):
    c = c_ref[...]
    s = (c * jax.nn.sigmoid(c)).astype(BF16)
    o_ref[...] = jnp.dot(s, w_ref[...].astype(BF16), preferred_element_type=F32) + b_ref[...]


def _modulation(cvec, ada_w, ada_b):
    d, n = ada_w.shape
    tn = 1536
    return pl.pallas_call(
        _mod_kernel,
        grid=(n // tn,),
        in_specs=[pl.BlockSpec((SUBLANES, d), lambda j: (0, 0)),
                  pl.BlockSpec((d, tn), lambda j: (0, j)),
                  pl.BlockSpec((1, tn), lambda j: (0, j))],
        out_specs=pl.BlockSpec((SUBLANES, tn), lambda j: (0, j)),
        out_shape=jax.ShapeDtypeStruct((SUBLANES, n), F32),
        compiler_params=_params(1),
        name="modulation",
    )(cvec, ada_w, ada_b)


HALO = 16


def _inproj_kernel(*refs, has_pos, tn, n_rg, seq_len):
    if has_pos:
        (x_ref, xp_ref, xn_ref, pos_ref, pp_ref, pn_ref, mod_ref, g_ref, w_hbm, cw_ref, cb_ref,
         u_ref, hv_ref, w_vmem, h_scr, sem) = refs
    else:
        (x_ref, xp_ref, xn_ref, mod_ref, g_ref, w_hbm, cw_ref, cb_ref,
         u_ref, hv_ref, w_vmem, h_scr, sem) = refs
    i = pl.program_id(0)
    j = pl.program_id(1)
    tm = x_ref.shape[0]

    @pl.when(jnp.logical_and(i == 0, j == 0))
    def _():
        cp = pltpu.make_async_copy(w_hbm, w_vmem, sem)
        cp.start()
        cp.wait()

    @pl.when(j == 0)
    def _():
        def normed(x_r, p_r):
            x = x_r[...]
            if has_pos:
                x = x + p_r[...]
            return (_rms(x, g_ref[...]) * (1.0 + mod_ref[0, 1:2, :]) + mod_ref[0, 0:1, :]).astype(BF16)

        h_scr[0:HALO, :] = normed(xp_ref, pp_ref if has_pos else None)
        h_scr[HALO:HALO + tm, :] = normed(x_ref, pos_ref if has_pos else None)
        h_scr[HALO + tm:, :] = normed(xn_ref, pn_ref if has_pos else None)

    w = w_vmem[:, pl.ds(pl.multiple_of(j * tn, tn), tn)]

    @pl.when(j < n_rg)
    def _():
        u_ref[...] = jnp.dot(h_scr[HALO:HALO + tm, :], w, preferred_element_type=F32)

    @pl.when(j >= n_rg)
    def _():
        ue = jnp.dot(h_scr[...], w, preferred_element_type=F32)
        t_in_seq = (i * tm + lax.broadcasted_iota(jnp.int32, (tm, tn), 0)) & (seq_len - 1)
        cw = cw_ref[...]
        prev = jnp.where(t_in_seq == 0, 0.0, _shift_rows(ue, -1)[HALO:HALO + tm])
        nxt = jnp.where(t_in_seq == seq_len - 1, 0.0, _shift_rows(ue, 1)[HALO:HALO + tm])
        acc = cb_ref[...] + cw[0:1, :] * prev + cw[1:2, :] * ue[HALO:HALO + tm] + cw[2:3, :] * nxt
        hv_ref[...] = acc.astype(BF16)


def _in_proj(x2d, pos, mod, norm_g, w_bf, conv_w, conv_b, seq_len, group_of_tile, tm=512, tn=1024):
    t, d = x2d.shape
    n = w_bf.shape[1]
    n_hv = conv_w.shape[1]
    n_rg = (n - n_hv) // tn
    has_pos = pos is not None
    hb = tm // HALO
    last_h = t // HALO - 1

    def prev_blk(i):
        return jnp.maximum(i * hb - 1, 0)

    def next_blk(i):
        return jnp.minimum((i + 1) * hb, last_h)

    in_specs = [pl.BlockSpec((tm, d), lambda i, j: (i, 0)),
                pl.BlockSpec((HALO, d), lambda i, j: (prev_blk(i), 0)),
                pl.BlockSpec((HALO, d), lambda i, j: (next_blk(i), 0))]
    args = [x2d, x2d, x2d]
    if has_pos:
        pos_tiles = pos.shape[0] // tm
        last_p = pos.shape[0] // HALO - 1
        in_specs += [pl.BlockSpec((tm, d), lambda i, j: (i % pos_tiles, 0)),
                     pl.BlockSpec((HALO, d), lambda i, j: (jnp.maximum((i % pos_tiles) * hb - 1, 0), 0)),
                     pl.BlockSpec((HALO, d), lambda i, j: (jnp.minimum((i % pos_tiles + 1) * hb, last_p), 0))]
        args += [pos, pos, pos]
    in_specs += [pl.BlockSpec((1, N_MOD, d), lambda i, j: (group_of_tile(i, tm), 0, 0)),
                 pl.BlockSpec((1, d), lambda i, j: (0, 0)),
                 pl.BlockSpec(memory_space=pl.ANY),
                 pl.BlockSpec((HY_CONV_W, tn), lambda i, j: (0, jnp.maximum(j - n_rg, 0))),
                 pl.BlockSpec((1, tn), lambda i, j: (0, jnp.maximum(j - n_rg, 0)))]
    args += [mod, norm_g, w_bf, conv_w, conv_b]
    return pl.pallas_call(
        functools.partial(_inproj_kernel, has_pos=has_pos, tn=tn, n_rg=n_rg, seq_len=seq_len),
        grid=(t // tm, n // tn),
        in_specs=in_specs,
        out_specs=[pl.BlockSpec((tm, tn), lambda i, j: (i, jnp.minimum(j, n_rg - 1))),
                   pl.BlockSpec((tm, tn), lambda i, j: (i, jnp.maximum(j - n_rg, 0)))],
        out_shape=[jax.ShapeDtypeStruct((t, n - n_hv), F32), jax.ShapeDtypeStruct((t, n_hv), BF16)],
        scratch_shapes=[pltpu.VMEM((d, n), BF16), pltpu.VMEM((tm + 2 * HALO, d), BF16),
                        pltpu.SemaphoreType.DMA(())],
        compiler_params=_params(2),
        name="in_proj",
    )(*args)


def _shift_rows(win, off):
    if off == 0:
        return win
    n = win.shape[0]
    return pltpu.roll(win, (-off) % n, axis=0)


def _scan_chunk(a, b, reverse):
    n = a.shape[0]
    row = lax.broadcasted_iota(jnp.int32, a.shape, 0)
    dist = 1
    while dist < n:
        if reverse:
            a_s = pltpu.roll(a, n - dist, axis=0)
            b_s = pltpu.roll(b, n - dist, axis=0)
            m = row < n - dist
        else:
            a_s = pltpu.roll(a, dist, axis=0)
            b_s = pltpu.roll(b, dist, axis=0)
            m = row >= dist
        b = jnp.where(m, a * b_s + b, b)
        a = jnp.where(m, a * a_s, a)
        dist *= 2
    return a, b


def _rglru_kernel(xr_ref, gr_ref, cw_ref, cb_ref, w4_ref, b4_ref, lam_ref, h0_ref,
                  y_ref, st_ref, af, bf, ab, bb, hf, *, seq_len, n_seq, t1, tc):
    hd = xr_ref.shape[-1]
    nlam = -lam_ref[...]
    sp = jnp.maximum(nlam, 0.0) + jnp.log1p(jnp.exp(-jnp.abs(nlam)))
    cw = cw_ref[...]
    cb = cb_ref[...]
    b4 = b4_ref[0]
    nc1 = seq_len // t1
    ncs = seq_len // tc

    def per_seq(s, carry0):
        def gates(c, carry):
            r0 = pl.multiple_of(c * t1, t1)
            cur = xr_ref[s, pl.ds(r0, t1), :]
            p0 = pl.multiple_of(jnp.maximum(r0 - SUBLANES, 0), SUBLANES)
            n0 = pl.multiple_of(jnp.minimum(r0 + t1, seq_len - SUBLANES), SUBLANES)
            prev = jnp.where(c > 0, xr_ref[s, pl.ds(p0, SUBLANES), :], 0.0)
            nxt = jnp.where(c < nc1 - 1, xr_ref[s, pl.ds(n0, SUBLANES), :], 0.0)
            win = jnp.concatenate([prev, cur, nxt], axis=0)
            xr = cb
            for k in range(RG_CONV_W):
                xr = xr + cw[k:k + 1, :] * _shift_rows(win, k - RG_CONV_W // 2)[SUBLANES:SUBLANES + t1]
            z = jnp.dot(xr.astype(BF16), w4_ref[0], preferred_element_type=F32) + b4
            for d_i, (a_scr, b_scr) in enumerate(((af, bf), (ab, bb))):
                r = _sigmoid(z[:, (2 * d_i) * hd:(2 * d_i + 1) * hd])
                gi = _sigmoid(z[:, (2 * d_i + 1) * hd:(2 * d_i + 2) * hd])
                log_a = (-RG_C) * r * sp[d_i:d_i + 1, :]
                a = jnp.exp(log_a)
                a_scr[pl.ds(r0, t1), :] = a
                b_scr[pl.ds(r0, t1), :] = jnp.sqrt(-jnp.tanh(log_a) * (a * a + 1.0)) * (gi * xr)
            return carry

        lax.fori_loop(0, nc1, gates, 0)
        h0 = h0_ref[s]

        def fwd(c, carry):
            r0 = pl.multiple_of(c * tc, tc)
            a, h = _scan_chunk(af[pl.ds(r0, tc), :], bf[pl.ds(r0, tc), :], False)
            h = a * carry + h
            hf[pl.ds(r0, tc), :] = h
            return h[tc - 1:tc, :]

        s_f = lax.fori_loop(0, ncs, fwd, h0[0:1, :])

        def bwd(cc, carry):
            r0 = pl.multiple_of((ncs - 1 - cc) * tc, tc)
            a, h = _scan_chunk(ab[pl.ds(r0, tc), :], bb[pl.ds(r0, tc), :], True)
            h = a * carry + h
            g = gr_ref[s, pl.ds(r0, tc), :]
            y_ref[s, pl.ds(r0, tc), :] = jax.nn.gelu(g) * (hf[pl.ds(r0, tc), :] + h)
            return h[0:1, :]

        s_b = lax.fori_loop(0, ncs, bwd, h0[1:2, :])
        st_ref[s] = jnp.concatenate([s_f, s_b], axis=0)
        return carry0

    lax.fori_loop(0, n_seq, per_seq, 0)


def _rglru(u3, h0, conv_w, conv_b, w4, b4, lam, n_seq):
    b, seq_len, _ = u3.shape
    hd = w4.shape[1]
    d_rg = hd * RG_HEADS
    t1 = min(seq_len, 256)
    tc = 64
    kern = functools.partial(_rglru_kernel, seq_len=seq_len, n_seq=n_seq, t1=t1, tc=tc)
    return pl.pallas_call(
        kern,
        grid=(b // n_seq, RG_HEADS),
        in_specs=[pl.BlockSpec((n_seq, seq_len, hd), lambda i, h: (i, 0, h)),
                  pl.BlockSpec((n_seq, seq_len, hd), lambda i, h: (i, 0, RG_HEADS + h)),
                  pl.BlockSpec((RG_CONV_W, hd), lambda i, h: (0, h)),
                  pl.BlockSpec((1, hd), lambda i, h: (0, h)),
                  pl.BlockSpec((1, hd, 4 * hd), lambda i, h: (h, 0, 0)),
                  pl.BlockSpec((1, 1, 4 * hd), lambda i, h: (h, 0, 0)),
                  pl.BlockSpec((2, hd), lambda i, h: (0, h)),
                  pl.BlockSpec((n_seq, 2, hd), lambda i, h: (i, 0, h))],
        out_specs=[pl.BlockSpec((n_seq, seq_len, hd), lambda i, h: (i, 0, h)),
                   pl.BlockSpec((n_seq, 2, hd), lambda i, h: (i, 0, h))],
        out_shape=[jax.ShapeDtypeStruct((b, seq_len, d_rg), F32),
                   jax.ShapeDtypeStruct((b, 2, d_rg), F32)],
        scratch_shapes=[pltpu.VMEM((seq_len, hd), F32)] * 5,
        compiler_params=_params(2),
        name="rglru",
    )(u3, u3, conv_w, conv_b, w4, b4, lam, h0)


def _filt_time_kernel(w1_ref, b1_ref, fr_ref, w2_ref, b2_ref, w3_ref, dec_ref,
                      g_ref, d_ref, st_ref, *, seq_len, tl):
    i = pl.program_id(0)
    hi = lax.Precision.HIGHEST
    posi = i * tl + lax.broadcasted_iota(jnp.int32, (tl, LANES), 0)
    pos = posi.astype(F32)
    lane = lax.broadcasted_iota(jnp.int32, (tl, LANES), 1)
    band = jnp.where(lane <= HY_BANDS, lane, lane - HY_BANDS).astype(F32)
    ang = (2.0 * math.pi) * pos / seq_len * band
    t = pos / seq_len
    feats = jnp.where(lane == 0, t,
                      jnp.where(lane <= HY_BANDS, jnp.cos(ang),
                                jnp.where(lane <= 2 * HY_BANDS, -jnp.sin(ang), 0.0)))
    fr = fr_ref[...]
    hid = jnp.sin(fr * (jnp.dot(feats, w1_ref[...], precision=hi, preferred_element_type=F32) + b1_ref[...]))
    hid = jnp.sin(fr * (jnp.dot(hid, w2_ref[...], precision=hi, preferred_element_type=F32) + b2_ref[...]))
    k = jnp.dot(hid, w3_ref[...], precision=hi, preferred_element_type=F32)
    k = k * jnp.exp(-t[:, 0:1] * jnp.abs(dec_ref[...]))
    c = k.shape[1] // 4
    first = posi[:, 0:1] == 0
    sign = jnp.where(posi[:, 0:1] % 2 == 0, 1.0, -1.0)

    @pl.when(i == 0)
    def _():
        st_ref[...] = jnp.zeros_like(st_ref)

    for o in range(2):
        kf = k[:, (2 * o) * c:(2 * o + 1) * c]
        kb = jnp.where(first, 0.0, k[:, (2 * o + 1) * c:(2 * o + 2) * c])
        g = kf + kb
        g_ref[:, o * c:(o + 1) * c] = g.astype(BF16)
        d_ref[:, o * c:(o + 1) * c] = (kf - kb).astype(BF16)
        st_ref[0:1, o * c:(o + 1) * c] += jnp.sum(kf * kf + kb * kb, axis=0, keepdims=True)
        st_ref[1:2, o * c:(o + 1) * c] += jnp.sum(sign * g, axis=0, keepdims=True)


def _filt_dft_kernel(c_ref, s_ref, g_ref, d_ref, st_ref, kre_ref, q_ref):
    scale = lax.rsqrt(st_ref[0:1, :] + EPS)
    kre_ref[...] = jnp.dot(c_ref[...], g_ref[...], preferred_element_type=F32) * scale
    q_ref[...] = jnp.dot(s_ref[...], d_ref[...], preferred_element_type=F32) * scale


def _hyena_filters(seq_len, cmat, smat, w1, b1, freq, w2, b2, w3, decay):
    n_hid = w1.shape[1]
    n_out = w3.shape[1]
    c2 = n_out // 2
    tl = min(seq_len, 512)
    w1p = jnp.zeros((LANES, n_hid), F32).at[:w1.shape[0]].set(w1)
    g, d, stats = pl.pallas_call(
        functools.partial(_filt_time_kernel, seq_len=seq_len, tl=tl),
        grid=(seq_len // tl,),
        in_specs=[pl.BlockSpec((LANES, n_hid), lambda i: (0, 0)),
                  pl.BlockSpec((1, n_hid), lambda i: (0, 0)),
                  pl.BlockSpec((1, n_hid), lambda i: (0, 0)),
                  pl.BlockSpec((n_hid, n_hid), lambda i: (0, 0)),
                  pl.BlockSpec((1, n_hid), lambda i: (0, 0)),
                  pl.BlockSpec((n_hid, n_out), lambda i: (0, 0)),
                  pl.BlockSpec((1, n_out), lambda i: (0, 0))],
        out_specs=[pl.BlockSpec((tl, c2), lambda i: (i, 0)),
                   pl.BlockSpec((tl, c2), lambda i: (i, 0)),
                   pl.BlockSpec((SUBLANES, c2), lambda i: (0, 0))],
        out_shape=[jax.ShapeDtypeStruct((seq_len, c2), BF16),
                   jax.ShapeDtypeStruct((seq_len, c2), BF16),
                   jax.ShapeDtypeStruct((SUBLANES, c2), F32)],
        compiler_params=_params(1),
        name="hyena_filter_taps",
    )(w1p, b1, freq, w2, b2, w3, decay)
    tm = min(seq_len, 512)
    tn = 512
    kre, q = pl.pallas_call(
        _filt_dft_kernel,
        grid=(c2 // tn, seq_len // tm),
        in_specs=[pl.BlockSpec((tm, seq_len), lambda n, m: (m, 0)),
                  pl.BlockSpec((tm, seq_len), lambda n, m: (m, 0)),
                  pl.BlockSpec((seq_len, tn), lambda n, m: (0, n)),
                  pl.BlockSpec((seq_len, tn), lambda n, m: (0, n)),
                  pl.BlockSpec((SUBLANES, tn), lambda n, m: (0, n))],
        out_specs=[pl.BlockSpec((tm, tn), lambda n, m: (m, n)),
                   pl.BlockSpec((tm, tn), lambda n, m: (m, n))],
        out_shape=[jax.ShapeDtypeStruct((seq_len, c2), F32)] * 2,
        compiler_params=_params(2),
        name="hyena_filter_dft",
    )(cmat, smat, g, d, stats)
    return kre, q, stats


def _dft_matrices(seq_len):
    assert seq_len & (seq_len - 1) == 0
    tm = min(seq_len, 256)
    out = jax.ShapeDtypeStruct((seq_len, seq_len), BF16)
    return pl.pallas_call(
        functools.partial(_dft_table_kernel, seq_len=seq_len),
        grid=(seq_len // tm,),
        in_specs=[],
        out_specs=[pl.BlockSpec((tm, seq_len), lambda i: (i, 0))] * 3,
        out_shape=[out] * 3,
        scratch_shapes=[pltpu.VMEM((tm, seq_len), F32)] * 2,
        compiler_params=_params(1),
        name="dft_tables",
    )()


def _dft_table_kernel(c_ref, s_ref, st_ref, c0, s0, *, seq_len):
    i = pl.program_id(0)
    tm = c_ref.shape[0]
    wrap = 2 * seq_len - 1
    row = lax.broadcasted_iota(jnp.int32, (tm, seq_len), 0)
    col = lax.broadcasted_iota(jnp.int32, (tm, seq_len), 1)

    @pl.when(i == 0)
    def _():
        ang = ((row * col) & wrap).astype(F32) * (math.pi / seq_len)
        c0[...] = jnp.cos(ang)
        s0[...] = jnp.sin(ang)

    k0 = i * tm
    ang0 = ((k0 * col[0:1, :]) & wrap).astype(F32) * (math.pi / seq_len)
    cn = jnp.cos(ang0)
    sn = jnp.sin(ang0)
    cmat = c0[...] * cn - s0[...] * sn
    smat = s0[...] * cn + c0[...] * sn
    alt_col = jnp.where((col & 1) == 0, 1.0, -1.0)
    alt_row = jnp.where(((row + k0) & 1) == 0, 1.0, -1.0)
    c_ref[...] = cmat.astype(BF16)
    s_ref[...] = jnp.where(row + k0 == 0, alt_col, smat).astype(BF16)
    st_ref[...] = jnp.where(col == 0, alt_row, smat).astype(BF16)


def _hy_fwd_kernel(c_ref, s_ref, u_ref, kre_ref, q_ref, st_ref, pre_ref, pm_ref, *, seq_len):
    m = pl.program_id(2)
    u = u_ref[0]
    a = jnp.dot(c_ref[...], u, preferred_element_type=F32)
    bv = jnp.dot(s_ref[...], u, preferred_element_type=F32)
    tm = a.shape[0]
    is0 = (m * tm + lax.broadcasted_iota(jnp.int32, a.shape, 0)) == 0
    k_nyq = st_ref[1:2, :] * lax.rsqrt(st_ref[0:1, :] + EPS)
    kre = kre_ref[...]
    q = jnp.where(is0, 0.0, q_ref[...])
    kre_b = jnp.where(is0, k_nyq, kre)
    wk = jnp.where(is0, 0.5 / seq_len, 1.0 / seq_len)
    pre_ref[0] = ((a * kre - bv * q) * wk).astype(BF16)
    pm_ref[0] = ((a * q + bv * kre_b) * wk).astype(BF16)


def _hy_inv_kernel(c_ref, st_ref, pre_ref, pm_ref, u_ref, x_ref, bias_ref, z_ref):
    y = jnp.dot(c_ref[...], pre_ref[0], preferred_element_type=F32)
    y = y + jnp.dot(st_ref[...], pm_ref[0], preferred_element_type=F32)
    y = y + u_ref[0].astype(F32) * bias_ref[...]
    z_ref[0] = (y * x_ref[0].astype(F32)).astype(BF16)


def _hyena_order(order, u_arr, u_cb, hvc, gate_cb, mats, filt, bias, tm, tc):
    cmat, smat, smat_t = mats
    kre, q, stats = filt
    b, seq_len, _ = hvc.shape
    c = kre.shape[1] // 2
    nct = c // tc
    grid = (b, nct, seq_len // tm)
    pre, pm = pl.pallas_call(
        functools.partial(_hy_fwd_kernel, seq_len=seq_len),
        grid=grid,
        in_specs=[pl.BlockSpec((tm, seq_len), lambda i, n, m: (m, 0)),
                  pl.BlockSpec((tm, seq_len), lambda i, n, m: (m, 0)),
                  pl.BlockSpec((1, seq_len, tc), lambda i, n, m: (i, 0, u_cb * nct + n)),
                  pl.BlockSpec((tm, tc), lambda i, n, m: (m, order * nct + n)),
                  pl.BlockSpec((tm, tc), lambda i, n, m: (m, order * nct + n)),
                  pl.BlockSpec((SUBLANES, tc), lambda i, n, m: (0, order * nct + n))],
        out_specs=[pl.BlockSpec((1, tm, tc), lambda i, n, m: (i, m, n))] * 2,
        out_shape=[jax.ShapeDtypeStruct((b, seq_len, c), BF16)] * 2,
        compiler_params=_params(3),
        name="hyena_fwd_dft",
    )(cmat, smat, u_arr, kre, q, stats)
    return pl.pallas_call(
        _hy_inv_kernel,
        grid=grid,
        in_specs=[pl.BlockSpec((tm, seq_len), lambda i, n, m: (m, 0)),
                  pl.BlockSpec((tm, seq_len), lambda i, n, m: (m, 0)),
                  pl.BlockSpec((1, seq_len, tc), lambda i, n, m: (i, 0, n)),
                  pl.BlockSpec((1, seq_len, tc), lambda i, n, m: (i, 0, n)),
                  pl.BlockSpec((1, tm, tc), lambda i, n, m: (i, m, u_cb * nct + n)),
                  pl.BlockSpec((1, tm, tc), lambda i, n, m: (i, m, gate_cb * nct + n)),
                  pl.BlockSpec((1, tc), lambda i, n, m: (0, n))],
        out_specs=pl.BlockSpec((1, tm, tc), lambda i, n, m: (i, m, n)),
        out_shape=jax.ShapeDtypeStruct((b, seq_len, c), BF16),
        compiler_params=_params(3),
        name="hyena_inv_dft",
    )(cmat, smat_t, pre, pm, u_arr, hvc, bias[order][None])


def _post_mixer_kernel(*refs, has_pos):
    if has_pos:
        (yrg_ref, yhy_ref, x_ref, pos_ref, mod_ref, gnr_ref, gnh_ref, wo_ref, n2_ref,
         rw_ref, rb_ref, x1_ref, h2_ref, idx_ref, gw_ref) = refs
    else:
        (yrg_ref, yhy_ref, x_ref, mod_ref, gnr_ref, gnh_ref, wo_ref, n2_ref,
         rw_ref, rb_ref, x1_ref, h2_ref, idx_ref, gw_ref) = refs
    d_rg = yrg_ref.shape[1]
    na = _rms(yrg_ref[...], gnr_ref[...]).astype(BF16)
    nb = _rms(yhy_ref[...].astype(F32), gnh_ref[...]).astype(BF16)
    y = jnp.dot(na, wo_ref[0:d_rg, :], preferred_element_type=F32)
    y = y + jnp.dot(nb, wo_ref[d_rg:, :], preferred_element_type=F32)
    x = x_ref[...]
    if has_pos:
        x = x + pos_ref[...]
    x1 = x + mod_ref[0, 2:3, :] * y
    x1_ref[...] = x1
    h2f = _rms(x1, n2_ref[...]) * (1.0 + mod_ref[0, 4:5, :]) + mod_ref[0, 3:4, :]
    half = h2f.shape[1] // 2
    _store_token_tiles(h2_ref, _pack_pair(h2f[:, :half], h2f[:, half:]))
    h2 = h2f.astype(BF16)
    scores = jax.nn.sigmoid(jnp.dot(h2, rw_ref[...], preferred_element_type=F32))
    sel = scores + rb_ref[...]
    n_exp = scores.shape[1]
    lane = lax.broadcasted_iota(jnp.int32, scores.shape, 1)
    col = lax.broadcasted_iota(jnp.int32, idx_ref.shape, 1)
    idx_acc = jnp.zeros(idx_ref.shape, jnp.int32)
    gw_acc = jnp.zeros(gw_ref.shape, F32)
    for k in range(TOP_K):
        mx = jnp.max(sel, axis=1, keepdims=True)
        pick = jnp.min(jnp.where(sel == mx, lane, n_exp), axis=1, keepdims=True)
        hit = lane == pick
        val = jnp.sum(jnp.where(hit, scores, 0.0), axis=1, keepdims=True)
        sel = jnp.where(hit, -jnp.inf, sel)
        idx_acc = jnp.where(col == k, pick, idx_acc)
        gw_acc = jnp.where(col == k, val, gw_acc)
    idx_ref[...] = idx_acc
    gw_ref[...] = gw_acc / jnp.sum(gw_acc, axis=1, keepdims=True) * ROUTED_SCALE


def _post_mixer(y_rg, y_hy, x2d, pos, mod, gn_rg, gn_hy, w_out_bf, norm2_g, router_w_bf, router_b,
                group_of_tile, tm=256):
    t, d = x2d.shape
    d_rg = y_rg.shape[1]
    d_hy = y_hy.shape[1]
    n_exp = router_w_bf.shape[1]
    has_pos = pos is not None
    in_specs = [pl.BlockSpec((tm, d_rg), lambda i: (i, 0)),
                pl.BlockSpec((tm, d_hy), lambda i: (i, 0)),
                pl.BlockSpec((tm, d), lambda i: (i, 0))]
    args = [y_rg, y_hy, x2d]
    if has_pos:
        pos_tiles = pos.shape[0] // tm
        in_specs.append(pl.BlockSpec((tm, d), lambda i: (i % pos_tiles, 0)))
        args.append(pos)
    in_specs += [pl.BlockSpec((1, N_MOD, d), lambda i: (group_of_tile(i, tm), 0, 0)),
                 pl.BlockSpec((1, d_rg), lambda i: (0, 0)),
                 pl.BlockSpec((1, d_hy), lambda i: (0, 0)),
                 pl.BlockSpec((d_rg + d_hy, d), lambda i: (0, 0)),
                 pl.BlockSpec((1, d), lambda i: (0, 0)),
                 pl.BlockSpec((d, n_exp), lambda i: (0, 0)),
                 pl.BlockSpec((1, n_exp), lambda i: (0, 0))]
    args += [mod, gn_rg, gn_hy, w_out_bf, norm2_g, router_w_bf, router_b]
    return pl.pallas_call(
        functools.partial(_post_mixer_kernel, has_pos=has_pos),
        grid=(t // tm,),
        in_specs=in_specs,
        out_specs=[pl.BlockSpec((tm, d), lambda i: (i, 0)),
                   pl.BlockSpec((tm * SUBLANES, LANES), lambda i: (i, 0)),
                   pl.BlockSpec((tm, TOP_K), lambda i: (i, 0)),
                   pl.BlockSpec((tm, TOP_K), lambda i: (i, 0))],
        out_shape=[jax.ShapeDtypeStruct((t, d), F32),
                   jax.ShapeDtypeStruct((t * SUBLANES, LANES), jnp.int32),
                   jax.ShapeDtypeStruct((t, TOP_K), jnp.int32),
                   jax.ShapeDtypeStruct((t, TOP_K), F32)],
        compiler_params=_params(1),
        name="post_mixer",
    )(*args)


def _route_kernel(idx_ref, dest_ref, be_ref, bv_ref, nu_ref, tri, cnt, base, pst, *, blk, n_blk):
    p = pl.program_id(0)
    i = pl.program_id(1)
    tm, top_k = idx_ref.shape
    ne = cnt.shape[1]

    def div_blk(n):
        return jnp.floor((n + 0.5) / blk)

    @pl.when(jnp.logical_and(p == 0, i == 0))
    def _():
        r = lax.broadcasted_iota(jnp.int32, (tm, tm), 0)
        c = lax.broadcasted_iota(jnp.int32, (tm, tm), 1)
        tri[...] = jnp.where(r > c, 1.0, 0.0).astype(BF16)
        cnt[...] = jnp.zeros_like(cnt)

    idx = idx_ref[...]
    lane = lax.broadcasted_iota(jnp.int32, (tm, ne), 1)
    hits = [lane == idx[:, k:k + 1] for k in range(top_k)]
    occ = jnp.zeros((tm, ne), F32)
    for h in hits:
        occ = occ + jnp.where(h, 1.0, 0.0)
    col_sum = jnp.sum(occ, axis=0, keepdims=True)

    @pl.when(p == 0)
    def _():
        cnt[0:1, :] += col_sum

    @pl.when(jnp.logical_and(p == 1, i == 0))
    def _():
        counts = cnt[...]
        padded = div_blk(counts + (blk - 1.0)) * blk
        r = lax.broadcasted_iota(jnp.int32, (ne, ne), 0)
        c = lax.broadcasted_iota(jnp.int32, (ne, ne), 1)
        upper = jnp.where(r <= c, 1.0, 0.0)
        pend = jnp.dot(padded, upper, precision=lax.Precision.HIGHEST, preferred_element_type=F32)
        pstart = pend - padded
        pst[...] = pstart
        base[...] = jnp.zeros_like(base)
        b0 = (lax.broadcasted_iota(jnp.int32, (n_blk, ne), 0) * blk).astype(F32)
        be = jnp.sum(jnp.where(pend[0:1, :] <= b0, 1.0, 0.0), axis=1, keepdims=True)
        be = jnp.minimum(be, ne - 1.0)
        own = lax.broadcasted_iota(jnp.int32, (n_blk, ne), 1).astype(F32) == be
        pst_b = jnp.sum(jnp.where(own, pstart[0:1, :], 0.0), axis=1, keepdims=True)
        cnt_b = jnp.sum(jnp.where(own, counts[0:1, :], 0.0), axis=1, keepdims=True)
        valid = jnp.clip(cnt_b - (b0[:, 0:1] - pst_b), 0.0, float(blk))
        be_ref[...] = be.astype(jnp.int32)
        bv_ref[...] = valid.astype(jnp.int32)
        total = jnp.max(pend[0:1, :], axis=1, keepdims=True)
        nu_ref[...] = jnp.broadcast_to(div_blk(total).astype(jnp.int32), nu_ref.shape)

    @pl.when(p == 1)
    def _():
        cum = jnp.dot(tri[...], occ.astype(BF16), preferred_element_type=F32) + base[0:1, :] + pst[0:1, :]
        col = lax.broadcasted_iota(jnp.int32, (tm, top_k), 1)
        acc = jnp.zeros((tm, top_k), F32)
        for k in range(top_k):
            v = jnp.sum(jnp.where(hits[k], cum, 0.0), axis=1, keepdims=True)
            acc = jnp.where(col == k, v, acc)
        dest_ref[...] = acc.astype(jnp.int32)
        base[0:1, :] += col_sum


def _route(idx, n_exp, blk, tm=512):
    n_tok, top_k = idx.shape
    n_blk = -(-n_tok * top_k // blk) + n_exp
    dest, be, bv, nu = pl.pallas_call(
        functools.partial(_route_kernel, blk=blk, n_blk=n_blk),
        grid=(2, n_tok // tm),
        in_specs=[pl.BlockSpec((tm, top_k), lambda p, i: (i, 0))],
        out_specs=[pl.BlockSpec((tm, top_k), lambda p, i: (i * p, 0)),
                   pl.BlockSpec((n_blk, 1), lambda p, i: (0, 0)),
                   pl.BlockSpec((n_blk, 1), lambda p, i: (0, 0)),
                   pl.BlockSpec((SUBLANES, LANES), lambda p, i: (0, 0))],
        out_shape=[jax.ShapeDtypeStruct((n_tok, top_k), jnp.int32),
                   jax.ShapeDtypeStruct((n_blk, 1), jnp.int32),
                   jax.ShapeDtypeStruct((n_blk, 1), jnp.int32),
                   jax.ShapeDtypeStruct((SUBLANES, LANES), jnp.int32)],
        scratch_shapes=[pltpu.VMEM((tm, tm), BF16), pltpu.VMEM((SUBLANES, n_exp), F32),
                        pltpu.VMEM((SUBLANES, n_exp), F32), pltpu.VMEM((SUBLANES, n_exp), F32)],
        compiler_params=_params(2),
        name="route",
    )(idx)
    return dest, be.reshape(n_blk), bv.reshape(n_blk), nu[0, 0:1]


def _dispatch_kernel(dest_ref, xa_ref, xl_ref, sg_ref, su_ref, sd_ref, out_ref, sh_ref, sem, *, n_a):
    tm = xa_ref.shape[0] // SUBLANES
    dh = SUBLANES * LANES

    def tile(ref, r):
        return ref.at[pl.ds(pl.multiple_of(r * SUBLANES, SUBLANES), SUBLANES), :]

    def run(src):
        def issue(t, c):
            for k in range(TOP_K):
                pltpu.make_async_copy(tile(src, t), tile(out_ref, dest_ref[t * TOP_K + k]), sem).start(priority=k % 2)
            return c

        lax.fori_loop(0, tm, issue, 0)

        lo, hi = _unpack_pair(_load_token_tiles(src, 0, tm))
        lo = lo.astype(BF16)
        hi = hi.astype(BF16)
        g = jnp.dot(lo, sg_ref[0:dh, :], preferred_element_type=F32)
        g = g + jnp.dot(hi, sg_ref[dh:, :], preferred_element_type=F32)
        u = jnp.dot(lo, su_ref[0:dh, :], preferred_element_type=F32)
        u = u + jnp.dot(hi, su_ref[dh:, :], preferred_element_type=F32)
        hmid = (g * _sigmoid(g) * u).astype(BF16)
        sh_ref[...] = jnp.dot(hmid, sd_ref[...], preferred_element_type=F32).astype(BF16)

        def drain(t, c):
            for k in range(TOP_K):
                pltpu.make_async_copy(tile(src, 0), tile(out_ref, 0), sem).wait()
            return c

        lax.fori_loop(0, tm, drain, 0, unroll=8)

    @pl.when(pl.program_id(0) < n_a)
    def _():
        run(xa_ref)

    @pl.when(pl.program_id(0) >= n_a)
    def _():
        run(xl_ref)


def _dispatch(dest_flat, h2p_a, h2p_l, n_rows, sg_bf, su_bf, sd_bf, tm=256):
    n_a = h2p_a.shape[0] // (tm * SUBLANES)
    n_l = h2p_l.shape[0] // (tm * SUBLANES)
    d, ds_ = sg_bf.shape
    return pl.pallas_call(
        functools.partial(_dispatch_kernel, n_a=n_a),
        grid=(n_a + n_l,),
        in_specs=[pl.BlockSpec((tm * TOP_K,), lambda i: (i,), memory_space=pltpu.SMEM),
                  pl.BlockSpec((tm * SUBLANES, LANES), lambda i: (jnp.minimum(i, n_a - 1), 0)),
                  pl.BlockSpec((tm * SUBLANES, LANES), lambda i: (jnp.maximum(i - n_a, 0), 0)),
                  pl.BlockSpec((d, ds_), lambda i: (0, 0)),
                  pl.BlockSpec((d, ds_), lambda i: (0, 0)),
                  pl.BlockSpec((ds_, d), lambda i: (0, 0))],
        out_specs=[pl.BlockSpec(memory_space=pl.ANY),
                   pl.BlockSpec((tm, d), lambda i: (i, 0))],
        out_shape=[jax.ShapeDtypeStruct((n_rows * SUBLANES, LANES), jnp.int32),
                   jax.ShapeDtypeStruct(((n_a + n_l) * tm, d), BF16)],
        scratch_shapes=[pltpu.SemaphoreType.DMA(())],
        compiler_params=_params(1),
        name="dispatch",
    )(dest_flat, h2p_a, h2p_l, sg_bf, su_bf, sd_bf)


def _expert_kernel(be_ref, bv_ref, nu_ref, x_ref, wg_hbm, wu_hbm, wd_hbm, o_ref,
                   wg_f, wu_f, wd_f, wg_b, wu_b, wd_b, sem, grp, *, n_blk):
    i = pl.program_id(0)
    n_used = nu_ref[0]
    e = be_ref[i]
    active = i < n_used
    changed = jnp.logical_or(i == 0, e != be_ref[jnp.maximum(i - 1, 0)])

    def expert_at(j):
        return be_ref[jnp.minimum(j, n_blk - 1)]

    def next_group(j):
        ej = expert_at(j)
        return lax.while_loop(lambda q: jnp.logical_and(q < n_used, expert_at(q) == ej), lambda q: q + 1, j + 1)

    def copies(ex, slot):
        return (pltpu.make_async_copy(wg_hbm.at[ex], wg_f.at[slot], sem.at[slot, 0]),
                pltpu.make_async_copy(wu_hbm.at[ex], wu_f.at[slot], sem.at[slot, 1]),
                pltpu.make_async_copy(wd_hbm.at[ex], wd_f.at[slot], sem.at[slot, 2]))

    @pl.when(jnp.logical_and(active, i == 0))
    def _():
        grp[0] = 0
        for cp in copies(e, 0):
            cp.start()
        n1 = next_group(i)

        @pl.when(n1 < n_used)
        def _():
            for cp in copies(expert_at(n1), 1):
                cp.start()

    @pl.when(jnp.logical_and(active, changed))
    def _():
        slot = grp[0] % 2
        for cp in copies(e, slot):
            cp.wait()
        wg_b[...] = wg_f[slot].astype(BF16)
        wu_b[...] = wu_f[slot].astype(BF16)
        wd_b[...] = wd_f[slot].astype(BF16)
        n2 = next_group(next_group(i))

        @pl.when(n2 < n_used)
        def _():
            for cp in copies(expert_at(n2), slot):
                cp.start()

        grp[0] = grp[0] + 1

    @pl.when(active)
    def _():
        tm = x_ref.shape[0] // SUBLANES
        dh = SUBLANES * LANES
        live = lax.broadcasted_iota(jnp.int32, (tm, dh), 0) < bv_ref[i]
        lo, hi = _unpack_pair(_load_token_tiles(x_ref, 0, tm))
        lo = jnp.where(live, lo, 0.0).astype(BF16)
        hi = jnp.where(live, hi, 0.0).astype(BF16)
        g = jnp.dot(lo, wg_b[0:dh, :], preferred_element_type=F32)
        g = g + jnp.dot(hi, wg_b[dh:, :], preferred_element_type=F32)
        u = jnp.dot(lo, wu_b[0:dh, :], preferred_element_type=F32)
        u = u + jnp.dot(hi, wu_b[dh:, :], preferred_element_type=F32)
        hmid = (g * _sigmoid(g) * u).astype(BF16)
        y = jnp.dot(hmid, wd_b[...], preferred_element_type=F32)
        _store_token_tiles(o_ref, _pack_pair(y[:, :dh], y[:, dh:]))


def _experts(blk_e, blk_valid, n_used, xb, wg, wu, wd):
    n_exp, d, de = wg.shape
    tm = EXPERT_ROWS
    n_blk = xb.shape[0] // (tm * SUBLANES)
    grid_spec = pltpu.PrefetchScalarGridSpec(
        num_scalar_prefetch=3,
        grid=(n_blk,),
        in_specs=[pl.BlockSpec((tm * SUBLANES, LANES), lambda i, be, bv, nu: (jnp.minimum(i, nu[0] - 1), 0)),
                  pl.BlockSpec(memory_space=pl.ANY),
                  pl.BlockSpec(memory_space=pl.ANY),
                  pl.BlockSpec(memory_space=pl.ANY)],
        out_specs=pl.BlockSpec((tm * SUBLANES, LANES), lambda i, be, bv, nu: (jnp.minimum(i, nu[0] - 1), 0)),
        scratch_shapes=[pltpu.VMEM((2, d, de), F32), pltpu.VMEM((2, d, de), F32), pltpu.VMEM((2, de, d), F32),
                        pltpu.VMEM((d, de), BF16), pltpu.VMEM((d, de), BF16), pltpu.VMEM((de, d), BF16),
                        pltpu.SemaphoreType.DMA((2, 3)), pltpu.SMEM((1,), jnp.int32)],
    )
    return pl.pallas_call(
        functools.partial(_expert_kernel, n_blk=n_blk),
        grid_spec=grid_spec,
        out_shape=jax.ShapeDtypeStruct(xb.shape, jnp.int32),
        compiler_params=_params(1),
        name="routed_experts",
    )(blk_e, blk_valid, n_used, xb, wg, wu, wd)


FINISH_ROWS = 64


def _finish_kernel(dcur_ref, dnxt_ref, sh_ref, x1_ref, gw_ref, mod_ref, fg_ref,
                   yb_hbm, o_ref, gbuf, gwb, sem, *, final_norm):
    i = pl.program_id(0)
    n_tiles = pl.num_programs(0)
    tm, d = x1_ref.shape
    dh = SUBLANES * LANES
    slot = i % 2

    def row_copy(row, t, k, s):
        src = yb_hbm.at[pl.ds(pl.multiple_of(row * SUBLANES, SUBLANES), SUBLANES), :]
        dst = gbuf.at[s, pl.ds(pl.multiple_of((k * tm + t) * SUBLANES, SUBLANES), SUBLANES), :]
        return pltpu.make_async_copy(src, dst, sem.at[s])

    def gather(d_ref, s):
        def issue(t, c):
            for k in range(TOP_K):
                row_copy(d_ref[t * TOP_K + k], t, k, s).start(priority=k % 2)
            return c

        lax.fori_loop(0, tm, issue, 0)

    @pl.when(i == 0)
    def _():
        gather(dcur_ref, 0)

    @pl.when(i + 1 < n_tiles)
    def _():
        gather(dnxt_ref, 1 - slot)

    gw = gw_ref[...]
    for k in range(TOP_K):
        gwb[k] = jnp.broadcast_to(gw[:, k:k + 1], (tm, LANES))

    def drain(t, c):
        for k in range(TOP_K):
            row_copy(0, 0, 0, slot).wait()
        return c

    lax.fori_loop(0, tm, drain, 0, unroll=8)

    rows_g = gbuf.at[slot]
    for r0 in range(0, tm, FINISH_ROWS):
        rs = slice(r0, r0 + FINISH_ROWS)
        ssq = jnp.zeros((FINISH_ROWS, 1), F32)
        for j in range(SUBLANES):
            cols = (slice(j * LANES, (j + 1) * LANES), slice(dh + j * LANES, dh + (j + 1) * LANES))
            acc = [sh_ref[rs, c].astype(F32) for c in cols]
            for k in range(TOP_K):
                halves = _unpack_pair(rows_g[pl.ds((k * tm + r0) * SUBLANES + j, FINISH_ROWS, stride=SUBLANES), :])
                w = gwb[k, rs, :]
                acc = [a + w * h for a, h in zip(acc, halves)]
            for c, a in zip(cols, acc):
                x2 = x1_ref[rs, c] + mod_ref[0, 5:6, c] * a
                o_ref[rs, c] = x2
                ssq = ssq + jnp.sum(x2 * x2, axis=-1, keepdims=True)
        if final_norm:
            inv = lax.rsqrt(ssq / d + EPS)
            o_ref[rs, :] = o_ref[rs, :] * inv * fg_ref[...]


def _finish(dest_flat, shared, x1, gw, yb, mod, final_g, row0, group_of_tile, final_norm, tm=256):
    n_rows, d = x1.shape
    t0 = row0 // tm
    n_tiles = n_rows // tm
    return pl.pallas_call(
        functools.partial(_finish_kernel, final_norm=final_norm),
        grid=(n_tiles,),
        in_specs=[pl.BlockSpec((tm * TOP_K,), lambda i: (t0 + i,), memory_space=pltpu.SMEM),
                  pl.BlockSpec((tm * TOP_K,), lambda i: (t0 + jnp.minimum(i + 1, n_tiles - 1),),
                               memory_space=pltpu.SMEM),
                  pl.BlockSpec((tm, d), lambda i: (t0 + i, 0)),
                  pl.BlockSpec((tm, d), lambda i: (i, 0)),
                  pl.BlockSpec((tm, TOP_K), lambda i: (i, 0)),
                  pl.BlockSpec((1, N_MOD, d), lambda i: (group_of_tile(i, tm), 0, 0)),
                  pl.BlockSpec((1, d), lambda i: (0, 0)),
                  pl.BlockSpec(memory_space=pl.ANY)],
        out_specs=pl.BlockSpec((tm, d), lambda i: (i, 0)),
        out_shape=jax.ShapeDtypeStruct((n_rows, d), F32),
        scratch_shapes=[pltpu.VMEM((2, TOP_K * tm * SUBLANES, LANES), jnp.int32),
                        pltpu.VMEM((TOP_K, tm, LANES), F32), pltpu.SemaphoreType.DMA((2,))],
        compiler_params=_params(1),
        name="finish",
    )(dest_flat, dest_flat, shared, x1, gw, mod, final_g, yb)


def _grid_pos_emb(rows, d):
    quarter = d // 4
    omega = 1.0 / (10000.0 ** (jnp.arange(quarter, dtype=F32) / quarter))
    r = jnp.arange(rows, dtype=F32)[:, None] * omega
    cc = jnp.arange(GRID_W, dtype=F32)[:, None] * omega
    by_row = jnp.concatenate([jnp.sin(r), jnp.cos(r)], axis=-1)
    by_col = jnp.concatenate([jnp.sin(cc), jnp.cos(cc)], axis=-1)
    full = jnp.concatenate([jnp.broadcast_to(by_row[:, None, :], (rows, GRID_W, d // 2)),
                            jnp.broadcast_to(by_col[None, :, :], (rows, GRID_W, d // 2))], axis=-1)
    return full.reshape(rows * GRID_W, d)


def _mixer_path(x2d, pos, n_b, seq_len, h0, mod, group_of_tile, p, mats, filt, n_seq):
    d_rg = p['gn_rg'].shape[1]
    c = p['gn_hy'].shape[1]
    u_rg, hvc = _in_proj(x2d, pos, mod, p['norm1_g'], p['w_in'], p['hy_conv_w'], p['hy_conv_b'],
                         seq_len, group_of_tile)
    u3 = u_rg.reshape(n_b, seq_len, u_rg.shape[1])
    hvc = hvc.reshape(n_b, seq_len, hvc.shape[1])
    y_rg, st = _rglru(u3, h0, p['rg_conv_w'], p['rg_conv_b'], p['rg_w4'], p['rg_b4'], p['rg_lam'], n_seq)
    tm = min(seq_len, 512)
    tc = 512 if seq_len > 512 else c
    z1 = _hyena_order(0, hvc, 0, hvc, 1, mats, filt, p['hy_bias'], tm, tc)
    y_hy = _hyena_order(1, z1, 0, hvc, 2, mats, filt, p['hy_bias'], tm, tc)
    x1, h2, idx, gw = _post_mixer(y_rg.reshape(-1, d_rg), y_hy.reshape(-1, c), x2d, pos, mod,
                                  p['gn_rg'], p['gn_hy'], p['w_out'], p['norm2_g'],
                                  p['router_w'], p['router_b'], group_of_tile)
    return x1, h2, idx, gw, st


def kernel(x_prompt, x_sample, state_rglru, c, c_ctx, ada_w, ada_b, norm1_g, norm2_g, w_in, rg_conv_w, rg_conv_b, rg_wa, rg_ba, rg_wx, rg_bx, rg_lam, hy_conv_w, hy_conv_b, hy_w1, hy_b1, hy_freq, hy_w2, hy_b2, hy_w3, hy_decay, hy_bias, gn_rg, gn_hy, w_out, router_w, router_b, exp_w_gate, exp_w_up, exp_w_down, sh_w_gate, sh_w_up, sh_w_down, final_g):
    n_cb, seq_c, d = x_prompt.shape
    n_lb, seq_l, _ = x_sample.shape
    depth = ada_w.shape[0]
    d_rg = gn_rg.shape[1]
    hd = d_rg // RG_HEADS
    t_c = n_cb * seq_c
    t_l = n_lb * seq_l
    assert n_lb + 1 <= SUBLANES
    assert d == 2 * SUBLANES * LANES

    pos = _grid_pos_emb(seq_l // GRID_W, d)
    cvec = jnp.zeros((SUBLANES, d), F32).at[0].set(c_ctx).at[1:1 + n_lb].set(c)
    mats_c = _dft_matrices(seq_c)
    mats_l = _dft_matrices(seq_l)

    def group_ctx(i, tm):
        return 0

    def group_lat(i, tm):
        return 1 + (i * tm) // seq_l

    xc = x_prompt.reshape(t_c, d)
    xs = x_sample.reshape(t_l, d)
    ctx_states = []
    for l in range(depth):
        last = l == depth - 1
        mod = _modulation(cvec, ada_w[l], ada_b[l][None]).reshape(SUBLANES, N_MOD, d)

        w4 = jnp.concatenate([rg_wa[l, 0], rg_wx[l, 0], rg_wa[l, 1], rg_wx[l, 1]], axis=-1).astype(BF16)
        b4 = jnp.concatenate([rg_ba[l, 0].reshape(RG_HEADS, 1, hd), rg_bx[l, 0].reshape(RG_HEADS, 1, hd),
                              rg_ba[l, 1].reshape(RG_HEADS, 1, hd), rg_bx[l, 1].reshape(RG_HEADS, 1, hd)], axis=-1)
        p = {
            'norm1_g': norm1_g[l][None], 'norm2_g': norm2_g[l][None], 'w_in': w_in[l].astype(BF16),
            'rg_conv_w': rg_conv_w[l], 'rg_conv_b': rg_conv_b[l][None], 'rg_w4': w4, 'rg_b4': b4,
            'rg_lam': rg_lam[l], 'hy_conv_w': hy_conv_w[l], 'hy_conv_b': hy_conv_b[l][None],
            'hy_bias': hy_bias[l], 'gn_rg': gn_rg[l][None], 'gn_hy': gn_hy[l][None],
            'w_out': w_out[l].astype(BF16), 'router_w': router_w[l].astype(BF16), 'router_b': router_b[l][None],
        }
        filt_args = (hy_w1[l], hy_b1[l][None], hy_freq[l][None], hy_w2[l], hy_b2[l][None], hy_w3[l],
                     hy_decay[l].reshape(1, -1))
        filt_c = _hyena_filters(seq_c, mats_c[0], mats_c[1], *filt_args)
        filt_l = _hyena_filters(seq_l, mats_l[0], mats_l[1], *filt_args)

        h0_c = jnp.zeros((n_cb, 2, d_rg), F32)
        x1_c, h2_c, idx_c, gw_c, st_c = _mixer_path(xc, None, n_cb, seq_c, h0_c, mod, group_ctx, p,
                                                    mats_c, filt_c, n_seq=min(8, n_cb))
        ctx_states.append(st_c)
        x1_l, h2_l, idx_l, gw_l, _ = _mixer_path(xs, pos if l == 0 else None, n_lb, seq_l,
                                                 state_rglru[:, l], mod, group_lat, p,
                                                 mats_l, filt_l, n_seq=1)

        idx_all = jnp.concatenate([idx_c, idx_l], axis=0)
        dest, blk_e, blk_valid, n_used = _route(idx_all, router_w.shape[-1], EXPERT_ROWS)
        dest_flat = dest.reshape(-1)
        sh = (sh_w_gate[l].astype(BF16), sh_w_up[l].astype(BF16), sh_w_down[l].astype(BF16))
        xb, shared = _dispatch(dest_flat, h2_c, h2_l, blk_e.shape[0] * EXPERT_ROWS, *sh)
        yb = _experts(blk_e, blk_valid, n_used, xb, exp_w_gate[l], exp_w_up[l], exp_w_down[l])
        xc = _finish(dest_flat, shared, x1_c, gw_c, yb, mod, final_g[None], 0, group_ctx, last)
        xs = _finish(dest_flat, shared, x1_l, gw_l, yb, mod, final_g[None], t_c, group_lat, last)

    new_state = jnp.stack(ctx_states, axis=1).astype(x_prompt.dtype)
    return (xc.reshape(n_cb, seq_c, d), xs.reshape(n_lb, seq_l, d), new_state)
```

```python
import functools
import math

import jax
import jax.numpy as jnp
from jax import lax
from jax.experimental import pallas as pl
from jax.experimental.pallas import tpu as pltpu

F32 = jnp.float32
BF16 = jnp.bfloat16

GRID_W = 64
RG_HEADS = 8
RG_CONV_W = 4
RG_C = 8.0
HY_CONV_W = 3
HY_BANDS = 16
TOP_K = 8
ROUTED_SCALE = 2.5
N_MOD = 6
EPS = 1e-6

LANES = 128
SUBLANES = 8
VMEM_LIMIT_BYTES = 56 * 1024 * 1024

EXPERT_ROWS = 288


def _params(n_axes, vmem=VMEM_LIMIT_BYTES):
    return pltpu.CompilerParams(dimension_semantics=("arbitrary",) * n_axes, vmem_limit_bytes=vmem)


def _rms(x, g):
    return x * lax.rsqrt(jnp.mean(x * x, axis=-1, keepdims=True) + EPS) * g


def _sigmoid(x):
    return 0.5 * jnp.tanh(0.5 * x) + 0.5


HI_HALF = -65536


def _pack_pair(lo, hi):
    lo_b = lax.bitcast_convert_type(lo.astype(BF16).astype(F32), jnp.int32)
    hi_b = lax.bitcast_convert_type(hi.astype(BF16).astype(F32), jnp.int32)
    return hi_b | lax.shift_right_logical(lo_b, 16)


def _unpack_pair(p):
    lo = lax.bitcast_convert_type(lax.shift_left(p, 16), F32)
    hi = lax.bitcast_convert_type(p & HI_HALF, F32)
    return lo, hi


def _store_token_tiles(ref, packed):
    m = packed.shape[0]
    for j in range(SUBLANES):
        ref[pl.ds(j, m, stride=SUBLANES), :] = packed[:, j * LANES:(j + 1) * LANES]


def _load_token_tiles(ref, row0, m):
    return jnp.concatenate(
        [ref[pl.ds(row0 * SUBLANES + j, m, stride=SUBLANES), :] for j in range(SUBLANES)], axis=1)


def _mod_kernel(c_ref, w_ref, b_ref, o_ref):
    c = c_ref[...]
    s = (c * jax.nn.sigmoid(c)).astype(BF16)
    o_ref[...] = jnp.dot(s, w_ref[...].astype(BF16), preferred_element_type=F32) + b_ref[...]


def _modulation(cvec, ada_w, ada_b):
    d, n = ada_w.shape
    tn = 1536
    return pl.pallas_call(
        _mod_kernel,
        grid=(n // tn,),
        in_specs=[pl.BlockSpec((SUBLANES, d), lambda j: (0, 0)),
                  pl.BlockSpec((d, tn), lambda j: (0, j)),
                  pl.BlockSpec((1, tn), lambda j: (0, j))],
        out_specs=pl.BlockSpec((SUBLANES, tn), lambda j: (0, j)),
        out_shape=jax.ShapeDtypeStruct((SUBLANES, n), F32),
        compiler_params=_params(1),
        name="modulation",
    )(cvec, ada_w, ada_b)


HALO = 16


def _inproj_kernel(*refs, has_pos, tn, n_rg, seq_len):
    if has_pos:
        (x_ref, xp_ref, xn_ref, pos_ref, pp_ref, pn_ref, mod_ref, g_ref, w_hbm, cw_ref, cb_ref,
         u_ref, hv_ref, w_vmem, h_scr, sem) = refs
    else:
        (x_ref, xp_ref, xn_ref, mod_ref, g_ref, w_hbm, cw_ref, cb_ref,
         u_ref, hv_ref, w_vmem, h_scr, sem) = refs
    i = pl.program_id(0)
    j = pl.program_id(1)
    tm = x_ref.shape[0]

    @pl.when(jnp.logical_and(i == 0, j == 0))
    def _():
        cp = pltpu.make_async_copy(w_hbm, w_vmem, sem)
        cp.start()
        cp.wait()

    @pl.when(j == 0)
    def _():
        def normed(x_r, p_r):
            x = x_r[...]
            if has_pos:
                x = x + p_r[...]
            return (_rms(x, g_ref[...]) * (1.0 + mod_ref[0, 1:2, :]) + mod_ref[0, 0:1, :]).astype(BF16)

        h_scr[0:HALO, :] = normed(xp_ref, pp_ref if has_pos else None)
        h_scr[HALO:HALO + tm, :] = normed(x_ref, pos_ref if has_pos else None)
        h_scr[HALO + tm:, :] = normed(xn_ref, pn_ref if has_pos else None)

    w = w_vmem[:, pl.ds(pl.multiple_of(j * tn, tn), tn)]

    @pl.when(j < n_rg)
    def _():
        u_ref[...] = jnp.dot(h_scr[HALO:HALO + tm, :], w, preferred_element_type=F32)

    @pl.when(j >= n_rg)
    def _():
        ue = jnp.dot(h_scr[...], w, preferred_element_type=F32)
        t_in_seq = (i * tm + lax.broadcasted_iota(jnp.int32, (tm, tn), 0)) & (seq_len - 1)
        cw = cw_ref[...]
        prev = jnp.where(t_in_seq == 0, 0.0, _shift_rows(ue, -1)[HALO:HALO + tm])
        nxt = jnp.where(t_in_seq == seq_len - 1, 0.0, _shift_rows(ue, 1)[HALO:HALO + tm])
        acc = cb_ref[...] + cw[0:1, :] * prev + cw[1:2, :] * ue[HALO:HALO + tm] + cw[2:3, :] * nxt
        hv_ref[...] = acc.astype(BF16)


def _in_proj(x2d, pos, mod, norm_g, w_bf, conv_w, conv_b, seq_len, group_of_tile, tm=512, tn=1024):
    t, d = x2d.shape
    n = w_bf.shape[1]
    n_hv = conv_w.shape[1]
    n_rg = (n - n_hv) // tn
    has_pos = pos is not None
    hb = tm // HALO
    last_h = t // HALO - 1

    def prev_blk(i):
        return jnp.maximum(i * hb - 1, 0)

    def next_blk(i):
        return jnp.minimum((i + 1) * hb, last_h)

    in_specs = [pl.BlockSpec((tm, d), lambda i, j: (i, 0)),
                pl.BlockSpec((HALO, d), lambda i, j: (prev_blk(i), 0)),
                pl.BlockSpec((HALO, d), lambda i, j: (next_blk(i), 0))]
    args = [x2d, x2d, x2d]
    if has_pos:
        pos_tiles = pos.shape[0] // tm
        last_p = pos.shape[0] // HALO - 1
        in_specs += [pl.BlockSpec((tm, d), lambda i, j: (i % pos_tiles, 0)),
                     pl.BlockSpec((HALO, d), lambda i, j: (jnp.maximum((i % pos_tiles) * hb - 1, 0), 0)),
                     pl.BlockSpec((HALO, d), lambda i, j: (jnp.minimum((i % pos_tiles + 1) * hb, last_p), 0))]
        args += [pos, pos, pos]
    in_specs += [pl.BlockSpec((1, N_MOD, d), lambda i, j: (group_of_tile(i, tm), 0, 0)),
                 pl.BlockSpec((1, d), lambda i, j: (0, 0)),
                 pl.BlockSpec(memory_space=pl.ANY),
                 pl.BlockSpec((HY_CONV_W, tn), lambda i, j: (0, jnp.maximum(j - n_rg, 0))),
                 pl.BlockSpec((1, tn), lambda i, j: (0, jnp.maximum(j - n_rg, 0)))]
    args += [mod, norm_g, w_bf, conv_w, conv_b]
    return pl.pallas_call(
        functools.partial(_inproj_kernel, has_pos=has_pos, tn=tn, n_rg=n_rg, seq_len=seq_len),
        grid=(t // tm, n // tn),
        in_specs=in_specs,
        out_specs=[pl.BlockSpec((tm, tn), lambda i, j: (i, jnp.minimum(j, n_rg - 1))),
                   pl.BlockSpec((tm, tn), lambda i, j: (i, jnp.maximum(j - n_rg, 0)))],
        out_shape=[jax.ShapeDtypeStruct((t, n - n_hv), F32), jax.ShapeDtypeStruct((t, n_hv), BF16)],
        scratch_shapes=[pltpu.VMEM((d, n), BF16), pltpu.VMEM((tm + 2 * HALO, d), BF16),
                        pltpu.SemaphoreType.DMA(())],
        compiler_params=_params(2),
        name="in_proj",
    )(*args)


def _shift_rows(win, off):
    if off == 0:
        return win
    n = win.shape[0]
    return pltpu.roll(win, (-off) % n, axis=0)


def _scan_chunk(a, b, reverse):
    n = a.shape[0]
    row = lax.broadcasted_iota(jnp.int32, a.shape, 0)
    dist = 1
    while dist < n:
        if reverse:
            a_s = pltpu.roll(a, n - dist, axis=0)
            b_s = pltpu.roll(b, n - dist, axis=0)
            m = row < n - dist
        else:
            a_s = pltpu.roll(a, dist, axis=0)
            b_s = pltpu.roll(b, dist, axis=0)
            m = row >= dist
        b = jnp.where(m, a * b_s + b, b)
        a = jnp.where(m, a * a_s, a)
        dist *= 2
    return a, b


def _rglru_kernel(xr_ref, gr_ref, cw_ref, cb_ref, w4_ref, b4_ref, lam_ref, h0_ref,
                  y_ref, st_ref, af, bf, ab, bb, hf, *, seq_len, n_seq, t1, tc):
    hd = xr_ref.shape[-1]
    nlam = -lam_ref[...]
    sp = jnp.maximum(nlam, 0.0) + jnp.log1p(jnp.exp(-jnp.abs(nlam)))
    cw = cw_ref[...]
    cb = cb_ref[...]
    b4 = b4_ref[0]
    nc1 = seq_len // t1
    ncs = seq_len // tc

    def per_seq(s, carry0):
        def gates(c, carry):
            r0 = pl.multiple_of(c * t1, t1)
            cur = xr_ref[s, pl.ds(r0, t1), :]
            p0 = pl.multiple_of(jnp.maximum(r0 - SUBLANES, 0), SUBLANES)
            n0 = pl.multiple_of(jnp.minimum(r0 + t1, seq_len - SUBLANES), SUBLANES)
            prev = jnp.where(c > 0, xr_ref[s, pl.ds(p0, SUBLANES), :], 0.0)
            nxt = jnp.where(c < nc1 - 1, xr_ref[s, pl.ds(n0, SUBLANES), :], 0.0)
            win = jnp.concatenate([prev, cur, nxt], axis=0)
            xr = cb
            for k in range(RG_CONV_W):
                xr = xr + cw[k:k + 1, :] * _shift_rows(win, k - RG_CONV_W // 2)[SUBLANES:SUBLANES + t1]
            z = jnp.dot(xr.astype(BF16), w4_ref[0], preferred_element_type=F32) + b4
            for d_i, (a_scr, b_scr) in enumerate(((af, bf), (ab, bb))):
                r = _sigmoid(z[:, (2 * d_i) * hd:(2 * d_i + 1) * hd])
                gi = _sigmoid(z[:, (2 * d_i + 1) * hd:(2 * d_i + 2) * hd])
                log_a = (-RG_C) * r * sp[d_i:d_i + 1, :]
                a = jnp.exp(log_a)
                a_scr[pl.ds(r0, t1), :] = a
                b_scr[pl.ds(r0, t1), :] = jnp.sqrt(-jnp.tanh(log_a) * (a * a + 1.0)) * (gi * xr)
            return carry

        lax.fori_loop(0, nc1, gates, 0)
        h0 = h0_ref[s]

        def fwd(c, carry):
            r0 = pl.multiple_of(c * tc, tc)
            a, h = _scan_chunk(af[pl.ds(r0, tc), :], bf[pl.ds(r0, tc), :], False)
            h = a * carry + h
            hf[pl.ds(r0, tc), :] = h
            return h[tc - 1:tc, :]

        s_f = lax.fori_loop(0, ncs, fwd, h0[0:1, :])

        def bwd(cc, carry):
            r0 = pl.multiple_of((ncs - 1 - cc) * tc, tc)
            a, h = _scan_chunk(ab[pl.ds(r0, tc), :], bb[pl.ds(r0, tc), :], True)
            h = a * carry + h
            g = gr_ref[s, pl.ds(r0, tc), :]
            y_ref[s, pl.ds(r0, tc), :] = jax.nn.gelu(g) * (hf[pl.ds(r0, tc), :] + h)
            return h[0:1, :]

        s_b = lax.fori_loop(0, ncs, bwd, h0[1:2, :])
        st_ref[s] = jnp.concatenate([s_f, s_b], axis=0)
        return carry0

    lax.fori_loop(0, n_seq, per_seq, 0)


def _rglru(u3, h0, conv_w, conv_b, w4, b4, lam, n_seq):
    b, seq_len, _ = u3.shape
    hd = w4.shape[1]
    d_rg = hd * RG_HEADS
    t1 = min(seq_len, 256)
    tc = 64
    kern = functools.partial(_rglru_kernel, seq_len=seq_len, n_seq=n_seq, t1=t1, tc=tc)
    return pl.pallas_call(
        kern,
        grid=(b // n_seq, RG_HEADS),
        in_specs=[pl.BlockSpec((n_seq, seq_len, hd), lambda i, h: (i, 0, h)),
                  pl.BlockSpec((n_seq, seq_len, hd), lambda i, h: (i, 0, RG_HEADS + h)),
                  pl.BlockSpec((RG_CONV_W, hd), lambda i, h: (0, h)),
                  pl.BlockSpec((1, hd), lambda i, h: (0, h)),
                  pl.BlockSpec((1, hd, 4 * hd), lambda i, h: (h, 0, 0)),
                  pl.BlockSpec((1, 1, 4 * hd), lambda i, h: (h, 0, 0)),
                  pl.BlockSpec((2, hd), lambda i, h: (0, h)),
                  pl.BlockSpec((n_seq, 2, hd), lambda i, h: (i, 0, h))],
        out_specs=[pl.BlockSpec((n_seq, seq_len, hd), lambda i, h: (i, 0, h)),
                   pl.BlockSpec((n_seq, 2, hd), lambda i, h: (i, 0, h))],
        out_shape=[jax.ShapeDtypeStruct((b, seq_len, d_rg), F32),
                   jax.ShapeDtypeStruct((b, 2, d_rg), F32)],
        scratch_shapes=[pltpu.VMEM((seq_len, hd), F32)] * 5,
        compiler_params=_params(2),
        name="rglru",
    )(u3, u3, conv_w, conv_b, w4, b4, lam, h0)


def _filt_time_kernel(w1_ref, b1_ref, fr_ref, w2_ref, b2_ref, w3_ref, dec_ref,
                      g_ref, d_ref, st_ref, *, seq_len, tl):
    i = pl.program_id(0)
    hi = lax.Precision.HIGHEST
    posi = i * tl + lax.broadcasted_iota(jnp.int32, (tl, LANES), 0)
    pos = posi.astype(F32)
    lane = lax.broadcasted_iota(jnp.int32, (tl, LANES), 1)
    band = jnp.where(lane <= HY_BANDS, lane, lane - HY_BANDS).astype(F32)
    ang = (2.0 * math.pi) * pos / seq_len * band
    t = pos / seq_len
    feats = jnp.where(lane == 0, t,
                      jnp.where(lane <= HY_BANDS, jnp.cos(ang),
                                jnp.where(lane <= 2 * HY_BANDS, -jnp.sin(ang), 0.0)))
    fr = fr_ref[...]
    hid = jnp.sin(fr * (jnp.dot(feats, w1_ref[...], precision=hi, preferred_element_type=F32) + b1_ref[...]))
    hid = jnp.sin(fr * (jnp.dot(hid, w2_ref[...], precision=hi, preferred_element_type=F32) + b2_ref[...]))
    k = jnp.dot(hid, w3_ref[...], precision=hi, preferred_element_type=F32)
    k = k * jnp.exp(-t[:, 0:1] * jnp.abs(dec_ref[...]))
    c = k.shape[1] // 4
    first = posi[:, 0:1] == 0
    sign = jnp.where(posi[:, 0:1] % 2 == 0, 1.0, -1.0)

    @pl.when(i == 0)
    def _():
        st_ref[...] = jnp.zeros_like(st_ref)

    for o in range(2):
        kf = k[:, (2 * o) * c:(2 * o + 1) * c]
        kb = jnp.where(first, 0.0, k[:, (2 * o + 1) * c:(2 * o + 2) * c])
        g = kf + kb
        g_ref[:, o * c:(o + 1) * c] = g.astype(BF16)
        d_ref[:, o * c:(o + 1) * c] = (kf - kb).astype(BF16)
        st_ref[0:1, o * c:(o + 1) * c] += jnp.sum(kf * kf + kb * kb, axis=0, keepdims=True)
        st_ref[1:2, o * c:(o + 1) * c] += jnp.sum(sign * g, axis=0, keepdims=True)


def _filt_dft_kernel(c_ref, s_ref, g_ref, d_ref, st_ref, kre_ref, q_ref):
    scale = lax.rsqrt(st_ref[0:1, :] + EPS)
    kre_ref[...] = jnp.dot(c_ref[...], g_ref[...], preferred_element_type=F32) * scale
    q_ref[...] = jnp.dot(s_ref[...], d_ref[...], preferred_element_type=F32) * scale


def _hyena_filters(seq_len, cmat, smat, w1, b1, freq, w2, b2, w3, decay):
    n_hid = w1.shape[1]
    n_out = w3.shape[1]
    c2 = n_out // 2
    tl = min(seq_len, 512)
    w1p = jnp.zeros((LANES, n_hid), F32).at[:w1.shape[0]].set(w1)
    g, d, stats = pl.pallas_call(
        functools.partial(_filt_time_kernel, seq_len=seq_len, tl=tl),
        grid=(seq_len // tl,),
        in_specs=[pl.BlockSpec((LANES, n_hid), lambda i: (0, 0)),
                  pl.BlockSpec((1, n_hid), lambda i: (0, 0)),
                  pl.BlockSpec((1, n_hid), lambda i: (0, 0)),
                  pl.BlockSpec((n_hid, n_hid), lambda i: (0, 0)),
                  pl.BlockSpec((1, n_hid), lambda i: (0, 0)),
                  pl.BlockSpec((n_hid, n_out), lambda i: (0, 0)),
                  pl.BlockSpec((1, n_out), lambda i: (0, 0))],
        out_specs=[pl.BlockSpec((tl, c2), lambda i: (i, 0)),
                   pl.BlockSpec((tl, c2), lambda i: (i, 0)),
                   pl.BlockSpec((SUBLANES, c2), lambda i: (0, 0))],
        out_shape=[jax.ShapeDtypeStruct((seq_len, c2), BF16),
                   jax.ShapeDtypeStruct((seq_len, c2), BF16),
                   jax.ShapeDtypeStruct((SUBLANES, c2), F32)],
        compiler_params=_params(1),
        name="hyena_filter_taps",
    )(w1p, b1, freq, w2, b2, w3, decay)
    tm = min(seq_len, 512)
    tn = 512
    kre, q = pl.pallas_call(
        _filt_dft_kernel,
        grid=(c2 // tn, seq_len // tm),
        in_specs=[pl.BlockSpec((tm, seq_len), lambda n, m: (m, 0)),
                  pl.BlockSpec((tm, seq_len), lambda n, m: (m, 0)),
                  pl.BlockSpec((seq_len, tn), lambda n, m: (0, n)),
                  pl.BlockSpec((seq_len, tn), lambda n, m: (0, n)),
                  pl.BlockSpec((SUBLANES, tn), lambda n, m: (0, n))],
        out_specs=[pl.BlockSpec((tm, tn), lambda n, m: (m, n)),
                   pl.BlockSpec((tm, tn), lambda n, m: (m, n))],
        out_shape=[jax.ShapeDtypeStruct((seq_len, c2), F32)] * 2,
        compiler_params=_params(2),
        name="hyena_filter_dft",
    )(cmat, smat, g, d, stats)
    return kre, q, stats


def _dft_matrices(seq_len):
    assert seq_len & (seq_len - 1) == 0
    tm = min(seq_len, 256)
    out = jax.ShapeDtypeStruct((seq_len, seq_len), BF16)
    return pl.pallas_call(
        functools.partial(_dft_table_kernel, seq_len=seq_len),
        grid=(seq_len // tm,),
        in_specs=[],
        out_specs=[pl.BlockSpec((tm, seq_len), lambda i: (i, 0))] * 3,
        out_shape=[out] * 3,
        scratch_shapes=[pltpu.VMEM((tm, seq_len), F32)] * 2,
        compiler_params=_params(1),
        name="dft_tables",
    )()


def _dft_table_kernel(c_ref, s_ref, st_ref, c0, s0, *, seq_len):
    i = pl.program_id(0)
    tm = c_ref.shape[0]
    wrap = 2 * seq_len - 1
    row = lax.broadcasted_iota(jnp.int32, (tm, seq_len), 0)
    col = lax.broadcasted_iota(jnp.int32, (tm, seq_len), 1)

    @pl.when(i == 0)
    def _():
        ang = ((row * col) & wrap).astype(F32) * (math.pi / seq_len)
        c0[...] = jnp.cos(ang)
        s0[...] = jnp.sin(ang)

    k0 = i * tm
    ang0 = ((k0 * col[0:1, :]) & wrap).astype(F32) * (math.pi / seq_len)
    cn = jnp.cos(ang0)
    sn = jnp.sin(ang0)
    cmat = c0[...] * cn - s0[...] * sn
    smat = s0[...] * cn + c0[...] * sn
    alt_col = jnp.where((col & 1) == 0, 1.0, -1.0)
    alt_row = jnp.where(((row + k0) & 1) == 0, 1.0, -1.0)
    c_ref[...] = cmat.astype(BF16)
    s_ref[...] = jnp.where(row + k0 == 0, alt_col, smat).astype(BF16)
    st_ref[...] = jnp.where(col == 0, alt_row, smat).astype(BF16)


def _hy_fwd_kernel(c_ref, s_ref, u_ref, kre_ref, q_ref, st_ref, pre_ref, pm_ref, *, seq_len):
    m = pl.program_id(2)
    u = u_ref[0]
    a = jnp.dot(c_ref[...], u, preferred_element_type=F32)
    bv = jnp.dot(s_ref[...], u, preferred_element_type=F32)
    tm = a.shape[0]
    is0 = (m * tm + lax.broadcasted_iota(jnp.int32, a.shape, 0)) == 0
    k_nyq = st_ref[1:2, :] * lax.rsqrt(st_ref[0:1, :] + EPS)
    kre = kre_ref[...]
    q = jnp.where(is0, 0.0, q_ref[...])
    kre_b = jnp.where(is0, k_nyq, kre)
    wk = jnp.where(is0, 0.5 / seq_len, 1.0 / seq_len)
    pre_ref[0] = ((a * kre - bv * q) * wk).astype(BF16)
    pm_ref[0] = ((a * q + bv * kre_b) * wk).astype(BF16)


def _hy_inv_kernel(c_ref, st_ref, pre_ref, pm_ref, u_ref, x_ref, bias_ref, z_ref):
    y = jnp.dot(c_ref[...], pre_ref[0], preferred_element_type=F32)
    y = y + jnp.dot(st_ref[...], pm_ref[0], preferred_element_type=F32)
    y = y + u_ref[0].astype(F32) * bias_ref[...]
    z_ref[0] = (y * x_ref[0].astype(F32)).astype(BF16)


def _hyena_order(order, u_arr, u_cb, hvc, gate_cb, mats, filt, bias, tm, tc):
    cmat, smat, smat_t = mats
    kre, q, stats = filt
    b, seq_len, _ = hvc.shape
    c = kre.shape[1] // 2
    nct = c // tc
    grid = (b, nct, seq_len // tm)
    pre, pm = pl.pallas_call(
        functools.partial(_hy_fwd_kernel, seq_len=seq_len),
        grid=grid,
        in_specs=[pl.BlockSpec((tm, seq_len), lambda i, n, m: (m, 0)),
                  pl.BlockSpec((tm, seq_len), lambda i, n, m: (m, 0)),
                  pl.BlockSpec((1, seq_len, tc), lambda i, n, m: (i, 0, u_cb * nct + n)),
                  pl.BlockSpec((tm, tc), lambda i, n, m: (m, order * nct + n)),
                  pl.BlockSpec((tm, tc), lambda i, n, m: (m, order * nct + n)),
                  pl.BlockSpec((SUBLANES, tc), lambda i, n, m: (0, order * nct + n))],
        out_specs=[pl.BlockSpec((1, tm, tc), lambda i, n, m: (i, m, n))] * 2,
        out_shape=[jax.ShapeDtypeStruct((b, seq_len, c), BF16)] * 2,
        compiler_params=_params(3),
        name="hyena_fwd_dft",
    )(cmat, smat, u_arr, kre, q, stats)
    return pl.pallas_call(
        _hy_inv_kernel,
        grid=grid,
        in_specs=[pl.BlockSpec((tm, seq_len), lambda i, n, m: (m, 0)),
                  pl.BlockSpec((tm, seq_len), lambda i, n, m: (m, 0)),
                  pl.BlockSpec((1, seq_len, tc), lambda i, n, m: (i, 0, n)),
                  pl.BlockSpec((1, seq_len, tc), lambda i, n, m: (i, 0, n)),
                  pl.BlockSpec((1, tm, tc), lambda i, n, m: (i, m, u_cb * nct + n)),
                  pl.BlockSpec((1, tm, tc), lambda i, n, m: (i, m, gate_cb * nct + n)),
                  pl.BlockSpec((1, tc), lambda i, n, m: (0, n))],
        out_specs=pl.BlockSpec((1, tm, tc), lambda i, n, m: (i, m, n)),
        out_shape=jax.ShapeDtypeStruct((b, seq_len, c), BF16),
        compiler_params=_params(3),
        name="hyena_inv_dft",
    )(cmat, smat_t, pre, pm, u_arr, hvc, bias[order][None])


def _post_mixer_kernel(*refs, has_pos):
    if has_pos:
        (yrg_ref, yhy_ref, x_ref, pos_ref, mod_ref, gnr_ref, gnh_ref, wo_ref, n2_ref,
         rw_ref, rb_ref, x1_ref, h2_ref, idx_ref, gw_ref) = refs
    else:
        (yrg_ref, yhy_ref, x_ref, mod_ref, gnr_ref, gnh_ref, wo_ref, n2_ref,
         rw_ref, rb_ref, x1_ref, h2_ref, idx_ref, gw_ref) = refs
    d_rg = yrg_ref.shape[1]
    na = _rms(yrg_ref[...], gnr_ref[...]).astype(BF16)
    nb = _rms(yhy_ref[...].astype(F32), gnh_ref[...]).astype(BF16)
    y = jnp.dot(na, wo_ref[0:d_rg, :], preferred_element_type=F32)
    y = y + jnp.dot(nb, wo_ref[d_rg:, :], preferred_element_type=F32)
    x = x_ref[...]
    if has_pos:
        x = x + pos_ref[...]
    x1 = x + mod_ref[0, 2:3, :] * y
    x1_ref[...] = x1
    h2f = _rms(x1, n2_ref[...]) * (1.0 + mod_ref[0, 4:5, :]) + mod_ref[0, 3:4, :]
    half = h2f.shape[1] // 2
    _store_token_tiles(h2_ref, _pack_pair(h2f[:, :half], h2f[:, half:]))
    h2 = h2f.astype(BF16)
    scores = jax.nn.sigmoid(jnp.dot(h2, rw_ref[...], preferred_element_type=F32))
    sel = scores + rb_ref[...]
    n_exp = scores.shape[1]
    lane = lax.broadcasted_iota(jnp.int32, scores.shape, 1)
    col = lax.broadcasted_iota(jnp.int32, idx_ref.shape, 1)
    idx_acc = jnp.zeros(idx_ref.shape, jnp.int32)
    gw_acc = jnp.zeros(gw_ref.shape, F32)
    for k in range(TOP_K):
        mx = jnp.max(sel, axis=1, keepdims=True)
        pick = jnp.min(jnp.where(sel == mx, lane, n_exp), axis=1, keepdims=True)
        hit = lane == pick
        val = jnp.sum(jnp.where(hit, scores, 0.0), axis=1, keepdims=True)
        sel = jnp.where(hit, -jnp.inf, sel)
        idx_acc = jnp.where(col == k, pick, idx_acc)
        gw_acc = jnp.where(col == k, val, gw_acc)
    idx_ref[...] = idx_acc
    gw_ref[...] = gw_acc / jnp.sum(gw_acc, axis=1, keepdims=True) * ROUTED_SCALE


def _post_mixer(y_rg, y_hy, x2d, pos, mod, gn_rg, gn_hy, w_out_bf, norm2_g, router_w_bf, router_b,
                group_of_tile, tm=256):
    t, d = x2d.shape
    d_rg = y_rg.shape[1]
    d_hy = y_hy.shape[1]
    n_exp = router_w_bf.shape[1]
    has_pos = pos is not None
    in_specs = [pl.BlockSpec((tm, d_rg), lambda i: (i, 0)),
                pl.BlockSpec((tm, d_hy), lambda i: (i, 0)),
                pl.BlockSpec((tm, d), lambda i: (i, 0))]
    args = [y_rg, y_hy, x2d]
    if has_pos:
        pos_tiles = pos.shape[0] // tm
        in_specs.append(pl.BlockSpec((tm, d), lambda i: (i % pos_tiles, 0)))
        args.append(pos)
    in_specs += [pl.BlockSpec((1, N_MOD, d), lambda i: (group_of_tile(i, tm), 0, 0)),
                 pl.BlockSpec((1, d_rg), lambda i: (0, 0)),
                 pl.BlockSpec((1, d_hy), lambda i: (0, 0)),
                 pl.BlockSpec((d_rg + d_hy, d), lambda i: (0, 0)),
                 pl.BlockSpec((1, d), lambda i: (0, 0)),
                 pl.BlockSpec((d, n_exp), lambda i: (0, 0)),
                 pl.BlockSpec((1, n_exp), lambda i: (0, 0))]
    args += [mod, gn_rg, gn_hy, w_out_bf, norm2_g, router_w_bf, router_b]
    return pl.pallas_call(
        functools.partial(_post_mixer_kernel, has_pos=has_pos),
        grid=(t // tm,),
        in_specs=in_specs,
        out_specs=[pl.BlockSpec((tm, d), lambda i: (i, 0)),
                   pl.BlockSpec((tm * SUBLANES, LANES), lambda i: (i, 0)),
                   pl.BlockSpec((tm, TOP_K), lambda i: (i, 0)),
                   pl.BlockSpec((tm, TOP_K), lambda i: (i, 0))],
        out_shape=[jax.ShapeDtypeStruct((t, d), F32),
                   jax.ShapeDtypeStruct((t * SUBLANES, LANES), jnp.int32),
                   jax.ShapeDtypeStruct((t, TOP_K), jnp.int32),
                   jax.ShapeDtypeStruct((t, TOP_K), F32)],
        compiler_params=_params(1),
        name="post_mixer",
    )(*args)


def _route_kernel(idx_ref, dest_ref, be_ref, bv_ref, nu_ref, tri, cnt, base, pst, *, blk, n_blk):
    p = pl.program_id(0)
    i = pl.program_id(1)
    tm, top_k = idx_ref.shape
    ne = cnt.shape[1]

    def div_blk(n):
        return jnp.floor((n + 0.5) / blk)

    @pl.when(jnp.logical_and(p == 0, i == 0))
    def _():
        r = lax.broadcasted_iota(jnp.int32, (tm, tm), 0)
        c = lax.broadcasted_iota(jnp.int32, (tm, tm), 1)
        tri[...] = jnp.where(r > c, 1.0, 0.0).astype(BF16)
        cnt[...] = jnp.zeros_like(cnt)

    idx = idx_ref[...]
    lane = lax.broadcasted_iota(jnp.int32, (tm, ne), 1)
    hits = [lane == idx[:, k:k + 1] for k in range(top_k)]
    occ = jnp.zeros((tm, ne), F32)
    for h in hits:
        occ = occ + jnp.where(h, 1.0, 0.0)
    col_sum = jnp.sum(occ, axis=0, keepdims=True)

    @pl.when(p == 0)
    def _():
        cnt[0:1, :] += col_sum

    @pl.when(jnp.logical_and(p == 1, i == 0))
    def _():
        counts = cnt[...]
        padded = div_blk(counts + (blk - 1.0)) * blk
        r = lax.broadcasted_iota(jnp.int32, (ne, ne), 0)
        c = lax.broadcasted_iota(jnp.int32, (ne, ne), 1)
        upper = jnp.where(r <= c, 1.0, 0.0)
        pend = jnp.dot(padded, upper, precision=lax.Precision.HIGHEST, preferred_element_type=F32)
        pstart = pend - padded
        pst[...] = pstart
        base[...] = jnp.zeros_like(base)
        b0 = (lax.broadcasted_iota(jnp.int32, (n_blk, ne), 0) * blk).astype(F32)
        be = jnp.sum(jnp.where(pend[0:1, :] <= b0, 1.0, 0.0), axis=1, keepdims=True)
        be = jnp.minimum(be, ne - 1.0)
        own = lax.broadcasted_iota(jnp.int32, (n_blk, ne), 1).astype(F32) == be
        pst_b = jnp.sum(jnp.where(own, pstart[0:1, :], 0.0), axis=1, keepdims=True)
        cnt_b = jnp.sum(jnp.where(own, counts[0:1, :], 0.0), axis=1, keepdims=True)
        valid = jnp.clip(cnt_b - (b0[:, 0:1] - pst_b), 0.0, float(blk))
        be_ref[...] = be.astype(jnp.int32)
        bv_ref[...] = valid.astype(jnp.int32)
        total = jnp.max(pend[0:1, :], axis=1, keepdims=True)
        nu_ref[...] = jnp.broadcast_to(div_blk(total).astype(jnp.int32), nu_ref.shape)

    @pl.when(p == 1)
    def _():
        cum = jnp.dot(tri[...], occ.astype(BF16), preferred_element_type=F32) + base[0:1, :] + pst[0:1, :]
        col = lax.broadcasted_iota(jnp.int32, (tm, top_k), 1)
        acc = jnp.zeros((tm, top_k), F32)
        for k in range(top_k):
            v = jnp.sum(jnp.where(hits[k], cum, 0.0), axis=1, keepdims=True)
            acc = jnp.where(col == k, v, acc)
        dest_ref[...] = acc.astype(jnp.int32)
        base[0:1, :] += col_sum


def _route(idx, n_exp, blk, tm=512):
    n_tok, top_k = idx.shape
    n_blk = -(-n_tok * top_k // blk) + n_exp
    dest, be, bv, nu = pl.pallas_call(
        functools.partial(_route_kernel, blk=blk, n_blk=n_blk),
        grid=(2, n_tok // tm),
        in_specs=[pl.BlockSpec((tm, top_k), lambda p, i: (i, 0))],
        out_specs=[pl.BlockSpec((tm, top_k), lambda p, i: (i * p, 0)),
                   pl.BlockSpec((n_blk, 1), lambda p, i: (0, 0)),
                   pl.BlockSpec((n_blk, 1), lambda p, i: (0, 0)),
                   pl.BlockSpec((SUBLANES, LANES), lambda p, i: (0, 0))],
        out_shape=[jax.ShapeDtypeStruct((n_tok, top_k), jnp.int32),
                   jax.ShapeDtypeStruct((n_blk, 1), jnp.int32),
                   jax.ShapeDtypeStruct((n_blk, 1), jnp.int32),
                   jax.ShapeDtypeStruct((SUBLANES, LANES), jnp.int32)],
        scratch_shapes=[pltpu.VMEM((tm, tm), BF16), pltpu.VMEM((SUBLANES, n_exp), F32),
                        pltpu.VMEM((SUBLANES, n_exp), F32), pltpu.VMEM((SUBLANES, n_exp), F32)],
        compiler_params=_params(2),
        name="route",
    )(idx)
    return dest, be.reshape(n_blk), bv.reshape(n_blk), nu[0, 0:1]


def _dispatch_kernel(dest_ref, xa_ref, xl_ref, sg_ref, su_ref, sd_ref, out_ref, sh_ref, sem, *, n_a):
    tm = xa_ref.shape[0] // SUBLANES
    dh = SUBLANES * LANES

    def tile(ref, r):
        return ref.at[pl.ds(pl.multiple_of(r * SUBLANES, SUBLANES), SUBLANES), :]

    def run(src):
        def issue(t, c):
            for k in range(TOP_K):
                pltpu.make_async_copy(tile(src, t), tile(out_ref, dest_ref[t * TOP_K + k]), sem).start(priority=k % 2)
            return c

        lax.fori_loop(0, tm, issue, 0)

        lo, hi = _unpack_pair(_load_token_tiles(src, 0, tm))
        lo = lo.astype(BF16)
        hi = hi.astype(BF16)
        g = jnp.dot(lo, sg_ref[0:dh, :], preferred_element_type=F32)
        g = g + jnp.dot(hi, sg_ref[dh:, :], preferred_element_type=F32)
        u = jnp.dot(lo, su_ref[0:dh, :], preferred_element_type=F32)
        u = u + jnp.dot(hi, su_ref[dh:, :], preferred_element_type=F32)
        hmid = (g * _sigmoid(g) * u).astype(BF16)
        sh_ref[...] = jnp.dot(hmid, sd_ref[...], preferred_element_type=F32).astype(BF16)

        def drain(t, c):
            for k in range(TOP_K):
                pltpu.make_async_copy(tile(src, 0), tile(out_ref, 0), sem).wait()
            return c

        lax.fori_loop(0, tm, drain, 0, unroll=8)

    @pl.when(pl.program_id(0) < n_a)
    def _():
        run(xa_ref)

    @pl.when(pl.program_id(0) >= n_a)
    def _():
        run(xl_ref)


def _dispatch(dest_flat, h2p_a, h2p_l, n_rows, sg_bf, su_bf, sd_bf, tm=256):
    n_a = h2p_a.shape[0] // (tm * SUBLANES)
    n_l = h2p_l.shape[0] // (tm * SUBLANES)
    d, ds_ = sg_bf.shape
    return pl.pallas_call(
        functools.partial(_dispatch_kernel, n_a=n_a),
        grid=(n_a + n_l,),
        in_specs=[pl.BlockSpec((tm * TOP_K,), lambda i: (i,), memory_space=pltpu.SMEM),
                  pl.BlockSpec((tm * SUBLANES, LANES), lambda i: (jnp.minimum(i, n_a - 1), 0)),
                  pl.BlockSpec((tm * SUBLANES, LANES), lambda i: (jnp.maximum(i - n_a, 0), 0)),
                  pl.BlockSpec((d, ds_), lambda i: (0, 0)),
                  pl.BlockSpec((d, ds_), lambda i: (0, 0)),
                  pl.BlockSpec((ds_, d), lambda i: (0, 0))],
        out_specs=[pl.BlockSpec(memory_space=pl.ANY),
                   pl.BlockSpec((tm, d), lambda i: (i, 0))],
        out_shape=[jax.ShapeDtypeStruct((n_rows * SUBLANES, LANES), jnp.int32),
                   jax.ShapeDtypeStruct(((n_a + n_l) * tm, d), BF16)],
        scratch_shapes=[pltpu.SemaphoreType.DMA(())],
        compiler_params=_params(1),
        name="dispatch",
    )(dest_flat, h2p_a, h2p_l, sg_bf, su_bf, sd_bf)


def _expert_kernel(be_ref, bv_ref, nu_ref, x_ref, wg_hbm, wu_hbm, wd_hbm, o_ref,
                   wg_f, wu_f, wd_f, wg_b, wu_b, wd_b, sem, grp, *, n_blk):
    i = pl.program_id(0)
    n_used = nu_ref[0]
    e = be_ref[i]
    active = i < n_used
    changed = jnp.logical_or(i == 0, e != be_ref[jnp.maximum(i - 1, 0)])

    def expert_at(j):
        return be_ref[jnp.minimum(j, n_blk - 1)]

    def next_group(j):
        ej = expert_at(j)
        return lax.while_loop(lambda q: jnp.logical_and(q < n_used, expert_at(q) == ej), lambda q: q + 1, j + 1)

    def copies(ex, slot):
        return (pltpu.make_async_copy(wg_hbm.at[ex], wg_f.at[slot], sem.at[slot, 0]),
                pltpu.make_async_copy(wu_hbm.at[ex], wu_f.at[slot], sem.at[slot, 1]),
                pltpu.make_async_copy(wd_hbm.at[ex], wd_f.at[slot], sem.at[slot, 2]))

    @pl.when(jnp.logical_and(active, i == 0))
    def _():
        grp[0] = 0
        for cp in copies(e, 0):
            cp.start()
        n1 = next_group(i)

        @pl.when(n1 < n_used)
        def _():
            for cp in copies(expert_at(n1), 1):
                cp.start()

    tm = x_ref.shape[0] // SUBLANES
    dh = SUBLANES * LANES
    d_model = wg_b.shape[0]
    kc = 512

    def load_rows():
        live = lax.broadcasted_iota(jnp.int32, (tm, dh), 0) < bv_ref[i]
        lo, hi = _unpack_pair(_load_token_tiles(x_ref, 0, tm))
        lo = jnp.where(live, lo, 0.0).astype(BF16)
        hi = jnp.where(live, hi, 0.0).astype(BF16)
        return lo, hi

    def store_rows(y):
        _store_token_tiles(o_ref, _pack_pair(y[:, :dh], y[:, dh:]))

    @pl.when(jnp.logical_and(active, changed))
    def _():
        slot = grp[0] % 2
        cur = copies(e, slot)
        n2 = next_group(next_group(i))
        more = n2 < n_used
        nxt = copies(expert_at(n2), slot)
        halves = load_rows()
        cur[0].wait()
        cur[1].wait()
        g = jnp.zeros((tm, wg_b.shape[1]), F32)
        u = jnp.zeros((tm, wg_b.shape[1]), F32)
        for c0 in range(0, d_model, kc):
            xc = halves[c0 // dh][:, c0 % dh:c0 % dh + kc]
            wgc = wg_f[slot, c0:c0 + kc, :].astype(BF16)
            wg_b[c0:c0 + kc, :] = wgc
            g = g + jnp.dot(xc, wgc, preferred_element_type=F32)
            wuc = wu_f[slot, c0:c0 + kc, :].astype(BF16)
            wu_b[c0:c0 + kc, :] = wuc
            u = u + jnp.dot(xc, wuc, preferred_element_type=F32)

        @pl.when(more)
        def _():
            nxt[0].start()
            nxt[1].start()

        hmid = (g * _sigmoid(g) * u).astype(BF16)
        cur[2].wait()
        ys = []
        for c0 in range(0, d_model, kc):
            wdc = wd_f[slot, :, c0:c0 + kc].astype(BF16)
            wd_b[:, c0:c0 + kc] = wdc
            ys.append(jnp.dot(hmid, wdc, preferred_element_type=F32))

        @pl.when(more)
        def _():
            nxt[2].start()

        store_rows(jnp.concatenate(ys, axis=1))
        grp[0] = grp[0] + 1

    @pl.when(jnp.logical_and(active, jnp.logical_not(changed)))
    def _():
        lo, hi = load_rows()
        g = jnp.dot(lo, wg_b[0:dh, :], preferred_element_type=F32)
        g = g + jnp.dot(hi, wg_b[dh:, :], preferred_element_type=F32)
        u = jnp.dot(lo, wu_b[0:dh, :], preferred_element_type=F32)
        u = u + jnp.dot(hi, wu_b[dh:, :], preferred_element_type=F32)
        hmid = (g * _sigmoid(g) * u).astype(BF16)
        store_rows(jnp.dot(hmid, wd_b[...], preferred_element_type=F32))


def _experts(blk_e, blk_valid, n_used, xb, wg, wu, wd):
    n_exp, d, de = wg.shape
    tm = EXPERT_ROWS
    n_blk = xb.shape[0] // (tm * SUBLANES)
    grid_spec = pltpu.PrefetchScalarGridSpec(
        num_scalar_prefetch=3,
        grid=(n_blk,),
        in_specs=[pl.BlockSpec((tm * SUBLANES, LANES), lambda i, be, bv, nu: (jnp.minimum(i, nu[0] - 1), 0)),
                  pl.BlockSpec(memory_space=pl.ANY),
                  pl.BlockSpec(memory_space=pl.ANY),
                  pl.BlockSpec(memory_space=pl.ANY)],
        out_specs=pl.BlockSpec((tm * SUBLANES, LANES), lambda i, be, bv, nu: (jnp.minimum(i, nu[0] - 1), 0)),
        scratch_shapes=[pltpu.VMEM((2, d, de), F32), pltpu.VMEM((2, d, de), F32), pltpu.VMEM((2, de, d), F32),
                        pltpu.VMEM((d, de), BF16), pltpu.VMEM((d, de), BF16), pltpu.VMEM((de, d), BF16),
                        pltpu.SemaphoreType.DMA((2, 3)), pltpu.SMEM((1,), jnp.int32)],
    )
    return pl.pallas_call(
        functools.partial(_expert_kernel, n_blk=n_blk),
        grid_spec=grid_spec,
        out_shape=jax.ShapeDtypeStruct(xb.shape, jnp.int32),
        compiler_params=_params(1),
        name="routed_experts",
    )(blk_e, blk_valid, n_used, xb, wg, wu, wd)


FINISH_ROWS = 64


def _finish_kernel(dcur_ref, dnxt_ref, sh_ref, x1_ref, gw_ref, mod_ref, fg_ref,
                   yb_hbm, o_ref, gbuf, gwb, sem, *, final_norm):
    i = pl.program_id(0)
    n_tiles = pl.num_programs(0)
    tm, d = x1_ref.shape
    dh = SUBLANES * LANES
    slot = i % 2

    def row_copy(row, t, k, s):
        src = yb_hbm.at[pl.ds(pl.multiple_of(row * SUBLANES, SUBLANES), SUBLANES), :]
        dst = gbuf.at[s, pl.ds(pl.multiple_of((k * tm + t) * SUBLANES, SUBLANES), SUBLANES), :]
        return pltpu.make_async_copy(src, dst, sem.at[s])

    def gather(d_ref, s):
        def issue(t, c):
            for k in range(TOP_K):
                row_copy(d_ref[t * TOP_K + k], t, k, s).start(priority=k % 2)
            return c

        lax.fori_loop(0, tm, issue, 0)

    @pl.when(i == 0)
    def _():
        gather(dcur_ref, 0)

    @pl.when(i + 1 < n_tiles)
    def _():
        gather(dnxt_ref, 1 - slot)

    gw = gw_ref[...]
    for k in range(TOP_K):
        gwb[k] = jnp.broadcast_to(gw[:, k:k + 1], (tm, LANES))

    def drain(t, c):
        for k in range(TOP_K):
            row_copy(0, 0, 0, slot).wait()
        return c

    lax.fori_loop(0, tm, drain, 0, unroll=8)

    rows_g = gbuf.at[slot]
    for r0 in range(0, tm, FINISH_ROWS):
        rs = slice(r0, r0 + FINISH_ROWS)
        ssq = jnp.zeros((FINISH_ROWS, 1), F32)
        for j in range(SUBLANES):
            cols = (slice(j * LANES, (j + 1) * LANES), slice(dh + j * LANES, dh + (j + 1) * LANES))
            acc = [sh_ref[rs, c].astype(F32) for c in cols]
            for k in range(TOP_K):
                halves = _unpack_pair(rows_g[pl.ds((k * tm + r0) * SUBLANES + j, FINISH_ROWS, stride=SUBLANES), :])
                w = gwb[k, rs, :]
                acc = [a + w * h for a, h in zip(acc, halves)]
            for c, a in zip(cols, acc):
                x2 = x1_ref[rs, c] + mod_ref[0, 5:6, c] * a
                o_ref[rs, c] = x2
                ssq = ssq + jnp.sum(x2 * x2, axis=-1, keepdims=True)
        if final_norm:
            inv = lax.rsqrt(ssq / d + EPS)
            o_ref[rs, :] = o_ref[rs, :] * inv * fg_ref[...]


def _finish(dest_flat, shared, x1, gw, yb, mod, final_g, row0, group_of_tile, final_norm, tm=256):
    n_rows, d = x1.shape
    t0 = row0 // tm
    n_tiles = n_rows // tm
    return pl.pallas_call(
        functools.partial(_finish_kernel, final_norm=final_norm),
        grid=(n_tiles,),
        in_specs=[pl.BlockSpec((tm * TOP_K,), lambda i: (t0 + i,), memory_space=pltpu.SMEM),
                  pl.BlockSpec((tm * TOP_K,), lambda i: (t0 + jnp.minimum(i + 1, n_tiles - 1),),
                               memory_space=pltpu.SMEM),
                  pl.BlockSpec((tm, d), lambda i: (t0 + i, 0)),
                  pl.BlockSpec((tm, d), lambda i: (i, 0)),
                  pl.BlockSpec((tm, TOP_K), lambda i: (i, 0)),
                  pl.BlockSpec((1, N_MOD, d), lambda i: (group_of_tile(i, tm), 0, 0)),
                  pl.BlockSpec((1, d), lambda i: (0, 0)),
                  pl.BlockSpec(memory_space=pl.ANY)],
        out_specs=pl.BlockSpec((tm, d), lambda i: (i, 0)),
        out_shape=jax.ShapeDtypeStruct((n_rows, d), F32),
        scratch_shapes=[pltpu.VMEM((2, TOP_K * tm * SUBLANES, LANES), jnp.int32),
                        pltpu.VMEM((TOP_K, tm, LANES), F32), pltpu.SemaphoreType.DMA((2,))],
        compiler_params=_params(1),
        name="finish",
    )(dest_flat, dest_flat, shared, x1, gw, mod, final_g, yb)


def _grid_pos_emb(rows, d):
    quarter = d // 4
    omega = 1.0 / (10000.0 ** (jnp.arange(quarter, dtype=F32) / quarter))
    r = jnp.arange(rows, dtype=F32)[:, None] * omega
    cc = jnp.arange(GRID_W, dtype=F32)[:, None] * omega
    by_row = jnp.concatenate([jnp.sin(r), jnp.cos(r)], axis=-1)
    by_col = jnp.concatenate([jnp.sin(cc), jnp.cos(cc)], axis=-1)
    full = jnp.concatenate([jnp.broadcast_to(by_row[:, None, :], (rows, GRID_W, d // 2)),
                            jnp.broadcast_to(by_col[None, :, :], (rows, GRID_W, d // 2))], axis=-1)
    return full.reshape(rows * GRID_W, d)


def _mixer_path(x2d, pos, n_b, seq_len, h0, mod, group_of_tile, p, mats, filt, n_seq):
    d_rg = p['gn_rg'].shape[1]
    c = p['gn_hy'].shape[1]
    u_rg, hvc = _in_proj(x2d, pos, mod, p['norm1_g'], p['w_in'], p['hy_conv_w'], p['hy_conv_b'],
                         seq_len, group_of_tile)
    u3 = u_rg.reshape(n_b, seq_len, u_rg.shape[1])
    hvc = hvc.reshape(n_b, seq_len, hvc.shape[1])
    y_rg, st = _rglru(u3, h0, p['rg_conv_w'], p['rg_conv_b'], p['rg_w4'], p['rg_b4'], p['rg_lam'], n_seq)
    tm = min(seq_len, 512)
    tc = 512 if seq_len > 512 else c
    z1 = _hyena_order(0, hvc, 0, hvc, 1, mats, filt, p['hy_bias'], tm, tc)
    y_hy = _hyena_order(1, z1, 0, hvc, 2, mats, filt, p['hy_bias'], tm, tc)
    x1, h2, idx, gw = _post_mixer(y_rg.reshape(-1, d_rg), y_hy.reshape(-1, c), x2d, pos, mod,
                                  p['gn_rg'], p['gn_hy'], p['w_out'], p['norm2_g'],
                                  p['router_w'], p['router_b'], group_of_tile)
    return x1, h2, idx, gw, st


def kernel(x_prompt, x_sample, state_rglru, c, c_ctx, ada_w, ada_b, norm1_g, norm2_g, w_in, rg_conv_w, rg_conv_b, rg_wa, rg_ba, rg_wx, rg_bx, rg_lam, hy_conv_w, hy_conv_b, hy_w1, hy_b1, hy_freq, hy_w2, hy_b2, hy_w3, hy_decay, hy_bias, gn_rg, gn_hy, w_out, router_w, router_b, exp_w_gate, exp_w_up, exp_w_down, sh_w_gate, sh_w_up, sh_w_down, final_g):
    n_cb, seq_c, d = x_prompt.shape
    n_lb, seq_l, _ = x_sample.shape
    depth = ada_w.shape[0]
    d_rg = gn_rg.shape[1]
    hd = d_rg // RG_HEADS
    t_c = n_cb * seq_c
    t_l = n_lb * seq_l
    assert n_lb + 1 <= SUBLANES
    assert d == 2 * SUBLANES * LANES

    pos = _grid_pos_emb(seq_l // GRID_W, d)
    cvec = jnp.zeros((SUBLANES, d), F32).at[0].set(c_ctx).at[1:1 + n_lb].set(c)
    mats_c = _dft_matrices(seq_c)
    mats_l = _dft_matrices(seq_l)

    def group_ctx(i, tm):
        return 0

    def group_lat(i, tm):
        return 1 + (i * tm) // seq_l

    xc = x_prompt.reshape(t_c, d)
    xs = x_sample.reshape(t_l, d)
    ctx_states = []
    for l in range(depth):
        last = l == depth - 1
        mod = _modulation(cvec, ada_w[l], ada_b[l][None]).reshape(SUBLANES, N_MOD, d)

        w4 = jnp.concatenate([rg_wa[l, 0], rg_wx[l, 0], rg_wa[l, 1], rg_wx[l, 1]], axis=-1).astype(BF16)
        b4 = jnp.concatenate([rg_ba[l, 0].reshape(RG_HEADS, 1, hd), rg_bx[l, 0].reshape(RG_HEADS, 1, hd),
                              rg_ba[l, 1].reshape(RG_HEADS, 1, hd), rg_bx[l, 1].reshape(RG_HEADS, 1, hd)], axis=-1)
        p = {
            'norm1_g': norm1_g[l][None], 'norm2_g': norm2_g[l][None], 'w_in': w_in[l].astype(BF16),
            'rg_conv_w': rg_conv_w[l], 'rg_conv_b': rg_conv_b[l][None], 'rg_w4': w4, 'rg_b4': b4,
            'rg_lam': rg_lam[l], 'hy_conv_w': hy_conv_w[l], 'hy_conv_b': hy_conv_b[l][None],
            'hy_bias': hy_bias[l], 'gn_rg': gn_rg[l][None], 'gn_hy': gn_hy[l][None],
            'w_out': w_out[l].astype(BF16), 'router_w': router_w[l].astype(BF16), 'router_b': router_b[l][None],
        }
        filt_args = (hy_w1[l], hy_b1[l][None], hy_freq[l][None], hy_w2[l], hy_b2[l][None], hy_w3[l],
                     hy_decay[l].reshape(1, -1))
        filt_c = _hyena_filters(seq_c, mats_c[0], mats_c[1], *filt_args)
        filt_l = _hyena_filters(seq_l, mats_l[0], mats_l[1], *filt_args)

        h0_c = jnp.zeros((n_cb, 2, d_rg), F32)
        x1_c, h2_c, idx_c, gw_c, st_c = _mixer_path(xc, None, n_cb, seq_c, h0_c, mod, group_ctx, p,
                                                    mats_c, filt_c, n_seq=min(8, n_cb))
        ctx_states.append(st_c)
        x1_l, h2_l, idx_l, gw_l, _ = _mixer_path(xs, pos if l == 0 else None, n_lb, seq_l,
                                                 state_rglru[:, l], mod, group_lat, p,
                                                 mats_l, filt_l, n_seq=1)

        idx_all = jnp.concatenate([idx_c, idx_l], axis=0)
        dest, blk_e, blk_valid, n_used = _route(idx_all, router_w.shape[-1], EXPERT_ROWS)
        dest_flat = dest.reshape(-1)
        sh = (sh_w_gate[l].astype(BF16), sh_w_up[l].astype(BF16), sh_w_down[l].astype(BF16))
        xb, shared = _dispatch(dest_flat, h2_c, h2_l, blk_e.shape[0] * EXPERT_ROWS, *sh)
        yb = _experts(blk_e, blk_valid, n_used, xb, exp_w_gate[l], exp_w_up[l], exp_w_down[l])
        xc = _finish(dest_flat, shared, x1_c, gw_c, yb, mod, final_g[None], 0, group_ctx, last)
        xs = _finish(dest_flat, shared, x1_l, gw_l, yb, mod, final_g[None], t_c, group_lat, last)

    new_state = jnp.stack(ctx_states, axis=1).astype(x_prompt.dtype)
    return (xc.reshape(n_cb, seq_c, d), xs.reshape(n_lb, seq_l, d), new_state)
```

```python
import functools
import math

import jax
import jax.numpy as jnp
from jax import lax
from jax.experimental import pallas as pl
from jax.experimental.pallas import tpu as pltpu

F32 = jnp.float32
BF16 = jnp.bfloat16

GRID_W = 64
RG_HEADS = 8
RG_CONV_W = 4
RG_C = 8.0
HY_CONV_W = 3
HY_BANDS = 16
TOP_K = 8
ROUTED_SCALE = 2.5
N_MOD = 6
EPS = 1e-6

LANES = 128
SUBLANES = 8
VMEM_LIMIT_BYTES = 56 * 1024 * 1024

EXPERT_ROWS = 288


def _params(n_axes, vmem=VMEM_LIMIT_BYTES):
    return pltpu.CompilerParams(dimension_semantics=("arbitrary",) * n_axes, vmem_limit_bytes=vmem)


def _rms(x, g):
    return x * lax.rsqrt(jnp.mean(x * x, axis=-1, keepdims=True) + EPS) * g


def _sigmoid(x):
    return 0.5 * jnp.tanh(0.5 * x) + 0.5


HI_HALF = -65536


def _pack_pair(lo, hi):
    lo_b = lax.bitcast_convert_type(lo.astype(BF16).astype(F32), jnp.int32)
    hi_b = lax.bitcast_convert_type(hi.astype(BF16).astype(F32), jnp.int32)
    return hi_b | lax.shift_right_logical(lo_b, 16)


def _unpack_pair(p):
    lo = lax.bitcast_convert_type(lax.shift_left(p, 16), F32)
    hi = lax.bitcast_convert_type(p & HI_HALF, F32)
    return lo, hi


def _store_token_tiles(ref, packed):
    m = packed.shape[0]
    for j in range(SUBLANES):
        ref[pl.ds(j, m, stride=SUBLANES), :] = packed[:, j * LANES:(j + 1) * LANES]


def _load_token_tiles(ref, row0, m):
    return jnp.concatenate(
        [ref[pl.ds(row0 * SUBLANES + j, m, stride=SUBLANES), :] for j in range(SUBLANES)], axis=1)


def _mod_kernel(c_ref, w_ref, b_ref, o_ref):
    c = c_ref[...]
    s = (c * jax.nn.sigmoid(c)).astype(BF16)
    o_ref[...] = jnp.dot(s, w_ref[...].astype(BF16), preferred_element_type=F32) + b_ref[...]


def _modulation(cvec, ada_w, ada_b):
    d, n = ada_w.shape
    tn = 1536
    return pl.pallas_call(
        _mod_kernel,
        grid=(n // tn,),
        in_specs=[pl.BlockSpec((SUBLANES, d), lambda j: (0, 0)),
                  pl.BlockSpec((d, tn), lambda j: (0, j)),
                  pl.BlockSpec((1, tn), lambda j: (0, j))],
        out_specs=pl.BlockSpec((SUBLANES, tn), lambda j: (0, j)),
        out_shape=jax.ShapeDtypeStruct((SUBLANES, n), F32),
        compiler_params=_params(1),
        name="modulation",
    )(cvec, ada_w, ada_b)


HALO = 16


def _inproj_kernel(*refs, has_pos, tn, n_rg, seq_len):
    if has_pos:
        (x_ref, xp_ref, xn_ref, pos_ref, pp_ref, pn_ref, mod_ref, g_ref, w_hbm, cw_ref, cb_ref,
         u_ref, hv_ref, w_vmem, h_scr, sem) = refs
    else:
        (x_ref, xp_ref, xn_ref, mod_ref, g_ref, w_hbm, cw_ref, cb_ref,
         u_ref, hv_ref, w_vmem, h_scr, sem) = refs
    i = pl.program_id(0)
    j = pl.program_id(1)
    tm = x_ref.shape[0]

    @pl.when(jnp.logical_and(i == 0, j == 0))
    def _():
        cp = pltpu.make_async_copy(w_hbm, w_vmem, sem)
        cp.start()
        cp.wait()

    @pl.when(j == 0)
    def _():
        def normed(x_r, p_r):
            x = x_r[...]
            if has_pos:
                x = x + p_r[...]
            return (_rms(x, g_ref[...]) * (1.0 + mod_ref[0, 1:2, :]) + mod_ref[0, 0:1, :]).astype(BF16)

        h_scr[0:HALO, :] = normed(xp_ref, pp_ref if has_pos else None)
        h_scr[HALO:HALO + tm, :] = normed(x_ref, pos_ref if has_pos else None)
        h_scr[HALO + tm:, :] = normed(xn_ref, pn_ref if has_pos else None)

    w = w_vmem[:, pl.ds(pl.multiple_of(j * tn, tn), tn)]

    @pl.when(j < n_rg)
    def _():
        u_ref[...] = jnp.dot(h_scr[HALO:HALO + tm, :], w, preferred_element_type=F32)

    @pl.when(j >= n_rg)
    def _():
        ue = jnp.dot(h_scr[...], w, preferred_element_type=F32)
        t_in_seq = (i * tm + lax.broadcasted_iota(jnp.int32, (tm, tn), 0)) & (seq_len - 1)
        cw = cw_ref[...]
        prev = jnp.where(t_in_seq == 0, 0.0, _shift_rows(ue, -1)[HALO:HALO + tm])
        nxt = jnp.where(t_in_seq == seq_len - 1, 0.0, _shift_rows(ue, 1)[HALO:HALO + tm])
        acc = cb_ref[...] + cw[0:1, :] * prev + cw[1:2, :] * ue[HALO:HALO + tm] + cw[2:3, :] * nxt
        hv_ref[...] = acc.astype(BF16)


def _in_proj(x2d, pos, mod, norm_g, w_bf, conv_w, conv_b, seq_len, group_of_tile, tm=512, tn=1024):
    t, d = x2d.shape
    n = w_bf.shape[1]
    n_hv = conv_w.shape[1]
    n_rg = (n - n_hv) // tn
    has_pos = pos is not None
    hb = tm // HALO
    last_h = t // HALO - 1

    def prev_blk(i):
        return jnp.maximum(i * hb - 1, 0)

    def next_blk(i):
        return jnp.minimum((i + 1) * hb, last_h)

    in_specs = [pl.BlockSpec((tm, d), lambda i, j: (i, 0)),
                pl.BlockSpec((HALO, d), lambda i, j: (prev_blk(i), 0)),
                pl.BlockSpec((HALO, d), lambda i, j: (next_blk(i), 0))]
    args = [x2d, x2d, x2d]
    if has_pos:
        pos_tiles = pos.shape[0] // tm
        last_p = pos.shape[0] // HALO - 1
        in_specs += [pl.BlockSpec((tm, d), lambda i, j: (i % pos_tiles, 0)),
                     pl.BlockSpec((HALO, d), lambda i, j: (jnp.maximum((i % pos_tiles) * hb - 1, 0), 0)),
                     pl.BlockSpec((HALO, d), lambda i, j: (jnp.minimum((i % pos_tiles + 1) * hb, last_p), 0))]
        args += [pos, pos, pos]
    in_specs += [pl.BlockSpec((1, N_MOD, d), lambda i, j: (group_of_tile(i, tm), 0, 0)),
                 pl.BlockSpec((1, d), lambda i, j: (0, 0)),
                 pl.BlockSpec(memory_space=pl.ANY),
                 pl.BlockSpec((HY_CONV_W, tn), lambda i, j: (0, jnp.maximum(j - n_rg, 0))),
                 pl.BlockSpec((1, tn), lambda i, j: (0, jnp.maximum(j - n_rg, 0)))]
    args += [mod, norm_g, w_bf, conv_w, conv_b]
    return pl.pallas_call(
        functools.partial(_inproj_kernel, has_pos=has_pos, tn=tn, n_rg=n_rg, seq_len=seq_len),
        grid=(t // tm, n // tn),
        in_specs=in_specs,
        out_specs=[pl.BlockSpec((tm, tn), lambda i, j: (i, jnp.minimum(j, n_rg - 1))),
                   pl.BlockSpec((tm, tn), lambda i, j: (i, jnp.maximum(j - n_rg, 0)))],
        out_shape=[jax.ShapeDtypeStruct((t, n - n_hv), F32), jax.ShapeDtypeStruct((t, n_hv), BF16)],
        scratch_shapes=[pltpu.VMEM((d, n), BF16), pltpu.VMEM((tm + 2 * HALO, d), BF16),
                        pltpu.SemaphoreType.DMA(())],
        compiler_params=_params(2),
        name="in_proj",
    )(*args)


def _shift_rows(win, off):
    if off == 0:
        return win
    n = win.shape[0]
    return pltpu.roll(win, (-off) % n, axis=0)


def _scan_chunk(a, b, reverse):
    n = a.shape[0]
    row = lax.broadcasted_iota(jnp.int32, a.shape, 0)
    dist = 1
    while dist < n:
        if reverse:
            a_s = pltpu.roll(a, n - dist, axis=0)
            b_s = pltpu.roll(b, n - dist, axis=0)
            m = row < n - dist
        else:
            a_s = pltpu.roll(a, dist, axis=0)
            b_s = pltpu.roll(b, dist, axis=0)
            m = row >= dist
        b = jnp.where(m, a * b_s + b, b)
        a = jnp.where(m, a * a_s, a)
        dist *= 2
    return a, b


def _rglru_kernel(xr_ref, gr_ref, cw_ref, cb_ref, w4_ref, b4_ref, lam_ref, h0_ref,
                  y_ref, st_ref, af, bf, ab, bb, hf, *, seq_len, n_seq, t1, tc):
    hd = xr_ref.shape[-1]
    nlam = -lam_ref[...]
    sp = jnp.maximum(nlam, 0.0) + jnp.log1p(jnp.exp(-jnp.abs(nlam)))
    cw = cw_ref[...]
    cb = cb_ref[...]
    b4 = b4_ref[0]
    nc1 = seq_len // t1
    ncs = seq_len // tc

    def per_seq(s, carry0):
        def gates(c, carry):
            r0 = pl.multiple_of(c * t1, t1)
            cur = xr_ref[s, pl.ds(r0, t1), :]
            p0 = pl.multiple_of(jnp.maximum(r0 - SUBLANES, 0), SUBLANES)
            n0 = pl.multiple_of(jnp.minimum(r0 + t1, seq_len - SUBLANES), SUBLANES)
            prev = jnp.where(c > 0, xr_ref[s, pl.ds(p0, SUBLANES), :], 0.0)
            nxt = jnp.where(c < nc1 - 1, xr_ref[s, pl.ds(n0, SUBLANES), :], 0.0)
            win = jnp.concatenate([prev, cur, nxt], axis=0)
            xr = cb
            for k in range(RG_CONV_W):
                xr = xr + cw[k:k + 1, :] * _shift_rows(win, k - RG_CONV_W // 2)[SUBLANES:SUBLANES + t1]
            z = jnp.dot(xr.astype(BF16), w4_ref[0], preferred_element_type=F32) + b4
            for d_i, (a_scr, b_scr) in enumerate(((af, bf), (ab, bb))):
                r = _sigmoid(z[:, (2 * d_i) * hd:(2 * d_i + 1) * hd])
                gi = _sigmoid(z[:, (2 * d_i + 1) * hd:(2 * d_i + 2) * hd])
                log_a = (-RG_C) * r * sp[d_i:d_i + 1, :]
                a = jnp.exp(log_a)
                a_scr[pl.ds(r0, t1), :] = a
                b_scr[pl.ds(r0, t1), :] = jnp.sqrt(-jnp.tanh(log_a) * (a * a + 1.0)) * (gi * xr)
            return carry

        lax.fori_loop(0, nc1, gates, 0)
        h0 = h0_ref[s]

        def fwd(c, carry):
            r0 = pl.multiple_of(c * tc, tc)
            a, h = _scan_chunk(af[pl.ds(r0, tc), :], bf[pl.ds(r0, tc), :], False)
            h = a * carry + h
            hf[pl.ds(r0, tc), :] = h
            return h[tc - 1:tc, :]

        s_f = lax.fori_loop(0, ncs, fwd, h0[0:1, :])

        def bwd(cc, carry):
            r0 = pl.multiple_of((ncs - 1 - cc) * tc, tc)
            a, h = _scan_chunk(ab[pl.ds(r0, tc), :], bb[pl.ds(r0, tc), :], True)
            h = a * carry + h
            g = gr_ref[s, pl.ds(r0, tc), :]
            y_ref[s, pl.ds(r0, tc), :] = jax.nn.gelu(g) * (hf[pl.ds(r0, tc), :] + h)
            return h[0:1, :]

        s_b = lax.fori_loop(0, ncs, bwd, h0[1:2, :])
        st_ref[s] = jnp.concatenate([s_f, s_b], axis=0)
        return carry0

    lax.fori_loop(0, n_seq, per_seq, 0)


def _rglru(u3, h0, conv_w, conv_b, w4, b4, lam, n_seq):
    b, seq_len, _ = u3.shape
    hd = w4.shape[1]
    d_rg = hd * RG_HEADS
    t1 = min(seq_len, 256)
    tc = 64
    kern = functools.partial(_rglru_kernel, seq_len=seq_len, n_seq=n_seq, t1=t1, tc=tc)
    return pl.pallas_call(
        kern,
        grid=(b // n_seq, RG_HEADS),
        in_specs=[pl.BlockSpec((n_seq, seq_len, hd), lambda i, h: (i, 0, h)),
                  pl.BlockSpec((n_seq, seq_len, hd), lambda i, h: (i, 0, RG_HEADS + h)),
                  pl.BlockSpec((RG_CONV_W, hd), lambda i, h: (0, h)),
                  pl.BlockSpec((1, hd), lambda i, h: (0, h)),
                  pl.BlockSpec((1, hd, 4 * hd), lambda i, h: (h, 0, 0)),
                  pl.BlockSpec((1, 1, 4 * hd), lambda i, h: (h, 0, 0)),
                  pl.BlockSpec((2, hd), lambda i, h: (0, h)),
                  pl.BlockSpec((n_seq, 2, hd), lambda i, h: (i, 0, h))],
        out_specs=[pl.BlockSpec((n_seq, seq_len, hd), lambda i, h: (i, 0, h)),
                   pl.BlockSpec((n_seq, 2, hd), lambda i, h: (i, 0, h))],
        out_shape=[jax.ShapeDtypeStruct((b, seq_len, d_rg), F32),
                   jax.ShapeDtypeStruct((b, 2, d_rg), F32)],
        scratch_shapes=[pltpu.VMEM((seq_len, hd), F32)] * 5,
        compiler_params=_params(2),
        name="rglru",
    )(u3, u3, conv_w, conv_b, w4, b4, lam, h0)


def _filt_time_kernel(w1_ref, b1_ref, fr_ref, w2_ref, b2_ref, w3_ref, dec_ref,
                      g_ref, d_ref, st_ref, *, seq_len, tl):
    i = pl.program_id(0)
    hi = lax.Precision.HIGHEST
    posi = i * tl + lax.broadcasted_iota(jnp.int32, (tl, LANES), 0)
    pos = posi.astype(F32)
    lane = lax.broadcasted_iota(jnp.int32, (tl, LANES), 1)
    band = jnp.where(lane <= HY_BANDS, lane, lane - HY_BANDS).astype(F32)
    ang = (2.0 * math.pi) * pos / seq_len * band
    t = pos / seq_len
    feats = jnp.where(lane == 0, t,
                      jnp.where(lane <= HY_BANDS, jnp.cos(ang),
                                jnp.where(lane <= 2 * HY_BANDS, -jnp.sin(ang), 0.0)))
    fr = fr_ref[...]
    hid = jnp.sin(fr * (jnp.dot(feats, w1_ref[...], precision=hi, preferred_element_type=F32) + b1_ref[...]))
    hid = jnp.sin(fr * (jnp.dot(hid, w2_ref[...], precision=hi, preferred_element_type=F32) + b2_ref[...]))
    k = jnp.dot(hid, w3_ref[...], precision=hi, preferred_element_type=F32)
    k = k * jnp.exp(-t[:, 0:1] * jnp.abs(dec_ref[...]))
    c = k.shape[1] // 4
    first = posi[:, 0:1] == 0
    sign = jnp.where(posi[:, 0:1] % 2 == 0, 1.0, -1.0)

    @pl.when(i == 0)
    def _():
        st_ref[...] = jnp.zeros_like(st_ref)

    for o in range(2):
        kf = k[:, (2 * o) * c:(2 * o + 1) * c]
        kb = jnp.where(first, 0.0, k[:, (2 * o + 1) * c:(2 * o + 2) * c])
        g = kf + kb
        g_ref[:, o * c:(o + 1) * c] = g.astype(BF16)
        d_ref[:, o * c:(o + 1) * c] = (kf - kb).astype(BF16)
        st_ref[0:1, o * c:(o + 1) * c] += jnp.sum(kf * kf + kb * kb, axis=0, keepdims=True)
        st_ref[1:2, o * c:(o + 1) * c] += jnp.sum(sign * g, axis=0, keepdims=True)


def _filt_dft_kernel(c_ref, s_ref, g_ref, d_ref, st_ref, kre_ref, q_ref):
    scale = lax.rsqrt(st_ref[0:1, :] + EPS)
    kre_ref[...] = jnp.dot(c_ref[...], g_ref[...], preferred_element_type=F32) * scale
    q_ref[...] = jnp.dot(s_ref[...], d_ref[...], preferred_element_type=F32) * scale


def _hyena_filters(seq_len, cmat, smat, w1, b1, freq, w2, b2, w3, decay):
    n_hid = w1.shape[1]
    n_out = w3.shape[1]
    c2 = n_out // 2
    tl = min(seq_len, 512)
    w1p = jnp.zeros((LANES, n_hid), F32).at[:w1.shape[0]].set(w1)
    g, d, stats = pl.pallas_call(
        functools.partial(_filt_time_kernel, seq_len=seq_len, tl=tl),
        grid=(seq_len // tl,),
        in_specs=[pl.BlockSpec((LANES, n_hid), lambda i: (0, 0)),
                  pl.BlockSpec((1, n_hid), lambda i: (0, 0)),
                  pl.BlockSpec((1, n_hid), lambda i: (0, 0)),
                  pl.BlockSpec((n_hid, n_hid), lambda i: (0, 0)),
                  pl.BlockSpec((1, n_hid), lambda i: (0, 0)),
                  pl.BlockSpec((n_hid, n_out), lambda i: (0, 0)),
                  pl.BlockSpec((1, n_out), lambda i: (0, 0))],
        out_specs=[pl.BlockSpec((tl, c2), lambda i: (i, 0)),
                   pl.BlockSpec((tl, c2), lambda i: (i, 0)),
                   pl.BlockSpec((SUBLANES, c2), lambda i: (0, 0))],
        out_shape=[jax.ShapeDtypeStruct((seq_len, c2), BF16),
                   jax.ShapeDtypeStruct((seq_len, c2), BF16),
                   jax.ShapeDtypeStruct((SUBLANES, c2), F32)],
        compiler_params=_params(1),
        name="hyena_filter_taps",
    )(w1p, b1, freq, w2, b2, w3, decay)
    tm = min(seq_len, 512)
    tn = 512
    kre, q = pl.pallas_call(
        _filt_dft_kernel,
        grid=(c2 // tn, seq_len // tm),
        in_specs=[pl.BlockSpec((tm, seq_len), lambda n, m: (m, 0)),
                  pl.BlockSpec((tm, seq_len), lambda n, m: (m, 0)),
                  pl.BlockSpec((seq_len, tn), lambda n, m: (0, n)),
                  pl.BlockSpec((seq_len, tn), lambda n, m: (0, n)),
                  pl.BlockSpec((SUBLANES, tn), lambda n, m: (0, n))],
        out_specs=[pl.BlockSpec((tm, tn), lambda n, m: (m, n)),
                   pl.BlockSpec((tm, tn), lambda n, m: (m, n))],
        out_shape=[jax.ShapeDtypeStruct((seq_len, c2), F32)] * 2,
        compiler_params=_params(2),
        name="hyena_filter_dft",
    )(cmat, smat, g, d, stats)
    return kre, q, stats


def _dft_matrices(seq_len):
    assert seq_len & (seq_len - 1) == 0
    tm = min(seq_len, 256)
    out = jax.ShapeDtypeStruct((seq_len, seq_len), BF16)
    return pl.pallas_call(
        functools.partial(_dft_table_kernel, seq_len=seq_len),
        grid=(seq_len // tm,),
        in_specs=[],
        out_specs=[pl.BlockSpec((tm, seq_len), lambda i: (i, 0))] * 3,
        out_shape=[out] * 3,
        scratch_shapes=[pltpu.VMEM((tm, seq_len), F32)] * 2,
        compiler_params=_params(1),
        name="dft_tables",
    )()


def _dft_table_kernel(c_ref, s_ref, st_ref, c0, s0, *, seq_len):
    i = pl.program_id(0)
    tm = c_ref.shape[0]
    wrap = 2 * seq_len - 1
    row = lax.broadcasted_iota(jnp.int32, (tm, seq_len), 0)
    col = lax.broadcasted_iota(jnp.int32, (tm, seq_len), 1)

    @pl.when(i == 0)
    def _():
        ang = ((row * col) & wrap).astype(F32) * (math.pi / seq_len)
        c0[...] = jnp.cos(ang)
        s0[...] = jnp.sin(ang)

    k0 = i * tm
    ang0 = ((k0 * col[0:1, :]) & wrap).astype(F32) * (math.pi / seq_len)
    cn = jnp.cos(ang0)
    sn = jnp.sin(ang0)
    cmat = c0[...] * cn - s0[...] * sn
    smat = s0[...] * cn + c0[...] * sn
    alt_col = jnp.where((col & 1) == 0, 1.0, -1.0)
    alt_row = jnp.where(((row + k0) & 1) == 0, 1.0, -1.0)
    c_ref[...] = cmat.astype(BF16)
    s_ref[...] = jnp.where(row + k0 == 0, alt_col, smat).astype(BF16)
    st_ref[...] = jnp.where(col == 0, alt_row, smat).astype(BF16)


def _hy_fwd_kernel(c_ref, s_ref, u_ref, kre_ref, q_ref, st_ref, pre_ref, pm_ref, *, seq_len):
    m = pl.program_id(2)
    u = u_ref[0]
    a = jnp.dot(c_ref[...], u, preferred_element_type=F32)
    bv = jnp.dot(s_ref[...], u, preferred_element_type=F32)
    tm = a.shape[0]
    is0 = (m * tm + lax.broadcasted_iota(jnp.int32, a.shape, 0)) == 0
    k_nyq = st_ref[1:2, :] * lax.rsqrt(st_ref[0:1, :] + EPS)
    kre = kre_ref[...]
    q = jnp.where(is0, 0.0, q_ref[...])
    kre_b = jnp.where(is0, k_nyq, kre)
    wk = jnp.where(is0, 0.5 / seq_len, 1.0 / seq_len)
    pre_ref[0] = ((a * kre - bv * q) * wk).astype(BF16)
    pm_ref[0] = ((a * q + bv * kre_b) * wk).astype(BF16)


def _hy_inv_kernel(c_ref, st_ref, pre_ref, pm_ref, u_ref, x_ref, bias_ref, z_ref):
    y = jnp.dot(c_ref[...], pre_ref[0], preferred_element_type=F32)
    y = y + jnp.dot(st_ref[...], pm_ref[0], preferred_element_type=F32)
    y = y + u_ref[0].astype(F32) * bias_ref[...]
    z_ref[0] = (y * x_ref[0].astype(F32)).astype(BF16)


def _hyena_order(order, u_arr, u_cb, hvc, gate_cb, mats, filt, bias, tm, tc):
    cmat, smat, smat_t = mats
    kre, q, stats = filt
    b, seq_len, _ = hvc.shape
    c = kre.shape[1] // 2
    nct = c // tc
    grid = (b, nct, seq_len // tm)
    pre, pm = pl.pallas_call(
        functools.partial(_hy_fwd_kernel, seq_len=seq_len),
        grid=grid,
        in_specs=[pl.BlockSpec((tm, seq_len), lambda i, n, m: (m, 0)),
                  pl.BlockSpec((tm, seq_len), lambda i, n, m: (m, 0)),
                  pl.BlockSpec((1, seq_len, tc), lambda i, n, m: (i, 0, u_cb * nct + n)),
                  pl.BlockSpec((tm, tc), lambda i, n, m: (m, order * nct + n)),
                  pl.BlockSpec((tm, tc), lambda i, n, m: (m, order * nct + n)),
                  pl.BlockSpec((SUBLANES, tc), lambda i, n, m: (0, order * nct + n))],
        out_specs=[pl.BlockSpec((1, tm, tc), lambda i, n, m: (i, m, n))] * 2,
        out_shape=[jax.ShapeDtypeStruct((b, seq_len, c), BF16)] * 2,
        compiler_params=_params(3),
        name="hyena_fwd_dft",
    )(cmat, smat, u_arr, kre, q, stats)
    return pl.pallas_call(
        _hy_inv_kernel,
        grid=grid,
        in_specs=[pl.BlockSpec((tm, seq_len), lambda i, n, m: (m, 0)),
                  pl.BlockSpec((tm, seq_len), lambda i, n, m: (m, 0)),
                  pl.BlockSpec((1, seq_len, tc), lambda i, n, m: (i, 0, n)),
                  pl.BlockSpec((1, seq_len, tc), lambda i, n, m: (i, 0, n)),
                  pl.BlockSpec((1, tm, tc), lambda i, n, m: (i, m, u_cb * nct + n)),
                  pl.BlockSpec((1, tm, tc), lambda i, n, m: (i, m, gate_cb * nct + n)),
                  pl.BlockSpec((1, tc), lambda i, n, m: (0, n))],
        out_specs=pl.BlockSpec((1, tm, tc), lambda i, n, m: (i, m, n)),
        out_shape=jax.ShapeDtypeStruct((b, seq_len, c), BF16),
        compiler_params=_params(3),
        name="hyena_inv_dft",
    )(cmat, smat_t, pre, pm, u_arr, hvc, bias[order][None])


def _post_mixer_kernel(*refs, has_pos):
    if has_pos:
        (yrg_ref, yhy_ref, x_ref, pos_ref, mod_ref, gnr_ref, gnh_ref, wo_ref, n2_ref,
         rw_ref, rb_ref, x1_ref, h2_ref, idx_ref, gw_ref) = refs
    else:
        (yrg_ref, yhy_ref, x_ref, mod_ref, gnr_ref, gnh_ref, wo_ref, n2_ref,
         rw_ref, rb_ref, x1_ref, h2_ref, idx_ref, gw_ref) = refs
    d_rg = yrg_ref.shape[1]
    na = _rms(yrg_ref[...], gnr_ref[...]).astype(BF16)
    nb = _rms(yhy_ref[...].astype(F32), gnh_ref[...]).astype(BF16)
    y = jnp.dot(na, wo_ref[0:d_rg, :], preferred_element_type=F32)
    y = y + jnp.dot(nb, wo_ref[d_rg:, :], preferred_element_type=F32)
    x = x_ref[...]
    if has_pos:
        x = x + pos_ref[...]
    x1 = x + mod_ref[0, 2:3, :] * y
    x1_ref[...] = x1
    h2f = _rms(x1, n2_ref[...]) * (1.0 + mod_ref[0, 4:5, :]) + mod_ref[0, 3:4, :]
    half = h2f.shape[1] // 2
    _store_token_tiles(h2_ref, _pack_pair(h2f[:, :half], h2f[:, half:]))
    h2 = h2f.astype(BF16)
    scores = jax.nn.sigmoid(jnp.dot(h2, rw_ref[...], preferred_element_type=F32))
    sel = scores + rb_ref[...]
    n_exp = scores.shape[1]
    lane = lax.broadcasted_iota(jnp.int32, scores.shape, 1)
    col = lax.broadcasted_iota(jnp.int32, idx_ref.shape, 1)
    idx_acc = jnp.zeros(idx_ref.shape, jnp.int32)
    gw_acc = jnp.zeros(gw_ref.shape, F32)
    for k in range(TOP_K):
        mx = jnp.max(sel, axis=1, keepdims=True)
        pick = jnp.min(jnp.where(sel == mx, lane, n_exp), axis=1, keepdims=True)
        hit = lane == pick
        val = jnp.sum(jnp.where(hit, scores, 0.0), axis=1, keepdims=True)
        sel = jnp.where(hit, -jnp.inf, sel)
        idx_acc = jnp.where(col == k, pick, idx_acc)
        gw_acc = jnp.where(col == k, val, gw_acc)
    idx_ref[...] = idx_acc
    gw_ref[...] = gw_acc / jnp.sum(gw_acc, axis=1, keepdims=True) * ROUTED_SCALE


def _post_mixer(y_rg, y_hy, x2d, pos, mod, gn_rg, gn_hy, w_out_bf, norm2_g, router_w_bf, router_b,
                group_of_tile, tm=256):
    t, d = x2d.shape
    d_rg = y_rg.shape[1]
    d_hy = y_hy.shape[1]
    n_exp = router_w_bf.shape[1]
    has_pos = pos is not None
    in_specs = [pl.BlockSpec((tm, d_rg), lambda i: (i, 0)),
                pl.BlockSpec((tm, d_hy), lambda i: (i, 0)),
                pl.BlockSpec((tm, d), lambda i: (i, 0))]
    args = [y_rg, y_hy, x2d]
    if has_pos:
        pos_tiles = pos.shape[0] // tm
        in_specs.append(pl.BlockSpec((tm, d), lambda i: (i % pos_tiles, 0)))
        args.append(pos)
    in_specs += [pl.BlockSpec((1, N_MOD, d), lambda i: (group_of_tile(i, tm), 0, 0)),
                 pl.BlockSpec((1, d_rg), lambda i: (0, 0)),
                 pl.BlockSpec((1, d_hy), lambda i: (0, 0)),
                 pl.BlockSpec((d_rg + d_hy, d), lambda i: (0, 0)),
                 pl.BlockSpec((1, d), lambda i: (0, 0)),
                 pl.BlockSpec((d, n_exp), lambda i: (0, 0)),
                 pl.BlockSpec((1, n_exp), lambda i: (0, 0))]
    args += [mod, gn_rg, gn_hy, w_out_bf, norm2_g, router_w_bf, router_b]
    return pl.pallas_call(
        functools.partial(_post_mixer_kernel, has_pos=has_pos),
        grid=(t // tm,),
        in_specs=in_specs,
        out_specs=[pl.BlockSpec((tm, d), lambda i: (i, 0)),
                   pl.BlockSpec((tm * SUBLANES, LANES), lambda i: (i, 0)),
                   pl.BlockSpec((tm, TOP_K), lambda i: (i, 0)),
                   pl.BlockSpec((tm, TOP_K), lambda i: (i, 0))],
        out_shape=[jax.ShapeDtypeStruct((t, d), F32),
                   jax.ShapeDtypeStruct((t * SUBLANES, LANES), jnp.int32),
                   jax.ShapeDtypeStruct((t, TOP_K), jnp.int32),
                   jax.ShapeDtypeStruct((t, TOP_K), F32)],
        compiler_params=_params(1),
        name="post_mixer",
    )(*args)


def _route_kernel(idx_ref, dest_ref, be_ref, bv_ref, nu_ref, tri, cnt, base, pst, *, blk, n_blk):
    p = pl.program_id(0)
    i = pl.program_id(1)
    tm, top_k = idx_ref.shape
    ne = cnt.shape[1]

    def div_blk(n):
        return jnp.floor((n + 0.5) / blk)

    @pl.when(jnp.logical_and(p == 0, i == 0))
    def _():
        r = lax.broadcasted_iota(jnp.int32, (tm, tm), 0)
        c = lax.broadcasted_iota(jnp.int32, (tm, tm), 1)
        tri[...] = jnp.where(r > c, 1.0, 0.0).astype(BF16)
        cnt[...] = jnp.zeros_like(cnt)

    idx = idx_ref[...]
    lane = lax.broadcasted_iota(jnp.int32, (tm, ne), 1)
    hits = [lane == idx[:, k:k + 1] for k in range(top_k)]
    occ = jnp.zeros((tm, ne), F32)
    for h in hits:
        occ = occ + jnp.where(h, 1.0, 0.0)
    col_sum = jnp.sum(occ, axis=0, keepdims=True)

    @pl.when(p == 0)
    def _():
        cnt[0:1, :] += col_sum

    @pl.when(jnp.logical_and(p == 1, i == 0))
    def _():
        counts = cnt[...]
        padded = div_blk(counts + (blk - 1.0)) * blk
        r = lax.broadcasted_iota(jnp.int32, (ne, ne), 0)
        c = lax.broadcasted_iota(jnp.int32, (ne, ne), 1)
        upper = jnp.where(r <= c, 1.0, 0.0)
        pend = jnp.dot(padded, upper, precision=lax.Precision.HIGHEST, preferred_element_type=F32)
        pstart = pend - padded
        pst[...] = pstart
        base[...] = jnp.zeros_like(base)
        b0 = (lax.broadcasted_iota(jnp.int32, (n_blk, ne), 0) * blk).astype(F32)
        be = jnp.sum(jnp.where(pend[0:1, :] <= b0, 1.0, 0.0), axis=1, keepdims=True)
        be = jnp.minimum(be, ne - 1.0)
        own = lax.broadcasted_iota(jnp.int32, (n_blk, ne), 1).astype(F32) == be
        pst_b = jnp.sum(jnp.where(own, pstart[0:1, :], 0.0), axis=1, keepdims=True)
        cnt_b = jnp.sum(jnp.where(own, counts[0:1, :], 0.0), axis=1, keepdims=True)
        valid = jnp.clip(cnt_b - (b0[:, 0:1] - pst_b), 0.0, float(blk))
        be_ref[...] = be.astype(jnp.int32)
        bv_ref[...] = valid.astype(jnp.int32)
        total = jnp.max(pend[0:1, :], axis=1, keepdims=True)
        nu_ref[...] = jnp.broadcast_to(div_blk(total).astype(jnp.int32), nu_ref.shape)

    @pl.when(p == 1)
    def _():
        cum = jnp.dot(tri[...], occ.astype(BF16), preferred_element_type=F32) + base[0:1, :] + pst[0:1, :]
        col = lax.broadcasted_iota(jnp.int32, (tm, top_k), 1)
        acc = jnp.zeros((tm, top_k), F32)
        for k in range(top_k):
            v = jnp.sum(jnp.where(hits[k], cum, 0.0), axis=1, keepdims=True)
            acc = jnp.where(col == k, v, acc)
        dest_ref[...] = acc.astype(jnp.int32)
        base[0:1, :] += col_sum


def _route(idx, n_exp, blk, tm=512):
    n_tok, top_k = idx.shape
    n_blk = -(-n_tok * top_k // blk) + n_exp
    dest, be, bv, nu = pl.pallas_call(
        functools.partial(_route_kernel, blk=blk, n_blk=n_blk),
        grid=(2, n_tok // tm),
        in_specs=[pl.BlockSpec((tm, top_k), lambda p, i: (i, 0))],
        out_specs=[pl.BlockSpec((tm, top_k), lambda p, i: (i * p, 0)),
                   pl.BlockSpec((n_blk, 1), lambda p, i: (0, 0)),
                   pl.BlockSpec((n_blk, 1), lambda p, i: (0, 0)),
                   pl.BlockSpec((SUBLANES, LANES), lambda p, i: (0, 0))],
        out_shape=[jax.ShapeDtypeStruct((n_tok, top_k), jnp.int32),
                   jax.ShapeDtypeStruct((n_blk, 1), jnp.int32),
                   jax.ShapeDtypeStruct((n_blk, 1), jnp.int32),
                   jax.ShapeDtypeStruct((SUBLANES, LANES), jnp.int32)],
        scratch_shapes=[pltpu.VMEM((tm, tm), BF16), pltpu.VMEM((SUBLANES, n_exp), F32),
                        pltpu.VMEM((SUBLANES, n_exp), F32), pltpu.VMEM((SUBLANES, n_exp), F32)],
        compiler_params=_params(2),
        name="route",
    )(idx)
    return dest, be.reshape(n_blk), bv.reshape(n_blk), nu[0, 0:1]


def _dispatch_kernel(dest_ref, xa_ref, xl_ref, sg_ref, su_ref, sd_ref, out_ref, sh_ref, sem, *, n_a):
    tm = xa_ref.shape[0] // SUBLANES
    dh = SUBLANES * LANES

    def tile(ref, r):
        return ref.at[pl.ds(pl.multiple_of(r * SUBLANES, SUBLANES), SUBLANES), :]

    def run(src):
        def issue(t, c):
            for k in range(TOP_K):
                pltpu.make_async_copy(tile(src, t), tile(out_ref, dest_ref[t * TOP_K + k]), sem).start(priority=k % 2)
            return c

        lax.fori_loop(0, tm, issue, 0)

        lo, hi = _unpack_pair(_load_token_tiles(src, 0, tm))
        lo = lo.astype(BF16)
        hi = hi.astype(BF16)
        g = jnp.dot(lo, sg_ref[0:dh, :], preferred_element_type=F32)
        g = g + jnp.dot(hi, sg_ref[dh:, :], preferred_element_type=F32)
        u = jnp.dot(lo, su_ref[0:dh, :], preferred_element_type=F32)
        u = u + jnp.dot(hi, su_ref[dh:, :], preferred_element_type=F32)
        hmid = (g * _sigmoid(g) * u).astype(BF16)
        sh_ref[...] = jnp.dot(hmid, sd_ref[...], preferred_element_type=F32).astype(BF16)

        def drain(t, c):
            for k in range(TOP_K):
                pltpu.make_async_copy(tile(src, 0), tile(out_ref, 0), sem).wait()
            return c

        lax.fori_loop(0, tm, drain, 0, unroll=8)

    @pl.when(pl.program_id(0) < n_a)
    def _():
        run(xa_ref)

    @pl.when(pl.program_id(0) >= n_a)
    def _():
        run(xl_ref)


def _dispatch(dest_flat, h2p_a, h2p_l, n_rows, sg_bf, su_bf, sd_bf, tm=256):
    n_a = h2p_a.shape[0] // (tm * SUBLANES)
    n_l = h2p_l.shape[0] // (tm * SUBLANES)
    d, ds_ = sg_bf.shape
    return pl.pallas_call(
        functools.partial(_dispatch_kernel, n_a=n_a),
        grid=(n_a + n_l,),
        in_specs=[pl.BlockSpec((tm * TOP_K,), lambda i: (i,), memory_space=pltpu.SMEM),
                  pl.BlockSpec((tm * SUBLANES, LANES), lambda i: (jnp.minimum(i, n_a - 1), 0)),
                  pl.BlockSpec((tm * SUBLANES, LANES), lambda i: (jnp.maximum(i - n_a, 0), 0)),
                  pl.BlockSpec((d, ds_), lambda i: (0, 0)),
                  pl.BlockSpec((d, ds_), lambda i: (0, 0)),
                  pl.BlockSpec((ds_, d), lambda i: (0, 0))],
        out_specs=[pl.BlockSpec(memory_space=pl.ANY),
                   pl.BlockSpec((tm, d), lambda i: (i, 0))],
        out_shape=[jax.ShapeDtypeStruct((n_rows * SUBLANES, LANES), jnp.int32),
                   jax.ShapeDtypeStruct(((n_a + n_l) * tm, d), BF16)],
        scratch_shapes=[pltpu.SemaphoreType.DMA(())],
        compiler_params=_params(1),
        name="dispatch",
    )(dest_flat, h2p_a, h2p_l, sg_bf, su_bf, sd_bf)


WEIGHT_DMA_PRIORITY = 1


def _expert_kernel(be_ref, bv_ref, nu_ref, x_ref, wg_hbm, wu_hbm, wd_hbm, o_ref,
                   wg_f, wu_f, wd_f, wg_b, wu_b, wd_b, sem, grp, *, n_blk):
    i = pl.program_id(0)
    n_used = nu_ref[0]
    e = be_ref[i]
    active = i < n_used
    changed = jnp.logical_or(i == 0, e != be_ref[jnp.maximum(i - 1, 0)])

    def expert_at(j):
        return be_ref[jnp.minimum(j, n_blk - 1)]

    def next_group(j):
        ej = expert_at(j)
        return lax.while_loop(lambda q: jnp.logical_and(q < n_used, expert_at(q) == ej), lambda q: q + 1, j + 1)

    def copies(ex, slot):
        return (pltpu.make_async_copy(wg_hbm.at[ex], wg_f.at[slot], sem.at[slot, 0]),
                pltpu.make_async_copy(wu_hbm.at[ex], wu_f.at[slot], sem.at[slot, 1]),
                pltpu.make_async_copy(wd_hbm.at[ex], wd_f.at[slot], sem.at[slot, 2]))

    @pl.when(jnp.logical_and(active, i == 0))
    def _():
        grp[0] = 0
        for cp in copies(e, 0):
            cp.start(priority=WEIGHT_DMA_PRIORITY)
        n1 = next_group(i)

        @pl.when(n1 < n_used)
        def _():
            for cp in copies(expert_at(n1), 1):
                cp.start(priority=WEIGHT_DMA_PRIORITY)

    @pl.when(jnp.logical_and(active, changed))
    def _():
        slot = grp[0] % 2
        for cp in copies(e, slot):
            cp.wait()
        wg_b[...] = wg_f[slot].astype(BF16)
        wu_b[...] = wu_f[slot].astype(BF16)
        wd_b[...] = wd_f[slot].astype(BF16)
        n2 = next_group(next_group(i))

        @pl.when(n2 < n_used)
        def _():
            for cp in copies(expert_at(n2), slot):
                cp.start(priority=WEIGHT_DMA_PRIORITY)

        grp[0] = grp[0] + 1

    @pl.when(active)
    def _():
        tm = x_ref.shape[0] // SUBLANES
        dh = SUBLANES * LANES
        live = lax.broadcasted_iota(jnp.int32, (tm, dh), 0) < bv_ref[i]
        lo, hi = _unpack_pair(_load_token_tiles(x_ref, 0, tm))
        lo = jnp.where(live, lo, 0.0).astype(BF16)
        hi = jnp.where(live, hi, 0.0).astype(BF16)
        g = jnp.dot(lo, wg_b[0:dh, :], preferred_element_type=F32)
        g = g + jnp.dot(hi, wg_b[dh:, :], preferred_element_type=F32)
        u = jnp.dot(lo, wu_b[0:dh, :], preferred_element_type=F32)
        u = u + jnp.dot(hi, wu_b[dh:, :], preferred_element_type=F32)
        hmid = (g * _sigmoid(g) * u).astype(BF16)
        y = jnp.dot(hmid, wd_b[...], preferred_element_type=F32)
        _store_token_tiles(o_ref, _pack_pair(y[:, :dh], y[:, dh:]))


def _experts(blk_e, blk_valid, n_used, xb, wg, wu, wd):
    n_exp, d, de = wg.shape
    tm = EXPERT_ROWS
    n_blk = xb.shape[0] // (tm * SUBLANES)
    grid_spec = pltpu.PrefetchScalarGridSpec(
        num_scalar_prefetch=3,
        grid=(n_blk,),
        in_specs=[pl.BlockSpec((tm * SUBLANES, LANES), lambda i, be, bv, nu: (jnp.minimum(i, nu[0] - 1), 0)),
                  pl.BlockSpec(memory_space=pl.ANY),
                  pl.BlockSpec(memory_space=pl.ANY),
                  pl.BlockSpec(memory_space=pl.ANY)],
        out_specs=pl.BlockSpec((tm * SUBLANES, LANES), lambda i, be, bv, nu: (jnp.minimum(i, nu[0] - 1), 0)),
        scratch_shapes=[pltpu.VMEM((2, d, de), F32), pltpu.VMEM((2, d, de), F32), pltpu.VMEM((2, de, d), F32),
                        pltpu.VMEM((d, de), BF16), pltpu.VMEM((d, de), BF16), pltpu.VMEM((de, d), BF16),
                        pltpu.SemaphoreType.DMA((2, 3)), pltpu.SMEM((1,), jnp.int32)],
    )
    return pl.pallas_call(
        functools.partial(_expert_kernel, n_blk=n_blk),
        grid_spec=grid_spec,
        out_shape=jax.ShapeDtypeStruct(xb.shape, jnp.int32),
        compiler_params=_params(1),
        name="routed_experts",
    )(blk_e, blk_valid, n_used, xb, wg, wu, wd)


FINISH_ROWS = 64


def _finish_kernel(dcur_ref, dnxt_ref, sh_ref, x1_ref, gw_ref, mod_ref, fg_ref,
                   yb_hbm, o_ref, gbuf, gwb, sem, *, final_norm):
    i = pl.program_id(0)
    n_tiles = pl.num_programs(0)
    tm, d = x1_ref.shape
    dh = SUBLANES * LANES
    slot = i % 2

    def row_copy(row, t, k, s):
        src = yb_hbm.at[pl.ds(pl.multiple_of(row * SUBLANES, SUBLANES), SUBLANES), :]
        dst = gbuf.at[s, pl.ds(pl.multiple_of((k * tm + t) * SUBLANES, SUBLANES), SUBLANES), :]
        return pltpu.make_async_copy(src, dst, sem.at[s])

    def gather(d_ref, s):
        def issue(t, c):
            for k in range(TOP_K):
                row_copy(d_ref[t * TOP_K + k], t, k, s).start(priority=k % 2)
            return c

        lax.fori_loop(0, tm, issue, 0)

    @pl.when(i == 0)
    def _():
        gather(dcur_ref, 0)

    @pl.when(i + 1 < n_tiles)
    def _():
        gather(dnxt_ref, 1 - slot)

    gw = gw_ref[...]
    for k in range(TOP_K):
        gwb[k] = jnp.broadcast_to(gw[:, k:k + 1], (tm, LANES))

    def drain(t, c):
        for k in range(TOP_K):
            row_copy(0, 0, 0, slot).wait()
        return c

    lax.fori_loop(0, tm, drain, 0, unroll=8)

    rows_g = gbuf.at[slot]
    for r0 in range(0, tm, FINISH_ROWS):
        rs = slice(r0, r0 + FINISH_ROWS)
        ssq = jnp.zeros((FINISH_ROWS, 1), F32)
        for j in range(SUBLANES):
            cols = (slice(j * LANES, (j + 1) * LANES), slice(dh + j * LANES, dh + (j + 1) * LANES))
            acc = [sh_ref[rs, c].astype(F32) for c in cols]
            for k in range(TOP_K):
                halves = _unpack_pair(rows_g[pl.ds((k * tm + r0) * SUBLANES + j, FINISH_ROWS, stride=SUBLANES), :])
                w = gwb[k, rs, :]
                acc = [a + w * h for a, h in zip(acc, halves)]
            for c, a in zip(cols, acc):
                x2 = x1_ref[rs, c] + mod_ref[0, 5:6, c] * a
                o_ref[rs, c] = x2
                ssq = ssq + jnp.sum(x2 * x2, axis=-1, keepdims=True)
        if final_norm:
            inv = lax.rsqrt(ssq / d + EPS)
            o_ref[rs, :] = o_ref[rs, :] * inv * fg_ref[...]


def _finish(dest_flat, shared, x1, gw, yb, mod, final_g, row0, group_of_tile, final_norm, tm=256):
    n_rows, d = x1.shape
    t0 = row0 // tm
    n_tiles = n_rows // tm
    return pl.pallas_call(
        functools.partial(_finish_kernel, final_norm=final_norm),
        grid=(n_tiles,),
        in_specs=[pl.BlockSpec((tm * TOP_K,), lambda i: (t0 + i,), memory_space=pltpu.SMEM),
                  pl.BlockSpec((tm * TOP_K,), lambda i: (t0 + jnp.minimum(i + 1, n_tiles - 1),),
                               memory_space=pltpu.SMEM),
                  pl.BlockSpec((tm, d), lambda i: (t0 + i, 0)),
                  pl.BlockSpec((tm, d), lambda i: (i, 0)),
                  pl.BlockSpec((tm, TOP_K), lambda i: (i, 0)),
                  pl.BlockSpec((1, N_MOD, d), lambda i: (group_of_tile(i, tm), 0, 0)),
                  pl.BlockSpec((1, d), lambda i: (0, 0)),
                  pl.BlockSpec(memory_space=pl.ANY)],
        out_specs=pl.BlockSpec((tm, d), lambda i: (i, 0)),
        out_shape=jax.ShapeDtypeStruct((n_rows, d), F32),
        scratch_shapes=[pltpu.VMEM((2, TOP_K * tm * SUBLANES, LANES), jnp.int32),
                        pltpu.VMEM((TOP_K, tm, LANES), F32), pltpu.SemaphoreType.DMA((2,))],
        compiler_params=_params(1),
        name="finish",
    )(dest_flat, dest_flat, shared, x1, gw, mod, final_g, yb)


def _grid_pos_emb(rows, d):
    quarter = d // 4
    omega = 1.0 / (10000.0 ** (jnp.arange(quarter, dtype=F32) / quarter))
    r = jnp.arange(rows, dtype=F32)[:, None] * omega
    cc = jnp.arange(GRID_W, dtype=F32)[:, None] * omega
    by_row = jnp.concatenate([jnp.sin(r), jnp.cos(r)], axis=-1)
    by_col = jnp.concatenate([jnp.sin(cc), jnp.cos(cc)], axis=-1)
    full = jnp.concatenate([jnp.broadcast_to(by_row[:, None, :], (rows, GRID_W, d // 2)),
                            jnp.broadcast_to(by_col[None, :, :], (rows, GRID_W, d // 2))], axis=-1)
    return full.reshape(rows * GRID_W, d)


def _mixer_path(x2d, pos, n_b, seq_len, h0, mod, group_of_tile, p, mats, filt, n_seq):
    d_rg = p['gn_rg'].shape[1]
    c = p['gn_hy'].shape[1]
    u_rg, hvc = _in_proj(x2d, pos, mod, p['norm1_g'], p['w_in'], p['hy_conv_w'], p['hy_conv_b'],
                         seq_len, group_of_tile)
    u3 = u_rg.reshape(n_b, seq_len, u_rg.shape[1])
    hvc = hvc.reshape(n_b, seq_len, hvc.shape[1])
    y_rg, st = _rglru(u3, h0, p['rg_conv_w'], p['rg_conv_b'], p['rg_w4'], p['rg_b4'], p['rg_lam'], n_seq)
    tm = min(seq_len, 512)
    tc = 512 if seq_len > 512 else c
    z1 = _hyena_order(0, hvc, 0, hvc, 1, mats, filt, p['hy_bias'], tm, tc)
    y_hy = _hyena_order(1, z1, 0, hvc, 2, mats, filt, p['hy_bias'], tm, tc)
    x1, h2, idx, gw = _post_mixer(y_rg.reshape(-1, d_rg), y_hy.reshape(-1, c), x2d, pos, mod,
                                  p['gn_rg'], p['gn_hy'], p['w_out'], p['norm2_g'],
                                  p['router_w'], p['router_b'], group_of_tile)
    return x1, h2, idx, gw, st


def kernel(x_prompt, x_sample, state_rglru, c, c_ctx, ada_w, ada_b, norm1_g, norm2_g, w_in, rg_conv_w, rg_conv_b, rg_wa, rg_ba, rg_wx, rg_bx, rg_lam, hy_conv_w, hy_conv_b, hy_w1, hy_b1, hy_freq, hy_w2, hy_b2, hy_w3, hy_decay, hy_bias, gn_rg, gn_hy, w_out, router_w, router_b, exp_w_gate, exp_w_up, exp_w_down, sh_w_gate, sh_w_up, sh_w_down, final_g):
    n_cb, seq_c, d = x_prompt.shape
    n_lb, seq_l, _ = x_sample.shape
    depth = ada_w.shape[0]
    d_rg = gn_rg.shape[1]
    hd = d_rg // RG_HEADS
    t_c = n_cb * seq_c
    t_l = n_lb * seq_l
    assert n_lb + 1 <= SUBLANES
    assert d == 2 * SUBLANES * LANES

    pos = _grid_pos_emb(seq_l // GRID_W, d)
    cvec = jnp.zeros((SUBLANES, d), F32).at[0].set(c_ctx).at[1:1 + n_lb].set(c)
    mats_c = _dft_matrices(seq_c)
    mats_l = _dft_matrices(seq_l)

    def group_ctx(i, tm):
        return 0

    def group_lat(i, tm):
        return 1 + (i * tm) // seq_l

    xc = x_prompt.reshape(t_c, d)
    xs = x_sample.reshape(t_l, d)
    ctx_states = []
    for l in range(depth):
        last = l == depth - 1
        mod = _modulation(cvec, ada_w[l], ada_b[l][None]).reshape(SUBLANES, N_MOD, d)

        w4 = jnp.concatenate([rg_wa[l, 0], rg_wx[l, 0], rg_wa[l, 1], rg_wx[l, 1]], axis=-1).astype(BF16)
        b4 = jnp.concatenate([rg_ba[l, 0].reshape(RG_HEADS, 1, hd), rg_bx[l, 0].reshape(RG_HEADS, 1, hd),
                              rg_ba[l, 1].reshape(RG_HEADS, 1, hd), rg_bx[l, 1].reshape(RG_HEADS, 1, hd)], axis=-1)
        p = {
            'norm1_g': norm1_g[l][None], 'norm2_g': norm2_g[l][None], 'w_in': w_in[l].astype(BF16),
            'rg_conv_w': rg_conv_w[l], 'rg_conv_b': rg_conv_b[l][None], 'rg_w4': w4, 'rg_b4': b4,
            'rg_lam': rg_lam[l], 'hy_conv_w': hy_conv_w[l], 'hy_conv_b': hy_conv_b[l][None],
            'hy_bias': hy_bias[l], 'gn_rg': gn_rg[l][None], 'gn_hy': gn_hy[l][None],
            'w_out': w_out[l].astype(BF16), 'router_w': router_w[l].astype(BF16), 'router_b': router_b[l][None],
        }
        filt_args = (hy_w1[l], hy_b1[l][None], hy_freq[l][None], hy_w2[l], hy_b2[l][None], hy_w3[l],
                     hy_decay[l].reshape(1, -1))
        filt_c = _hyena_filters(seq_c, mats_c[0], mats_c[1], *filt_args)
        filt_l = _hyena_filters(seq_l, mats_l[0], mats_l[1], *filt_args)

        h0_c = jnp.zeros((n_cb, 2, d_rg), F32)
        x1_c, h2_c, idx_c, gw_c, st_c = _mixer_path(xc, None, n_cb, seq_c, h0_c, mod, group_ctx, p,
                                                    mats_c, filt_c, n_seq=min(8, n_cb))
        ctx_states.append(st_c)
        x1_l, h2_l, idx_l, gw_l, _ = _mixer_path(xs, pos if l == 0 else None, n_lb, seq_l,
                                                 state_rglru[:, l], mod, group_lat, p,
                                                 mats_l, filt_l, n_seq=1)

        idx_all = jnp.concatenate([idx_c, idx_l], axis=0)
        dest, blk_e, blk_valid, n_used = _route(idx_all, router_w.shape[-1], EXPERT_ROWS)
        dest_flat = dest.reshape(-1)
        sh = (sh_w_gate[l].astype(BF16), sh_w_up[l].astype(BF16), sh_w_down[l].astype(BF16))
        xb, shared = _dispatch(dest_flat, h2_c, h2_l, blk_e.shape[0] * EXPERT_ROWS, *sh)
        yb = _experts(blk_e, blk_valid, n_used, xb, exp_w_gate[l], exp_w_up[l], exp_w_down[l])
        xc = _finish(dest_flat, shared, x1_c, gw_c, yb, mod, final_g[None], 0, group_ctx, last)
        xs = _finish(dest_flat, shared, x1_l, gw_l, yb, mod, final_g[None], t_c, group_lat, last)

    new_state = jnp.stack(ctx_states, axis=1).astype(x_prompt.dtype)
    return (xc.reshape(n_cb, seq_c, d), xs.reshape(n_lb, seq_l, d), new_state)
```

```python
import functools
import math

import jax
import jax.numpy as jnp
from jax import lax
from jax.experimental import pallas as pl
from jax.experimental.pallas import tpu as pltpu

F32 = jnp.float32
BF16 = jnp.bfloat16

GRID_W = 64
RG_HEADS = 8
RG_CONV_W = 4
RG_C = 8.0
HY_CONV_W = 3
HY_BANDS = 16
TOP_K = 8
ROUTED_SCALE = 2.5
N_MOD = 6
EPS = 1e-6

LANES = 128
SUBLANES = 8
VMEM_LIMIT_BYTES = 56 * 1024 * 1024

EXPERT_ROWS = 288


def _params(n_axes, vmem=VMEM_LIMIT_BYTES):
    return pltpu.CompilerParams(dimension_semantics=("arbitrary",) * n_axes, vmem_limit_bytes=vmem)


def _rms(x, g):
    return x * lax.rsqrt(jnp.mean(x * x, axis=-1, keepdims=True) + EPS) * g


def _sigmoid(x):
    return 0.5 * jnp.tanh(0.5 * x) + 0.5


HI_HALF = -65536


def _pack_pair(lo, hi):
    lo_b = lax.bitcast_convert_type(lo.astype(BF16).astype(F32), jnp.int32)
    hi_b = lax.bitcast_convert_type(hi.astype(BF16).astype(F32), jnp.int32)
    return hi_b | lax.shift_right_logical(lo_b, 16)


def _unpack_pair(p):
    lo = lax.bitcast_convert_type(lax.shift_left(p, 16), F32)
    hi = lax.bitcast_convert_type(p & HI_HALF, F32)
    return lo, hi


def _store_token_tiles(ref, packed):
    m = packed.shape[0]
    for j in range(SUBLANES):
        ref[pl.ds(j, m, stride=SUBLANES), :] = packed[:, j * LANES:(j + 1) * LANES]


def _load_token_tiles(ref, row0, m):
    return jnp.concatenate(
        [ref[pl.ds(row0 * SUBLANES + j, m, stride=SUBLANES), :] for j in range(SUBLANES)], axis=1)


def _mod_kernel(c_ref, w_ref, b_ref, o_ref):
    c = c_ref[...]
    s = (c * jax.nn.sigmoid(c)).astype(BF16)
    o_ref[...] = jnp.dot(s, w_ref[...].astype(BF16), preferred_element_type=F32) + b_ref[...]


def _modulation(cvec, ada_w, ada_b):
    d, n = ada_w.shape
    tn = 1536
    return pl.pallas_call(
        _mod_kernel,
        grid=(n // tn,),
        in_specs=[pl.BlockSpec((SUBLANES, d), lambda j: (0, 0)),
                  pl.BlockSpec((d, tn), lambda j: (0, j)),
                  pl.BlockSpec((1, tn), lambda j: (0, j))],
        out_specs=pl.BlockSpec((SUBLANES, tn), lambda j: (0, j)),
        out_shape=jax.ShapeDtypeStruct((SUBLANES, n), F32),
        compiler_params=_params(1),
        name="modulation",
    )(cvec, ada_w, ada_b)


HALO = 16


def _inproj_kernel(*refs, has_pos, tn, n_rg, seq_len):
    if has_pos:
        (x_ref, xp_ref, xn_ref, pos_ref, pp_ref, pn_ref, mod_ref, g_ref, w_hbm, cw_ref, cb_ref,
         u_ref, hv_ref, w_vmem, h_scr, par_scr, sem) = refs
    else:
        (x_ref, xp_ref, xn_ref, mod_ref, g_ref, w_hbm, cw_ref, cb_ref,
         u_ref, hv_ref, w_vmem, h_scr, par_scr, sem) = refs
    i = pl.program_id(0)
    j = pl.program_id(1)
    tm = x_ref.shape[0]

    @pl.when(jnp.logical_and(i == 0, j == 0))
    def _():
        cp = pltpu.make_async_copy(w_hbm, w_vmem, sem)
        cp.start()
        cp.wait()

    @pl.when(j == 0)
    def _():
        def normed(x_r, p_r):
            x = x_r[...]
            if has_pos:
                x = x + p_r[...]
            return (_rms(x, g_ref[...]) * (1.0 + mod_ref[0, 1:2, :]) + mod_ref[0, 0:1, :]).astype(BF16)

        h_scr[0:HALO, :] = normed(xp_ref, pp_ref if has_pos else None)
        h_scr[HALO:HALO + tm, :] = normed(x_ref, pos_ref if has_pos else None)
        h_scr[HALO + tm:, :] = normed(xn_ref, pn_ref if has_pos else None)

    w = w_vmem[:, pl.ds(pl.multiple_of(j * tn, tn), tn)]

    @pl.when(j < n_rg)
    def _():
        u_ref[...] = jnp.dot(h_scr[HALO:HALO + tm, :], w, preferred_element_type=F32)

    @pl.when(j >= n_rg)
    def _():
        ue = jnp.dot(h_scr[...], w, preferred_element_type=F32)
        t_in_seq = (i * tm + lax.broadcasted_iota(jnp.int32, (tm, tn), 0)) & (seq_len - 1)
        cw = cw_ref[...]
        prev = jnp.where(t_in_seq == 0, 0.0, _shift_rows(ue, -1)[HALO:HALO + tm])
        nxt = jnp.where(t_in_seq == seq_len - 1, 0.0, _shift_rows(ue, 1)[HALO:HALO + tm])
        acc = cb_ref[...] + cw[0:1, :] * prev + cw[1:2, :] * ue[HALO:HALO + tm] + cw[2:3, :] * nxt
        n_seq, _, rows, _ = hv_ref.shape
        for c0 in range(0, tn, LANES):
            par_scr[c0 // LANES] = acc[:, c0:c0 + LANES]
        for s in range(n_seq):
            for par in range(2):
                for c0 in range(0, tn, LANES):
                    picked = par_scr[c0 // LANES, pl.ds(s * 2 * rows + par, rows, stride=2), :]
                    hv_ref[s, par, :, c0:c0 + LANES] = picked.astype(BF16)


def _in_proj(x2d, pos, mod, norm_g, w_bf, conv_w, conv_b, seq_len, group_of_tile, tm=512, tn=1024):
    t, d = x2d.shape
    n = w_bf.shape[1]
    n_hv = conv_w.shape[1]
    n_rg = (n - n_hv) // tn
    has_pos = pos is not None
    hb = tm // HALO
    last_h = t // HALO - 1
    seqs = max(1, tm // seq_len)
    tiles_per_seq = max(1, seq_len // tm)
    rows = min(tm, seq_len) // 2

    def prev_blk(i):
        return jnp.maximum(i * hb - 1, 0)

    def next_blk(i):
        return jnp.minimum((i + 1) * hb, last_h)

    in_specs = [pl.BlockSpec((tm, d), lambda i, j: (i, 0)),
                pl.BlockSpec((HALO, d), lambda i, j: (prev_blk(i), 0)),
                pl.BlockSpec((HALO, d), lambda i, j: (next_blk(i), 0))]
    args = [x2d, x2d, x2d]
    if has_pos:
        pos_tiles = pos.shape[0] // tm
        last_p = pos.shape[0] // HALO - 1
        in_specs += [pl.BlockSpec((tm, d), lambda i, j: (i % pos_tiles, 0)),
                     pl.BlockSpec((HALO, d), lambda i, j: (jnp.maximum((i % pos_tiles) * hb - 1, 0), 0)),
                     pl.BlockSpec((HALO, d), lambda i, j: (jnp.minimum((i % pos_tiles + 1) * hb, last_p), 0))]
        args += [pos, pos, pos]
    in_specs += [pl.BlockSpec((1, N_MOD, d), lambda i, j: (group_of_tile(i, tm), 0, 0)),
                 pl.BlockSpec((1, d), lambda i, j: (0, 0)),
                 pl.BlockSpec(memory_space=pl.ANY),
                 pl.BlockSpec((HY_CONV_W, tn), lambda i, j: (0, jnp.maximum(j - n_rg, 0))),
                 pl.BlockSpec((1, tn), lambda i, j: (0, jnp.maximum(j - n_rg, 0)))]
    args += [mod, norm_g, w_bf, conv_w, conv_b]
    return pl.pallas_call(
        functools.partial(_inproj_kernel, has_pos=has_pos, tn=tn, n_rg=n_rg, seq_len=seq_len),
        grid=(t // tm, n // tn),
        in_specs=in_specs,
        out_specs=[pl.BlockSpec((tm, tn), lambda i, j: (i, jnp.minimum(j, n_rg - 1))),
                   pl.BlockSpec((seqs, 2, rows, tn),
                                lambda i, j: (i // tiles_per_seq, 0, i % tiles_per_seq,
                                              jnp.maximum(j - n_rg, 0)))],
        out_shape=[jax.ShapeDtypeStruct((t, n - n_hv), F32),
                   jax.ShapeDtypeStruct((t // seq_len, 2, seq_len // 2, n_hv), BF16)],
        scratch_shapes=[pltpu.VMEM((d, n), BF16), pltpu.VMEM((tm + 2 * HALO, d), BF16),
                        pltpu.VMEM((tn // LANES, tm, LANES), F32), pltpu.SemaphoreType.DMA(())],
        compiler_params=_params(2),
        name="in_proj",
    )(*args)


def _shift_rows(win, off):
    if off == 0:
        return win
    n = win.shape[0]
    return pltpu.roll(win, (-off) % n, axis=0)


def _scan_chunk(a, b, reverse):
    n = a.shape[0]
    row = lax.broadcasted_iota(jnp.int32, a.shape, 0)
    dist = 1
    while dist < n:
        if reverse:
            a_s = pltpu.roll(a, n - dist, axis=0)
            b_s = pltpu.roll(b, n - dist, axis=0)
            m = row < n - dist
        else:
            a_s = pltpu.roll(a, dist, axis=0)
            b_s = pltpu.roll(b, dist, axis=0)
            m = row >= dist
        b = jnp.where(m, a * b_s + b, b)
        a = jnp.where(m, a * a_s, a)
        dist *= 2
    return a, b


def _rglru_kernel(xr_ref, gr_ref, cw_ref, cb_ref, w4_ref, b4_ref, lam_ref, h0_ref,
                  y_ref, st_ref, af, bf, ab, bb, hf, *, seq_len, n_seq, t1, tc):
    hd = xr_ref.shape[-1]
    nlam = -lam_ref[...]
    sp = jnp.maximum(nlam, 0.0) + jnp.log1p(jnp.exp(-jnp.abs(nlam)))
    cw = cw_ref[...]
    cb = cb_ref[...]
    b4 = b4_ref[0]
    nc1 = seq_len // t1
    ncs = seq_len // tc

    def per_seq(s, carry0):
        def gates(c, carry):
            r0 = pl.multiple_of(c * t1, t1)
            cur = xr_ref[s, pl.ds(r0, t1), :]
            p0 = pl.multiple_of(jnp.maximum(r0 - SUBLANES, 0), SUBLANES)
            n0 = pl.multiple_of(jnp.minimum(r0 + t1, seq_len - SUBLANES), SUBLANES)
            prev = jnp.where(c > 0, xr_ref[s, pl.ds(p0, SUBLANES), :], 0.0)
            nxt = jnp.where(c < nc1 - 1, xr_ref[s, pl.ds(n0, SUBLANES), :], 0.0)
            win = jnp.concatenate([prev, cur, nxt], axis=0)
            xr = cb
            for k in range(RG_CONV_W):
                xr = xr + cw[k:k + 1, :] * _shift_rows(win, k - RG_CONV_W // 2)[SUBLANES:SUBLANES + t1]
            z = jnp.dot(xr.astype(BF16), w4_ref[0], preferred_element_type=F32) + b4
            for d_i, (a_scr, b_scr) in enumerate(((af, bf), (ab, bb))):
                r = _sigmoid(z[:, (2 * d_i) * hd:(2 * d_i + 1) * hd])
                gi = _sigmoid(z[:, (2 * d_i + 1) * hd:(2 * d_i + 2) * hd])
                log_a = (-RG_C) * r * sp[d_i:d_i + 1, :]
                a = jnp.exp(log_a)
                a_scr[pl.ds(r0, t1), :] = a
                b_scr[pl.ds(r0, t1), :] = jnp.sqrt(-jnp.tanh(log_a) * (a * a + 1.0)) * (gi * xr)
            return carry

        lax.fori_loop(0, nc1, gates, 0)
        h0 = h0_ref[s]

        def fwd(c, carry):
            r0 = pl.multiple_of(c * tc, tc)
            a, h = _scan_chunk(af[pl.ds(r0, tc), :], bf[pl.ds(r0, tc), :], False)
            h = a * carry + h
            hf[pl.ds(r0, tc), :] = h
            return h[tc - 1:tc, :]

        s_f = lax.fori_loop(0, ncs, fwd, h0[0:1, :])

        def bwd(cc, carry):
            r0 = pl.multiple_of((ncs - 1 - cc) * tc, tc)
            a, h = _scan_chunk(ab[pl.ds(r0, tc), :], bb[pl.ds(r0, tc), :], True)
            h = a * carry + h
            g = gr_ref[s, pl.ds(r0, tc), :]
            y_ref[s, pl.ds(r0, tc), :] = jax.nn.gelu(g) * (hf[pl.ds(r0, tc), :] + h)
            return h[0:1, :]

        s_b = lax.fori_loop(0, ncs, bwd, h0[1:2, :])
        st_ref[s] = jnp.concatenate([s_f, s_b], axis=0)
        return carry0

    lax.fori_loop(0, n_seq, per_seq, 0)


def _rglru(u3, h0, conv_w, conv_b, w4, b4, lam, n_seq):
    b, seq_len, _ = u3.shape
    hd = w4.shape[1]
    d_rg = hd * RG_HEADS
    t1 = min(seq_len, 256)
    tc = 64
    kern = functools.partial(_rglru_kernel, seq_len=seq_len, n_seq=n_seq, t1=t1, tc=tc)
    return pl.pallas_call(
        kern,
        grid=(b // n_seq, RG_HEADS),
        in_specs=[pl.BlockSpec((n_seq, seq_len, hd), lambda i, h: (i, 0, h)),
                  pl.BlockSpec((n_seq, seq_len, hd), lambda i, h: (i, 0, RG_HEADS + h)),
                  pl.BlockSpec((RG_CONV_W, hd), lambda i, h: (0, h)),
                  pl.BlockSpec((1, hd), lambda i, h: (0, h)),
                  pl.BlockSpec((1, hd, 4 * hd), lambda i, h: (h, 0, 0)),
                  pl.BlockSpec((1, 1, 4 * hd), lambda i, h: (h, 0, 0)),
                  pl.BlockSpec((2, hd), lambda i, h: (0, h)),
                  pl.BlockSpec((n_seq, 2, hd), lambda i, h: (i, 0, h))],
        out_specs=[pl.BlockSpec((n_seq, seq_len, hd), lambda i, h: (i, 0, h)),
                   pl.BlockSpec((n_seq, 2, hd), lambda i, h: (i, 0, h))],
        out_shape=[jax.ShapeDtypeStruct((b, seq_len, d_rg), F32),
                   jax.ShapeDtypeStruct((b, 2, d_rg), F32)],
        scratch_shapes=[pltpu.VMEM((seq_len, hd), F32)] * 5,
        compiler_params=_params(2),
        name="rglru",
    )(u3, u3, conv_w, conv_b, w4, b4, lam, h0)


def _filt_time_kernel(w1_ref, b1_ref, fr_ref, w2_ref, b2_ref, w3_ref, dec_ref,
                      g_ref, d_ref, st_ref, *, seq_len, tl):
    i = pl.program_id(0)
    hi = lax.Precision.HIGHEST
    half = seq_len // 2
    r = i * tl + lax.broadcasted_iota(jnp.int32, (tl, LANES), 0)
    posi = 2 * (r & (half - 1)) + jnp.where(r >= half, 1, 0)
    pos = posi.astype(F32)
    lane = lax.broadcasted_iota(jnp.int32, (tl, LANES), 1)
    band = jnp.where(lane <= HY_BANDS, lane, lane - HY_BANDS).astype(F32)
    ang = (2.0 * math.pi) * pos / seq_len * band
    t = pos / seq_len
    feats = jnp.where(lane == 0, t,
                      jnp.where(lane <= HY_BANDS, jnp.cos(ang),
                                jnp.where(lane <= 2 * HY_BANDS, -jnp.sin(ang), 0.0)))
    fr = fr_ref[...]
    hid = jnp.sin(fr * (jnp.dot(feats, w1_ref[...], precision=hi, preferred_element_type=F32) + b1_ref[...]))
    hid = jnp.sin(fr * (jnp.dot(hid, w2_ref[...], precision=hi, preferred_element_type=F32) + b2_ref[...]))
    k = jnp.dot(hid, w3_ref[...], precision=hi, preferred_element_type=F32)
    k = k * jnp.exp(-t[:, 0:1] * jnp.abs(dec_ref[...]))
    c = k.shape[1] // 4
    first = posi[:, 0:1] == 0
    p4 = posi[:, 0:1] & 3
    sign = jnp.where(p4 == 0, 1.0, jnp.where(p4 == 2, -1.0, 0.0))

    @pl.when(i == 0)
    def _():
        st_ref[...] = jnp.zeros_like(st_ref)

    for o in range(2):
        kf = k[:, (2 * o) * c:(2 * o + 1) * c]
        kb = jnp.where(first, 0.0, k[:, (2 * o + 1) * c:(2 * o + 2) * c])
        g = kf + kb
        g_ref[:, o * c:(o + 1) * c] = g.astype(BF16)
        d_ref[:, o * c:(o + 1) * c] = (kf - kb).astype(BF16)
        st_ref[0:1, o * c:(o + 1) * c] += jnp.sum(kf * kf + kb * kb, axis=0, keepdims=True)
        st_ref[1:2, o * c:(o + 1) * c] += jnp.sum(sign * g, axis=0, keepdims=True)


def _half_dfts(tabs, cos_even, sin_even, cos_odd, sin_odd):
    ce, se, co, so = tabs
    ae = jnp.dot(ce[...], cos_even, preferred_element_type=F32)
    ao = jnp.dot(co[...], cos_odd, preferred_element_type=F32)
    be = jnp.dot(se[...], sin_even, preferred_element_type=F32)
    bo = jnp.dot(so[...], sin_odd, preferred_element_type=F32)
    return ae, ao, be, bo


def _filt_dft_kernel(ce_ref, se_ref, co_ref, so_ref, g_ref, d_ref, st_ref, kf_ref, ks_ref, qf_ref, qs_ref):
    m = pl.program_id(1)
    half = g_ref.shape[0] // 2
    ae, ao, be, bo = _half_dfts((ce_ref, se_ref, co_ref, so_ref),
                                g_ref[0:half, :], d_ref[0:half, :], g_ref[half:, :], d_ref[half:, :])
    scale = lax.rsqrt(st_ref[0:1, :] + EPS)
    is0 = (m * ae.shape[0] + lax.broadcasted_iota(jnp.int32, ae.shape, 0)) == 0
    kf_ref[...] = (ae + ao) * scale
    ks_ref[...] = (ae - ao) * scale
    qf_ref[...] = jnp.where(is0, st_ref[1:2, :], be + bo) * scale
    qs_ref[...] = jnp.where(is0, bo, bo - be) * scale


def _hyena_filters(seq_len, tabs, w1, b1, freq, w2, b2, w3, decay):
    n_hid = w1.shape[1]
    n_out = w3.shape[1]
    c2 = n_out // 2
    tl = min(seq_len, 512)
    w1p = jnp.zeros((LANES, n_hid), F32).at[:w1.shape[0]].set(w1)
    g, d, stats = pl.pallas_call(
        functools.partial(_filt_time_kernel, seq_len=seq_len, tl=tl),
        grid=(seq_len // tl,),
        in_specs=[pl.BlockSpec((LANES, n_hid), lambda i: (0, 0)),
                  pl.BlockSpec((1, n_hid), lambda i: (0, 0)),
                  pl.BlockSpec((1, n_hid), lambda i: (0, 0)),
                  pl.BlockSpec((n_hid, n_hid), lambda i: (0, 0)),
                  pl.BlockSpec((1, n_hid), lambda i: (0, 0)),
                  pl.BlockSpec((n_hid, n_out), lambda i: (0, 0)),
                  pl.BlockSpec((1, n_out), lambda i: (0, 0))],
        out_specs=[pl.BlockSpec((tl, c2), lambda i: (i, 0)),
                   pl.BlockSpec((tl, c2), lambda i: (i, 0)),
                   pl.BlockSpec((SUBLANES, c2), lambda i: (0, 0))],
        out_shape=[jax.ShapeDtypeStruct((seq_len, c2), BF16),
                   jax.ShapeDtypeStruct((seq_len, c2), BF16),
                   jax.ShapeDtypeStruct((SUBLANES, c2), F32)],
        compiler_params=_params(1),
        name="hyena_filter_taps",
    )(w1p, b1, freq, w2, b2, w3, decay)
    half = seq_len // 2
    tm = min(half, 512)
    tn = 512
    tab = pl.BlockSpec((tm, half), lambda n, m: (m, 0))
    return pl.pallas_call(
        _filt_dft_kernel,
        grid=(c2 // tn, half // tm),
        in_specs=[tab, tab, tab, tab,
                  pl.BlockSpec((seq_len, tn), lambda n, m: (0, n)),
                  pl.BlockSpec((seq_len, tn), lambda n, m: (0, n)),
                  pl.BlockSpec((SUBLANES, tn), lambda n, m: (0, n))],
        out_specs=[pl.BlockSpec((tm, tn), lambda n, m: (m, n))] * 4,
        out_shape=[jax.ShapeDtypeStruct((half, c2), F32)] * 4,
        compiler_params=_params(2),
        name="hyena_filter_dft",
    )(tabs[0], tabs[1], tabs[2], tabs[3], g, d, stats)


def _dft_matrices(seq_len):
    assert seq_len & (seq_len - 1) == 0
    half = seq_len // 2
    tm = min(half, 256)
    out = jax.ShapeDtypeStruct((half, half), BF16)
    return pl.pallas_call(
        functools.partial(_dft_table_kernel, seq_len=seq_len),
        grid=(half // tm,),
        in_specs=[],
        out_specs=[pl.BlockSpec((tm, half), lambda i: (i, 0))] * 7,
        out_shape=[out] * 7,
        scratch_shapes=[pltpu.VMEM((tm, half), F32)] * 4,
        compiler_params=_params(1),
        name="dft_tables",
    )()


def _dft_table_kernel(ce_ref, se_ref, co_ref, so_ref, set_ref, cot_ref, sot_ref, c0e, s0e, c0o, s0o, *, seq_len):
    i = pl.program_id(0)
    tm, half = ce_ref.shape
    wrap = 2 * seq_len - 1
    unit = math.pi / seq_len
    row = lax.broadcasted_iota(jnp.int32, (tm, half), 0)
    col = lax.broadcasted_iota(jnp.int32, (tm, half), 1)
    col1 = col[0:1, :]

    def trig(n):
        ang = (n & wrap).astype(F32) * unit
        return jnp.cos(ang), jnp.sin(ang)

    @pl.when(i == 0)
    def _():
        c0e[...], s0e[...] = trig(2 * row * col)
        c0o[...], s0o[...] = trig(row * (2 * col + 1))

    def rotate(c0, s0, n):
        cn, sn = trig(n)
        return c0[...] * cn - s0[...] * sn, s0[...] * cn + c0[...] * sn

    r0 = i * tm
    ce, se = rotate(c0e, s0e, 2 * r0 * col1)
    co, so = rotate(c0o, s0o, r0 * (2 * col1 + 1))
    cot, sot = rotate(c0e, s0e, (2 * r0 + 1) * col1)
    alt_col = jnp.where((col & 1) == 0, 1.0, -1.0)
    alt_row = jnp.where(((row + r0) & 1) == 0, 1.0, -1.0)
    first_row = row + r0 == 0
    ce_ref[...] = ce.astype(BF16)
    se_ref[...] = jnp.where(first_row, alt_col, se).astype(BF16)
    co_ref[...] = co.astype(BF16)
    so_ref[...] = jnp.where(first_row, alt_col, so).astype(BF16)
    set_ref[...] = jnp.where(col == 0, alt_row, se).astype(BF16)
    cot_ref[...] = cot.astype(BF16)
    sot_ref[...] = jnp.where(col == 0, alt_row, sot).astype(BF16)


def _hy_fwd_kernel(ce_ref, se_ref, co_ref, so_ref, u_ref, kf_ref, ks_ref, qf_ref, qs_ref,
                   ee_ref, eo_ref, de_ref, do_ref, *, seq_len):
    m = pl.program_id(2)
    ue, uo = u_ref[0, 0], u_ref[0, 1]
    ae, ao, be, bo = _half_dfts((ce_ref, se_ref, co_ref, so_ref), ue, ue, uo, uo)
    is0 = (m * ae.shape[0] + lax.broadcasted_iota(jnp.int32, ae.shape, 0)) == 0
    a_f, a_s = ae + ao, ae - ao
    b_f, b_s = be + bo, bo - be
    kf, ks, qf, qs = kf_ref[...], ks_ref[...], qf_ref[...], qs_ref[...]
    pre_f = jnp.where(is0, a_f * kf, a_f * kf - b_f * qf)
    pre_s = jnp.where(is0, a_s * ks, a_s * ks - b_s * qs)
    pm_f = a_f * qf + b_f * kf
    pm_s = a_s * qs + b_s * ks
    mid_re = be * qf - bo * qs
    mid_mim = be * qs + bo * qf
    inv_n = 0.5 / seq_len
    w = jnp.where(is0, inv_n, 2.0 * inv_n)
    ee_ref[0] = ((pre_f + pre_s) * w).astype(BF16)
    eo_ref[0] = ((pre_f - pre_s) * w).astype(BF16)
    de_ref[0] = (jnp.where(is0, mid_re, pm_f - pm_s) * (2.0 * inv_n)).astype(BF16)
    do_ref[0] = (jnp.where(is0, mid_mim, pm_f + pm_s) * (2.0 * inv_n)).astype(BF16)


def _hy_inv_kernel(*refs, natural_out):
    if natural_out:
        ce_ref, set_ref, cot_ref, sot_ref, ee_ref, eo_ref, de_ref, do_ref, u_ref, x_ref, bias_ref, z_ref, scr = refs
    else:
        ce_ref, set_ref, cot_ref, sot_ref, ee_ref, eo_ref, de_ref, do_ref, u_ref, x_ref, bias_ref, z_ref = refs
    y_e = jnp.dot(ce_ref[...], ee_ref[0], preferred_element_type=F32)
    y_e = y_e + jnp.dot(set_ref[...], de_ref[0], preferred_element_type=F32)
    y_o = jnp.dot(cot_ref[...], eo_ref[0], preferred_element_type=F32)
    y_o = y_o + jnp.dot(sot_ref[...], do_ref[0], preferred_element_type=F32)
    bias = bias_ref[...]
    z_e = (y_e + u_ref[0, 0].astype(F32) * bias) * x_ref[0, 0].astype(F32)
    z_o = (y_o + u_ref[0, 1].astype(F32) * bias) * x_ref[0, 1].astype(F32)
    if natural_out:
        th = z_e.shape[0]
        for j in range(z_e.shape[1] // LANES):
            scr[j, pl.ds(0, th, stride=2), :] = z_e[:, j * LANES:(j + 1) * LANES]
            scr[j, pl.ds(1, th, stride=2), :] = z_o[:, j * LANES:(j + 1) * LANES]
        for j in range(z_e.shape[1] // LANES):
            z_ref[0, :, j * LANES:(j + 1) * LANES] = scr[j].astype(BF16)
    else:
        z_ref[0, 0] = z_e.astype(BF16)
        z_ref[0, 1] = z_o.astype(BF16)


def _hyena_order(order, u_arr, u_cb, hvc, gate_cb, tabs, filt, bias, tm, tc, natural_out):
    ce, se, co, so, se_t, co_t, so_t = tabs
    kf, ks, qf, qs = filt
    b, _, half, _ = hvc.shape
    seq_len = 2 * half
    c = kf.shape[1] // 2
    nct = c // tc
    grid = (b, nct, half // tm)
    tab = pl.BlockSpec((tm, half), lambda i, n, m: (m, 0))
    spec = pl.BlockSpec((tm, tc), lambda i, n, m: (m, order * nct + n))
    freq = pl.BlockSpec((1, tm, tc), lambda i, n, m: (i, m, n))
    ee, eo, de, do = pl.pallas_call(
        functools.partial(_hy_fwd_kernel, seq_len=seq_len),
        grid=grid,
        in_specs=[tab, tab, tab, tab,
                  pl.BlockSpec((1, 2, half, tc), lambda i, n, m: (i, 0, 0, u_cb * nct + n)),
                  spec, spec, spec, spec],
        out_specs=[freq] * 4,
        out_shape=[jax.ShapeDtypeStruct((b, half, c), BF16)] * 4,
        compiler_params=_params(3),
        name="hyena_fwd_dft",
    )(ce, se, co, so, u_arr, kf, ks, qf, qs)
    whole = pl.BlockSpec((1, half, tc), lambda i, n, m: (i, 0, n))
    if natural_out:
        out_spec = pl.BlockSpec((1, 2 * tm, tc), lambda i, n, m: (i, m, n))
        out_shape = jax.ShapeDtypeStruct((b, seq_len, c), BF16)
        scratch = [pltpu.VMEM((tc // LANES, 2 * tm, LANES), F32)]
    else:
        out_spec = pl.BlockSpec((1, 2, tm, tc), lambda i, n, m: (i, 0, m, n))
        out_shape = jax.ShapeDtypeStruct((b, 2, half, c), BF16)
        scratch = []
    return pl.pallas_call(
        functools.partial(_hy_inv_kernel, natural_out=natural_out),
        grid=grid,
        in_specs=[tab, tab, tab, tab, whole, whole, whole, whole,
                  pl.BlockSpec((1, 2, tm, tc), lambda i, n, m: (i, 0, m, u_cb * nct + n)),
                  pl.BlockSpec((1, 2, tm, tc), lambda i, n, m: (i, 0, m, gate_cb * nct + n)),
                  pl.BlockSpec((1, tc), lambda i, n, m: (0, n))],
        out_specs=out_spec,
        out_shape=out_shape,
        scratch_shapes=scratch,
        compiler_params=_params(3),
        name="hyena_inv_dft",
    )(ce, se_t, co_t, so_t, ee, eo, de, do, u_arr, hvc, bias[order][None])


def _post_mixer_kernel(*refs, has_pos):
    if has_pos:
        (yrg_ref, yhy_ref, x_ref, pos_ref, mod_ref, gnr_ref, gnh_ref, wo_ref, n2_ref,
         rw_ref, rb_ref, x1_ref, h2_ref, idx_ref, gw_ref) = refs
    else:
        (yrg_ref, yhy_ref, x_ref, mod_ref, gnr_ref, gnh_ref, wo_ref, n2_ref,
         rw_ref, rb_ref, x1_ref, h2_ref, idx_ref, gw_ref) = refs
    d_rg = yrg_ref.shape[1]
    na = _rms(yrg_ref[...], gnr_ref[...]).astype(BF16)
    nb = _rms(yhy_ref[...].astype(F32), gnh_ref[...]).astype(BF16)
    y = jnp.dot(na, wo_ref[0:d_rg, :], preferred_element_type=F32)
    y = y + jnp.dot(nb, wo_ref[d_rg:, :], preferred_element_type=F32)
    x = x_ref[...]
    if has_pos:
        x = x + pos_ref[...]
    x1 = x + mod_ref[0, 2:3, :] * y
    x1_ref[...] = x1
    h2f = _rms(x1, n2_ref[...]) * (1.0 + mod_ref[0, 4:5, :]) + mod_ref[0, 3:4, :]
    half = h2f.shape[1] // 2
    _store_token_tiles(h2_ref, _pack_pair(h2f[:, :half], h2f[:, half:]))
    h2 = h2f.astype(BF16)
    scores = jax.nn.sigmoid(jnp.dot(h2, rw_ref[...], preferred_element_type=F32))
    sel = scores + rb_ref[...]
    n_exp = scores.shape[1]
    lane = lax.broadcasted_iota(jnp.int32, scores.shape, 1)
    col = lax.broadcasted_iota(jnp.int32, idx_ref.shape, 1)
    idx_acc = jnp.zeros(idx_ref.shape, jnp.int32)
    gw_acc = jnp.zeros(gw_ref.shape, F32)
    for k in range(TOP_K):
        mx = jnp.max(sel, axis=1, keepdims=True)
        pick = jnp.min(jnp.where(sel == mx, lane, n_exp), axis=1, keepdims=True)
        hit = lane == pick
        val = jnp.sum(jnp.where(hit, scores, 0.0), axis=1, keepdims=True)
        sel = jnp.where(hit, -jnp.inf, sel)
        idx_acc = jnp.where(col == k, pick, idx_acc)
        gw_acc = jnp.where(col == k, val, gw_acc)
    idx_ref[...] = idx_acc
    gw_ref[...] = gw_acc / jnp.sum(gw_acc, axis=1, keepdims=True) * ROUTED_SCALE


def _post_mixer(y_rg, y_hy, x2d, pos, mod, gn_rg, gn_hy, w_out_bf, norm2_g, router_w_bf, router_b,
                group_of_tile, tm=256):
    t, d = x2d.shape
    d_rg = y_rg.shape[1]
    d_hy = y_hy.shape[1]
    n_exp = router_w_bf.shape[1]
    has_pos = pos is not None
    in_specs = [pl.BlockSpec((tm, d_rg), lambda i: (i, 0)),
                pl.BlockSpec((tm, d_hy), lambda i: (i, 0)),
                pl.BlockSpec((tm, d), lambda i: (i, 0))]
    args = [y_rg, y_hy, x2d]
    if has_pos:
        pos_tiles = pos.shape[0] // tm
        in_specs.append(pl.BlockSpec((tm, d), lambda i: (i % pos_tiles, 0)))
        args.append(pos)
    in_specs += [pl.BlockSpec((1, N_MOD, d), lambda i: (group_of_tile(i, tm), 0, 0)),
                 pl.BlockSpec((1, d_rg), lambda i: (0, 0)),
                 pl.BlockSpec((1, d_hy), lambda i: (0, 0)),
                 pl.BlockSpec((d_rg + d_hy, d), lambda i: (0, 0)),
                 pl.BlockSpec((1, d), lambda i: (0, 0)),
                 pl.BlockSpec((d, n_exp), lambda i: (0, 0)),
                 pl.BlockSpec((1, n_exp), lambda i: (0, 0))]
    args += [mod, gn_rg, gn_hy, w_out_bf, norm2_g, router_w_bf, router_b]
    return pl.pallas_call(
        functools.partial(_post_mixer_kernel, has_pos=has_pos),
        grid=(t // tm,),
        in_specs=in_specs,
        out_specs=[pl.BlockSpec((tm, d), lambda i: (i, 0)),
                   pl.BlockSpec((tm * SUBLANES, LANES), lambda i: (i, 0)),
                   pl.BlockSpec((tm, TOP_K), lambda i: (i, 0)),
                   pl.BlockSpec((tm, TOP_K), lambda i: (i, 0))],
        out_shape=[jax.ShapeDtypeStruct((t, d), F32),
                   jax.ShapeDtypeStruct((t * SUBLANES, LANES), jnp.int32),
                   jax.ShapeDtypeStruct((t, TOP_K), jnp.int32),
                   jax.ShapeDtypeStruct((t, TOP_K), F32)],
        compiler_params=_params(1),
        name="post_mixer",
    )(*args)


def _route_kernel(idx_ref, dest_ref, be_ref, bv_ref, nu_ref, tri, cnt, base, pst, *, blk, n_blk):
    p = pl.program_id(0)
    i = pl.program_id(1)
    tm, top_k = idx_ref.shape
    ne = cnt.shape[1]

    def div_blk(n):
        return jnp.floor((n + 0.5) / blk)

    @pl.when(jnp.logical_and(p == 0, i == 0))
    def _():
        r = lax.broadcasted_iota(jnp.int32, (tm, tm), 0)
        c = lax.broadcasted_iota(jnp.int32, (tm, tm), 1)
        tri[...] = jnp.where(r > c, 1.0, 0.0).astype(BF16)
        cnt[...] = jnp.zeros_like(cnt)

    idx = idx_ref[...]
    lane = lax.broadcasted_iota(jnp.int32, (tm, ne), 1)
    hits = [lane == idx[:, k:k + 1] for k in range(top_k)]
    occ = jnp.zeros((tm, ne), F32)
    for h in hits:
        occ = occ + jnp.where(h, 1.0, 0.0)
    col_sum = jnp.sum(occ, axis=0, keepdims=True)

    @pl.when(p == 0)
    def _():
        cnt[0:1, :] += col_sum

    @pl.when(jnp.logical_and(p == 1, i == 0))
    def _():
        counts = cnt[...]
        padded = div_blk(counts + (blk - 1.0)) * blk
        r = lax.broadcasted_iota(jnp.int32, (ne, ne), 0)
        c = lax.broadcasted_iota(jnp.int32, (ne, ne), 1)
        upper = jnp.where(r <= c, 1.0, 0.0)
        pend = jnp.dot(padded, upper, precision=lax.Precision.HIGHEST, preferred_element_type=F32)
        pstart = pend - padded
        pst[...] = pstart
        base[...] = jnp.zeros_like(base)
        b0 = (lax.broadcasted_iota(jnp.int32, (n_blk, ne), 0) * blk).astype(F32)
        be = jnp.sum(jnp.where(pend[0:1, :] <= b0, 1.0, 0.0), axis=1, keepdims=True)
        be = jnp.minimum(be, ne - 1.0)
        own = lax.broadcasted_iota(jnp.int32, (n_blk, ne), 1).astype(F32) == be
        pst_b = jnp.sum(jnp.where(own, pstart[0:1, :], 0.0), axis=1, keepdims=True)
        cnt_b = jnp.sum(jnp.where(own, counts[0:1, :], 0.0), axis=1, keepdims=True)
        valid = jnp.clip(cnt_b - (b0[:, 0:1] - pst_b), 0.0, float(blk))
        be_ref[...] = be.astype(jnp.int32)
        bv_ref[...] = valid.astype(jnp.int32)
        total = jnp.max(pend[0:1, :], axis=1, keepdims=True)
        nu_ref[...] = jnp.broadcast_to(div_blk(total).astype(jnp.int32), nu_ref.shape)

    @pl.when(p == 1)
    def _():
        cum = jnp.dot(tri[...], occ.astype(BF16), preferred_element_type=F32) + base[0:1, :] + pst[0:1, :]
        col = lax.broadcasted_iota(jnp.int32, (tm, top_k), 1)
        acc = jnp.zeros((tm, top_k), F32)
        for k in range(top_k):
            v = jnp.sum(jnp.where(hits[k], cum, 0.0), axis=1, keepdims=True)
            acc = jnp.where(col == k, v, acc)
        dest_ref[...] = acc.astype(jnp.int32)
        base[0:1, :] += col_sum


def _route(idx, n_exp, blk, tm=512):
    n_tok, top_k = idx.shape
    n_blk = -(-n_tok * top_k // blk) + n_exp
    dest, be, bv, nu = pl.pallas_call(
        functools.partial(_route_kernel, blk=blk, n_blk=n_blk),
        grid=(2, n_tok // tm),
        in_specs=[pl.BlockSpec((tm, top_k), lambda p, i: (i, 0))],
        out_specs=[pl.BlockSpec((tm, top_k), lambda p, i: (i * p, 0)),
                   pl.BlockSpec((n_blk, 1), lambda p, i: (0, 0)),
                   pl.BlockSpec((n_blk, 1), lambda p, i: (0, 0)),
                   pl.BlockSpec((SUBLANES, LANES), lambda p, i: (0, 0))],
        out_shape=[jax.ShapeDtypeStruct((n_tok, top_k), jnp.int32),
                   jax.ShapeDtypeStruct((n_blk, 1), jnp.int32),
                   jax.ShapeDtypeStruct((n_blk, 1), jnp.int32),
                   jax.ShapeDtypeStruct((SUBLANES, LANES), jnp.int32)],
        scratch_shapes=[pltpu.VMEM((tm, tm), BF16), pltpu.VMEM((SUBLANES, n_exp), F32),
                        pltpu.VMEM((SUBLANES, n_exp), F32), pltpu.VMEM((SUBLANES, n_exp), F32)],
        compiler_params=_params(2),
        name="route",
    )(idx)
    return dest, be.reshape(n_blk), bv.reshape(n_blk), nu[0, 0:1]


def _dispatch_kernel(dest_ref, xa_ref, xl_ref, sg_ref, su_ref, sd_ref, out_ref, sh_ref, sem, *, n_a):
    tm = xa_ref.shape[0] // SUBLANES
    dh = SUBLANES * LANES

    def tile(ref, r):
        return ref.at[pl.ds(pl.multiple_of(r * SUBLANES, SUBLANES), SUBLANES), :]

    def run(src):
        def issue(t, c):
            for k in range(TOP_K):
                pltpu.make_async_copy(tile(src, t), tile(out_ref, dest_ref[t * TOP_K + k]), sem).start(priority=k % 2)
            return c

        lax.fori_loop(0, tm, issue, 0)

        lo, hi = _unpack_pair(_load_token_tiles(src, 0, tm))
        lo = lo.astype(BF16)
        hi = hi.astype(BF16)
        g = jnp.dot(lo, sg_ref[0:dh, :], preferred_element_type=F32)
        g = g + jnp.dot(hi, sg_ref[dh:, :], preferred_element_type=F32)
        u = jnp.dot(lo, su_ref[0:dh, :], preferred_element_type=F32)
        u = u + jnp.dot(hi, su_ref[dh:, :], preferred_element_type=F32)
        hmid = (g * _sigmoid(g) * u).astype(BF16)
        sh_ref[...] = jnp.dot(hmid, sd_ref[...], preferred_element_type=F32).astype(BF16)

        def drain(t, c):
            for k in range(TOP_K):
                pltpu.make_async_copy(tile(src, 0), tile(out_ref, 0), sem).wait()
            return c

        lax.fori_loop(0, tm, drain, 0, unroll=8)

    @pl.when(pl.program_id(0) < n_a)
    def _():
        run(xa_ref)

    @pl.when(pl.program_id(0) >= n_a)
    def _():
        run(xl_ref)


def _dispatch(dest_flat, h2p_a, h2p_l, n_rows, sg_bf, su_bf, sd_bf, tm=256):
    n_a = h2p_a.shape[0] // (tm * SUBLANES)
    n_l = h2p_l.shape[0] // (tm * SUBLANES)
    d, ds_ = sg_bf.shape
    return pl.pallas_call(
        functools.partial(_dispatch_kernel, n_a=n_a),
        grid=(n_a + n_l,),
        in_specs=[pl.BlockSpec((tm * TOP_K,), lambda i: (i,), memory_space=pltpu.SMEM),
                  pl.BlockSpec((tm * SUBLANES, LANES), lambda i: (jnp.minimum(i, n_a - 1), 0)),
                  pl.BlockSpec((tm * SUBLANES, LANES), lambda i: (jnp.maximum(i - n_a, 0), 0)),
                  pl.BlockSpec((d, ds_), lambda i: (0, 0)),
                  pl.BlockSpec((d, ds_), lambda i: (0, 0)),
                  pl.BlockSpec((ds_, d), lambda i: (0, 0))],
        out_specs=[pl.BlockSpec(memory_space=pl.ANY),
                   pl.BlockSpec((tm, d), lambda i: (i, 0))],
        out_shape=[jax.ShapeDtypeStruct((n_rows * SUBLANES, LANES), jnp.int32),
                   jax.ShapeDtypeStruct(((n_a + n_l) * tm, d), BF16)],
        scratch_shapes=[pltpu.SemaphoreType.DMA(())],
        compiler_params=_params(1),
        name="dispatch",
    )(dest_flat, h2p_a, h2p_l, sg_bf, su_bf, sd_bf)


WEIGHT_DMA_PRIORITY = 1


def _expert_kernel(be_ref, bv_ref, nu_ref, x_ref, wg_hbm, wu_hbm, wd_hbm, o_ref,
                   wg_f, wu_f, wd_f, wg_b, wu_b, wd_b, sem, grp, *, n_blk):
    i = pl.program_id(0)
    n_used = nu_ref[0]
    e = be_ref[i]
    active = i < n_used
    changed = jnp.logical_or(i == 0, e != be_ref[jnp.maximum(i - 1, 0)])

    def expert_at(j):
        return be_ref[jnp.minimum(j, n_blk - 1)]

    def next_group(j):
        ej = expert_at(j)
        return lax.while_loop(lambda q: jnp.logical_and(q < n_used, expert_at(q) == ej), lambda q: q + 1, j + 1)

    def copies(ex, slot):
        return (pltpu.make_async_copy(wg_hbm.at[ex], wg_f.at[slot], sem.at[slot, 0]),
                pltpu.make_async_copy(wu_hbm.at[ex], wu_f.at[slot], sem.at[slot, 1]),
                pltpu.make_async_copy(wd_hbm.at[ex], wd_f.at[slot], sem.at[slot, 2]))

    @pl.when(jnp.logical_and(active, i == 0))
    def _():
        grp[0] = 0
        for cp in copies(e, 0):
            cp.start(priority=WEIGHT_DMA_PRIORITY)
        n1 = next_group(i)

        @pl.when(n1 < n_used)
        def _():
            for cp in copies(expert_at(n1), 1):
                cp.start(priority=WEIGHT_DMA_PRIORITY)

    @pl.when(jnp.logical_and(active, changed))
    def _():
        slot = grp[0] % 2
        for cp in copies(e, slot):
            cp.wait()
        wg_b[...] = wg_f[slot].astype(BF16)
        wu_b[...] = wu_f[slot].astype(BF16)
        wd_b[...] = wd_f[slot].astype(BF16)
        n2 = next_group(next_group(i))

        @pl.when(n2 < n_used)
        def _():
            for cp in copies(expert_at(n2), slot):
                cp.start(priority=WEIGHT_DMA_PRIORITY)

        grp[0] = grp[0] + 1

    @pl.when(active)
    def _():
        tm = x_ref.shape[0] // SUBLANES
        dh = SUBLANES * LANES
        live = lax.broadcasted_iota(jnp.int32, (tm, dh), 0) < bv_ref[i]
        lo, hi = _unpack_pair(_load_token_tiles(x_ref, 0, tm))
        lo = jnp.where(live, lo, 0.0).astype(BF16)
        hi = jnp.where(live, hi, 0.0).astype(BF16)
        g = jnp.dot(lo, wg_b[0:dh, :], preferred_element_type=F32)
        g = g + jnp.dot(hi, wg_b[dh:, :], preferred_element_type=F32)
        u = jnp.dot(lo, wu_b[0:dh, :], preferred_element_type=F32)
        u = u + jnp.dot(hi, wu_b[dh:, :], preferred_element_type=F32)
        hmid = (g * _sigmoid(g) * u).astype(BF16)
        y = jnp.dot(hmid, wd_b[...], preferred_element_type=F32)
        _store_token_tiles(o_ref, _pack_pair(y[:, :dh], y[:, dh:]))


def _experts(blk_e, blk_valid, n_used, xb, wg, wu, wd):
    n_exp, d, de = wg.shape
    tm = EXPERT_ROWS
    n_blk = xb.shape[0] // (tm * SUBLANES)
    grid_spec = pltpu.PrefetchScalarGridSpec(
        num_scalar_prefetch=3,
        grid=(n_blk,),
        in_specs=[pl.BlockSpec((tm * SUBLANES, LANES), lambda i, be, bv, nu: (jnp.minimum(i, nu[0] - 1), 0)),
                  pl.BlockSpec(memory_space=pl.ANY),
                  pl.BlockSpec(memory_space=pl.ANY),
                  pl.BlockSpec(memory_space=pl.ANY)],
        out_specs=pl.BlockSpec((tm * SUBLANES, LANES), lambda i, be, bv, nu: (jnp.minimum(i, nu[0] - 1), 0)),
        scratch_shapes=[pltpu.VMEM((2, d, de), F32), pltpu.VMEM((2, d, de), F32), pltpu.VMEM((2, de, d), F32),
                        pltpu.VMEM((d, de), BF16), pltpu.VMEM((d, de), BF16), pltpu.VMEM((de, d), BF16),
                        pltpu.SemaphoreType.DMA((2, 3)), pltpu.SMEM((1,), jnp.int32)],
    )
    return pl.pallas_call(
        functools.partial(_expert_kernel, n_blk=n_blk),
        grid_spec=grid_spec,
        out_shape=jax.ShapeDtypeStruct(xb.shape, jnp.int32),
        compiler_params=_params(1),
        name="routed_experts",
    )(blk_e, blk_valid, n_used, xb, wg, wu, wd)


FINISH_ROWS = 64


def _finish_kernel(dcur_ref, dnxt_ref, sh_ref, x1_ref, gw_ref, mod_ref, fg_ref,
                   yb_hbm, o_ref, gbuf, gwb, sem, *, final_norm):
    i = pl.program_id(0)
    n_tiles = pl.num_programs(0)
    tm, d = x1_ref.shape
    dh = SUBLANES * LANES
    slot = i % 2

    def row_copy(row, t, k, s):
        src = yb_hbm.at[pl.ds(pl.multiple_of(row * SUBLANES, SUBLANES), SUBLANES), :]
        dst = gbuf.at[s, pl.ds(pl.multiple_of((k * tm + t) * SUBLANES, SUBLANES), SUBLANES), :]
        return pltpu.make_async_copy(src, dst, sem.at[s])

    def gather(d_ref, s):
        def issue(t, c):
            for k in range(TOP_K):
                row_copy(d_ref[t * TOP_K + k], t, k, s).start(priority=k % 2)
            return c

        lax.fori_loop(0, tm, issue, 0)

    @pl.when(i == 0)
    def _():
        gather(dcur_ref, 0)

    @pl.when(i + 1 < n_tiles)
    def _():
        gather(dnxt_ref, 1 - slot)

    gw = gw_ref[...]
    for k in range(TOP_K):
        gwb[k] = jnp.broadcast_to(gw[:, k:k + 1], (tm, LANES))

    def drain(t, c):
        for k in range(TOP_K):
            row_copy(0, 0, 0, slot).wait()
        return c

    lax.fori_loop(0, tm, drain, 0, unroll=8)

    rows_g = gbuf.at[slot]
    for r0 in range(0, tm, FINISH_ROWS):
        rs = slice(r0, r0 + FINISH_ROWS)
        ssq = jnp.zeros((FINISH_ROWS, 1), F32)
        for j in range(SUBLANES):
            cols = (slice(j * LANES, (j + 1) * LANES), slice(dh + j * LANES, dh + (j + 1) * LANES))
            acc = [sh_ref[rs, c].astype(F32) for c in cols]
            for k in range(TOP_K):
                halves = _unpack_pair(rows_g[pl.ds((k * tm + r0) * SUBLANES + j, FINISH_ROWS, stride=SUBLANES), :])
                w = gwb[k, rs, :]
                acc = [a + w * h for a, h in zip(acc, halves)]
            for c, a in zip(cols, acc):
                x2 = x1_ref[rs, c] + mod_ref[0, 5:6, c] * a
                o_ref[rs, c] = x2
                ssq = ssq + jnp.sum(x2 * x2, axis=-1, keepdims=True)
        if final_norm:
            inv = lax.rsqrt(ssq / d + EPS)
            o_ref[rs, :] = o_ref[rs, :] * inv * fg_ref[...]


def _finish(dest_flat, shared, x1, gw, yb, mod, final_g, row0, group_of_tile, final_norm, tm=256):
    n_rows, d = x1.shape
    t0 = row0 // tm
    n_tiles = n_rows // tm
    return pl.pallas_call(
        functools.partial(_finish_kernel, final_norm=final_norm),
        grid=(n_tiles,),
        in_specs=[pl.BlockSpec((tm * TOP_K,), lambda i: (t0 + i,), memory_space=pltpu.SMEM),
                  pl.BlockSpec((tm * TOP_K,), lambda i: (t0 + jnp.minimum(i + 1, n_tiles - 1),),
                               memory_space=pltpu.SMEM),
                  pl.BlockSpec((tm, d), lambda i: (t0 + i, 0)),
                  pl.BlockSpec((tm, d), lambda i: (i, 0)),
                  pl.BlockSpec((tm, TOP_K), lambda i: (i, 0)),
                  pl.BlockSpec((1, N_MOD, d), lambda i: (group_of_tile(i, tm), 0, 0)),
                  pl.BlockSpec((1, d), lambda i: (0, 0)),
                  pl.BlockSpec(memory_space=pl.ANY)],
        out_specs=pl.BlockSpec((tm, d), lambda i: (i, 0)),
        out_shape=jax.ShapeDtypeStruct((n_rows, d), F32),
        scratch_shapes=[pltpu.VMEM((2, TOP_K * tm * SUBLANES, LANES), jnp.int32),
                        pltpu.VMEM((TOP_K, tm, LANES), F32), pltpu.SemaphoreType.DMA((2,))],
        compiler_params=_params(1),
        name="finish",
    )(dest_flat, dest_flat, shared, x1, gw, mod, final_g, yb)


def _grid_pos_emb(rows, d):
    quarter = d // 4
    omega = 1.0 / (10000.0 ** (jnp.arange(quarter, dtype=F32) / quarter))
    r = jnp.arange(rows, dtype=F32)[:, None] * omega
    cc = jnp.arange(GRID_W, dtype=F32)[:, None] * omega
    by_row = jnp.concatenate([jnp.sin(r), jnp.cos(r)], axis=-1)
    by_col = jnp.concatenate([jnp.sin(cc), jnp.cos(cc)], axis=-1)
    full = jnp.concatenate([jnp.broadcast_to(by_row[:, None, :], (rows, GRID_W, d // 2)),
                            jnp.broadcast_to(by_col[None, :, :], (rows, GRID_W, d // 2))], axis=-1)
    return full.reshape(rows * GRID_W, d)


def _mixer_path(x2d, pos, n_b, seq_len, h0, mod, group_of_tile, p, mats, filt, n_seq):
    d_rg = p['gn_rg'].shape[1]
    c = p['gn_hy'].shape[1]
    u_rg, hvc = _in_proj(x2d, pos, mod, p['norm1_g'], p['w_in'], p['hy_conv_w'], p['hy_conv_b'],
                         seq_len, group_of_tile)
    u3 = u_rg.reshape(n_b, seq_len, u_rg.shape[1])
    y_rg, st = _rglru(u3, h0, p['rg_conv_w'], p['rg_conv_b'], p['rg_w4'], p['rg_b4'], p['rg_lam'], n_seq)
    tm = min(seq_len // 2, 512)
    tc = 512 if seq_len > 512 else c
    z1 = _hyena_order(0, hvc, 0, hvc, 1, mats, filt, p['hy_bias'], tm, tc, False)
    y_hy = _hyena_order(1, z1, 0, hvc, 2, mats, filt, p['hy_bias'], tm, tc, True)
    x1, h2, idx, gw = _post_mixer(y_rg.reshape(-1, d_rg), y_hy.reshape(-1, c), x2d, pos, mod,
                                  p['gn_rg'], p['gn_hy'], p['w_out'], p['norm2_g'],
                                  p['router_w'], p['router_b'], group_of_tile)
    return x1, h2, idx, gw, st


def kernel(x_prompt, x_sample, state_rglru, c, c_ctx, ada_w, ada_b, norm1_g, norm2_g, w_in, rg_conv_w, rg_conv_b, rg_wa, rg_ba, rg_wx, rg_bx, rg_lam, hy_conv_w, hy_conv_b, hy_w1, hy_b1, hy_freq, hy_w2, hy_b2, hy_w3, hy_decay, hy_bias, gn_rg, gn_hy, w_out, router_w, router_b, exp_w_gate, exp_w_up, exp_w_down, sh_w_gate, sh_w_up, sh_w_down, final_g):
    n_cb, seq_c, d = x_prompt.shape
    n_lb, seq_l, _ = x_sample.shape
    depth = ada_w.shape[0]
    d_rg = gn_rg.shape[1]
    hd = d_rg // RG_HEADS
    t_c = n_cb * seq_c
    t_l = n_lb * seq_l
    assert n_lb + 1 <= SUBLANES
    assert d == 2 * SUBLANES * LANES

    pos = _grid_pos_emb(seq_l // GRID_W, d)
    cvec = jnp.zeros((SUBLANES, d), F32).at[0].set(c_ctx).at[1:1 + n_lb].set(c)
    mats_c = _dft_matrices(seq_c)
    mats_l = _dft_matrices(seq_l)

    def group_ctx(i, tm):
        return 0

    def group_lat(i, tm):
        return 1 + (i * tm) // seq_l

    xc = x_prompt.reshape(t_c, d)
    xs = x_sample.reshape(t_l, d)
    ctx_states = []
    for l in range(depth):
        last = l == depth - 1
        mod = _modulation(cvec, ada_w[l], ada_b[l][None]).reshape(SUBLANES, N_MOD, d)

        w4 = jnp.concatenate([rg_wa[l, 0], rg_wx[l, 0], rg_wa[l, 1], rg_wx[l, 1]], axis=-1).astype(BF16)
        b4 = jnp.concatenate([rg_ba[l, 0].reshape(RG_HEADS, 1, hd), rg_bx[l, 0].reshape(RG_HEADS, 1, hd),
                              rg_ba[l, 1].reshape(RG_HEADS, 1, hd), rg_bx[l, 1].reshape(RG_HEADS, 1, hd)], axis=-1)
        p = {
            'norm1_g': norm1_g[l][None], 'norm2_g': norm2_g[l][None], 'w_in': w_in[l].astype(BF16),
            'rg_conv_w': rg_conv_w[l], 'rg_conv_b': rg_conv_b[l][None], 'rg_w4': w4, 'rg_b4': b4,
            'rg_lam': rg_lam[l], 'hy_conv_w': hy_conv_w[l], 'hy_conv_b': hy_conv_b[l][None],
            'hy_bias': hy_bias[l], 'gn_rg': gn_rg[l][None], 'gn_hy': gn_hy[l][None],
            'w_out': w_out[l].astype(BF16), 'router_w': router_w[l].astype(BF16), 'router_b': router_b[l][None],
        }
        filt_args = (hy_w1[l], hy_b1[l][None], hy_freq[l][None], hy_w2[l], hy_b2[l][None], hy_w3[l],
                     hy_decay[l].reshape(1, -1))
        filt_c = _hyena_filters(seq_c, mats_c[:4], *filt_args)
        filt_l = _hyena_filters(seq_l, mats_l[:4], *filt_args)

        h0_c = jnp.zeros((n_cb, 2, d_rg), F32)
        x1_c, h2_c, idx_c, gw_c, st_c = _mixer_path(xc, None, n_cb, seq_c, h0_c, mod, group_ctx, p,
                                                    mats_c, filt_c, n_seq=min(8, n_cb))
        ctx_states.append(st_c)
        x1_l, h2_l, idx_l, gw_l, _ = _mixer_path(xs, pos if l == 0 else None, n_lb, seq_l,
                                                 state_rglru[:, l], mod, group_lat, p,
                                                 mats_l, filt_l, n_seq=1)

        idx_all = jnp.concatenate([idx_c, idx_l], axis=0)
        dest, blk_e, blk_valid, n_used = _route(idx_all, router_w.shape[-1], EXPERT_ROWS)
        dest_flat = dest.reshape(-1)
        sh = (sh_w_gate[l].astype(BF16), sh_w_up[l].astype(BF16), sh_w_down[l].astype(BF16))
        xb, shared = _dispatch(dest_flat, h2_c, h2_l, blk_e.shape[0] * EXPERT_ROWS, *sh)
        yb = _experts(blk_e, blk_valid, n_used, xb, exp_w_gate[l], exp_w_up[l], exp_w_down[l])
        xc = _finish(dest_flat, shared, x1_c, gw_c, yb, mod, final_g[None], 0, group_ctx, last)
        xs = _finish(dest_flat, shared, x1_l, gw_l, yb, mod, final_g[None], t_c, group_lat, last)

    new_state = jnp.stack(ctx_states, axis=1).astype(x_prompt.dtype)
    return (xc.reshape(n_cb, seq_c, d), xs.reshape(n_lb, seq_l, d), new_state)
```

```python
import functools
import math

import jax
import jax.numpy as jnp
from jax import lax
from jax.experimental import pallas as pl
from jax.experimental.pallas import tpu as pltpu

F32 = jnp.float32
BF16 = jnp.bfloat16

GRID_W = 64
RG_HEADS = 8
RG_CONV_W = 4
RG_C = 8.0
HY_CONV_W = 3
HY_BANDS = 16
TOP_K = 8
ROUTED_SCALE = 2.5
N_MOD = 6
EPS = 1e-6

LANES = 128
SUBLANES = 8
VMEM_LIMIT_BYTES = 56 * 1024 * 1024

EXPERT_ROWS = 288


def _params(n_axes, vmem=VMEM_LIMIT_BYTES):
    return pltpu.CompilerParams(dimension_semantics=("arbitrary",) * n_axes, vmem_limit_bytes=vmem)


def _rms(x, g):
    return x * lax.rsqrt(jnp.mean(x * x, axis=-1, keepdims=True) + EPS) * g


def _sigmoid(x):
    return 0.5 * jnp.tanh(0.5 * x) + 0.5


HI_HALF = -65536


def _pack_pair(lo, hi):
    lo_b = lax.bitcast_convert_type(lo.astype(BF16).astype(F32), jnp.int32)
    hi_b = lax.bitcast_convert_type(hi.astype(BF16).astype(F32), jnp.int32)
    return hi_b | lax.shift_right_logical(lo_b, 16)


def _unpack_pair(p):
    lo = lax.bitcast_convert_type(lax.shift_left(p, 16), F32)
    hi = lax.bitcast_convert_type(p & HI_HALF, F32)
    return lo, hi


def _store_token_tiles(ref, packed):
    m = packed.shape[0]
    for j in range(SUBLANES):
        ref[pl.ds(j, m, stride=SUBLANES), :] = packed[:, j * LANES:(j + 1) * LANES]


def _load_token_tiles(ref, row0, m):
    return jnp.concatenate(
        [ref[pl.ds(row0 * SUBLANES + j, m, stride=SUBLANES), :] for j in range(SUBLANES)], axis=1)


def _mod_kernel(c_ref, w_ref, b_ref, o_ref):
    c = c_ref[...]
    s = (c * jax.nn.sigmoid(c)).astype(BF16)
    o_ref[...] = jnp.dot(s, w_ref[...].astype(BF16), preferred_element_type=F32) + b_ref[...]


def _modulation(cvec, ada_w, ada_b):
    d, n = ada_w.shape
    tn = 1536
    return pl.pallas_call(
        _mod_kernel,
        grid=(n // tn,),
        in_specs=[pl.BlockSpec((SUBLANES, d), lambda j: (0, 0)),
                  pl.BlockSpec((d, tn), lambda j: (0, j)),
                  pl.BlockSpec((1, tn), lambda j: (0, j))],
        out_specs=pl.BlockSpec((SUBLANES, tn), lambda j: (0, j)),
        out_shape=jax.ShapeDtypeStruct((SUBLANES, n), F32),
        compiler_params=_params(1),
        name="modulation",
    )(cvec, ada_w, ada_b)


HALO = 16


def _inproj_kernel(*refs, has_pos, tn, n_rg, seq_len):
    if has_pos:
        (x_ref, xp_ref, xn_ref, pos_ref, pp_ref, pn_ref, mod_ref, g_ref, w_hbm, cw_ref, cb_ref,
         u_ref, hv_ref, w_vmem, h_scr, par_scr, sem) = refs
    else:
        (x_ref, xp_ref, xn_ref, mod_ref, g_ref, w_hbm, cw_ref, cb_ref,
         u_ref, hv_ref, w_vmem, h_scr, par_scr, sem) = refs
    i = pl.program_id(0)
    j = pl.program_id(1)
    tm = x_ref.shape[0]

    @pl.when(jnp.logical_and(i == 0, j == 0))
    def _():
        cp = pltpu.make_async_copy(w_hbm, w_vmem, sem)
        cp.start()
        cp.wait()

    @pl.when(j == 0)
    def _():
        def normed(x_r, p_r):
            x = x_r[...]
            if has_pos:
                x = x + p_r[...]
            return (_rms(x, g_ref[...]) * (1.0 + mod_ref[0, 1:2, :]) + mod_ref[0, 0:1, :]).astype(BF16)

        h_scr[0:HALO, :] = normed(xp_ref, pp_ref if has_pos else None)
        h_scr[HALO:HALO + tm, :] = normed(x_ref, pos_ref if has_pos else None)
        h_scr[HALO + tm:, :] = normed(xn_ref, pn_ref if has_pos else None)

    w = w_vmem[:, pl.ds(pl.multiple_of(j * tn, tn), tn)]

    @pl.when(j < n_rg)
    def _():
        u_ref[...] = jnp.dot(h_scr[HALO:HALO + tm, :], w, preferred_element_type=F32)

    @pl.when(j >= n_rg)
    def _():
        ue = jnp.dot(h_scr[...], w, preferred_element_type=F32)
        t_in_seq = (i * tm + lax.broadcasted_iota(jnp.int32, (tm, tn), 0)) & (seq_len - 1)
        cw = cw_ref[...]
        prev = jnp.where(t_in_seq == 0, 0.0, _shift_rows(ue, -1)[HALO:HALO + tm])
        nxt = jnp.where(t_in_seq == seq_len - 1, 0.0, _shift_rows(ue, 1)[HALO:HALO + tm])
        acc = cb_ref[...] + cw[0:1, :] * prev + cw[1:2, :] * ue[HALO:HALO + tm] + cw[2:3, :] * nxt
        n_seq, _, rows, _ = hv_ref.shape
        for c0 in range(0, tn, LANES):
            par_scr[c0 // LANES] = acc[:, c0:c0 + LANES]
        for s in range(n_seq):
            for par in range(2):
                for c0 in range(0, tn, LANES):
                    picked = par_scr[c0 // LANES, pl.ds(s * 2 * rows + par, rows, stride=2), :]
                    hv_ref[s, par, :, c0:c0 + LANES] = picked.astype(BF16)


def _in_proj(x2d, pos, mod, norm_g, w_bf, conv_w, conv_b, seq_len, group_of_tile, tm=512, tn=1024):
    t, d = x2d.shape
    n = w_bf.shape[1]
    n_hv = conv_w.shape[1]
    n_rg = (n - n_hv) // tn
    has_pos = pos is not None
    hb = tm // HALO
    last_h = t // HALO - 1
    seqs = max(1, tm // seq_len)
    tiles_per_seq = max(1, seq_len // tm)
    rows = min(tm, seq_len) // 2

    def prev_blk(i):
        return jnp.maximum(i * hb - 1, 0)

    def next_blk(i):
        return jnp.minimum((i + 1) * hb, last_h)

    in_specs = [pl.BlockSpec((tm, d), lambda i, j: (i, 0)),
                pl.BlockSpec((HALO, d), lambda i, j: (prev_blk(i), 0)),
                pl.BlockSpec((HALO, d), lambda i, j: (next_blk(i), 0))]
    args = [x2d, x2d, x2d]
    if has_pos:
        pos_tiles = pos.shape[0] // tm
        last_p = pos.shape[0] // HALO - 1
        in_specs += [pl.BlockSpec((tm, d), lambda i, j: (i % pos_tiles, 0)),
                     pl.BlockSpec((HALO, d), lambda i, j: (jnp.maximum((i % pos_tiles) * hb - 1, 0), 0)),
                     pl.BlockSpec((HALO, d), lambda i, j: (jnp.minimum((i % pos_tiles + 1) * hb, last_p), 0))]
        args += [pos, pos, pos]
    in_specs += [pl.BlockSpec((1, N_MOD, d), lambda i, j: (group_of_tile(i, tm), 0, 0)),
                 pl.BlockSpec((1, d), lambda i, j: (0, 0)),
                 pl.BlockSpec(memory_space=pl.ANY),
                 pl.BlockSpec((HY_CONV_W, tn), lambda i, j: (0, jnp.maximum(j - n_rg, 0))),
                 pl.BlockSpec((1, tn), lambda i, j: (0, jnp.maximum(j - n_rg, 0)))]
    args += [mod, norm_g, w_bf, conv_w, conv_b]
    return pl.pallas_call(
        functools.partial(_inproj_kernel, has_pos=has_pos, tn=tn, n_rg=n_rg, seq_len=seq_len),
        grid=(t // tm, n // tn),
        in_specs=in_specs,
        out_specs=[pl.BlockSpec((tm, tn), lambda i, j: (i, jnp.minimum(j, n_rg - 1))),
                   pl.BlockSpec((seqs, 2, rows, tn),
                                lambda i, j: (i // tiles_per_seq, 0, i % tiles_per_seq,
                                              jnp.maximum(j - n_rg, 0)))],
        out_shape=[jax.ShapeDtypeStruct((t, n - n_hv), F32),
                   jax.ShapeDtypeStruct((t // seq_len, 2, seq_len // 2, n_hv), BF16)],
        scratch_shapes=[pltpu.VMEM((d, n), BF16), pltpu.VMEM((tm + 2 * HALO, d), BF16),
                        pltpu.VMEM((tn // LANES, tm, LANES), F32), pltpu.SemaphoreType.DMA(())],
        compiler_params=_params(2),
        name="in_proj",
    )(*args)


def _shift_rows(win, off):
    if off == 0:
        return win
    n = win.shape[0]
    return pltpu.roll(win, (-off) % n, axis=0)


def _scan_chunk(a, b, reverse):
    n = a.shape[0]
    row = lax.broadcasted_iota(jnp.int32, a.shape, 0)
    dist = 1
    while dist < n:
        if reverse:
            a_s = pltpu.roll(a, n - dist, axis=0)
            b_s = pltpu.roll(b, n - dist, axis=0)
            m = row < n - dist
        else:
            a_s = pltpu.roll(a, dist, axis=0)
            b_s = pltpu.roll(b, dist, axis=0)
            m = row >= dist
        b = jnp.where(m, a * b_s + b, b)
        a = jnp.where(m, a * a_s, a)
        dist *= 2
    return a, b


def _rglru_kernel(xr_ref, gr_ref, cw_ref, cb_ref, w4_ref, b4_ref, lam_ref, h0_ref,
                  y_ref, st_ref, af, bf, ab, bb, hf, *, seq_len, n_seq, t1, tc):
    hd = xr_ref.shape[-1]
    nlam = -lam_ref[...]
    sp = jnp.maximum(nlam, 0.0) + jnp.log1p(jnp.exp(-jnp.abs(nlam)))
    cw = cw_ref[...]
    cb = cb_ref[...]
    b4 = b4_ref[0]
    nc1 = seq_len // t1
    ncs = seq_len // tc

    def per_seq(s, carry0):
        def gates(c, carry):
            r0 = pl.multiple_of(c * t1, t1)
            cur = xr_ref[s, pl.ds(r0, t1), :]
            p0 = pl.multiple_of(jnp.maximum(r0 - SUBLANES, 0), SUBLANES)
            n0 = pl.multiple_of(jnp.minimum(r0 + t1, seq_len - SUBLANES), SUBLANES)
            prev = jnp.where(c > 0, xr_ref[s, pl.ds(p0, SUBLANES), :], 0.0)
            nxt = jnp.where(c < nc1 - 1, xr_ref[s, pl.ds(n0, SUBLANES), :], 0.0)
            win = jnp.concatenate([prev, cur, nxt], axis=0)
            xr = cb
            for k in range(RG_CONV_W):
                xr = xr + cw[k:k + 1, :] * _shift_rows(win, k - RG_CONV_W // 2)[SUBLANES:SUBLANES + t1]
            z = jnp.dot(xr.astype(BF16), w4_ref[0], preferred_element_type=F32) + b4
            for d_i, (a_scr, b_scr) in enumerate(((af, bf), (ab, bb))):
                r = _sigmoid(z[:, (2 * d_i) * hd:(2 * d_i + 1) * hd])
                gi = _sigmoid(z[:, (2 * d_i + 1) * hd:(2 * d_i + 2) * hd])
                log_a = (-RG_C) * r * sp[d_i:d_i + 1, :]
                a = jnp.exp(log_a)
                a_scr[pl.ds(r0, t1), :] = a
                b_scr[pl.ds(r0, t1), :] = jnp.sqrt(-jnp.tanh(log_a) * (a * a + 1.0)) * (gi * xr)
            return carry

        lax.fori_loop(0, nc1, gates, 0)
        h0 = h0_ref[s]

        def fwd(c, carry):
            r0 = pl.multiple_of(c * tc, tc)
            a, h = _scan_chunk(af[pl.ds(r0, tc), :], bf[pl.ds(r0, tc), :], False)
            h = a * carry + h
            hf[pl.ds(r0, tc), :] = h
            return h[tc - 1:tc, :]

        s_f = lax.fori_loop(0, ncs, fwd, h0[0:1, :])

        def bwd(cc, carry):
            r0 = pl.multiple_of((ncs - 1 - cc) * tc, tc)
            a, h = _scan_chunk(ab[pl.ds(r0, tc), :], bb[pl.ds(r0, tc), :], True)
            h = a * carry + h
            g = gr_ref[s, pl.ds(r0, tc), :]
            y_ref[s, pl.ds(r0, tc), :] = jax.nn.gelu(g) * (hf[pl.ds(r0, tc), :] + h)
            return h[0:1, :]

        s_b = lax.fori_loop(0, ncs, bwd, h0[1:2, :])
        st_ref[s] = jnp.concatenate([s_f, s_b], axis=0)
        return carry0

    lax.fori_loop(0, n_seq, per_seq, 0)


def _rglru(u3, h0, conv_w, conv_b, w4, b4, lam, n_seq):
    b, seq_len, _ = u3.shape
    hd = w4.shape[1]
    d_rg = hd * RG_HEADS
    t1 = min(seq_len, 256)
    tc = 64
    kern = functools.partial(_rglru_kernel, seq_len=seq_len, n_seq=n_seq, t1=t1, tc=tc)
    return pl.pallas_call(
        kern,
        grid=(b // n_seq, RG_HEADS),
        in_specs=[pl.BlockSpec((n_seq, seq_len, hd), lambda i, h: (i, 0, h)),
                  pl.BlockSpec((n_seq, seq_len, hd), lambda i, h: (i, 0, RG_HEADS + h)),
                  pl.BlockSpec((RG_CONV_W, hd), lambda i, h: (0, h)),
                  pl.BlockSpec((1, hd), lambda i, h: (0, h)),
                  pl.BlockSpec((1, hd, 4 * hd), lambda i, h: (h, 0, 0)),
                  pl.BlockSpec((1, 1, 4 * hd), lambda i, h: (h, 0, 0)),
                  pl.BlockSpec((2, hd), lambda i, h: (0, h)),
                  pl.BlockSpec((n_seq, 2, hd), lambda i, h: (i, 0, h))],
        out_specs=[pl.BlockSpec((n_seq, seq_len, hd), lambda i, h: (i, 0, h)),
                   pl.BlockSpec((n_seq, 2, hd), lambda i, h: (i, 0, h))],
        out_shape=[jax.ShapeDtypeStruct((b, seq_len, d_rg), F32),
                   jax.ShapeDtypeStruct((b, 2, d_rg), F32)],
        scratch_shapes=[pltpu.VMEM((seq_len, hd), F32)] * 5,
        compiler_params=_params(2),
        name="rglru",
    )(u3, u3, conv_w, conv_b, w4, b4, lam, h0)


def _filt_time_kernel(w1_ref, b1_ref, fr_ref, w2_ref, b2_ref, w3_ref, dec_ref,
                      g_ref, d_ref, st_ref, *, seq_len, tl):
    i = pl.program_id(0)
    hi = lax.Precision.HIGHEST
    half = seq_len // 2
    r = i * tl + lax.broadcasted_iota(jnp.int32, (tl, LANES), 0)
    posi = 2 * (r & (half - 1)) + jnp.where(r >= half, 1, 0)
    pos = posi.astype(F32)
    lane = lax.broadcasted_iota(jnp.int32, (tl, LANES), 1)
    band = jnp.where(lane <= HY_BANDS, lane, lane - HY_BANDS).astype(F32)
    ang = (2.0 * math.pi) * pos / seq_len * band
    t = pos / seq_len
    feats = jnp.where(lane == 0, t,
                      jnp.where(lane <= HY_BANDS, jnp.cos(ang),
                                jnp.where(lane <= 2 * HY_BANDS, -jnp.sin(ang), 0.0)))
    fr = fr_ref[...]
    hid = jnp.sin(fr * (jnp.dot(feats, w1_ref[...], precision=hi, preferred_element_type=F32) + b1_ref[...]))
    hid = jnp.sin(fr * (jnp.dot(hid, w2_ref[...], precision=hi, preferred_element_type=F32) + b2_ref[...]))
    k = jnp.dot(hid.astype(BF16), w3_ref[...].astype(BF16), preferred_element_type=F32)
    k = k * jnp.exp(-t[:, 0:1] * jnp.abs(dec_ref[...]))
    c = k.shape[1] // 4
    first = posi[:, 0:1] == 0
    p4 = posi[:, 0:1] & 3
    sign = jnp.where(p4 == 0, 1.0, jnp.where(p4 == 2, -1.0, 0.0))

    @pl.when(i == 0)
    def _():
        st_ref[...] = jnp.zeros_like(st_ref)

    for o in range(2):
        kf = k[:, (2 * o) * c:(2 * o + 1) * c]
        kb = jnp.where(first, 0.0, k[:, (2 * o + 1) * c:(2 * o + 2) * c])
        g = kf + kb
        g_ref[:, o * c:(o + 1) * c] = g.astype(BF16)
        d_ref[:, o * c:(o + 1) * c] = (kf - kb).astype(BF16)
        st_ref[0:1, o * c:(o + 1) * c] += jnp.sum(kf * kf + kb * kb, axis=0, keepdims=True)
        st_ref[1:2, o * c:(o + 1) * c] += jnp.sum(sign * g, axis=0, keepdims=True)


def _half_dfts(tabs, cos_even, sin_even, cos_odd, sin_odd):
    ce, se, co, so = tabs
    ae = jnp.dot(ce[...], cos_even, preferred_element_type=F32)
    ao = jnp.dot(co[...], cos_odd, preferred_element_type=F32)
    be = jnp.dot(se[...], sin_even, preferred_element_type=F32)
    bo = jnp.dot(so[...], sin_odd, preferred_element_type=F32)
    return ae, ao, be, bo


def _filt_dft_kernel(ce_ref, se_ref, co_ref, so_ref, g_ref, d_ref, st_ref, kf_ref, ks_ref, qf_ref, qs_ref):
    m = pl.program_id(1)
    half = g_ref.shape[0] // 2
    ae, ao, be, bo = _half_dfts((ce_ref, se_ref, co_ref, so_ref),
                                g_ref[0:half, :], d_ref[0:half, :], g_ref[half:, :], d_ref[half:, :])
    scale = lax.rsqrt(st_ref[0:1, :] + EPS)
    is0 = (m * ae.shape[0] + lax.broadcasted_iota(jnp.int32, ae.shape, 0)) == 0
    kf_ref[...] = (ae + ao) * scale
    ks_ref[...] = (ae - ao) * scale
    qf_ref[...] = jnp.where(is0, st_ref[1:2, :], be + bo) * scale
    qs_ref[...] = jnp.where(is0, bo, bo - be) * scale


def _hyena_filters(seq_len, tabs, w1, b1, freq, w2, b2, w3, decay):
    n_hid = w1.shape[1]
    n_out = w3.shape[1]
    c2 = n_out // 2
    tl = min(seq_len, 512)
    w1p = jnp.zeros((LANES, n_hid), F32).at[:w1.shape[0]].set(w1)
    g, d, stats = pl.pallas_call(
        functools.partial(_filt_time_kernel, seq_len=seq_len, tl=tl),
        grid=(seq_len // tl,),
        in_specs=[pl.BlockSpec((LANES, n_hid), lambda i: (0, 0)),
                  pl.BlockSpec((1, n_hid), lambda i: (0, 0)),
                  pl.BlockSpec((1, n_hid), lambda i: (0, 0)),
                  pl.BlockSpec((n_hid, n_hid), lambda i: (0, 0)),
                  pl.BlockSpec((1, n_hid), lambda i: (0, 0)),
                  pl.BlockSpec((n_hid, n_out), lambda i: (0, 0)),
                  pl.BlockSpec((1, n_out), lambda i: (0, 0))],
        out_specs=[pl.BlockSpec((tl, c2), lambda i: (i, 0)),
                   pl.BlockSpec((tl, c2), lambda i: (i, 0)),
                   pl.BlockSpec((SUBLANES, c2), lambda i: (0, 0))],
        out_shape=[jax.ShapeDtypeStruct((seq_len, c2), BF16),
                   jax.ShapeDtypeStruct((seq_len, c2), BF16),
                   jax.ShapeDtypeStruct((SUBLANES, c2), F32)],
        compiler_params=_params(1),
        name="hyena_filter_taps",
    )(w1p, b1, freq, w2, b2, w3, decay)
    half = seq_len // 2
    tm = min(half, 512)
    tn = 512
    tab = pl.BlockSpec((tm, half), lambda n, m: (m, 0))
    return pl.pallas_call(
        _filt_dft_kernel,
        grid=(c2 // tn, half // tm),
        in_specs=[tab, tab, tab, tab,
                  pl.BlockSpec((seq_len, tn), lambda n, m: (0, n)),
                  pl.BlockSpec((seq_len, tn), lambda n, m: (0, n)),
                  pl.BlockSpec((SUBLANES, tn), lambda n, m: (0, n))],
        out_specs=[pl.BlockSpec((tm, tn), lambda n, m: (m, n))] * 4,
        out_shape=[jax.ShapeDtypeStruct((half, c2), F32)] * 4,
        compiler_params=_params(2),
        name="hyena_filter_dft",
    )(tabs[0], tabs[1], tabs[2], tabs[3], g, d, stats)


def _dft_matrices(seq_len):
    assert seq_len & (seq_len - 1) == 0
    half = seq_len // 2
    tm = min(half, 256)
    out = jax.ShapeDtypeStruct((half, half), BF16)
    return pl.pallas_call(
        functools.partial(_dft_table_kernel, seq_len=seq_len),
        grid=(half // tm,),
        in_specs=[],
        out_specs=[pl.BlockSpec((tm, half), lambda i: (i, 0))] * 7,
        out_shape=[out] * 7,
        scratch_shapes=[pltpu.VMEM((tm, half), F32)] * 4,
        compiler_params=_params(1),
        name="dft_tables",
    )()


def _dft_table_kernel(ce_ref, se_ref, co_ref, so_ref, set_ref, cot_ref, sot_ref, c0e, s0e, c0o, s0o, *, seq_len):
    i = pl.program_id(0)
    tm, half = ce_ref.shape
    wrap = 2 * seq_len - 1
    unit = math.pi / seq_len
    row = lax.broadcasted_iota(jnp.int32, (tm, half), 0)
    col = lax.broadcasted_iota(jnp.int32, (tm, half), 1)
    col1 = col[0:1, :]

    def trig(n):
        ang = (n & wrap).astype(F32) * unit
        return jnp.cos(ang), jnp.sin(ang)

    @pl.when(i == 0)
    def _():
        c0e[...], s0e[...] = trig(2 * row * col)
        c0o[...], s0o[...] = trig(row * (2 * col + 1))

    def rotate(c0, s0, n):
        cn, sn = trig(n)
        return c0[...] * cn - s0[...] * sn, s0[...] * cn + c0[...] * sn

    r0 = i * tm
    ce, se = rotate(c0e, s0e, 2 * r0 * col1)
    co, so = rotate(c0o, s0o, r0 * (2 * col1 + 1))
    cot, sot = rotate(c0e, s0e, (2 * r0 + 1) * col1)
    alt_col = jnp.where((col & 1) == 0, 1.0, -1.0)
    alt_row = jnp.where(((row + r0) & 1) == 0, 1.0, -1.0)
    first_row = row + r0 == 0
    ce_ref[...] = ce.astype(BF16)
    se_ref[...] = jnp.where(first_row, alt_col, se).astype(BF16)
    co_ref[...] = co.astype(BF16)
    so_ref[...] = jnp.where(first_row, alt_col, so).astype(BF16)
    set_ref[...] = jnp.where(col == 0, alt_row, se).astype(BF16)
    cot_ref[...] = cot.astype(BF16)
    sot_ref[...] = jnp.where(col == 0, alt_row, sot).astype(BF16)


def _hy_fwd_kernel(ce_ref, se_ref, co_ref, so_ref, u_ref, kf_ref, ks_ref, qf_ref, qs_ref,
                   ee_ref, eo_ref, de_ref, do_ref, *, seq_len):
    m = pl.program_id(2)
    kf, ks, qf, qs = kf_ref[...], ks_ref[...], qf_ref[...], qs_ref[...]
    is0 = (m * kf.shape[0] + lax.broadcasted_iota(jnp.int32, kf.shape, 0)) == 0
    inv_n = 0.5 / seq_len
    w = jnp.where(is0, inv_n, 2.0 * inv_n)
    for s in range(u_ref.shape[0]):
        ue, uo = u_ref[s, 0], u_ref[s, 1]
        ae, ao, be, bo = _half_dfts((ce_ref, se_ref, co_ref, so_ref), ue, ue, uo, uo)
        a_f, a_s = ae + ao, ae - ao
        b_f, b_s = be + bo, bo - be
        pre_f = jnp.where(is0, a_f * kf, a_f * kf - b_f * qf)
        pre_s = jnp.where(is0, a_s * ks, a_s * ks - b_s * qs)
        pm_f = a_f * qf + b_f * kf
        pm_s = a_s * qs + b_s * ks
        mid_re = be * qf - bo * qs
        mid_mim = be * qs + bo * qf
        ee_ref[s] = ((pre_f + pre_s) * w).astype(BF16)
        eo_ref[s] = ((pre_f - pre_s) * w).astype(BF16)
        de_ref[s] = (jnp.where(is0, mid_re, pm_f - pm_s) * (2.0 * inv_n)).astype(BF16)
        do_ref[s] = (jnp.where(is0, mid_mim, pm_f + pm_s) * (2.0 * inv_n)).astype(BF16)


def _hy_inv_kernel(*refs, natural_out):
    if natural_out:
        ce_ref, set_ref, cot_ref, sot_ref, ee_ref, eo_ref, de_ref, do_ref, u_ref, x_ref, bias_ref, z_ref, scr = refs
    else:
        ce_ref, set_ref, cot_ref, sot_ref, ee_ref, eo_ref, de_ref, do_ref, u_ref, x_ref, bias_ref, z_ref = refs
    bias = bias_ref[...]
    for s in range(u_ref.shape[0]):
        y_e = jnp.dot(ce_ref[...], ee_ref[s], preferred_element_type=F32)
        y_e = y_e + jnp.dot(set_ref[...], de_ref[s], preferred_element_type=F32)
        y_o = jnp.dot(cot_ref[...], eo_ref[s], preferred_element_type=F32)
        y_o = y_o + jnp.dot(sot_ref[...], do_ref[s], preferred_element_type=F32)
        z_e = (y_e + u_ref[s, 0].astype(F32) * bias) * x_ref[s, 0].astype(F32)
        z_o = (y_o + u_ref[s, 1].astype(F32) * bias) * x_ref[s, 1].astype(F32)
        if natural_out:
            th = z_e.shape[0]
            for j in range(z_e.shape[1] // LANES):
                scr[j, pl.ds(0, th, stride=2), :] = z_e[:, j * LANES:(j + 1) * LANES]
                scr[j, pl.ds(1, th, stride=2), :] = z_o[:, j * LANES:(j + 1) * LANES]
            for j in range(z_e.shape[1] // LANES):
                z_ref[s, :, j * LANES:(j + 1) * LANES] = scr[j].astype(BF16)
        else:
            z_ref[s, 0] = z_e.astype(BF16)
            z_ref[s, 1] = z_o.astype(BF16)


def _hyena_order(order, u_arr, u_cb, hvc, gate_cb, tabs, filt, bias, tm, tc, natural_out):
    ce, se, co, so, se_t, co_t, so_t = tabs
    kf, ks, qf, qs = filt
    b, _, half, _ = hvc.shape
    seq_len = 2 * half
    c = kf.shape[1] // 2
    nct = c // tc
    sb = max(1, min(b, 1024 // seq_len))
    grid = (b // sb, nct, half // tm)
    tab = pl.BlockSpec((tm, half), lambda i, n, m: (m, 0))
    spec = pl.BlockSpec((tm, tc), lambda i, n, m: (m, order * nct + n))
    freq = pl.BlockSpec((sb, tm, tc), lambda i, n, m: (i, m, n))
    ee, eo, de, do = pl.pallas_call(
        functools.partial(_hy_fwd_kernel, seq_len=seq_len),
        grid=grid,
        in_specs=[tab, tab, tab, tab,
                  pl.BlockSpec((sb, 2, half, tc), lambda i, n, m: (i, 0, 0, u_cb * nct + n)),
                  spec, spec, spec, spec],
        out_specs=[freq] * 4,
        out_shape=[jax.ShapeDtypeStruct((b, half, c), BF16)] * 4,
        compiler_params=_params(3),
        name="hyena_fwd_dft",
    )(ce, se, co, so, u_arr, kf, ks, qf, qs)
    whole = pl.BlockSpec((sb, half, tc), lambda i, n, m: (i, 0, n))
    if natural_out:
        out_spec = pl.BlockSpec((sb, 2 * tm, tc), lambda i, n, m: (i, m, n))
        out_shape = jax.ShapeDtypeStruct((b, seq_len, c), BF16)
        scratch = [pltpu.VMEM((tc // LANES, 2 * tm, LANES), F32)]
    else:
        out_spec = pl.BlockSpec((sb, 2, tm, tc), lambda i, n, m: (i, 0, m, n))
        out_shape = jax.ShapeDtypeStruct((b, 2, half, c), BF16)
        scratch = []
    return pl.pallas_call(
        functools.partial(_hy_inv_kernel, natural_out=natural_out),
        grid=grid,
        in_specs=[tab, tab, tab, tab, whole, whole, whole, whole,
                  pl.BlockSpec((sb, 2, tm, tc), lambda i, n, m: (i, 0, m, u_cb * nct + n)),
                  pl.BlockSpec((sb, 2, tm, tc), lambda i, n, m: (i, 0, m, gate_cb * nct + n)),
                  pl.BlockSpec((1, tc), lambda i, n, m: (0, n))],
        out_specs=out_spec,
        out_shape=out_shape,
        scratch_shapes=scratch,
        compiler_params=_params(3),
        name="hyena_inv_dft",
    )(ce, se_t, co_t, so_t, ee, eo, de, do, u_arr, hvc, bias[order][None])


def _post_mixer_kernel(*refs, has_pos):
    if has_pos:
        (yrg_ref, yhy_ref, x_ref, pos_ref, mod_ref, gnr_ref, gnh_ref, wo_ref, n2_ref,
         rw_ref, rb_ref, x1_ref, h2_ref, idx_ref, gw_ref) = refs
    else:
        (yrg_ref, yhy_ref, x_ref, mod_ref, gnr_ref, gnh_ref, wo_ref, n2_ref,
         rw_ref, rb_ref, x1_ref, h2_ref, idx_ref, gw_ref) = refs
    d_rg = yrg_ref.shape[1]
    na = _rms(yrg_ref[...], gnr_ref[...]).astype(BF16)
    nb = _rms(yhy_ref[...].astype(F32), gnh_ref[...]).astype(BF16)
    y = jnp.dot(na, wo_ref[0:d_rg, :], preferred_element_type=F32)
    y = y + jnp.dot(nb, wo_ref[d_rg:, :], preferred_element_type=F32)
    x = x_ref[...]
    if has_pos:
        x = x + pos_ref[...]
    x1 = x + mod_ref[0, 2:3, :] * y
    x1_ref[...] = x1
    h2f = _rms(x1, n2_ref[...]) * (1.0 + mod_ref[0, 4:5, :]) + mod_ref[0, 3:4, :]
    half = h2f.shape[1] // 2
    _store_token_tiles(h2_ref, _pack_pair(h2f[:, :half], h2f[:, half:]))
    h2 = h2f.astype(BF16)
    scores = jax.nn.sigmoid(jnp.dot(h2, rw_ref[...], preferred_element_type=F32))
    sel = scores + rb_ref[...]
    n_exp = scores.shape[1]
    lane = lax.broadcasted_iota(jnp.int32, scores.shape, 1)
    col = lax.broadcasted_iota(jnp.int32, idx_ref.shape, 1)
    idx_acc = jnp.zeros(idx_ref.shape, jnp.int32)
    gw_acc = jnp.zeros(gw_ref.shape, F32)
    for k in range(TOP_K):
        mx = jnp.max(sel, axis=1, keepdims=True)
        pick = jnp.min(jnp.where(sel == mx, lane, n_exp), axis=1, keepdims=True)
        hit = lane == pick
        val = jnp.sum(jnp.where(hit, scores, 0.0), axis=1, keepdims=True)
        sel = jnp.where(hit, -jnp.inf, sel)
        idx_acc = jnp.where(col == k, pick, idx_acc)
        gw_acc = jnp.where(col == k, val, gw_acc)
    idx_ref[...] = idx_acc
    gw_ref[...] = gw_acc / jnp.sum(gw_acc, axis=1, keepdims=True) * ROUTED_SCALE


def _post_mixer(y_rg, y_hy, x2d, pos, mod, gn_rg, gn_hy, w_out_bf, norm2_g, router_w_bf, router_b,
                group_of_tile, tm=256):
    t, d = x2d.shape
    d_rg = y_rg.shape[1]
    d_hy = y_hy.shape[1]
    n_exp = router_w_bf.shape[1]
    has_pos = pos is not None
    in_specs = [pl.BlockSpec((tm, d_rg), lambda i: (i, 0)),
                pl.BlockSpec((tm, d_hy), lambda i: (i, 0)),
                pl.BlockSpec((tm, d), lambda i: (i, 0))]
    args = [y_rg, y_hy, x2d]
    if has_pos:
        pos_tiles = pos.shape[0] // tm
        in_specs.append(pl.BlockSpec((tm, d), lambda i: (i % pos_tiles, 0)))
        args.append(pos)
    in_specs += [pl.BlockSpec((1, N_MOD, d), lambda i: (group_of_tile(i, tm), 0, 0)),
                 pl.BlockSpec((1, d_rg), lambda i: (0, 0)),
                 pl.BlockSpec((1, d_hy), lambda i: (0, 0)),
                 pl.BlockSpec((d_rg + d_hy, d), lambda i: (0, 0)),
                 pl.BlockSpec((1, d), lambda i: (0, 0)),
                 pl.BlockSpec((d, n_exp), lambda i: (0, 0)),
                 pl.BlockSpec((1, n_exp), lambda i: (0, 0))]
    args += [mod, gn_rg, gn_hy, w_out_bf, norm2_g, router_w_bf, router_b]
    return pl.pallas_call(
        functools.partial(_post_mixer_kernel, has_pos=has_pos),
        grid=(t // tm,),
        in_specs=in_specs,
        out_specs=[pl.BlockSpec((tm, d), lambda i: (i, 0)),
                   pl.BlockSpec((tm * SUBLANES, LANES), lambda i: (i, 0)),
                   pl.BlockSpec((tm, TOP_K), lambda i: (i, 0)),
                   pl.BlockSpec((tm, TOP_K), lambda i: (i, 0))],
        out_shape=[jax.ShapeDtypeStruct((t, d), F32),
                   jax.ShapeDtypeStruct((t * SUBLANES, LANES), jnp.int32),
                   jax.ShapeDtypeStruct((t, TOP_K), jnp.int32),
                   jax.ShapeDtypeStruct((t, TOP_K), F32)],
        compiler_params=_params(1),
        name="post_mixer",
    )(*args)


def _route_kernel(idx_ref, dest_ref, be_ref, bv_ref, nu_ref, tri, cnt, base, pst, *, blk, n_blk):
    p = pl.program_id(0)
    i = pl.program_id(1)
    tm, top_k = idx_ref.shape
    ne = cnt.shape[1]

    def div_blk(n):
        return jnp.floor((n + 0.5) / blk)

    @pl.when(jnp.logical_and(p == 0, i == 0))
    def _():
        r = lax.broadcasted_iota(jnp.int32, (tm, tm), 0)
        c = lax.broadcasted_iota(jnp.int32, (tm, tm), 1)
        tri[...] = jnp.where(r > c, 1.0, 0.0).astype(BF16)
        cnt[...] = jnp.zeros_like(cnt)

    idx = idx_ref[...]
    lane = lax.broadcasted_iota(jnp.int32, (tm, ne), 1)
    hits = [lane == idx[:, k:k + 1] for k in range(top_k)]
    occ = jnp.zeros((tm, ne), F32)
    for h in hits:
        occ = occ + jnp.where(h, 1.0, 0.0)
    col_sum = jnp.sum(occ, axis=0, keepdims=True)

    @pl.when(p == 0)
    def _():
        cnt[0:1, :] += col_sum

    @pl.when(jnp.logical_and(p == 1, i == 0))
    def _():
        counts = cnt[...]
        padded = div_blk(counts + (blk - 1.0)) * blk
        r = lax.broadcasted_iota(jnp.int32, (ne, ne), 0)
        c = lax.broadcasted_iota(jnp.int32, (ne, ne), 1)
        upper = jnp.where(r <= c, 1.0, 0.0)
        pend = jnp.dot(padded, upper, precision=lax.Precision.HIGHEST, preferred_element_type=F32)
        pstart = pend - padded
        pst[...] = pstart
        base[...] = jnp.zeros_like(base)
        b0 = (lax.broadcasted_iota(jnp.int32, (n_blk, ne), 0) * blk).astype(F32)
        be = jnp.sum(jnp.where(pend[0:1, :] <= b0, 1.0, 0.0), axis=1, keepdims=True)
        be = jnp.minimum(be, ne - 1.0)
        own = lax.broadcasted_iota(jnp.int32, (n_blk, ne), 1).astype(F32) == be
        pst_b = jnp.sum(jnp.where(own, pstart[0:1, :], 0.0), axis=1, keepdims=True)
        cnt_b = jnp.sum(jnp.where(own, counts[0:1, :], 0.0), axis=1, keepdims=True)
        valid = jnp.clip(cnt_b - (b0[:, 0:1] - pst_b), 0.0, float(blk))
        be_ref[...] = be.astype(jnp.int32)
        bv_ref[...] = valid.astype(jnp.int32)
        total = jnp.max(pend[0:1, :], axis=1, keepdims=True)
        nu_ref[...] = jnp.broadcast_to(div_blk(total).astype(jnp.int32), nu_ref.shape)

    @pl.when(p == 1)
    def _():
        cum = jnp.dot(tri[...], occ.astype(BF16), preferred_element_type=F32) + base[0:1, :] + pst[0:1, :]
        col = lax.broadcasted_iota(jnp.int32, (tm, top_k), 1)
        acc = jnp.zeros((tm, top_k), F32)
        for k in range(top_k):
            v = jnp.sum(jnp.where(hits[k], cum, 0.0), axis=1, keepdims=True)
            acc = jnp.where(col == k, v, acc)
        dest_ref[...] = acc.astype(jnp.int32)
        base[0:1, :] += col_sum


def _route(idx, n_exp, blk, tm=512):
    n_tok, top_k = idx.shape
    n_blk = -(-n_tok * top_k // blk) + n_exp
    dest, be, bv, nu = pl.pallas_call(
        functools.partial(_route_kernel, blk=blk, n_blk=n_blk),
        grid=(2, n_tok // tm),
        in_specs=[pl.BlockSpec((tm, top_k), lambda p, i: (i, 0))],
        out_specs=[pl.BlockSpec((tm, top_k), lambda p, i: (i * p, 0)),
                   pl.BlockSpec((n_blk, 1), lambda p, i: (0, 0)),
                   pl.BlockSpec((n_blk, 1), lambda p, i: (0, 0)),
                   pl.BlockSpec((SUBLANES, LANES), lambda p, i: (0, 0))],
        out_shape=[jax.ShapeDtypeStruct((n_tok, top_k), jnp.int32),
                   jax.ShapeDtypeStruct((n_blk, 1), jnp.int32),
                   jax.ShapeDtypeStruct((n_blk, 1), jnp.int32),
                   jax.ShapeDtypeStruct((SUBLANES, LANES), jnp.int32)],
        scratch_shapes=[pltpu.VMEM((tm, tm), BF16), pltpu.VMEM((SUBLANES, n_exp), F32),
                        pltpu.VMEM((SUBLANES, n_exp), F32), pltpu.VMEM((SUBLANES, n_exp), F32)],
        compiler_params=_params(2),
        name="route",
    )(idx)
    return dest, be.reshape(n_blk), bv.reshape(n_blk), nu[0, 0:1]


def _dispatch_kernel(dest_ref, xa_ref, xl_ref, sg_ref, su_ref, sd_ref, out_ref, sh_ref, sem, *, n_a):
    tm = xa_ref.shape[0] // SUBLANES
    dh = SUBLANES * LANES

    def tile(ref, r):
        return ref.at[pl.ds(pl.multiple_of(r * SUBLANES, SUBLANES), SUBLANES), :]

    def run(src):
        def issue(t, c):
            for k in range(TOP_K):
                pltpu.make_async_copy(tile(src, t), tile(out_ref, dest_ref[t * TOP_K + k]), sem).start(priority=k % 2)
            return c

        lax.fori_loop(0, tm, issue, 0)

        lo, hi = _unpack_pair(_load_token_tiles(src, 0, tm))
        lo = lo.astype(BF16)
        hi = hi.astype(BF16)
        g = jnp.dot(lo, sg_ref[0:dh, :], preferred_element_type=F32)
        g = g + jnp.dot(hi, sg_ref[dh:, :], preferred_element_type=F32)
        u = jnp.dot(lo, su_ref[0:dh, :], preferred_element_type=F32)
        u = u + jnp.dot(hi, su_ref[dh:, :], preferred_element_type=F32)
        hmid = (g * _sigmoid(g) * u).astype(BF16)
        sh_ref[...] = jnp.dot(hmid, sd_ref[...], preferred_element_type=F32).astype(BF16)

        def drain(t, c):
            for k in range(TOP_K):
                pltpu.make_async_copy(tile(src, 0), tile(out_ref, 0), sem).wait()
            return c

        lax.fori_loop(0, tm, drain, 0, unroll=8)

    @pl.when(pl.program_id(0) < n_a)
    def _():
        run(xa_ref)

    @pl.when(pl.program_id(0) >= n_a)
    def _():
        run(xl_ref)


def _dispatch(dest_flat, h2p_a, h2p_l, n_rows, sg_bf, su_bf, sd_bf, tm=256):
    n_a = h2p_a.shape[0] // (tm * SUBLANES)
    n_l = h2p_l.shape[0] // (tm * SUBLANES)
    d, ds_ = sg_bf.shape
    return pl.pallas_call(
        functools.partial(_dispatch_kernel, n_a=n_a),
        grid=(n_a + n_l,),
        in_specs=[pl.BlockSpec((tm * TOP_K,), lambda i: (i,), memory_space=pltpu.SMEM),
                  pl.BlockSpec((tm * SUBLANES, LANES), lambda i: (jnp.minimum(i, n_a - 1), 0)),
                  pl.BlockSpec((tm * SUBLANES, LANES), lambda i: (jnp.maximum(i - n_a, 0), 0)),
                  pl.BlockSpec((d, ds_), lambda i: (0, 0)),
                  pl.BlockSpec((d, ds_), lambda i: (0, 0)),
                  pl.BlockSpec((ds_, d), lambda i: (0, 0))],
        out_specs=[pl.BlockSpec(memory_space=pl.ANY),
                   pl.BlockSpec((tm, d), lambda i: (i, 0))],
        out_shape=[jax.ShapeDtypeStruct((n_rows * SUBLANES, LANES), jnp.int32),
                   jax.ShapeDtypeStruct(((n_a + n_l) * tm, d), BF16)],
        scratch_shapes=[pltpu.SemaphoreType.DMA(())],
        compiler_params=_params(1),
        name="dispatch",
    )(dest_flat, h2p_a, h2p_l, sg_bf, su_bf, sd_bf)


WEIGHT_DMA_PRIORITY = 1


def _expert_kernel(be_ref, bv_ref, nu_ref, x_ref, wg_hbm, wu_hbm, wd_hbm, o_ref,
                   wg_f, wu_f, wd_f, wg_b, wu_b, wd_b, sem, grp, *, n_blk):
    i = pl.program_id(0)
    n_used = nu_ref[0]
    e = be_ref[i]
    active = i < n_used
    changed = jnp.logical_or(i == 0, e != be_ref[jnp.maximum(i - 1, 0)])

    def expert_at(j):
        return be_ref[jnp.minimum(j, n_blk - 1)]

    def next_group(j):
        ej = expert_at(j)
        return lax.while_loop(lambda q: jnp.logical_and(q < n_used, expert_at(q) == ej), lambda q: q + 1, j + 1)

    def copies(ex, slot):
        return (pltpu.make_async_copy(wg_hbm.at[ex], wg_f.at[slot], sem.at[slot, 0]),
                pltpu.make_async_copy(wu_hbm.at[ex], wu_f.at[slot], sem.at[slot, 1]),
                pltpu.make_async_copy(wd_hbm.at[ex], wd_f.at[slot], sem.at[slot, 2]))

    @pl.when(jnp.logical_and(active, i == 0))
    def _():
        grp[0] = 0
        for cp in copies(e, 0):
            cp.start(priority=WEIGHT_DMA_PRIORITY)
        n1 = next_group(i)

        @pl.when(n1 < n_used)
        def _():
            for cp in copies(expert_at(n1), 1):
                cp.start(priority=WEIGHT_DMA_PRIORITY)

    @pl.when(jnp.logical_and(active, changed))
    def _():
        slot = grp[0] % 2
        for cp in copies(e, slot):
            cp.wait()
        wg_b[...] = wg_f[slot].astype(BF16)
        wu_b[...] = wu_f[slot].astype(BF16)
        wd_b[...] = wd_f[slot].astype(BF16)
        n2 = next_group(next_group(i))

        @pl.when(n2 < n_used)
        def _():
            for cp in copies(expert_at(n2), slot):
                cp.start(priority=WEIGHT_DMA_PRIORITY)

        grp[0] = grp[0] + 1

    @pl.when(active)
    def _():
        tm = x_ref.shape[0] // SUBLANES
        dh = SUBLANES * LANES
        live = lax.broadcasted_iota(jnp.int32, (tm, dh), 0) < bv_ref[i]
        lo, hi = _unpack_pair(_load_token_tiles(x_ref, 0, tm))
        lo = jnp.where(live, lo, 0.0).astype(BF16)
        hi = jnp.where(live, hi, 0.0).astype(BF16)
        g = jnp.dot(lo, wg_b[0:dh, :], preferred_element_type=F32)
        g = g + jnp.dot(hi, wg_b[dh:, :], preferred_element_type=F32)
        u = jnp.dot(lo, wu_b[0:dh, :], preferred_element_type=F32)
        u = u + jnp.dot(hi, wu_b[dh:, :], preferred_element_type=F32)
        hmid = (g * _sigmoid(g) * u).astype(BF16)
        y = jnp.dot(hmid, wd_b[...], preferred_element_type=F32)
        _store_token_tiles(o_ref, _pack_pair(y[:, :dh], y[:, dh:]))


def _experts(blk_e, blk_valid, n_used, xb, wg, wu, wd):
    n_exp, d, de = wg.shape
    tm = EXPERT_ROWS
    n_blk = xb.shape[0] // (tm * SUBLANES)
    grid_spec = pltpu.PrefetchScalarGridSpec(
        num_scalar_prefetch=3,
        grid=(n_blk,),
        in_specs=[pl.BlockSpec((tm * SUBLANES, LANES), lambda i, be, bv, nu: (jnp.minimum(i, nu[0] - 1), 0)),
                  pl.BlockSpec(memory_space=pl.ANY),
                  pl.BlockSpec(memory_space=pl.ANY),
                  pl.BlockSpec(memory_space=pl.ANY)],
        out_specs=pl.BlockSpec((tm * SUBLANES, LANES), lambda i, be, bv, nu: (jnp.minimum(i, nu[0] - 1), 0)),
        scratch_shapes=[pltpu.VMEM((2, d, de), F32), pltpu.VMEM((2, d, de), F32), pltpu.VMEM((2, de, d), F32),
                        pltpu.VMEM((d, de), BF16), pltpu.VMEM((d, de), BF16), pltpu.VMEM((de, d), BF16),
                        pltpu.SemaphoreType.DMA((2, 3)), pltpu.SMEM((1,), jnp.int32)],
    )
    return pl.pallas_call(
        functools.partial(_expert_kernel, n_blk=n_blk),
        grid_spec=grid_spec,
        out_shape=jax.ShapeDtypeStruct(xb.shape, jnp.int32),
        compiler_params=_params(1),
        name="routed_experts",
    )(blk_e, blk_valid, n_used, xb, wg, wu, wd)


FINISH_ROWS = 64


def _finish_kernel(dcur_ref, dnxt_ref, sh_ref, x1_ref, gw_ref, mod_ref, fg_ref,
                   yb_hbm, o_ref, gbuf, gwb, sem, *, final_norm):
    i = pl.program_id(0)
    n_tiles = pl.num_programs(0)
    tm, d = x1_ref.shape
    dh = SUBLANES * LANES
    slot = i % 2

    def row_copy(row, t, k, s):
        src = yb_hbm.at[pl.ds(pl.multiple_of(row * SUBLANES, SUBLANES), SUBLANES), :]
        dst = gbuf.at[s, pl.ds(pl.multiple_of((k * tm + t) * SUBLANES, SUBLANES), SUBLANES), :]
        return pltpu.make_async_copy(src, dst, sem.at[s])

    def gather(d_ref, s):
        def issue(t, c):
            for k in range(TOP_K):
                row_copy(d_ref[t * TOP_K + k], t, k, s).start(priority=k % 2)
            return c

        lax.fori_loop(0, tm, issue, 0)

    @pl.when(i == 0)
    def _():
        gather(dcur_ref, 0)

    @pl.when(i + 1 < n_tiles)
    def _():
        gather(dnxt_ref, 1 - slot)

    gw = gw_ref[...]
    for k in range(TOP_K):
        gwb[k] = jnp.broadcast_to(gw[:, k:k + 1], (tm, LANES))

    def drain(t, c):
        for k in range(TOP_K):
            row_copy(0, 0, 0, slot).wait()
        return c

    lax.fori_loop(0, tm, drain, 0, unroll=8)

    rows_g = gbuf.at[slot]
    for r0 in range(0, tm, FINISH_ROWS):
        rs = slice(r0, r0 + FINISH_ROWS)
        ssq = jnp.zeros((FINISH_ROWS, 1), F32)
        for j in range(SUBLANES):
            cols = (slice(j * LANES, (j + 1) * LANES), slice(dh + j * LANES, dh + (j + 1) * LANES))
            acc = [sh_ref[rs, c].astype(F32) for c in cols]
            for k in range(TOP_K):
                halves = _unpack_pair(rows_g[pl.ds((k * tm + r0) * SUBLANES + j, FINISH_ROWS, stride=SUBLANES), :])
                w = gwb[k, rs, :]
                acc = [a + w * h for a, h in zip(acc, halves)]
            for c, a in zip(cols, acc):
                x2 = x1_ref[rs, c] + mod_ref[0, 5:6, c] * a
                o_ref[rs, c] = x2
                ssq = ssq + jnp.sum(x2 * x2, axis=-1, keepdims=True)
        if final_norm:
            inv = lax.rsqrt(ssq / d + EPS)
            o_ref[rs, :] = o_ref[rs, :] * inv * fg_ref[...]


def _finish(dest_flat, shared, x1, gw, yb, mod, final_g, row0, group_of_tile, final_norm, tm=256):
    n_rows, d = x1.shape
    t0 = row0 // tm
    n_tiles = n_rows // tm
    return pl.pallas_call(
        functools.partial(_finish_kernel, final_norm=final_norm),
        grid=(n_tiles,),
        in_specs=[pl.BlockSpec((tm * TOP_K,), lambda i: (t0 + i,), memory_space=pltpu.SMEM),
                  pl.BlockSpec((tm * TOP_K,), lambda i: (t0 + jnp.minimum(i + 1, n_tiles - 1),),
                               memory_space=pltpu.SMEM),
                  pl.BlockSpec((tm, d), lambda i: (t0 + i, 0)),
                  pl.BlockSpec((tm, d), lambda i: (i, 0)),
                  pl.BlockSpec((tm, TOP_K), lambda i: (i, 0)),
                  pl.BlockSpec((1, N_MOD, d), lambda i: (group_of_tile(i, tm), 0, 0)),
                  pl.BlockSpec((1, d), lambda i: (0, 0)),
                  pl.BlockSpec(memory_space=pl.ANY)],
        out_specs=pl.BlockSpec((tm, d), lambda i: (i, 0)),
        out_shape=jax.ShapeDtypeStruct((n_rows, d), F32),
        scratch_shapes=[pltpu.VMEM((2, TOP_K * tm * SUBLANES, LANES), jnp.int32),
                        pltpu.VMEM((TOP_K, tm, LANES), F32), pltpu.SemaphoreType.DMA((2,))],
        compiler_params=_params(1),
        name="finish",
    )(dest_flat, dest_flat, shared, x1, gw, mod, final_g, yb)


def _grid_pos_emb(rows, d):
    quarter = d // 4
    omega = 1.0 / (10000.0 ** (jnp.arange(quarter, dtype=F32) / quarter))
    r = jnp.arange(rows, dtype=F32)[:, None] * omega
    cc = jnp.arange(GRID_W, dtype=F32)[:, None] * omega
    by_row = jnp.concatenate([jnp.sin(r), jnp.cos(r)], axis=-1)
    by_col = jnp.concatenate([jnp.sin(cc), jnp.cos(cc)], axis=-1)
    full = jnp.concatenate([jnp.broadcast_to(by_row[:, None, :], (rows, GRID_W, d // 2)),
                            jnp.broadcast_to(by_col[None, :, :], (rows, GRID_W, d // 2))], axis=-1)
    return full.reshape(rows * GRID_W, d)


def _mixer_path(x2d, pos, n_b, seq_len, h0, mod, group_of_tile, p, mats, filt, n_seq):
    d_rg = p['gn_rg'].shape[1]
    c = p['gn_hy'].shape[1]
    u_rg, hvc = _in_proj(x2d, pos, mod, p['norm1_g'], p['w_in'], p['hy_conv_w'], p['hy_conv_b'],
                         seq_len, group_of_tile)
    u3 = u_rg.reshape(n_b, seq_len, u_rg.shape[1])
    y_rg, st = _rglru(u3, h0, p['rg_conv_w'], p['rg_conv_b'], p['rg_w4'], p['rg_b4'], p['rg_lam'], n_seq)
    tm = min(seq_len // 2, 512)
    tc = 512 if seq_len > 512 else c
    z1 = _hyena_order(0, hvc, 0, hvc, 1, mats, filt, p['hy_bias'], tm, tc, False)
    y_hy = _hyena_order(1, z1, 0, hvc, 2, mats, filt, p['hy_bias'], tm, tc, True)
    x1, h2, idx, gw = _post_mixer(y_rg.reshape(-1, d_rg), y_hy.reshape(-1, c), x2d, pos, mod,
                                  p['gn_rg'], p['gn_hy'], p['w_out'], p['norm2_g'],
                                  p['router_w'], p['router_b'], group_of_tile)
    return x1, h2, idx, gw, st


def kernel(x_prompt, x_sample, state_rglru, c, c_ctx, ada_w, ada_b, norm1_g, norm2_g, w_in, rg_conv_w, rg_conv_b, rg_wa, rg_ba, rg_wx, rg_bx, rg_lam, hy_conv_w, hy_conv_b, hy_w1, hy_b1, hy_freq, hy_w2, hy_b2, hy_w3, hy_decay, hy_bias, gn_rg, gn_hy, w_out, router_w, router_b, exp_w_gate, exp_w_up, exp_w_down, sh_w_gate, sh_w_up, sh_w_down, final_g):
    n_cb, seq_c, d = x_prompt.shape
    n_lb, seq_l, _ = x_sample.shape
    depth = ada_w.shape[0]
    d_rg = gn_rg.shape[1]
    hd = d_rg // RG_HEADS
    t_c = n_cb * seq_c
    t_l = n_lb * seq_l
    assert n_lb + 1 <= SUBLANES
    assert d == 2 * SUBLANES * LANES

    pos = _grid_pos_emb(seq_l // GRID_W, d)
    cvec = jnp.zeros((SUBLANES, d), F32).at[0].set(c_ctx).at[1:1 + n_lb].set(c)
    mats_c = _dft_matrices(seq_c)
    mats_l = _dft_matrices(seq_l)

    def group_ctx(i, tm):
        return 0

    def group_lat(i, tm):
        return 1 + (i * tm) // seq_l

    xc = x_prompt.reshape(t_c, d)
    xs = x_sample.reshape(t_l, d)
    ctx_states = []
    for l in range(depth):
        last = l == depth - 1
        mod = _modulation(cvec, ada_w[l], ada_b[l][None]).reshape(SUBLANES, N_MOD, d)

        w4 = jnp.concatenate([rg_wa[l, 0], rg_wx[l, 0], rg_wa[l, 1], rg_wx[l, 1]], axis=-1).astype(BF16)
        b4 = jnp.concatenate([rg_ba[l, 0].reshape(RG_HEADS, 1, hd), rg_bx[l, 0].reshape(RG_HEADS, 1, hd),
                              rg_ba[l, 1].reshape(RG_HEADS, 1, hd), rg_bx[l, 1].reshape(RG_HEADS, 1, hd)], axis=-1)
        p = {
            'norm1_g': norm1_g[l][None], 'norm2_g': norm2_g[l][None], 'w_in': w_in[l].astype(BF16),
            'rg_conv_w': rg_conv_w[l], 'rg_conv_b': rg_conv_b[l][None], 'rg_w4': w4, 'rg_b4': b4,
            'rg_lam': rg_lam[l], 'hy_conv_w': hy_conv_w[l], 'hy_conv_b': hy_conv_b[l][None],
            'hy_bias': hy_bias[l], 'gn_rg': gn_rg[l][None], 'gn_hy': gn_hy[l][None],
            'w_out': w_out[l].astype(BF16), 'router_w': router_w[l].astype(BF16), 'router_b': router_b[l][None],
        }
        filt_args = (hy_w1[l], hy_b1[l][None], hy_freq[l][None], hy_w2[l], hy_b2[l][None], hy_w3[l],
                     hy_decay[l].reshape(1, -1))
        filt_c = _hyena_filters(seq_c, mats_c[:4], *filt_args)
        filt_l = _hyena_filters(seq_l, mats_l[:4], *filt_args)

        h0_c = jnp.zeros((n_cb, 2, d_rg), F32)
        x1_c, h2_c, idx_c, gw_c, st_c = _mixer_path(xc, None, n_cb, seq_c, h0_c, mod, group_ctx, p,
                                                    mats_c, filt_c, n_seq=min(8, n_cb))
        ctx_states.append(st_c)
        x1_l, h2_l, idx_l, gw_l, _ = _mixer_path(xs, pos if l == 0 else None, n_lb, seq_l,
                                                 state_rglru[:, l], mod, group_lat, p,
                                                 mats_l, filt_l, n_seq=1)

        idx_all = jnp.concatenate([idx_c, idx_l], axis=0)
        dest, blk_e, blk_valid, n_used = _route(idx_all, router_w.shape[-1], EXPERT_ROWS)
        dest_flat = dest.reshape(-1)
        sh = (sh_w_gate[l].astype(BF16), sh_w_up[l].astype(BF16), sh_w_down[l].astype(BF16))
        xb, shared = _dispatch(dest_flat, h2_c, h2_l, blk_e.shape[0] * EXPERT_ROWS, *sh)
        yb = _experts(blk_e, blk_valid, n_used, xb, exp_w_gate[l], exp_w_up[l], exp_w_down[l])
        xc = _finish(dest_flat, shared, x1_c, gw_c, yb, mod, final_g[None], 0, group_ctx, last)
        xs = _finish(dest_flat, shared, x1_l, gw_l, yb, mod, final_g[None], t_c, group_lat, last)

    new_state = jnp.stack(ctx_states, axis=1).astype(x_prompt.dtype)
    return (xc.reshape(n_cb, seq_c, d), xs.reshape(n_lb, seq_l, d), new_state)
```

```python
import functools
import math

import jax
import jax.numpy as jnp
from jax import lax
from jax.experimental import pallas as pl
from jax.experimental.pallas import tpu as pltpu

F32 = jnp.float32
BF16 = jnp.bfloat16

GRID_W = 64
RG_HEADS = 8
RG_CONV_W = 4
RG_C = 8.0
HY_CONV_W = 3
HY_BANDS = 16
TOP_K = 8
ROUTED_SCALE = 2.5
N_MOD = 6
EPS = 1e-6

LANES = 128
SUBLANES = 8
VMEM_LIMIT_BYTES = 56 * 1024 * 1024

EXPERT_ROWS = 288


def _params(n_axes, vmem=VMEM_LIMIT_BYTES):
    return pltpu.CompilerParams(dimension_semantics=("arbitrary",) * n_axes, vmem_limit_bytes=vmem)


def _rms(x, g):
    return x * lax.rsqrt(jnp.mean(x * x, axis=-1, keepdims=True) + EPS) * g


def _sigmoid(x):
    return 0.5 * jnp.tanh(0.5 * x) + 0.5


HI_HALF = -65536


def _pack_pair(lo, hi):
    lo_b = lax.bitcast_convert_type(lo.astype(BF16).astype(F32), jnp.int32)
    hi_b = lax.bitcast_convert_type(hi.astype(BF16).astype(F32), jnp.int32)
    return hi_b | lax.shift_right_logical(lo_b, 16)


def _unpack_pair(p):
    lo = lax.bitcast_convert_type(lax.shift_left(p, 16), F32)
    hi = lax.bitcast_convert_type(p & HI_HALF, F32)
    return lo, hi


def _store_token_tiles(ref, packed):
    m = packed.shape[0]
    for j in range(SUBLANES):
        ref[pl.ds(j, m, stride=SUBLANES), :] = packed[:, j * LANES:(j + 1) * LANES]


def _load_token_tiles(ref, row0, m):
    return jnp.concatenate(
        [ref[pl.ds(row0 * SUBLANES + j, m, stride=SUBLANES), :] for j in range(SUBLANES)], axis=1)


def _mod_kernel(c_ref, w_ref, b_ref, o_ref):
    c = c_ref[...]
    s = (c * jax.nn.sigmoid(c)).astype(BF16)
    o_ref[...] = jnp.dot(s, w_ref[...].astype(BF16), preferred_element_type=F32) + b_ref[...]


def _modulation(cvec, ada_w, ada_b):
    d, n = ada_w.shape
    tn = 1536
    return pl.pallas_call(
        _mod_kernel,
        grid=(n // tn,),
        in_specs=[pl.BlockSpec((SUBLANES, d), lambda j: (0, 0)),
                  pl.BlockSpec((d, tn), lambda j: (0, j)),
                  pl.BlockSpec((1, tn), lambda j: (0, j))],
        out_specs=pl.BlockSpec((SUBLANES, tn), lambda j: (0, j)),
        out_shape=jax.ShapeDtypeStruct((SUBLANES, n), F32),
        compiler_params=_params(1),
        name="modulation",
    )(cvec, ada_w, ada_b)


HALO = 16


def _inproj_kernel(*refs, has_pos, tn, n_rg, seq_len):
    if has_pos:
        (x_ref, xp_ref, xn_ref, pos_ref, pp_ref, pn_ref, mod_ref, g_ref, w_hbm, cw_ref, cb_ref,
         u_ref, hv_ref, w_vmem, h_scr, par_scr, sem) = refs
    else:
        (x_ref, xp_ref, xn_ref, mod_ref, g_ref, w_hbm, cw_ref, cb_ref,
         u_ref, hv_ref, w_vmem, h_scr, par_scr, sem) = refs
    i = pl.program_id(0)
    j = pl.program_id(1)
    tm = x_ref.shape[0]

    @pl.when(jnp.logical_and(i == 0, j == 0))
    def _():
        cp = pltpu.make_async_copy(w_hbm, w_vmem, sem)
        cp.start()
        cp.wait()

    @pl.when(j == 0)
    def _():
        gain = g_ref[...] * (1.0 + mod_ref[0, 1:2, :])

        def normed(x_r, p_r):
            x = x_r[...]
            if has_pos:
                x = x + p_r[...]
            return (_rms(x, gain) + mod_ref[0, 0:1, :]).astype(BF16)

        h_scr[0:HALO, :] = normed(xp_ref, pp_ref if has_pos else None)
        h_scr[HALO:HALO + tm, :] = normed(x_ref, pos_ref if has_pos else None)
        h_scr[HALO + tm:, :] = normed(xn_ref, pn_ref if has_pos else None)

    w = w_vmem[:, pl.ds(pl.multiple_of(j * tn, tn), tn)]

    @pl.when(j < n_rg)
    def _():
        u_ref[...] = jnp.dot(h_scr[HALO:HALO + tm, :], w, preferred_element_type=F32)

    @pl.when(j >= n_rg)
    def _():
        ue = jnp.dot(h_scr[...], w, preferred_element_type=F32)
        t_in_seq = (i * tm + lax.broadcasted_iota(jnp.int32, (tm, tn), 0)) & (seq_len - 1)
        cw = cw_ref[...]
        prev = jnp.where(t_in_seq == 0, 0.0, _shift_rows(ue, -1)[HALO:HALO + tm])
        nxt = jnp.where(t_in_seq == seq_len - 1, 0.0, _shift_rows(ue, 1)[HALO:HALO + tm])
        acc = cb_ref[...] + cw[0:1, :] * prev + cw[1:2, :] * ue[HALO:HALO + tm] + cw[2:3, :] * nxt
        n_seq, _, rows, _ = hv_ref.shape
        for c0 in range(0, tn, LANES):
            par_scr[c0 // LANES] = acc[:, c0:c0 + LANES]
        for s in range(n_seq):
            for par in range(2):
                for c0 in range(0, tn, LANES):
                    picked = par_scr[c0 // LANES, pl.ds(s * 2 * rows + par, rows, stride=2), :]
                    hv_ref[s, par, :, c0:c0 + LANES] = picked.astype(BF16)


def _in_proj(x2d, pos, mod, norm_g, w_bf, conv_w, conv_b, seq_len, group_of_tile, tm=512, tn=1024):
    t, d = x2d.shape
    n = w_bf.shape[1]
    n_hv = conv_w.shape[1]
    n_rg = (n - n_hv) // tn
    has_pos = pos is not None
    hb = tm // HALO
    last_h = t // HALO - 1
    seqs = max(1, tm // seq_len)
    tiles_per_seq = max(1, seq_len // tm)
    rows = min(tm, seq_len) // 2

    def prev_blk(i):
        return jnp.maximum(i * hb - 1, 0)

    def next_blk(i):
        return jnp.minimum((i + 1) * hb, last_h)

    in_specs = [pl.BlockSpec((tm, d), lambda i, j: (i, 0)),
                pl.BlockSpec((HALO, d), lambda i, j: (prev_blk(i), 0)),
                pl.BlockSpec((HALO, d), lambda i, j: (next_blk(i), 0))]
    args = [x2d, x2d, x2d]
    if has_pos:
        pos_tiles = pos.shape[0] // tm
        last_p = pos.shape[0] // HALO - 1
        in_specs += [pl.BlockSpec((tm, d), lambda i, j: (i % pos_tiles, 0)),
                     pl.BlockSpec((HALO, d), lambda i, j: (jnp.maximum((i % pos_tiles) * hb - 1, 0), 0)),
                     pl.BlockSpec((HALO, d), lambda i, j: (jnp.minimum((i % pos_tiles + 1) * hb, last_p), 0))]
        args += [pos, pos, pos]
    in_specs += [pl.BlockSpec((1, N_MOD, d), lambda i, j: (group_of_tile(i, tm), 0, 0)),
                 pl.BlockSpec((1, d), lambda i, j: (0, 0)),
                 pl.BlockSpec(memory_space=pl.ANY),
                 pl.BlockSpec((HY_CONV_W, tn), lambda i, j: (0, jnp.maximum(j - n_rg, 0))),
                 pl.BlockSpec((1, tn), lambda i, j: (0, jnp.maximum(j - n_rg, 0)))]
    args += [mod, norm_g, w_bf, conv_w, conv_b]
    return pl.pallas_call(
        functools.partial(_inproj_kernel, has_pos=has_pos, tn=tn, n_rg=n_rg, seq_len=seq_len),
        grid=(t // tm, n // tn),
        in_specs=in_specs,
        out_specs=[pl.BlockSpec((tm, tn), lambda i, j: (i, jnp.minimum(j, n_rg - 1))),
                   pl.BlockSpec((seqs, 2, rows, tn),
                                lambda i, j: (i // tiles_per_seq, 0, i % tiles_per_seq,
                                              jnp.maximum(j - n_rg, 0)))],
        out_shape=[jax.ShapeDtypeStruct((t, n - n_hv), F32),
                   jax.ShapeDtypeStruct((t // seq_len, 2, seq_len // 2, n_hv), BF16)],
        scratch_shapes=[pltpu.VMEM((d, n), BF16), pltpu.VMEM((tm + 2 * HALO, d), BF16),
                        pltpu.VMEM((tn // LANES, tm, LANES), F32), pltpu.SemaphoreType.DMA(())],
        compiler_params=_params(2),
        name="in_proj",
    )(*args)


def _shift_rows(win, off):
    if off == 0:
        return win
    n = win.shape[0]
    return pltpu.roll(win, (-off) % n, axis=0)


def _scan_chunk(a, b, reverse):
    n = a.shape[0]
    row = lax.broadcasted_iota(jnp.int32, a.shape, 0)
    dist = 1
    while dist < n:
        if reverse:
            a_s = pltpu.roll(a, n - dist, axis=0)
            b_s = pltpu.roll(b, n - dist, axis=0)
            m = row < n - dist
        else:
            a_s = pltpu.roll(a, dist, axis=0)
            b_s = pltpu.roll(b, dist, axis=0)
            m = row >= dist
        b = jnp.where(m, a * b_s + b, b)
        a = jnp.where(m, a * a_s, a)
        dist *= 2
    return a, b


def _rglru_kernel(xr_ref, gr_ref, cw_ref, cb_ref, w4_ref, b4_ref, lam_ref, h0_ref,
                  y_ref, st_ref, af, bf, ab, bb, hf, *, seq_len, n_seq, t1, tc):
    hd = xr_ref.shape[-1]
    nlam = -lam_ref[...]
    sp = jnp.maximum(nlam, 0.0) + jnp.log1p(jnp.exp(-jnp.abs(nlam)))
    cw = cw_ref[...]
    cb = cb_ref[...]
    b4 = b4_ref[0]
    nc1 = seq_len // t1
    ncs = seq_len // tc

    def per_seq(s, carry0):
        def gates(c, carry):
            r0 = pl.multiple_of(c * t1, t1)
            cur = xr_ref[s, pl.ds(r0, t1), :]
            p0 = pl.multiple_of(jnp.maximum(r0 - SUBLANES, 0), SUBLANES)
            n0 = pl.multiple_of(jnp.minimum(r0 + t1, seq_len - SUBLANES), SUBLANES)
            prev = jnp.where(c > 0, xr_ref[s, pl.ds(p0, SUBLANES), :], 0.0)
            nxt = jnp.where(c < nc1 - 1, xr_ref[s, pl.ds(n0, SUBLANES), :], 0.0)
            win = jnp.concatenate([prev, cur, nxt], axis=0)
            xr = cb
            for k in range(RG_CONV_W):
                xr = xr + cw[k:k + 1, :] * _shift_rows(win, k - RG_CONV_W // 2)[SUBLANES:SUBLANES + t1]
            z = jnp.dot(xr.astype(BF16), w4_ref[0], preferred_element_type=F32) + b4
            for d_i, (a_scr, b_scr) in enumerate(((af, bf), (ab, bb))):
                r = _sigmoid(z[:, (2 * d_i) * hd:(2 * d_i + 1) * hd])
                gi = _sigmoid(z[:, (2 * d_i + 1) * hd:(2 * d_i + 2) * hd])
                log_a = (-RG_C) * r * sp[d_i:d_i + 1, :]
                a = jnp.exp(log_a)
                a_scr[pl.ds(r0, t1), :] = a
                b_scr[pl.ds(r0, t1), :] = jnp.sqrt(-jnp.tanh(log_a) * (a * a + 1.0)) * (gi * xr)
            return carry

        lax.fori_loop(0, nc1, gates, 0)
        h0 = h0_ref[s]

        def fwd(c, carry):
            r0 = pl.multiple_of(c * tc, tc)
            a, h = _scan_chunk(af[pl.ds(r0, tc), :], bf[pl.ds(r0, tc), :], False)
            h = a * carry + h
            hf[pl.ds(r0, tc), :] = h
            return h[tc - 1:tc, :]

        s_f = lax.fori_loop(0, ncs, fwd, h0[0:1, :])

        def bwd(cc, carry):
            r0 = pl.multiple_of((ncs - 1 - cc) * tc, tc)
            a, h = _scan_chunk(ab[pl.ds(r0, tc), :], bb[pl.ds(r0, tc), :], True)
            h = a * carry + h
            g = gr_ref[s, pl.ds(r0, tc), :]
            y_ref[s, pl.ds(r0, tc), :] = jax.nn.gelu(g) * (hf[pl.ds(r0, tc), :] + h)
            return h[0:1, :]

        s_b = lax.fori_loop(0, ncs, bwd, h0[1:2, :])
        st_ref[s] = jnp.concatenate([s_f, s_b], axis=0)
        return carry0

    lax.fori_loop(0, n_seq, per_seq, 0)


def _rglru(u3, h0, conv_w, conv_b, w4, b4, lam, n_seq):
    b, seq_len, _ = u3.shape
    hd = w4.shape[1]
    d_rg = hd * RG_HEADS
    t1 = min(seq_len, 256)
    tc = 64
    kern = functools.partial(_rglru_kernel, seq_len=seq_len, n_seq=n_seq, t1=t1, tc=tc)
    return pl.pallas_call(
        kern,
        grid=(b // n_seq, RG_HEADS),
        in_specs=[pl.BlockSpec((n_seq, seq_len, hd), lambda i, h: (i, 0, h)),
                  pl.BlockSpec((n_seq, seq_len, hd), lambda i, h: (i, 0, RG_HEADS + h)),
                  pl.BlockSpec((RG_CONV_W, hd), lambda i, h: (0, h)),
                  pl.BlockSpec((1, hd), lambda i, h: (0, h)),
                  pl.BlockSpec((1, hd, 4 * hd), lambda i, h: (h, 0, 0)),
                  pl.BlockSpec((1, 1, 4 * hd), lambda i, h: (h, 0, 0)),
                  pl.BlockSpec((2, hd), lambda i, h: (0, h)),
                  pl.BlockSpec((n_seq, 2, hd), lambda i, h: (i, 0, h))],
        out_specs=[pl.BlockSpec((n_seq, seq_len, hd), lambda i, h: (i, 0, h)),
                   pl.BlockSpec((n_seq, 2, hd), lambda i, h: (i, 0, h))],
        out_shape=[jax.ShapeDtypeStruct((b, seq_len, d_rg), F32),
                   jax.ShapeDtypeStruct((b, 2, d_rg), F32)],
        scratch_shapes=[pltpu.VMEM((seq_len, hd), F32)] * 5,
        compiler_params=_params(2),
        name="rglru",
    )(u3, u3, conv_w, conv_b, w4, b4, lam, h0)


def _filt_time_kernel(w1_ref, b1_ref, fr_ref, w2_ref, b2_ref, w3_ref, dec_ref,
                      g_ref, d_ref, st_ref, *, seq_len, tl):
    i = pl.program_id(0)
    hi = lax.Precision.HIGHEST
    half = seq_len // 2
    r = i * tl + lax.broadcasted_iota(jnp.int32, (tl, LANES), 0)
    posi = 2 * (r & (half - 1)) + jnp.where(r >= half, 1, 0)
    pos = posi.astype(F32)
    lane = lax.broadcasted_iota(jnp.int32, (tl, LANES), 1)
    band = jnp.where(lane <= HY_BANDS, lane, lane - HY_BANDS).astype(F32)
    ang = (2.0 * math.pi) * pos / seq_len * band
    t = pos / seq_len
    feats = jnp.where(lane == 0, t,
                      jnp.where(lane <= HY_BANDS, jnp.cos(ang),
                                jnp.where(lane <= 2 * HY_BANDS, -jnp.sin(ang), 0.0)))
    fr = fr_ref[...]
    hid = jnp.sin(fr * (jnp.dot(feats, w1_ref[...], precision=hi, preferred_element_type=F32) + b1_ref[...]))
    hid = jnp.sin(fr * (jnp.dot(hid, w2_ref[...], precision=hi, preferred_element_type=F32) + b2_ref[...]))
    k = jnp.dot(hid.astype(BF16), w3_ref[...].astype(BF16), preferred_element_type=F32)
    k = k * jnp.exp(-t[:, 0:1] * jnp.abs(dec_ref[...]))
    c = k.shape[1] // 4
    first = posi[:, 0:1] == 0
    p4 = posi[:, 0:1] & 3
    sign = jnp.where(p4 == 0, 1.0, jnp.where(p4 == 2, -1.0, 0.0))

    @pl.when(i == 0)
    def _():
        st_ref[...] = jnp.zeros_like(st_ref)

    for o in range(2):
        kf = k[:, (2 * o) * c:(2 * o + 1) * c]
        kb = jnp.where(first, 0.0, k[:, (2 * o + 1) * c:(2 * o + 2) * c])
        g = kf + kb
        g_ref[:, o * c:(o + 1) * c] = g.astype(BF16)
        d_ref[:, o * c:(o + 1) * c] = (kf - kb).astype(BF16)
        st_ref[0:1, o * c:(o + 1) * c] += jnp.sum(kf * kf + kb * kb, axis=0, keepdims=True)
        st_ref[1:2, o * c:(o + 1) * c] += jnp.sum(sign * g, axis=0, keepdims=True)


def _half_dfts(tabs, cos_even, sin_even, cos_odd, sin_odd):
    ce, se, co, so = tabs
    ae = jnp.dot(ce[...], cos_even, preferred_element_type=F32)
    ao = jnp.dot(co[...], cos_odd, preferred_element_type=F32)
    be = jnp.dot(se[...], sin_even, preferred_element_type=F32)
    bo = jnp.dot(so[...], sin_odd, preferred_element_type=F32)
    return ae, ao, be, bo


def _filt_dft_kernel(ce_ref, se_ref, co_ref, so_ref, g_ref, d_ref, st_ref, kf_ref, ks_ref, qf_ref, qs_ref):
    m = pl.program_id(1)
    half = g_ref.shape[0] // 2
    ae, ao, be, bo = _half_dfts((ce_ref, se_ref, co_ref, so_ref),
                                g_ref[0:half, :], d_ref[0:half, :], g_ref[half:, :], d_ref[half:, :])
    scale = lax.rsqrt(st_ref[0:1, :] + EPS)
    is0 = (m * ae.shape[0] + lax.broadcasted_iota(jnp.int32, ae.shape, 0)) == 0
    kf_ref[...] = (ae + ao) * scale
    ks_ref[...] = (ae - ao) * scale
    qf_ref[...] = jnp.where(is0, st_ref[1:2, :], be + bo) * scale
    qs_ref[...] = jnp.where(is0, bo, bo - be) * scale


def _hyena_filters(seq_len, tabs, w1, b1, freq, w2, b2, w3, decay):
    n_hid = w1.shape[1]
    n_out = w3.shape[1]
    c2 = n_out // 2
    tl = min(seq_len, 512)
    w1p = jnp.zeros((LANES, n_hid), F32).at[:w1.shape[0]].set(w1)
    g, d, stats = pl.pallas_call(
        functools.partial(_filt_time_kernel, seq_len=seq_len, tl=tl),
        grid=(seq_len // tl,),
        in_specs=[pl.BlockSpec((LANES, n_hid), lambda i: (0, 0)),
                  pl.BlockSpec((1, n_hid), lambda i: (0, 0)),
                  pl.BlockSpec((1, n_hid), lambda i: (0, 0)),
                  pl.BlockSpec((n_hid, n_hid), lambda i: (0, 0)),
                  pl.BlockSpec((1, n_hid), lambda i: (0, 0)),
                  pl.BlockSpec((n_hid, n_out), lambda i: (0, 0)),
                  pl.BlockSpec((1, n_out), lambda i: (0, 0))],
        out_specs=[pl.BlockSpec((tl, c2), lambda i: (i, 0)),
                   pl.BlockSpec((tl, c2), lambda i: (i, 0)),
                   pl.BlockSpec((SUBLANES, c2), lambda i: (0, 0))],
        out_shape=[jax.ShapeDtypeStruct((seq_len, c2), BF16),
                   jax.ShapeDtypeStruct((seq_len, c2), BF16),
                   jax.ShapeDtypeStruct((SUBLANES, c2), F32)],
        compiler_params=_params(1),
        name="hyena_filter_taps",
    )(w1p, b1, freq, w2, b2, w3, decay)
    half = seq_len // 2
    tm = min(half, 512)
    tn = 512
    tab = pl.BlockSpec((tm, half), lambda n, m: (m, 0))
    return pl.pallas_call(
        _filt_dft_kernel,
        grid=(c2 // tn, half // tm),
        in_specs=[tab, tab, tab, tab,
                  pl.BlockSpec((seq_len, tn), lambda n, m: (0, n)),
                  pl.BlockSpec((seq_len, tn), lambda n, m: (0, n)),
                  pl.BlockSpec((SUBLANES, tn), lambda n, m: (0, n))],
        out_specs=[pl.BlockSpec((tm, tn), lambda n, m: (m, n))] * 4,
        out_shape=[jax.ShapeDtypeStruct((half, c2), F32)] * 4,
        compiler_params=_params(2),
        name="hyena_filter_dft",
    )(tabs[0], tabs[1], tabs[2], tabs[3], g, d, stats)


def _dft_matrices(seq_len):
    assert seq_len & (seq_len - 1) == 0
    half = seq_len // 2
    tm = min(half, 256)
    out = jax.ShapeDtypeStruct((half, half), BF16)
    return pl.pallas_call(
        functools.partial(_dft_table_kernel, seq_len=seq_len),
        grid=(half // tm,),
        in_specs=[],
        out_specs=[pl.BlockSpec((tm, half), lambda i: (i, 0))] * 7,
        out_shape=[out] * 7,
        scratch_shapes=[pltpu.VMEM((tm, half), F32)] * 4,
        compiler_params=_params(1),
        name="dft_tables",
    )()


def _dft_table_kernel(ce_ref, se_ref, co_ref, so_ref, set_ref, cot_ref, sot_ref, c0e, s0e, c0o, s0o, *, seq_len):
    i = pl.program_id(0)
    tm, half = ce_ref.shape
    wrap = 2 * seq_len - 1
    unit = math.pi / seq_len
    row = lax.broadcasted_iota(jnp.int32, (tm, half), 0)
    col = lax.broadcasted_iota(jnp.int32, (tm, half), 1)
    col1 = col[0:1, :]

    def trig(n):
        ang = (n & wrap).astype(F32) * unit
        return jnp.cos(ang), jnp.sin(ang)

    @pl.when(i == 0)
    def _():
        c0e[...], s0e[...] = trig(2 * row * col)
        c0o[...], s0o[...] = trig(row * (2 * col + 1))

    def rotate(c0, s0, n):
        cn, sn = trig(n)
        return c0[...] * cn - s0[...] * sn, s0[...] * cn + c0[...] * sn

    r0 = i * tm
    ce, se = rotate(c0e, s0e, 2 * r0 * col1)
    co, so = rotate(c0o, s0o, r0 * (2 * col1 + 1))
    cot, sot = rotate(c0e, s0e, (2 * r0 + 1) * col1)
    alt_col = jnp.where((col & 1) == 0, 1.0, -1.0)
    alt_row = jnp.where(((row + r0) & 1) == 0, 1.0, -1.0)
    first_row = row + r0 == 0
    ce_ref[...] = ce.astype(BF16)
    se_ref[...] = jnp.where(first_row, alt_col, se).astype(BF16)
    co_ref[...] = co.astype(BF16)
    so_ref[...] = jnp.where(first_row, alt_col, so).astype(BF16)
    set_ref[...] = jnp.where(col == 0, alt_row, se).astype(BF16)
    cot_ref[...] = cot.astype(BF16)
    sot_ref[...] = jnp.where(col == 0, alt_row, sot).astype(BF16)


def _hy_fwd_kernel(ce_ref, se_ref, co_ref, so_ref, u_ref, kf_ref, ks_ref, qf_ref, qs_ref,
                   ee_ref, eo_ref, de_ref, do_ref, *, seq_len):
    m = pl.program_id(2)
    kf, ks, qf, qs = kf_ref[...], ks_ref[...], qf_ref[...], qs_ref[...]
    is0 = (m * kf.shape[0] + lax.broadcasted_iota(jnp.int32, kf.shape, 0)) == 0
    inv_n = 0.5 / seq_len
    w = jnp.where(is0, inv_n, 2.0 * inv_n)
    for s in range(u_ref.shape[0]):
        ue, uo = u_ref[s, 0], u_ref[s, 1]
        ae, ao, be, bo = _half_dfts((ce_ref, se_ref, co_ref, so_ref), ue, ue, uo, uo)
        a_f, a_s = ae + ao, ae - ao
        b_f, b_s = be + bo, bo - be
        pre_f = jnp.where(is0, a_f * kf, a_f * kf - b_f * qf)
        pre_s = jnp.where(is0, a_s * ks, a_s * ks - b_s * qs)
        pm_f = a_f * qf + b_f * kf
        pm_s = a_s * qs + b_s * ks
        mid_re = be * qf - bo * qs
        mid_mim = be * qs + bo * qf
        ee_ref[s] = ((pre_f + pre_s) * w).astype(BF16)
        eo_ref[s] = ((pre_f - pre_s) * w).astype(BF16)
        de_ref[s] = (jnp.where(is0, mid_re, pm_f - pm_s) * (2.0 * inv_n)).astype(BF16)
        do_ref[s] = (jnp.where(is0, mid_mim, pm_f + pm_s) * (2.0 * inv_n)).astype(BF16)


def _hy_inv_kernel(*refs, natural_out):
    if natural_out:
        ce_ref, set_ref, cot_ref, sot_ref, ee_ref, eo_ref, de_ref, do_ref, u_ref, x_ref, bias_ref, z_ref, scr = refs
    else:
        ce_ref, set_ref, cot_ref, sot_ref, ee_ref, eo_ref, de_ref, do_ref, u_ref, x_ref, bias_ref, z_ref = refs
    bias = bias_ref[...]
    for s in range(u_ref.shape[0]):
        y_e = jnp.dot(ce_ref[...], ee_ref[s], preferred_element_type=F32)
        y_e = y_e + jnp.dot(set_ref[...], de_ref[s], preferred_element_type=F32)
        y_o = jnp.dot(cot_ref[...], eo_ref[s], preferred_element_type=F32)
        y_o = y_o + jnp.dot(sot_ref[...], do_ref[s], preferred_element_type=F32)
        z_e = (y_e + u_ref[s, 0].astype(F32) * bias) * x_ref[s, 0].astype(F32)
        z_o = (y_o + u_ref[s, 1].astype(F32) * bias) * x_ref[s, 1].astype(F32)
        if natural_out:
            th = z_e.shape[0]
            for j in range(z_e.shape[1] // LANES):
                scr[j, pl.ds(0, th, stride=2), :] = z_e[:, j * LANES:(j + 1) * LANES]
                scr[j, pl.ds(1, th, stride=2), :] = z_o[:, j * LANES:(j + 1) * LANES]
            for j in range(z_e.shape[1] // LANES):
                z_ref[s, :, j * LANES:(j + 1) * LANES] = scr[j].astype(BF16)
        else:
            z_ref[s, 0] = z_e.astype(BF16)
            z_ref[s, 1] = z_o.astype(BF16)


def _hyena_order(order, u_arr, u_cb, hvc, gate_cb, tabs, filt, bias, tm, tc, natural_out):
    ce, se, co, so, se_t, co_t, so_t = tabs
    kf, ks, qf, qs = filt
    b, _, half, _ = hvc.shape
    seq_len = 2 * half
    c = kf.shape[1] // 2
    nct = c // tc
    sb = max(1, min(b, 1024 // seq_len))
    grid = (b // sb, nct, half // tm)
    tab = pl.BlockSpec((tm, half), lambda i, n, m: (m, 0))
    spec = pl.BlockSpec((tm, tc), lambda i, n, m: (m, order * nct + n))
    freq = pl.BlockSpec((sb, tm, tc), lambda i, n, m: (i, m, n))
    ee, eo, de, do = pl.pallas_call(
        functools.partial(_hy_fwd_kernel, seq_len=seq_len),
        grid=grid,
        in_specs=[tab, tab, tab, tab,
                  pl.BlockSpec((sb, 2, half, tc), lambda i, n, m: (i, 0, 0, u_cb * nct + n)),
                  spec, spec, spec, spec],
        out_specs=[freq] * 4,
        out_shape=[jax.ShapeDtypeStruct((b, half, c), BF16)] * 4,
        compiler_params=_params(3),
        name="hyena_fwd_dft",
    )(ce, se, co, so, u_arr, kf, ks, qf, qs)
    whole = pl.BlockSpec((sb, half, tc), lambda i, n, m: (i, 0, n))
    if natural_out:
        out_spec = pl.BlockSpec((sb, 2 * tm, tc), lambda i, n, m: (i, m, n))
        out_shape = jax.ShapeDtypeStruct((b, seq_len, c), BF16)
        scratch = [pltpu.VMEM((tc // LANES, 2 * tm, LANES), F32)]
    else:
        out_spec = pl.BlockSpec((sb, 2, tm, tc), lambda i, n, m: (i, 0, m, n))
        out_shape = jax.ShapeDtypeStruct((b, 2, half, c), BF16)
        scratch = []
    return pl.pallas_call(
        functools.partial(_hy_inv_kernel, natural_out=natural_out),
        grid=grid,
        in_specs=[tab, tab, tab, tab, whole, whole, whole, whole,
                  pl.BlockSpec((sb, 2, tm, tc), lambda i, n, m: (i, 0, m, u_cb * nct + n)),
                  pl.BlockSpec((sb, 2, tm, tc), lambda i, n, m: (i, 0, m, gate_cb * nct + n)),
                  pl.BlockSpec((1, tc), lambda i, n, m: (0, n))],
        out_specs=out_spec,
        out_shape=out_shape,
        scratch_shapes=scratch,
        compiler_params=_params(3),
        name="hyena_inv_dft",
    )(ce, se_t, co_t, so_t, ee, eo, de, do, u_arr, hvc, bias[order][None])


def _post_mixer_kernel(*refs, has_pos):
    if has_pos:
        (yrg_ref, yhy_ref, x_ref, pos_ref, mod_ref, gnr_ref, gnh_ref, wo_ref, n2_ref,
         rw_ref, rb_ref, x1_ref, h2_ref, idx_ref, gw_ref) = refs
    else:
        (yrg_ref, yhy_ref, x_ref, mod_ref, gnr_ref, gnh_ref, wo_ref, n2_ref,
         rw_ref, rb_ref, x1_ref, h2_ref, idx_ref, gw_ref) = refs
    d_rg = yrg_ref.shape[1]
    na = _rms(yrg_ref[...], gnr_ref[...]).astype(BF16)
    nb = _rms(yhy_ref[...].astype(F32), gnh_ref[...]).astype(BF16)
    y = jnp.dot(na, wo_ref[0:d_rg, :], preferred_element_type=F32)
    y = y + jnp.dot(nb, wo_ref[d_rg:, :], preferred_element_type=F32)
    x = x_ref[...]
    if has_pos:
        x = x + pos_ref[...]
    x1 = x + mod_ref[0, 2:3, :] * y
    x1_ref[...] = x1
    h2f = _rms(x1, n2_ref[...] * (1.0 + mod_ref[0, 4:5, :])) + mod_ref[0, 3:4, :]
    half = h2f.shape[1] // 2
    _store_token_tiles(h2_ref, _pack_pair(h2f[:, :half], h2f[:, half:]))
    h2 = h2f.astype(BF16)
    scores = jax.nn.sigmoid(jnp.dot(h2, rw_ref[...], preferred_element_type=F32))
    sel = scores + rb_ref[...]
    n_exp = scores.shape[1]
    lane = lax.broadcasted_iota(jnp.int32, scores.shape, 1).astype(F32)
    col = lax.broadcasted_iota(jnp.int32, idx_ref.shape, 1)
    idx_acc = jnp.zeros(idx_ref.shape, F32)
    gw_acc = jnp.zeros(gw_ref.shape, F32)
    for k in range(TOP_K):
        mx = jnp.max(sel, axis=1, keepdims=True)
        pick = jnp.min(jnp.where(sel == mx, lane, float(n_exp)), axis=1, keepdims=True)
        hit = lane == pick
        val = jnp.sum(jnp.where(hit, scores, 0.0), axis=1, keepdims=True)
        sel = jnp.where(hit, -jnp.inf, sel)
        idx_acc = jnp.where(col == k, pick, idx_acc)
        gw_acc = jnp.where(col == k, val, gw_acc)
    idx_ref[...] = idx_acc.astype(jnp.int32)
    gw_ref[...] = gw_acc / jnp.sum(gw_acc, axis=1, keepdims=True) * ROUTED_SCALE


def _post_mixer(y_rg, y_hy, x2d, pos, mod, gn_rg, gn_hy, w_out_bf, norm2_g, router_w_bf, router_b,
                group_of_tile, tm=256):
    t, d = x2d.shape
    d_rg = y_rg.shape[1]
    d_hy = y_hy.shape[1]
    n_exp = router_w_bf.shape[1]
    has_pos = pos is not None
    in_specs = [pl.BlockSpec((tm, d_rg), lambda i: (i, 0)),
                pl.BlockSpec((tm, d_hy), lambda i: (i, 0)),
                pl.BlockSpec((tm, d), lambda i: (i, 0))]
    args = [y_rg, y_hy, x2d]
    if has_pos:
        pos_tiles = pos.shape[0] // tm
        in_specs.append(pl.BlockSpec((tm, d), lambda i: (i % pos_tiles, 0)))
        args.append(pos)
    in_specs += [pl.BlockSpec((1, N_MOD, d), lambda i: (group_of_tile(i, tm), 0, 0)),
                 pl.BlockSpec((1, d_rg), lambda i: (0, 0)),
                 pl.BlockSpec((1, d_hy), lambda i: (0, 0)),
                 pl.BlockSpec((d_rg + d_hy, d), lambda i: (0, 0)),
                 pl.BlockSpec((1, d), lambda i: (0, 0)),
                 pl.BlockSpec((d, n_exp), lambda i: (0, 0)),
                 pl.BlockSpec((1, n_exp), lambda i: (0, 0))]
    args += [mod, gn_rg, gn_hy, w_out_bf, norm2_g, router_w_bf, router_b]
    return pl.pallas_call(
        functools.partial(_post_mixer_kernel, has_pos=has_pos),
        grid=(t // tm,),
        in_specs=in_specs,
        out_specs=[pl.BlockSpec((tm, d), lambda i: (i, 0)),
                   pl.BlockSpec((tm * SUBLANES, LANES), lambda i: (i, 0)),
                   pl.BlockSpec((tm, TOP_K), lambda i: (i, 0)),
                   pl.BlockSpec((tm, TOP_K), lambda i: (i, 0))],
        out_shape=[jax.ShapeDtypeStruct((t, d), F32),
                   jax.ShapeDtypeStruct((t * SUBLANES, LANES), jnp.int32),
                   jax.ShapeDtypeStruct((t, TOP_K), jnp.int32),
                   jax.ShapeDtypeStruct((t, TOP_K), F32)],
        compiler_params=_params(1),
        name="post_mixer",
    )(*args)


def _route_kernel(idx_ref, dest_ref, be_ref, bv_ref, nu_ref, tri, cnt, base, pst, *, blk, n_blk):
    p = pl.program_id(0)
    i = pl.program_id(1)
    tm, top_k = idx_ref.shape
    ne = cnt.shape[1]

    def div_blk(n):
        return jnp.floor((n + 0.5) / blk)

    @pl.when(jnp.logical_and(p == 0, i == 0))
    def _():
        r = lax.broadcasted_iota(jnp.int32, (tm, tm), 0)
        c = lax.broadcasted_iota(jnp.int32, (tm, tm), 1)
        tri[...] = jnp.where(r > c, 1.0, 0.0).astype(BF16)
        cnt[...] = jnp.zeros_like(cnt)

    idx = idx_ref[...]
    lane = lax.broadcasted_iota(jnp.int32, (tm, ne), 1)
    hits = [lane == idx[:, k:k + 1] for k in range(top_k)]
    occ = jnp.zeros((tm, ne), F32)
    for h in hits:
        occ = occ + jnp.where(h, 1.0, 0.0)
    col_sum = jnp.sum(occ, axis=0, keepdims=True)

    @pl.when(p == 0)
    def _():
        cnt[0:1, :] += col_sum

    @pl.when(jnp.logical_and(p == 1, i == 0))
    def _():
        counts = cnt[...]
        padded = div_blk(counts + (blk - 1.0)) * blk
        r = lax.broadcasted_iota(jnp.int32, (ne, ne), 0)
        c = lax.broadcasted_iota(jnp.int32, (ne, ne), 1)
        upper = jnp.where(r <= c, 1.0, 0.0)
        pend = jnp.dot(padded, upper, precision=lax.Precision.HIGHEST, preferred_element_type=F32)
        pstart = pend - padded
        pst[...] = pstart
        base[...] = jnp.zeros_like(base)
        b0 = (lax.broadcasted_iota(jnp.int32, (n_blk, ne), 0) * blk).astype(F32)
        be = jnp.sum(jnp.where(pend[0:1, :] <= b0, 1.0, 0.0), axis=1, keepdims=True)
        be = jnp.minimum(be, ne - 1.0)
        own = lax.broadcasted_iota(jnp.int32, (n_blk, ne), 1).astype(F32) == be
        pst_b = jnp.sum(jnp.where(own, pstart[0:1, :], 0.0), axis=1, keepdims=True)
        cnt_b = jnp.sum(jnp.where(own, counts[0:1, :], 0.0), axis=1, keepdims=True)
        valid = jnp.clip(cnt_b - (b0[:, 0:1] - pst_b), 0.0, float(blk))
        be_ref[...] = be.astype(jnp.int32)
        bv_ref[...] = valid.astype(jnp.int32)
        total = jnp.max(pend[0:1, :], axis=1, keepdims=True)
        nu_ref[...] = jnp.broadcast_to(div_blk(total).astype(jnp.int32), nu_ref.shape)

    @pl.when(p == 1)
    def _():
        cum = jnp.dot(tri[...], occ.astype(BF16), preferred_element_type=F32) + base[0:1, :] + pst[0:1, :]
        col = lax.broadcasted_iota(jnp.int32, (tm, top_k), 1)
        acc = jnp.zeros((tm, top_k), F32)
        for k in range(top_k):
            v = jnp.sum(jnp.where(hits[k], cum, 0.0), axis=1, keepdims=True)
            acc = jnp.where(col == k, v, acc)
        dest_ref[...] = acc.astype(jnp.int32)
        base[0:1, :] += col_sum


def _route(idx, n_exp, blk, tm=512):
    n_tok, top_k = idx.shape
    n_blk = -(-n_tok * top_k // blk) + n_exp
    dest, be, bv, nu = pl.pallas_call(
        functools.partial(_route_kernel, blk=blk, n_blk=n_blk),
        grid=(2, n_tok // tm),
        in_specs=[pl.BlockSpec((tm, top_k), lambda p, i: (i, 0))],
        out_specs=[pl.BlockSpec((tm, top_k), lambda p, i: (i * p, 0)),
                   pl.BlockSpec((n_blk, 1), lambda p, i: (0, 0)),
                   pl.BlockSpec((n_blk, 1), lambda p, i: (0, 0)),
                   pl.BlockSpec((SUBLANES, LANES), lambda p, i: (0, 0))],
        out_shape=[jax.ShapeDtypeStruct((n_tok, top_k), jnp.int32),
                   jax.ShapeDtypeStruct((n_blk, 1), jnp.int32),
                   jax.ShapeDtypeStruct((n_blk, 1), jnp.int32),
                   jax.ShapeDtypeStruct((SUBLANES, LANES), jnp.int32)],
        scratch_shapes=[pltpu.VMEM((tm, tm), BF16), pltpu.VMEM((SUBLANES, n_exp), F32),
                        pltpu.VMEM((SUBLANES, n_exp), F32), pltpu.VMEM((SUBLANES, n_exp), F32)],
        compiler_params=_params(2),
        name="route",
    )(idx)
    return dest, be.reshape(n_blk), bv.reshape(n_blk), nu[0, 0:1]


def _dispatch_kernel(dest_ref, xa_ref, xl_ref, sg_ref, su_ref, sd_ref, out_ref, sh_ref, sem, *, n_a):
    tm = xa_ref.shape[0] // SUBLANES
    dh = SUBLANES * LANES

    def tile(ref, r):
        return ref.at[pl.ds(pl.multiple_of(r * SUBLANES, SUBLANES), SUBLANES), :]

    def run(src):
        def issue(t, c):
            for k in range(TOP_K):
                pltpu.make_async_copy(tile(src, t), tile(out_ref, dest_ref[t * TOP_K + k]), sem).start(priority=k % 2)
            return c

        lax.fori_loop(0, tm, issue, 0)

        lo, hi = _unpack_pair(_load_token_tiles(src, 0, tm))
        lo = lo.astype(BF16)
        hi = hi.astype(BF16)
        g = jnp.dot(lo, sg_ref[0:dh, :], preferred_element_type=F32)
        g = g + jnp.dot(hi, sg_ref[dh:, :], preferred_element_type=F32)
        u = jnp.dot(lo, su_ref[0:dh, :], preferred_element_type=F32)
        u = u + jnp.dot(hi, su_ref[dh:, :], preferred_element_type=F32)
        hmid = (g * _sigmoid(g) * u).astype(BF16)
        sh_ref[...] = jnp.dot(hmid, sd_ref[...], preferred_element_type=F32).astype(BF16)

        def drain(t, c):
            for k in range(TOP_K):
                pltpu.make_async_copy(tile(src, 0), tile(out_ref, 0), sem).wait()
            return c

        lax.fori_loop(0, tm, drain, 0, unroll=8)

    @pl.when(pl.program_id(0) < n_a)
    def _():
        run(xa_ref)

    @pl.when(pl.program_id(0) >= n_a)
    def _():
        run(xl_ref)


def _dispatch(dest_flat, h2p_a, h2p_l, n_rows, sg_bf, su_bf, sd_bf, tm=256):
    n_a = h2p_a.shape[0] // (tm * SUBLANES)
    n_l = h2p_l.shape[0] // (tm * SUBLANES)
    d, ds_ = sg_bf.shape
    return pl.pallas_call(
        functools.partial(_dispatch_kernel, n_a=n_a),
        grid=(n_a + n_l,),
        in_specs=[pl.BlockSpec((tm * TOP_K,), lambda i: (i,), memory_space=pltpu.SMEM),
                  pl.BlockSpec((tm * SUBLANES, LANES), lambda i: (jnp.minimum(i, n_a - 1), 0)),
                  pl.BlockSpec((tm * SUBLANES, LANES), lambda i: (jnp.maximum(i - n_a, 0), 0)),
                  pl.BlockSpec((d, ds_), lambda i: (0, 0)),
                  pl.BlockSpec((d, ds_), lambda i: (0, 0)),
                  pl.BlockSpec((ds_, d), lambda i: (0, 0))],
        out_specs=[pl.BlockSpec(memory_space=pl.ANY),
                   pl.BlockSpec((tm, d), lambda i: (i, 0))],
        out_shape=[jax.ShapeDtypeStruct((n_rows * SUBLANES, LANES), jnp.int32),
                   jax.ShapeDtypeStruct(((n_a + n_l) * tm, d), BF16)],
        scratch_shapes=[pltpu.SemaphoreType.DMA(())],
        compiler_params=_params(1),
        name="dispatch",
    )(dest_flat, h2p_a, h2p_l, sg_bf, su_bf, sd_bf)


WEIGHT_DMA_PRIORITY = 1


def _expert_kernel(be_ref, bv_ref, nu_ref, x_ref, wg_hbm, wu_hbm, wd_hbm, o_ref,
                   wg_f, wu_f, wd_f, wg_b, wu_b, wd_b, sem, grp, *, n_blk):
    i = pl.program_id(0)
    n_used = nu_ref[0]
    e = be_ref[i]
    active = i < n_used
    changed = jnp.logical_or(i == 0, e != be_ref[jnp.maximum(i - 1, 0)])

    def expert_at(j):
        return be_ref[jnp.minimum(j, n_blk - 1)]

    def next_group(j):
        ej = expert_at(j)
        return lax.while_loop(lambda q: jnp.logical_and(q < n_used, expert_at(q) == ej), lambda q: q + 1, j + 1)

    def copies(ex, slot):
        return (pltpu.make_async_copy(wg_hbm.at[ex], wg_f.at[slot], sem.at[slot, 0]),
                pltpu.make_async_copy(wu_hbm.at[ex], wu_f.at[slot], sem.at[slot, 1]),
                pltpu.make_async_copy(wd_hbm.at[ex], wd_f.at[slot], sem.at[slot, 2]))

    @pl.when(jnp.logical_and(active, i == 0))
    def _():
        grp[0] = 0
        for cp in copies(e, 0):
            cp.start(priority=WEIGHT_DMA_PRIORITY)
        n1 = next_group(i)

        @pl.when(n1 < n_used)
        def _():
            for cp in copies(expert_at(n1), 1):
                cp.start(priority=WEIGHT_DMA_PRIORITY)

    @pl.when(jnp.logical_and(active, changed))
    def _():
        slot = grp[0] % 2
        for cp in copies(e, slot):
            cp.wait()
        wg_b[...] = wg_f[slot].astype(BF16)
        wu_b[...] = wu_f[slot].astype(BF16)
        wd_b[...] = wd_f[slot].astype(BF16)
        n2 = next_group(next_group(i))

        @pl.when(n2 < n_used)
        def _():
            for cp in copies(expert_at(n2), slot):
                cp.start(priority=WEIGHT_DMA_PRIORITY)

        grp[0] = grp[0] + 1

    @pl.when(active)
    def _():
        tm = x_ref.shape[0] // SUBLANES
        dh = SUBLANES * LANES
        live = lax.broadcasted_iota(jnp.int32, (tm, dh), 0) < bv_ref[i]
        lo, hi = _unpack_pair(_load_token_tiles(x_ref, 0, tm))
        lo = jnp.where(live, lo, 0.0).astype(BF16)
        hi = jnp.where(live, hi, 0.0).astype(BF16)
        g = jnp.dot(lo, wg_b[0:dh, :], preferred_element_type=F32)
        g = g + jnp.dot(hi, wg_b[dh:, :], preferred_element_type=F32)
        u = jnp.dot(lo, wu_b[0:dh, :], preferred_element_type=F32)
        u = u + jnp.dot(hi, wu_b[dh:, :], preferred_element_type=F32)
        hmid = (g * _sigmoid(g) * u).astype(BF16)
        y = jnp.dot(hmid, wd_b[...], preferred_element_type=F32)
        _store_token_tiles(o_ref, _pack_pair(y[:, :dh], y[:, dh:]))


def _experts(blk_e, blk_valid, n_used, xb, wg, wu, wd):
    n_exp, d, de = wg.shape
    tm = EXPERT_ROWS
    n_blk = xb.shape[0] // (tm * SUBLANES)
    grid_spec = pltpu.PrefetchScalarGridSpec(
        num_scalar_prefetch=3,
        grid=(n_blk,),
        in_specs=[pl.BlockSpec((tm * SUBLANES, LANES), lambda i, be, bv, nu: (jnp.minimum(i, nu[0] - 1), 0)),
                  pl.BlockSpec(memory_space=pl.ANY),
                  pl.BlockSpec(memory_space=pl.ANY),
                  pl.BlockSpec(memory_space=pl.ANY)],
        out_specs=pl.BlockSpec((tm * SUBLANES, LANES), lambda i, be, bv, nu: (jnp.minimum(i, nu[0] - 1), 0)),
        scratch_shapes=[pltpu.VMEM((2, d, de), F32), pltpu.VMEM((2, d, de), F32), pltpu.VMEM((2, de, d), F32),
                        pltpu.VMEM((d, de), BF16), pltpu.VMEM((d, de), BF16), pltpu.VMEM((de, d), BF16),
                        pltpu.SemaphoreType.DMA((2, 3)), pltpu.SMEM((1,), jnp.int32)],
    )
    return pl.pallas_call(
        functools.partial(_expert_kernel, n_blk=n_blk),
        grid_spec=grid_spec,
        out_shape=jax.ShapeDtypeStruct(xb.shape, jnp.int32),
        compiler_params=_params(1),
        name="routed_experts",
    )(blk_e, blk_valid, n_used, xb, wg, wu, wd)


FINISH_ROWS = 64


def _finish_kernel(dcur_ref, dnxt_ref, sh_ref, x1_ref, gw_ref, mod_ref, fg_ref,
                   yb_hbm, o_ref, gbuf, gwb, sem, *, final_norm):
    i = pl.program_id(0)
    n_tiles = pl.num_programs(0)
    tm, d = x1_ref.shape
    dh = SUBLANES * LANES
    slot = i % 2

    def row_copy(row, t, k, s):
        src = yb_hbm.at[pl.ds(pl.multiple_of(row * SUBLANES, SUBLANES), SUBLANES), :]
        dst = gbuf.at[s, pl.ds(pl.multiple_of((k * tm + t) * SUBLANES, SUBLANES), SUBLANES), :]
        return pltpu.make_async_copy(src, dst, sem.at[s])

    def gather(d_ref, s):
        def issue(t, c):
            for k in range(TOP_K):
                row_copy(d_ref[t * TOP_K + k], t, k, s).start(priority=k % 2)
            return c

        lax.fori_loop(0, tm, issue, 0)

    @pl.when(i == 0)
    def _():
        gather(dcur_ref, 0)

    @pl.when(i + 1 < n_tiles)
    def _():
        gather(dnxt_ref, 1 - slot)

    gw = gw_ref[...]
    for k in range(TOP_K):
        gwb[k] = jnp.broadcast_to(gw[:, k:k + 1], (tm, LANES))

    def drain(t, c):
        for k in range(TOP_K):
            row_copy(0, 0, 0, slot).wait()
        return c

    lax.fori_loop(0, tm, drain, 0, unroll=8)

    rows_g = gbuf.at[slot]
    for r0 in range(0, tm, FINISH_ROWS):
        rs = slice(r0, r0 + FINISH_ROWS)
        ssq = jnp.zeros((FINISH_ROWS, 1), F32)
        for j in range(SUBLANES):
            cols = (slice(j * LANES, (j + 1) * LANES), slice(dh + j * LANES, dh + (j + 1) * LANES))
            acc = [sh_ref[rs, c].astype(F32) for c in cols]
            for k in range(TOP_K):
                halves = _unpack_pair(rows_g[pl.ds((k * tm + r0) * SUBLANES + j, FINISH_ROWS, stride=SUBLANES), :])
                w = gwb[k, rs, :]
                acc = [a + w * h for a, h in zip(acc, halves)]
            for c, a in zip(cols, acc):
                x2 = x1_ref[rs, c] + mod_ref[0, 5:6, c] * a
                o_ref[rs, c] = x2
                ssq = ssq + jnp.sum(x2 * x2, axis=-1, keepdims=True)
        if final_norm:
            inv = lax.rsqrt(ssq / d + EPS)
            o_ref[rs, :] = o_ref[rs, :] * inv * fg_ref[...]


def _finish(dest_flat, shared, x1, gw, yb, mod, final_g, row0, group_of_tile, final_norm, tm=256):
    n_rows, d = x1.shape
    t0 = row0 // tm
    n_tiles = n_rows // tm
    return pl.pallas_call(
        functools.partial(_finish_kernel, final_norm=final_norm),
        grid=(n_tiles,),
        in_specs=[pl.BlockSpec((tm * TOP_K,), lambda i: (t0 + i,), memory_space=pltpu.SMEM),
                  pl.BlockSpec((tm * TOP_K,), lambda i: (t0 + jnp.minimum(i + 1, n_tiles - 1),),
                               memory_space=pltpu.SMEM),
                  pl.BlockSpec((tm, d), lambda i: (t0 + i, 0)),
                  pl.BlockSpec((tm, d), lambda i: (i, 0)),
                  pl.BlockSpec((tm, TOP_K), lambda i: (i, 0)),
                  pl.BlockSpec((1, N_MOD, d), lambda i: (group_of_tile(i, tm), 0, 0)),
                  pl.BlockSpec((1, d), lambda i: (0, 0)),
                  pl.BlockSpec(memory_space=pl.ANY)],
        out_specs=pl.BlockSpec((tm, d), lambda i: (i, 0)),
        out_shape=jax.ShapeDtypeStruct((n_rows, d), F32),
        scratch_shapes=[pltpu.VMEM((2, TOP_K * tm * SUBLANES, LANES), jnp.int32),
                        pltpu.VMEM((TOP_K, tm, LANES), F32), pltpu.SemaphoreType.DMA((2,))],
        compiler_params=_params(1),
        name="finish",
    )(dest_flat, dest_flat, shared, x1, gw, mod, final_g, yb)


def _grid_pos_emb(rows, d):
    quarter = d // 4
    omega = 1.0 / (10000.0 ** (jnp.arange(quarter, dtype=F32) / quarter))
    r = jnp.arange(rows, dtype=F32)[:, None] * omega
    cc = jnp.arange(GRID_W, dtype=F32)[:, None] * omega
    by_row = jnp.concatenate([jnp.sin(r), jnp.cos(r)], axis=-1)
    by_col = jnp.concatenate([jnp.sin(cc), jnp.cos(cc)], axis=-1)
    full = jnp.concatenate([jnp.broadcast_to(by_row[:, None, :], (rows, GRID_W, d // 2)),
                            jnp.broadcast_to(by_col[None, :, :], (rows, GRID_W, d // 2))], axis=-1)
    return full.reshape(rows * GRID_W, d)


def _mixer_path(x2d, pos, n_b, seq_len, h0, mod, group_of_tile, p, mats, filt, n_seq):
    d_rg = p['gn_rg'].shape[1]
    c = p['gn_hy'].shape[1]
    u_rg, hvc = _in_proj(x2d, pos, mod, p['norm1_g'], p['w_in'], p['hy_conv_w'], p['hy_conv_b'],
                         seq_len, group_of_tile)
    u3 = u_rg.reshape(n_b, seq_len, u_rg.shape[1])
    y_rg, st = _rglru(u3, h0, p['rg_conv_w'], p['rg_conv_b'], p['rg_w4'], p['rg_b4'], p['rg_lam'], n_seq)
    tm = min(seq_len // 2, 512)
    tc = 512 if seq_len > 512 else c
    z1 = _hyena_order(0, hvc, 0, hvc, 1, mats, filt, p['hy_bias'], tm, tc, False)
    y_hy = _hyena_order(1, z1, 0, hvc, 2, mats, filt, p['hy_bias'], tm, tc, True)
    x1, h2, idx, gw = _post_mixer(y_rg.reshape(-1, d_rg), y_hy.reshape(-1, c), x2d, pos, mod,
                                  p['gn_rg'], p['gn_hy'], p['w_out'], p['norm2_g'],
                                  p['router_w'], p['router_b'], group_of_tile)
    return x1, h2, idx, gw, st


def kernel(x_prompt, x_sample, state_rglru, c, c_ctx, ada_w, ada_b, norm1_g, norm2_g, w_in, rg_conv_w, rg_conv_b, rg_wa, rg_ba, rg_wx, rg_bx, rg_lam, hy_conv_w, hy_conv_b, hy_w1, hy_b1, hy_freq, hy_w2, hy_b2, hy_w3, hy_decay, hy_bias, gn_rg, gn_hy, w_out, router_w, router_b, exp_w_gate, exp_w_up, exp_w_down, sh_w_gate, sh_w_up, sh_w_down, final_g):
    n_cb, seq_c, d = x_prompt.shape
    n_lb, seq_l, _ = x_sample.shape
    depth = ada_w.shape[0]
    d_rg = gn_rg.shape[1]
    hd = d_rg // RG_HEADS
    t_c = n_cb * seq_c
    t_l = n_lb * seq_l
    assert n_lb + 1 <= SUBLANES
    assert d == 2 * SUBLANES * LANES

    pos = _grid_pos_emb(seq_l // GRID_W, d)
    cvec = jnp.zeros((SUBLANES, d), F32).at[0].set(c_ctx).at[1:1 + n_lb].set(c)
    mats_c = _dft_matrices(seq_c)
    mats_l = _dft_matrices(seq_l)

    def group_ctx(i, tm):
        return 0

    def group_lat(i, tm):
        return 1 + (i * tm) // seq_l

    xc = x_prompt.reshape(t_c, d)
    xs = x_sample.reshape(t_l, d)
    ctx_states = []
    for l in range(depth):
        last = l == depth - 1
        mod = _modulation(cvec, ada_w[l], ada_b[l][None]).reshape(SUBLANES, N_MOD, d)

        w4 = jnp.concatenate([rg_wa[l, 0], rg_wx[l, 0], rg_wa[l, 1], rg_wx[l, 1]], axis=-1).astype(BF16)
        b4 = jnp.concatenate([rg_ba[l, 0].reshape(RG_HEADS, 1, hd), rg_bx[l, 0].reshape(RG_HEADS, 1, hd),
                              rg_ba[l, 1].reshape(RG_HEADS, 1, hd), rg_bx[l, 1].reshape(RG_HEADS, 1, hd)], axis=-1)
        p = {
            'norm1_g': norm1_g[l][None], 'norm2_g': norm2_g[l][None], 'w_in': w_in[l].astype(BF16),
            'rg_conv_w': rg_conv_w[l], 'rg_conv_b': rg_conv_b[l][None], 'rg_w4': w4, 'rg_b4': b4,
            'rg_lam': rg_lam[l], 'hy_conv_w': hy_conv_w[l], 'hy_conv_b': hy_conv_b[l][None],
            'hy_bias': hy_bias[l], 'gn_rg': gn_rg[l][None], 'gn_hy': gn_hy[l][None],
            'w_out': w_out[l].astype(BF16), 'router_w': router_w[l].astype(BF16), 'router_b': router_b[l][None],
        }
        filt_args = (hy_w1[l], hy_b1[l][None], hy_freq[l][None], hy_w2[l], hy_b2[l][None], hy_w3[l],
                     hy_decay[l].reshape(1, -1))
        filt_c = _hyena_filters(seq_c, mats_c[:4], *filt_args)
        filt_l = _hyena_filters(seq_l, mats_l[:4], *filt_args)

        h0_c = jnp.zeros((n_cb, 2, d_rg), F32)
        x1_c, h2_c, idx_c, gw_c, st_c = _mixer_path(xc, None, n_cb, seq_c, h0_c, mod, group_ctx, p,
                                                    mats_c, filt_c, n_seq=min(8, n_cb))
        ctx_states.append(st_c)
        x1_l, h2_l, idx_l, gw_l, _ = _mixer_path(xs, pos if l == 0 else None, n_lb, seq_l,
                                                 state_rglru[:, l], mod, group_lat, p,
                                                 mats_l, filt_l, n_seq=1)

        idx_all = jnp.concatenate([idx_c, idx_l], axis=0)
        dest, blk_e, blk_valid, n_used = _route(idx_all, router_w.shape[-1], EXPERT_ROWS)
        dest_flat = dest.reshape(-1)
        sh = (sh_w_gate[l].astype(BF16), sh_w_up[l].astype(BF16), sh_w_down[l].astype(BF16))
        xb, shared = _dispatch(dest_flat, h2_c, h2_l, blk_e.shape[0] * EXPERT_ROWS, *sh)
        yb = _experts(blk_e, blk_valid, n_used, xb, exp_w_gate[l], exp_w_up[l], exp_w_down[l])
        xc = _finish(dest_flat, shared, x1_c, gw_c, yb, mod, final_g[None], 0, group_ctx, last)
        xs = _finish(dest_flat, shared, x1_l, gw_l, yb, mod, final_g[None], t_c, group_lat, last)

    new_state = jnp.stack(ctx_states, axis=1).astype(x_prompt.dtype)
    return (xc.reshape(n_cb, seq_c, d), xs.reshape(n_lb, seq_l, d), new_state)
```

```python
import functools
import math

import jax
import jax.numpy as jnp
from jax import lax
from jax.experimental import pallas as pl
from jax.experimental.pallas import tpu as pltpu

F32 = jnp.float32
BF16 = jnp.bfloat16

GRID_W = 64
RG_HEADS = 8
RG_CONV_W = 4
RG_C = 8.0
HY_CONV_W = 3
HY_BANDS = 16
TOP_K = 8
ROUTED_SCALE = 2.5
N_MOD = 6
EPS = 1e-6

LANES = 128
SUBLANES = 8
VMEM_LIMIT_BYTES = 56 * 1024 * 1024

EXPERT_ROWS = 576


def _params(n_axes, vmem=VMEM_LIMIT_BYTES):
    return pltpu.CompilerParams(dimension_semantics=("arbitrary",) * n_axes, vmem_limit_bytes=vmem)


def _rms(x, g):
    return x * lax.rsqrt(jnp.mean(x * x, axis=-1, keepdims=True) + EPS) * g


def _sigmoid(x):
    return 0.5 * jnp.tanh(0.5 * x) + 0.5


HI_HALF = -65536


def _pack_pair(lo, hi):
    lo_b = lax.bitcast_convert_type(lo.astype(BF16).astype(F32), jnp.int32)
    hi_b = lax.bitcast_convert_type(hi.astype(BF16).astype(F32), jnp.int32)
    return hi_b | lax.shift_right_logical(lo_b, 16)


def _unpack_pair(p):
    lo = lax.bitcast_convert_type(lax.shift_left(p, 16), F32)
    hi = lax.bitcast_convert_type(p & HI_HALF, F32)
    return lo, hi


def _store_token_tiles(ref, packed):
    m = packed.shape[0]
    for j in range(SUBLANES):
        ref[pl.ds(j, m, stride=SUBLANES), :] = packed[:, j * LANES:(j + 1) * LANES]


def _load_token_tiles(ref, row0, m):
    return jnp.concatenate(
        [ref[pl.ds(row0 * SUBLANES + j, m, stride=SUBLANES), :] for j in range(SUBLANES)], axis=1)


def _mod_kernel(c_ref, w_ref, b_ref, o_ref):
    c = c_ref[...]
    s = (c * jax.nn.sigmoid(c)).astype(BF16)
    o_ref[...] = jnp.dot(s, w_ref[...].astype(BF16), preferred_element_type=F32) + b_ref[...]


def _modulation(cvec, ada_w, ada_b):
    d, n = ada_w.shape
    tn = 1536
    return pl.pallas_call(
        _mod_kernel,
        grid=(n // tn,),
        in_specs=[pl.BlockSpec((SUBLANES, d), lambda j: (0, 0)),
                  pl.BlockSpec((d, tn), lambda j: (0, j)),
                  pl.BlockSpec((1, tn), lambda j: (0, j))],
        out_specs=pl.BlockSpec((SUBLANES, tn), lambda j: (0, j)),
        out_shape=jax.ShapeDtypeStruct((SUBLANES, n), F32),
        compiler_params=_params(1),
        name="modulation",
    )(cvec, ada_w, ada_b)


HALO = 16


def _inproj_kernel(*refs, has_pos, tn, n_rg, seq_len):
    if has_pos:
        (x_ref, xp_ref, xn_ref, pos_ref, pp_ref, pn_ref, mod_ref, g_ref, w_hbm, cw_ref, cb_ref,
         u_ref, hv_ref, w_vmem, h_scr, par_scr, sem) = refs
    else:
        (x_ref, xp_ref, xn_ref, mod_ref, g_ref, w_hbm, cw_ref, cb_ref,
         u_ref, hv_ref, w_vmem, h_scr, par_scr, sem) = refs
    i = pl.program_id(0)
    j = pl.program_id(1)
    tm = x_ref.shape[0]

    @pl.when(jnp.logical_and(i == 0, j == 0))
    def _():
        cp = pltpu.make_async_copy(w_hbm, w_vmem, sem)
        cp.start()
        cp.wait()

    @pl.when(j == 0)
    def _():
        gain = g_ref[...] * (1.0 + mod_ref[0, 1:2, :])

        def normed(x_r, p_r):
            x = x_r[...]
            if has_pos:
                x = x + p_r[...]
            return (_rms(x, gain) + mod_ref[0, 0:1, :]).astype(BF16)

        h_scr[0:HALO, :] = normed(xp_ref, pp_ref if has_pos else None)
        h_scr[HALO:HALO + tm, :] = normed(x_ref, pos_ref if has_pos else None)
        h_scr[HALO + tm:, :] = normed(xn_ref, pn_ref if has_pos else None)

    w = w_vmem[:, pl.ds(pl.multiple_of(j * tn, tn), tn)]

    @pl.when(j < n_rg)
    def _():
        u_ref[...] = jnp.dot(h_scr[HALO:HALO + tm, :], w, preferred_element_type=F32)

    @pl.when(j >= n_rg)
    def _():
        ue = jnp.dot(h_scr[...], w, preferred_element_type=F32)
        t_in_seq = (i * tm + lax.broadcasted_iota(jnp.int32, (tm, tn), 0)) & (seq_len - 1)
        cw = cw_ref[...]
        prev = jnp.where(t_in_seq == 0, 0.0, _shift_rows(ue, -1)[HALO:HALO + tm])
        nxt = jnp.where(t_in_seq == seq_len - 1, 0.0, _shift_rows(ue, 1)[HALO:HALO + tm])
        acc = cb_ref[...] + cw[0:1, :] * prev + cw[1:2, :] * ue[HALO:HALO + tm] + cw[2:3, :] * nxt
        n_seq, _, rows, _ = hv_ref.shape
        for c0 in range(0, tn, LANES):
            par_scr[c0 // LANES] = acc[:, c0:c0 + LANES]
        for s in range(n_seq):
            for par in range(2):
                for c0 in range(0, tn, LANES):
                    picked = par_scr[c0 // LANES, pl.ds(s * 2 * rows + par, rows, stride=2), :]
                    hv_ref[s, par, :, c0:c0 + LANES] = picked.astype(BF16)


def _in_proj(x2d, pos, mod, norm_g, w_bf, conv_w, conv_b, seq_len, group_of_tile, tm=512, tn=1024):
    t, d = x2d.shape
    n = w_bf.shape[1]
    n_hv = conv_w.shape[1]
    n_rg = (n - n_hv) // tn
    has_pos = pos is not None
    hb = tm // HALO
    last_h = t // HALO - 1
    seqs = max(1, tm // seq_len)
    tiles_per_seq = max(1, seq_len // tm)
    rows = min(tm, seq_len) // 2

    def prev_blk(i):
        return jnp.maximum(i * hb - 1, 0)

    def next_blk(i):
        return jnp.minimum((i + 1) * hb, last_h)

    in_specs = [pl.BlockSpec((tm, d), lambda i, j: (i, 0)),
                pl.BlockSpec((HALO, d), lambda i, j: (prev_blk(i), 0)),
                pl.BlockSpec((HALO, d), lambda i, j: (next_blk(i), 0))]
    args = [x2d, x2d, x2d]
    if has_pos:
        pos_tiles = pos.shape[0] // tm
        last_p = pos.shape[0] // HALO - 1
        in_specs += [pl.BlockSpec((tm, d), lambda i, j: (i % pos_tiles, 0)),
                     pl.BlockSpec((HALO, d), lambda i, j: (jnp.maximum((i % pos_tiles) * hb - 1, 0), 0)),
                     pl.BlockSpec((HALO, d), lambda i, j: (jnp.minimum((i % pos_tiles + 1) * hb, last_p), 0))]
        args += [pos, pos, pos]
    in_specs += [pl.BlockSpec((1, N_MOD, d), lambda i, j: (group_of_tile(i, tm), 0, 0)),
                 pl.BlockSpec((1, d), lambda i, j: (0, 0)),
                 pl.BlockSpec(memory_space=pl.ANY),
                 pl.BlockSpec((HY_CONV_W, tn), lambda i, j: (0, jnp.maximum(j - n_rg, 0))),
                 pl.BlockSpec((1, tn), lambda i, j: (0, jnp.maximum(j - n_rg, 0)))]
    args += [mod, norm_g, w_bf, conv_w, conv_b]
    return pl.pallas_call(
        functools.partial(_inproj_kernel, has_pos=has_pos, tn=tn, n_rg=n_rg, seq_len=seq_len),
        grid=(t // tm, n // tn),
        in_specs=in_specs,
        out_specs=[pl.BlockSpec((tm, tn), lambda i, j: (i, jnp.minimum(j, n_rg - 1))),
                   pl.BlockSpec((seqs, 2, rows, tn),
                                lambda i, j: (i // tiles_per_seq, 0, i % tiles_per_seq,
                                              jnp.maximum(j - n_rg, 0)))],
        out_shape=[jax.ShapeDtypeStruct((t, n - n_hv), F32),
                   jax.ShapeDtypeStruct((t // seq_len, 2, seq_len // 2, n_hv), BF16)],
        scratch_shapes=[pltpu.VMEM((d, n), BF16), pltpu.VMEM((tm + 2 * HALO, d), BF16),
                        pltpu.VMEM((tn // LANES, tm, LANES), F32), pltpu.SemaphoreType.DMA(())],
        compiler_params=_params(2),
        name="in_proj",
    )(*args)


def _shift_rows(win, off):
    if off == 0:
        return win
    n = win.shape[0]
    return pltpu.roll(win, (-off) % n, axis=0)


def _scan_chunk(a, b, reverse):
    n = a.shape[0]
    row = lax.broadcasted_iota(jnp.int32, a.shape, 0)
    dist = 1
    while dist < n:
        if reverse:
            a_s = pltpu.roll(a, n - dist, axis=0)
            b_s = pltpu.roll(b, n - dist, axis=0)
            m = row < n - dist
        else:
            a_s = pltpu.roll(a, dist, axis=0)
            b_s = pltpu.roll(b, dist, axis=0)
            m = row >= dist
        b = jnp.where(m, a * b_s + b, b)
        a = jnp.where(m, a * a_s, a)
        dist *= 2
    return a, b


def _rglru_kernel(xr_ref, gr_ref, cw_ref, cb_ref, w4_ref, b4_ref, lam_ref, h0_ref,
                  y_ref, st_ref, af, bf, ab, bb, hf, *, seq_len, n_seq, t1, tc):
    hd = xr_ref.shape[-1]
    nlam = -lam_ref[...]
    sp = jnp.maximum(nlam, 0.0) + jnp.log1p(jnp.exp(-jnp.abs(nlam)))
    cw = cw_ref[...]
    cb = cb_ref[...]
    b4 = b4_ref[0]
    nc1 = seq_len // t1
    ncs = seq_len // tc

    def per_seq(s, carry0):
        def gates(c, carry):
            r0 = pl.multiple_of(c * t1, t1)
            cur = xr_ref[s, pl.ds(r0, t1), :]
            p0 = pl.multiple_of(jnp.maximum(r0 - SUBLANES, 0), SUBLANES)
            n0 = pl.multiple_of(jnp.minimum(r0 + t1, seq_len - SUBLANES), SUBLANES)
            prev = jnp.where(c > 0, xr_ref[s, pl.ds(p0, SUBLANES), :], 0.0)
            nxt = jnp.where(c < nc1 - 1, xr_ref[s, pl.ds(n0, SUBLANES), :], 0.0)
            win = jnp.concatenate([prev, cur, nxt], axis=0)
            xr = cb
            for k in range(RG_CONV_W):
                xr = xr + cw[k:k + 1, :] * _shift_rows(win, k - RG_CONV_W // 2)[SUBLANES:SUBLANES + t1]
            z = jnp.dot(xr.astype(BF16), w4_ref[0], preferred_element_type=F32) + b4
            for d_i, (a_scr, b_scr) in enumerate(((af, bf), (ab, bb))):
                r = _sigmoid(z[:, (2 * d_i) * hd:(2 * d_i + 1) * hd])
                gi = _sigmoid(z[:, (2 * d_i + 1) * hd:(2 * d_i + 2) * hd])
                log_a = (-RG_C) * r * sp[d_i:d_i + 1, :]
                a = jnp.exp(log_a)
                a_scr[pl.ds(r0, t1), :] = a
                b_scr[pl.ds(r0, t1), :] = jnp.sqrt(-jnp.tanh(log_a) * (a * a + 1.0)) * (gi * xr)
            return carry

        lax.fori_loop(0, nc1, gates, 0)
        h0 = h0_ref[s]

        def fwd(c, carry):
            r0 = pl.multiple_of(c * tc, tc)
            a, h = _scan_chunk(af[pl.ds(r0, tc), :], bf[pl.ds(r0, tc), :], False)
            h = a * carry + h
            hf[pl.ds(r0, tc), :] = h
            return h[tc - 1:tc, :]

        s_f = lax.fori_loop(0, ncs, fwd, h0[0:1, :])

        def bwd(cc, carry):
            r0 = pl.multiple_of((ncs - 1 - cc) * tc, tc)
            a, h = _scan_chunk(ab[pl.ds(r0, tc), :], bb[pl.ds(r0, tc), :], True)
            h = a * carry + h
            g = gr_ref[s, pl.ds(r0, tc), :]
            y_ref[s, pl.ds(r0, tc), :] = jax.nn.gelu(g) * (hf[pl.ds(r0, tc), :] + h)
            return h[0:1, :]

        s_b = lax.fori_loop(0, ncs, bwd, h0[1:2, :])
        st_ref[s] = jnp.concatenate([s_f, s_b], axis=0)
        return carry0

    lax.fori_loop(0, n_seq, per_seq, 0)


def _rglru(u3, h0, conv_w, conv_b, w4, b4, lam, n_seq):
    b, seq_len, _ = u3.shape
    hd = w4.shape[1]
    d_rg = hd * RG_HEADS
    t1 = min(seq_len, 256)
    tc = 64
    kern = functools.partial(_rglru_kernel, seq_len=seq_len, n_seq=n_seq, t1=t1, tc=tc)
    return pl.pallas_call(
        kern,
        grid=(b // n_seq, RG_HEADS),
        in_specs=[pl.BlockSpec((n_seq, seq_len, hd), lambda i, h: (i, 0, h)),
                  pl.BlockSpec((n_seq, seq_len, hd), lambda i, h: (i, 0, RG_HEADS + h)),
                  pl.BlockSpec((RG_CONV_W, hd), lambda i, h: (0, h)),
                  pl.BlockSpec((1, hd), lambda i, h: (0, h)),
                  pl.BlockSpec((1, hd, 4 * hd), lambda i, h: (h, 0, 0)),
                  pl.BlockSpec((1, 1, 4 * hd), lambda i, h: (h, 0, 0)),
                  pl.BlockSpec((2, hd), lambda i, h: (0, h)),
                  pl.BlockSpec((n_seq, 2, hd), lambda i, h: (i, 0, h))],
        out_specs=[pl.BlockSpec((n_seq, seq_len, hd), lambda i, h: (i, 0, h)),
                   pl.BlockSpec((n_seq, 2, hd), lambda i, h: (i, 0, h))],
        out_shape=[jax.ShapeDtypeStruct((b, seq_len, d_rg), F32),
                   jax.ShapeDtypeStruct((b, 2, d_rg), F32)],
        scratch_shapes=[pltpu.VMEM((seq_len, hd), F32)] * 5,
        compiler_params=_params(2),
        name="rglru",
    )(u3, u3, conv_w, conv_b, w4, b4, lam, h0)


def _filt_time_kernel(w1_ref, b1_ref, fr_ref, w2_ref, b2_ref, w3_ref, dec_ref,
                      g_ref, d_ref, st_ref, *, seq_len, tl):
    i = pl.program_id(0)
    hi = lax.Precision.HIGHEST
    half = seq_len // 2
    r = i * tl + lax.broadcasted_iota(jnp.int32, (tl, LANES), 0)
    posi = 2 * (r & (half - 1)) + jnp.where(r >= half, 1, 0)
    pos = posi.astype(F32)
    lane = lax.broadcasted_iota(jnp.int32, (tl, LANES), 1)
    band = jnp.where(lane <= HY_BANDS, lane, lane - HY_BANDS).astype(F32)
    ang = (2.0 * math.pi) * pos / seq_len * band
    t = pos / seq_len
    feats = jnp.where(lane == 0, t,
                      jnp.where(lane <= HY_BANDS, jnp.cos(ang),
                                jnp.where(lane <= 2 * HY_BANDS, -jnp.sin(ang), 0.0)))
    fr = fr_ref[...]
    hid = jnp.sin(fr * (jnp.dot(feats, w1_ref[...], precision=hi, preferred_element_type=F32) + b1_ref[...]))
    hid = jnp.sin(fr * (jnp.dot(hid, w2_ref[...], precision=hi, preferred_element_type=F32) + b2_ref[...]))
    k = jnp.dot(hid.astype(BF16), w3_ref[...].astype(BF16), preferred_element_type=F32)
    k = k * jnp.exp(-t[:, 0:1] * jnp.abs(dec_ref[...]))
    c = k.shape[1] // 4
    first = posi[:, 0:1] == 0
    p4 = posi[:, 0:1] & 3
    sign = jnp.where(p4 == 0, 1.0, jnp.where(p4 == 2, -1.0, 0.0))

    @pl.when(i == 0)
    def _():
        st_ref[...] = jnp.zeros_like(st_ref)

    for o in range(2):
        kf = k[:, (2 * o) * c:(2 * o + 1) * c]
        kb = jnp.where(first, 0.0, k[:, (2 * o + 1) * c:(2 * o + 2) * c])
        g = kf + kb
        g_ref[:, o * c:(o + 1) * c] = g.astype(BF16)
        d_ref[:, o * c:(o + 1) * c] = (kf - kb).astype(BF16)
        st_ref[0:1, o * c:(o + 1) * c] += jnp.sum(kf * kf + kb * kb, axis=0, keepdims=True)
        st_ref[1:2, o * c:(o + 1) * c] += jnp.sum(sign * g, axis=0, keepdims=True)


def _half_dfts(tabs, cos_even, sin_even, cos_odd, sin_odd):
    ce, se, co, so = tabs
    ae = jnp.dot(ce[...], cos_even, preferred_element_type=F32)
    ao = jnp.dot(co[...], cos_odd, preferred_element_type=F32)
    be = jnp.dot(se[...], sin_even, preferred_element_type=F32)
    bo = jnp.dot(so[...], sin_odd, preferred_element_type=F32)
    return ae, ao, be, bo


def _filt_dft_kernel(ce_ref, se_ref, co_ref, so_ref, g_ref, d_ref, st_ref, kf_ref, ks_ref, qf_ref, qs_ref):
    m = pl.program_id(1)
    half = g_ref.shape[0] // 2
    ae, ao, be, bo = _half_dfts((ce_ref, se_ref, co_ref, so_ref),
                                g_ref[0:half, :], d_ref[0:half, :], g_ref[half:, :], d_ref[half:, :])
    scale = lax.rsqrt(st_ref[0:1, :] + EPS)
    is0 = (m * ae.shape[0] + lax.broadcasted_iota(jnp.int32, ae.shape, 0)) == 0
    kf_ref[...] = (ae + ao) * scale
    ks_ref[...] = (ae - ao) * scale
    qf_ref[...] = jnp.where(is0, st_ref[1:2, :], be + bo) * scale
    qs_ref[...] = jnp.where(is0, bo, bo - be) * scale


def _hyena_filters(seq_len, tabs, w1, b1, freq, w2, b2, w3, decay):
    n_hid = w1.shape[1]
    n_out = w3.shape[1]
    c2 = n_out // 2
    tl = min(seq_len, 512)
    w1p = jnp.zeros((LANES, n_hid), F32).at[:w1.shape[0]].set(w1)
    g, d, stats = pl.pallas_call(
        functools.partial(_filt_time_kernel, seq_len=seq_len, tl=tl),
        grid=(seq_len // tl,),
        in_specs=[pl.BlockSpec((LANES, n_hid), lambda i: (0, 0)),
                  pl.BlockSpec((1, n_hid), lambda i: (0, 0)),
                  pl.BlockSpec((1, n_hid), lambda i: (0, 0)),
                  pl.BlockSpec((n_hid, n_hid), lambda i: (0, 0)),
                  pl.BlockSpec((1, n_hid), lambda i: (0, 0)),
                  pl.BlockSpec((n_hid, n_out), lambda i: (0, 0)),
                  pl.BlockSpec((1, n_out), lambda i: (0, 0))],
        out_specs=[pl.BlockSpec((tl, c2), lambda i: (i, 0)),
                   pl.BlockSpec((tl, c2), lambda i: (i, 0)),
                   pl.BlockSpec((SUBLANES, c2), lambda i: (0, 0))],
        out_shape=[jax.ShapeDtypeStruct((seq_len, c2), BF16),
                   jax.ShapeDtypeStruct((seq_len, c2), BF16),
                   jax.ShapeDtypeStruct((SUBLANES, c2), F32)],
        compiler_params=_params(1),
        name="hyena_filter_taps",
    )(w1p, b1, freq, w2, b2, w3, decay)
    half = seq_len // 2
    tm = min(half, 512)
    tn = 512
    tab = pl.BlockSpec((tm, half), lambda n, m: (m, 0))
    return pl.pallas_call(
        _filt_dft_kernel,
        grid=(c2 // tn, half // tm),
        in_specs=[tab, tab, tab, tab,
                  pl.BlockSpec((seq_len, tn), lambda n, m: (0, n)),
                  pl.BlockSpec((seq_len, tn), lambda n, m: (0, n)),
                  pl.BlockSpec((SUBLANES, tn), lambda n, m: (0, n))],
        out_specs=[pl.BlockSpec((tm, tn), lambda n, m: (m, n))] * 4,
        out_shape=[jax.ShapeDtypeStruct((half, c2), F32)] * 4,
        compiler_params=_params(2),
        name="hyena_filter_dft",
    )(tabs[0], tabs[1], tabs[2], tabs[3], g, d, stats)


def _dft_matrices(seq_len):
    assert seq_len & (seq_len - 1) == 0
    half = seq_len // 2
    tm = min(half, 256)
    out = jax.ShapeDtypeStruct((half, half), BF16)
    return pl.pallas_call(
        functools.partial(_dft_table_kernel, seq_len=seq_len),
        grid=(half // tm,),
        in_specs=[],
        out_specs=[pl.BlockSpec((tm, half), lambda i: (i, 0))] * 7,
        out_shape=[out] * 7,
        scratch_shapes=[pltpu.VMEM((tm, half), F32)] * 4,
        compiler_params=_params(1),
        name="dft_tables",
    )()


def _dft_table_kernel(ce_ref, se_ref, co_ref, so_ref, set_ref, cot_ref, sot_ref, c0e, s0e, c0o, s0o, *, seq_len):
    i = pl.program_id(0)
    tm, half = ce_ref.shape
    wrap = 2 * seq_len - 1
    unit = math.pi / seq_len
    row = lax.broadcasted_iota(jnp.int32, (tm, half), 0)
    col = lax.broadcasted_iota(jnp.int32, (tm, half), 1)
    col1 = col[0:1, :]

    def trig(n):
        ang = (n & wrap).astype(F32) * unit
        return jnp.cos(ang), jnp.sin(ang)

    @pl.when(i == 0)
    def _():
        c0e[...], s0e[...] = trig(2 * row * col)
        c0o[...], s0o[...] = trig(row * (2 * col + 1))

    def rotate(c0, s0, n):
        cn, sn = trig(n)
        return c0[...] * cn - s0[...] * sn, s0[...] * cn + c0[...] * sn

    r0 = i * tm
    ce, se = rotate(c0e, s0e, 2 * r0 * col1)
    co, so = rotate(c0o, s0o, r0 * (2 * col1 + 1))
    cot, sot = rotate(c0e, s0e, (2 * r0 + 1) * col1)
    alt_col = jnp.where((col & 1) == 0, 1.0, -1.0)
    alt_row = jnp.where(((row + r0) & 1) == 0, 1.0, -1.0)
    first_row = row + r0 == 0
    ce_ref[...] = ce.astype(BF16)
    se_ref[...] = jnp.where(first_row, alt_col, se).astype(BF16)
    co_ref[...] = co.astype(BF16)
    so_ref[...] = jnp.where(first_row, alt_col, so).astype(BF16)
    set_ref[...] = jnp.where(col == 0, alt_row, se).astype(BF16)
    cot_ref[...] = cot.astype(BF16)
    sot_ref[...] = jnp.where(col == 0, alt_row, sot).astype(BF16)


def _hy_fwd_kernel(ce_ref, se_ref, co_ref, so_ref, u_ref, kf_ref, ks_ref, qf_ref, qs_ref,
                   ee_ref, eo_ref, de_ref, do_ref, *, seq_len):
    m = pl.program_id(2)
    kf, ks, qf, qs = kf_ref[...], ks_ref[...], qf_ref[...], qs_ref[...]
    is0 = (m * kf.shape[0] + lax.broadcasted_iota(jnp.int32, kf.shape, 0)) == 0
    inv_n = 0.5 / seq_len
    w = jnp.where(is0, inv_n, 2.0 * inv_n)
    for s in range(u_ref.shape[0]):
        ue, uo = u_ref[s, 0], u_ref[s, 1]
        ae, ao, be, bo = _half_dfts((ce_ref, se_ref, co_ref, so_ref), ue, ue, uo, uo)
        a_f, a_s = ae + ao, ae - ao
        b_f, b_s = be + bo, bo - be
        pre_f = jnp.where(is0, a_f * kf, a_f * kf - b_f * qf)
        pre_s = jnp.where(is0, a_s * ks, a_s * ks - b_s * qs)
        pm_f = a_f * qf + b_f * kf
        pm_s = a_s * qs + b_s * ks
        mid_re = be * qf - bo * qs
        mid_mim = be * qs + bo * qf
        ee_ref[s] = ((pre_f + pre_s) * w).astype(BF16)
        eo_ref[s] = ((pre_f - pre_s) * w).astype(BF16)
        de_ref[s] = (jnp.where(is0, mid_re, pm_f - pm_s) * (2.0 * inv_n)).astype(BF16)
        do_ref[s] = (jnp.where(is0, mid_mim, pm_f + pm_s) * (2.0 * inv_n)).astype(BF16)


def _hy_inv_kernel(*refs, natural_out):
    if natural_out:
        ce_ref, set_ref, cot_ref, sot_ref, ee_ref, eo_ref, de_ref, do_ref, u_ref, x_ref, bias_ref, z_ref, scr = refs
    else:
        ce_ref, set_ref, cot_ref, sot_ref, ee_ref, eo_ref, de_ref, do_ref, u_ref, x_ref, bias_ref, z_ref = refs
    bias = bias_ref[...]
    for s in range(u_ref.shape[0]):
        y_e = jnp.dot(ce_ref[...], ee_ref[s], preferred_element_type=F32)
        y_e = y_e + jnp.dot(set_ref[...], de_ref[s], preferred_element_type=F32)
        y_o = jnp.dot(cot_ref[...], eo_ref[s], preferred_element_type=F32)
        y_o = y_o + jnp.dot(sot_ref[...], do_ref[s], preferred_element_type=F32)
        z_e = (y_e + u_ref[s, 0].astype(F32) * bias) * x_ref[s, 0].astype(F32)
        z_o = (y_o + u_ref[s, 1].astype(F32) * bias) * x_ref[s, 1].astype(F32)
        if natural_out:
            th = z_e.shape[0]
            for j in range(z_e.shape[1] // LANES):
                scr[j, pl.ds(0, th, stride=2), :] = z_e[:, j * LANES:(j + 1) * LANES]
                scr[j, pl.ds(1, th, stride=2), :] = z_o[:, j * LANES:(j + 1) * LANES]
            for j in range(z_e.shape[1] // LANES):
                z_ref[s, :, j * LANES:(j + 1) * LANES] = scr[j].astype(BF16)
        else:
            z_ref[s, 0] = z_e.astype(BF16)
            z_ref[s, 1] = z_o.astype(BF16)


def _hyena_order(order, u_arr, u_cb, hvc, gate_cb, tabs, filt, bias, tm, tc, natural_out):
    ce, se, co, so, se_t, co_t, so_t = tabs
    kf, ks, qf, qs = filt
    b, _, half, _ = hvc.shape
    seq_len = 2 * half
    c = kf.shape[1] // 2
    nct = c // tc
    sb = max(1, min(b, 1024 // seq_len))
    grid = (b // sb, nct, half // tm)
    tab = pl.BlockSpec((tm, half), lambda i, n, m: (m, 0))
    spec = pl.BlockSpec((tm, tc), lambda i, n, m: (m, order * nct + n))
    freq = pl.BlockSpec((sb, tm, tc), lambda i, n, m: (i, m, n))
    ee, eo, de, do = pl.pallas_call(
        functools.partial(_hy_fwd_kernel, seq_len=seq_len),
        grid=grid,
        in_specs=[tab, tab, tab, tab,
                  pl.BlockSpec((sb, 2, half, tc), lambda i, n, m: (i, 0, 0, u_cb * nct + n)),
                  spec, spec, spec, spec],
        out_specs=[freq] * 4,
        out_shape=[jax.ShapeDtypeStruct((b, half, c), BF16)] * 4,
        compiler_params=_params(3),
        name="hyena_fwd_dft",
    )(ce, se, co, so, u_arr, kf, ks, qf, qs)
    whole = pl.BlockSpec((sb, half, tc), lambda i, n, m: (i, 0, n))
    if natural_out:
        out_spec = pl.BlockSpec((sb, 2 * tm, tc), lambda i, n, m: (i, m, n))
        out_shape = jax.ShapeDtypeStruct((b, seq_len, c), BF16)
        scratch = [pltpu.VMEM((tc // LANES, 2 * tm, LANES), F32)]
    else:
        out_spec = pl.BlockSpec((sb, 2, tm, tc), lambda i, n, m: (i, 0, m, n))
        out_shape = jax.ShapeDtypeStruct((b, 2, half, c), BF16)
        scratch = []
    return pl.pallas_call(
        functools.partial(_hy_inv_kernel, natural_out=natural_out),
        grid=grid,
        in_specs=[tab, tab, tab, tab, whole, whole, whole, whole,
                  pl.BlockSpec((sb, 2, tm, tc), lambda i, n, m: (i, 0, m, u_cb * nct + n)),
                  pl.BlockSpec((sb, 2, tm, tc), lambda i, n, m: (i, 0, m, gate_cb * nct + n)),
                  pl.BlockSpec((1, tc), lambda i, n, m: (0, n))],
        out_specs=out_spec,
        out_shape=out_shape,
        scratch_shapes=scratch,
        compiler_params=_params(3),
        name="hyena_inv_dft",
    )(ce, se_t, co_t, so_t, ee, eo, de, do, u_arr, hvc, bias[order][None])


def _post_mixer_kernel(*refs, has_pos):
    if has_pos:
        (yrg_ref, yhy_ref, x_ref, pos_ref, mod_ref, gnr_ref, gnh_ref, wo_ref, n2_ref,
         rw_ref, rb_ref, x1_ref, h2_ref, idx_ref, gw_ref) = refs
    else:
        (yrg_ref, yhy_ref, x_ref, mod_ref, gnr_ref, gnh_ref, wo_ref, n2_ref,
         rw_ref, rb_ref, x1_ref, h2_ref, idx_ref, gw_ref) = refs
    d_rg = yrg_ref.shape[1]
    na = _rms(yrg_ref[...], gnr_ref[...]).astype(BF16)
    nb = _rms(yhy_ref[...].astype(F32), gnh_ref[...]).astype(BF16)
    y = jnp.dot(na, wo_ref[0:d_rg, :], preferred_element_type=F32)
    y = y + jnp.dot(nb, wo_ref[d_rg:, :], preferred_element_type=F32)
    x = x_ref[...]
    if has_pos:
        x = x + pos_ref[...]
    x1 = x + mod_ref[0, 2:3, :] * y
    x1_ref[...] = x1
    h2f = _rms(x1, n2_ref[...] * (1.0 + mod_ref[0, 4:5, :])) + mod_ref[0, 3:4, :]
    half = h2f.shape[1] // 2
    _store_token_tiles(h2_ref, _pack_pair(h2f[:, :half], h2f[:, half:]))
    h2 = h2f.astype(BF16)
    scores = jax.nn.sigmoid(jnp.dot(h2, rw_ref[...], preferred_element_type=F32))
    sel = scores + rb_ref[...]
    n_exp = scores.shape[1]
    lane = lax.broadcasted_iota(jnp.int32, scores.shape, 1).astype(F32)
    col = lax.broadcasted_iota(jnp.int32, idx_ref.shape, 1)
    idx_acc = jnp.zeros(idx_ref.shape, F32)
    gw_acc = jnp.zeros(gw_ref.shape, F32)
    for k in range(TOP_K):
        mx = jnp.max(sel, axis=1, keepdims=True)
        pick = jnp.min(jnp.where(sel == mx, lane, float(n_exp)), axis=1, keepdims=True)
        hit = lane == pick
        val = jnp.sum(jnp.where(hit, scores, 0.0), axis=1, keepdims=True)
        sel = jnp.where(hit, -jnp.inf, sel)
        idx_acc = jnp.where(col == k, pick, idx_acc)
        gw_acc = jnp.where(col == k, val, gw_acc)
    idx_ref[...] = idx_acc.astype(jnp.int32)
    gw_ref[...] = gw_acc / jnp.sum(gw_acc, axis=1, keepdims=True) * ROUTED_SCALE


def _post_mixer(y_rg, y_hy, x2d, pos, mod, gn_rg, gn_hy, w_out_bf, norm2_g, router_w_bf, router_b,
                group_of_tile, tm=256):
    t, d = x2d.shape
    d_rg = y_rg.shape[1]
    d_hy = y_hy.shape[1]
    n_exp = router_w_bf.shape[1]
    has_pos = pos is not None
    in_specs = [pl.BlockSpec((tm, d_rg), lambda i: (i, 0)),
                pl.BlockSpec((tm, d_hy), lambda i: (i, 0)),
                pl.BlockSpec((tm, d), lambda i: (i, 0))]
    args = [y_rg, y_hy, x2d]
    if has_pos:
        pos_tiles = pos.shape[0] // tm
        in_specs.append(pl.BlockSpec((tm, d), lambda i: (i % pos_tiles, 0)))
        args.append(pos)
    in_specs += [pl.BlockSpec((1, N_MOD, d), lambda i: (group_of_tile(i, tm), 0, 0)),
                 pl.BlockSpec((1, d_rg), lambda i: (0, 0)),
                 pl.BlockSpec((1, d_hy), lambda i: (0, 0)),
                 pl.BlockSpec((d_rg + d_hy, d), lambda i: (0, 0)),
                 pl.BlockSpec((1, d), lambda i: (0, 0)),
                 pl.BlockSpec((d, n_exp), lambda i: (0, 0)),
                 pl.BlockSpec((1, n_exp), lambda i: (0, 0))]
    args += [mod, gn_rg, gn_hy, w_out_bf, norm2_g, router_w_bf, router_b]
    return pl.pallas_call(
        functools.partial(_post_mixer_kernel, has_pos=has_pos),
        grid=(t // tm,),
        in_specs=in_specs,
        out_specs=[pl.BlockSpec((tm, d), lambda i: (i, 0)),
                   pl.BlockSpec((tm * SUBLANES, LANES), lambda i: (i, 0)),
                   pl.BlockSpec((tm, TOP_K), lambda i: (i, 0)),
                   pl.BlockSpec((tm, TOP_K), lambda i: (i, 0))],
        out_shape=[jax.ShapeDtypeStruct((t, d), F32),
                   jax.ShapeDtypeStruct((t * SUBLANES, LANES), jnp.int32),
                   jax.ShapeDtypeStruct((t, TOP_K), jnp.int32),
                   jax.ShapeDtypeStruct((t, TOP_K), F32)],
        compiler_params=_params(1),
        name="post_mixer",
    )(*args)


def _route_kernel(idx_ref, dest_ref, be_ref, bv_ref, nu_ref, tri, cnt, base, pst, *, blk, n_blk):
    p = pl.program_id(0)
    i = pl.program_id(1)
    tm, top_k = idx_ref.shape
    ne = cnt.shape[1]

    def div_blk(n):
        return jnp.floor((n + 0.5) / blk)

    @pl.when(jnp.logical_and(p == 0, i == 0))
    def _():
        r = lax.broadcasted_iota(jnp.int32, (tm, tm), 0)
        c = lax.broadcasted_iota(jnp.int32, (tm, tm), 1)
        tri[...] = jnp.where(r > c, 1.0, 0.0).astype(BF16)
        cnt[...] = jnp.zeros_like(cnt)

    idx = idx_ref[...]
    lane = lax.broadcasted_iota(jnp.int32, (tm, ne), 1)
    hits = [lane == idx[:, k:k + 1] for k in range(top_k)]
    occ = jnp.zeros((tm, ne), F32)
    for h in hits:
        occ = occ + jnp.where(h, 1.0, 0.0)
    col_sum = jnp.sum(occ, axis=0, keepdims=True)

    @pl.when(p == 0)
    def _():
        cnt[0:1, :] += col_sum

    @pl.when(jnp.logical_and(p == 1, i == 0))
    def _():
        counts = cnt[...]
        padded = div_blk(counts + (blk - 1.0)) * blk
        r = lax.broadcasted_iota(jnp.int32, (ne, ne), 0)
        c = lax.broadcasted_iota(jnp.int32, (ne, ne), 1)
        upper = jnp.where(r <= c, 1.0, 0.0)
        pend = jnp.dot(padded, upper, precision=lax.Precision.HIGHEST, preferred_element_type=F32)
        pstart = pend - padded
        pst[...] = pstart
        base[...] = jnp.zeros_like(base)
        b0 = (lax.broadcasted_iota(jnp.int32, (n_blk, ne), 0) * blk).astype(F32)
        be = jnp.sum(jnp.where(pend[0:1, :] <= b0, 1.0, 0.0), axis=1, keepdims=True)
        be = jnp.minimum(be, ne - 1.0)
        own = lax.broadcasted_iota(jnp.int32, (n_blk, ne), 1).astype(F32) == be
        pst_b = jnp.sum(jnp.where(own, pstart[0:1, :], 0.0), axis=1, keepdims=True)
        cnt_b = jnp.sum(jnp.where(own, counts[0:1, :], 0.0), axis=1, keepdims=True)
        valid = jnp.clip(cnt_b - (b0[:, 0:1] - pst_b), 0.0, float(blk))
        be_ref[...] = be.astype(jnp.int32)
        bv_ref[...] = valid.astype(jnp.int32)
        total = jnp.max(pend[0:1, :], axis=1, keepdims=True)
        nu_ref[...] = jnp.broadcast_to(div_blk(total).astype(jnp.int32), nu_ref.shape)

    @pl.when(p == 1)
    def _():
        cum = jnp.dot(tri[...], occ.astype(BF16), preferred_element_type=F32) + base[0:1, :] + pst[0:1, :]
        col = lax.broadcasted_iota(jnp.int32, (tm, top_k), 1)
        acc = jnp.zeros((tm, top_k), F32)
        for k in range(top_k):
            v = jnp.sum(jnp.where(hits[k], cum, 0.0), axis=1, keepdims=True)
            acc = jnp.where(col == k, v, acc)
        dest_ref[...] = acc.astype(jnp.int32)
        base[0:1, :] += col_sum


def _route(idx, n_exp, blk, tm=512):
    n_tok, top_k = idx.shape
    n_blk = -(-n_tok * top_k // blk) + n_exp
    dest, be, bv, nu = pl.pallas_call(
        functools.partial(_route_kernel, blk=blk, n_blk=n_blk),
        grid=(2, n_tok // tm),
        in_specs=[pl.BlockSpec((tm, top_k), lambda p, i: (i, 0))],
        out_specs=[pl.BlockSpec((tm, top_k), lambda p, i: (i * p, 0)),
                   pl.BlockSpec((n_blk, 1), lambda p, i: (0, 0)),
                   pl.BlockSpec((n_blk, 1), lambda p, i: (0, 0)),
                   pl.BlockSpec((SUBLANES, LANES), lambda p, i: (0, 0))],
        out_shape=[jax.ShapeDtypeStruct((n_tok, top_k), jnp.int32),
                   jax.ShapeDtypeStruct((n_blk, 1), jnp.int32),
                   jax.ShapeDtypeStruct((n_blk, 1), jnp.int32),
                   jax.ShapeDtypeStruct((SUBLANES, LANES), jnp.int32)],
        scratch_shapes=[pltpu.VMEM((tm, tm), BF16), pltpu.VMEM((SUBLANES, n_exp), F32),
                        pltpu.VMEM((SUBLANES, n_exp), F32), pltpu.VMEM((SUBLANES, n_exp), F32)],
        compiler_params=_params(2),
        name="route",
    )(idx)
    return dest, be.reshape(n_blk), bv.reshape(n_blk), nu[0, 0:1]


def _dispatch_kernel(dest_ref, xa_ref, xl_ref, sg_ref, su_ref, sd_ref, out_ref, sh_ref, sem, *, n_a):
    tm = xa_ref.shape[0] // SUBLANES
    dh = SUBLANES * LANES

    def tile(ref, r):
        return ref.at[pl.ds(pl.multiple_of(r * SUBLANES, SUBLANES), SUBLANES), :]

    def run(src):
        def issue(t, c):
            for k in range(TOP_K):
                pltpu.make_async_copy(tile(src, t), tile(out_ref, dest_ref[t * TOP_K + k]), sem).start(priority=k % 2)
            return c

        lax.fori_loop(0, tm, issue, 0)

        lo, hi = _unpack_pair(_load_token_tiles(src, 0, tm))
        lo = lo.astype(BF16)
        hi = hi.astype(BF16)
        g = jnp.dot(lo, sg_ref[0:dh, :], preferred_element_type=F32)
        g = g + jnp.dot(hi, sg_ref[dh:, :], preferred_element_type=F32)
        u = jnp.dot(lo, su_ref[0:dh, :], preferred_element_type=F32)
        u = u + jnp.dot(hi, su_ref[dh:, :], preferred_element_type=F32)
        hmid = (g * _sigmoid(g) * u).astype(BF16)
        sh_ref[...] = jnp.dot(hmid, sd_ref[...], preferred_element_type=F32).astype(BF16)

        def drain(t, c):
            for k in range(TOP_K):
                pltpu.make_async_copy(tile(src, 0), tile(out_ref, 0), sem).wait()
            return c

        lax.fori_loop(0, tm, drain, 0, unroll=8)

    @pl.when(pl.program_id(0) < n_a)
    def _():
        run(xa_ref)

    @pl.when(pl.program_id(0) >= n_a)
    def _():
        run(xl_ref)


def _dispatch(dest_flat, h2p_a, h2p_l, n_rows, sg_bf, su_bf, sd_bf, tm=256):
    n_a = h2p_a.shape[0] // (tm * SUBLANES)
    n_l = h2p_l.shape[0] // (tm * SUBLANES)
    d, ds_ = sg_bf.shape
    return pl.pallas_call(
        functools.partial(_dispatch_kernel, n_a=n_a),
        grid=(n_a + n_l,),
        in_specs=[pl.BlockSpec((tm * TOP_K,), lambda i: (i,), memory_space=pltpu.SMEM),
                  pl.BlockSpec((tm * SUBLANES, LANES), lambda i: (jnp.minimum(i, n_a - 1), 0)),
                  pl.BlockSpec((tm * SUBLANES, LANES), lambda i: (jnp.maximum(i - n_a, 0), 0)),
                  pl.BlockSpec((d, ds_), lambda i: (0, 0)),
                  pl.BlockSpec((d, ds_), lambda i: (0, 0)),
                  pl.BlockSpec((ds_, d), lambda i: (0, 0))],
        out_specs=[pl.BlockSpec(memory_space=pl.ANY),
                   pl.BlockSpec((tm, d), lambda i: (i, 0))],
        out_shape=[jax.ShapeDtypeStruct((n_rows * SUBLANES, LANES), jnp.int32),
                   jax.ShapeDtypeStruct(((n_a + n_l) * tm, d), BF16)],
        scratch_shapes=[pltpu.SemaphoreType.DMA(())],
        compiler_params=_params(1),
        name="dispatch",
    )(dest_flat, h2p_a, h2p_l, sg_bf, su_bf, sd_bf)


WEIGHT_DMA_PRIORITY = 1


def _expert_kernel(be_ref, bv_ref, nu_ref, x_ref, wg_hbm, wu_hbm, wd_hbm, o_ref,
                   wg_f, wu_f, wd_f, wg_b, wu_b, wd_b, sem, grp, *, n_blk):
    i = pl.program_id(0)
    n_used = nu_ref[0]
    e = be_ref[i]
    active = i < n_used
    changed = jnp.logical_or(i == 0, e != be_ref[jnp.maximum(i - 1, 0)])

    def expert_at(j):
        return be_ref[jnp.minimum(j, n_blk - 1)]

    def next_group(j):
        ej = expert_at(j)
        return lax.while_loop(lambda q: jnp.logical_and(q < n_used, expert_at(q) == ej), lambda q: q + 1, j + 1)

    def copies(ex, slot):
        return (pltpu.make_async_copy(wg_hbm.at[ex], wg_f.at[slot], sem.at[slot, 0]),
                pltpu.make_async_copy(wu_hbm.at[ex], wu_f.at[slot], sem.at[slot, 1]),
                pltpu.make_async_copy(wd_hbm.at[ex], wd_f.at[slot], sem.at[slot, 2]))

    @pl.when(jnp.logical_and(active, i == 0))
    def _():
        grp[0] = 0
        for cp in copies(e, 0):
            cp.start(priority=WEIGHT_DMA_PRIORITY)
        n1 = next_group(i)

        @pl.when(n1 < n_used)
        def _():
            for cp in copies(expert_at(n1), 1):
                cp.start(priority=WEIGHT_DMA_PRIORITY)

    @pl.when(jnp.logical_and(active, changed))
    def _():
        slot = grp[0] % 2
        for cp in copies(e, slot):
            cp.wait()
        wg_b[...] = wg_f[slot].astype(BF16)
        wu_b[...] = wu_f[slot].astype(BF16)
        wd_b[...] = wd_f[slot].astype(BF16)
        n2 = next_group(next_group(i))

        @pl.when(n2 < n_used)
        def _():
            for cp in copies(expert_at(n2), slot):
                cp.start(priority=WEIGHT_DMA_PRIORITY)

        grp[0] = grp[0] + 1

    @pl.when(active)
    def _():
        tm = x_ref.shape[0] // SUBLANES
        dh = SUBLANES * LANES
        live = lax.broadcasted_iota(jnp.int32, (tm, dh), 0) < bv_ref[i]
        lo, hi = _unpack_pair(_load_token_tiles(x_ref, 0, tm))
        lo = jnp.where(live, lo, 0.0).astype(BF16)
        hi = jnp.where(live, hi, 0.0).astype(BF16)
        g = jnp.dot(lo, wg_b[0:dh, :], preferred_element_type=F32)
        g = g + jnp.dot(hi, wg_b[dh:, :], preferred_element_type=F32)
        u = jnp.dot(lo, wu_b[0:dh, :], preferred_element_type=F32)
        u = u + jnp.dot(hi, wu_b[dh:, :], preferred_element_type=F32)
        hmid = (g * _sigmoid(g) * u).astype(BF16)
        y = jnp.dot(hmid, wd_b[...], preferred_element_type=F32)
        _store_token_tiles(o_ref, _pack_pair(y[:, :dh], y[:, dh:]))


def _experts(blk_e, blk_valid, n_used, xb, wg, wu, wd):
    n_exp, d, de = wg.shape
    tm = EXPERT_ROWS
    n_blk = xb.shape[0] // (tm * SUBLANES)
    grid_spec = pltpu.PrefetchScalarGridSpec(
        num_scalar_prefetch=3,
        grid=(n_blk,),
        in_specs=[pl.BlockSpec((tm * SUBLANES, LANES), lambda i, be, bv, nu: (jnp.minimum(i, nu[0] - 1), 0)),
                  pl.BlockSpec(memory_space=pl.ANY),
                  pl.BlockSpec(memory_space=pl.ANY),
                  pl.BlockSpec(memory_space=pl.ANY)],
        out_specs=pl.BlockSpec((tm * SUBLANES, LANES), lambda i, be, bv, nu: (jnp.minimum(i, nu[0] - 1), 0)),
        scratch_shapes=[pltpu.VMEM((2, d, de), F32), pltpu.VMEM((2, d, de), F32), pltpu.VMEM((2, de, d), F32),
                        pltpu.VMEM((d, de), BF16), pltpu.VMEM((d, de), BF16), pltpu.VMEM((de, d), BF16),
                        pltpu.SemaphoreType.DMA((2, 3)), pltpu.SMEM((1,), jnp.int32)],
    )
    return pl.pallas_call(
        functools.partial(_expert_kernel, n_blk=n_blk),
        grid_spec=grid_spec,
        out_shape=jax.ShapeDtypeStruct(xb.shape, jnp.int32),
        compiler_params=_params(1),
        name="routed_experts",
    )(blk_e, blk_valid, n_used, xb, wg, wu, wd)


FINISH_ROWS = 64


def _finish_kernel(dcur_ref, dnxt_ref, sh_ref, x1_ref, gw_ref, mod_ref, fg_ref,
                   yb_hbm, o_ref, gbuf, gwb, sem, *, final_norm):
    i = pl.program_id(0)
    n_tiles = pl.num_programs(0)
    tm, d = x1_ref.shape
    dh = SUBLANES * LANES
    slot = i % 2

    def row_copy(row, t, k, s):
        src = yb_hbm.at[pl.ds(pl.multiple_of(row * SUBLANES, SUBLANES), SUBLANES), :]
        dst = gbuf.at[s, pl.ds(pl.multiple_of((k * tm + t) * SUBLANES, SUBLANES), SUBLANES), :]
        return pltpu.make_async_copy(src, dst, sem.at[s])

    def gather(d_ref, s):
        def issue(t, c):
            for k in range(TOP_K):
                row_copy(d_ref[t * TOP_K + k], t, k, s).start(priority=k % 2)
            return c

        lax.fori_loop(0, tm, issue, 0)

    @pl.when(i == 0)
    def _():
        gather(dcur_ref, 0)

    @pl.when(i + 1 < n_tiles)
    def _():
        gather(dnxt_ref, 1 - slot)

    gw = gw_ref[...]
    for k in range(TOP_K):
        gwb[k] = jnp.broadcast_to(gw[:, k:k + 1], (tm, LANES))

    def drain(t, c):
        for k in range(TOP_K):
            row_copy(0, 0, 0, slot).wait()
        return c

    lax.fori_loop(0, tm, drain, 0, unroll=8)

    rows_g = gbuf.at[slot]
    for r0 in range(0, tm, FINISH_ROWS):
        rs = slice(r0, r0 + FINISH_ROWS)
        ssq = jnp.zeros((FINISH_ROWS, 1), F32)
        for j in range(SUBLANES):
            cols = (slice(j * LANES, (j + 1) * LANES), slice(dh + j * LANES, dh + (j + 1) * LANES))
            acc = [sh_ref[rs, c].astype(F32) for c in cols]
            for k in range(TOP_K):
                halves = _unpack_pair(rows_g[pl.ds((k * tm + r0) * SUBLANES + j, FINISH_ROWS, stride=SUBLANES), :])
                w = gwb[k, rs, :]
                acc = [a + w * h for a, h in zip(acc, halves)]
            for c, a in zip(cols, acc):
                x2 = x1_ref[rs, c] + mod_ref[0, 5:6, c] * a
                o_ref[rs, c] = x2
                ssq = ssq + jnp.sum(x2 * x2, axis=-1, keepdims=True)
        if final_norm:
            inv = lax.rsqrt(ssq / d + EPS)
            o_ref[rs, :] = o_ref[rs, :] * inv * fg_ref[...]


def _finish(dest_flat, shared, x1, gw, yb, mod, final_g, row0, group_of_tile, final_norm, tm=256):
    n_rows, d = x1.shape
    t0 = row0 // tm
    n_tiles = n_rows // tm
    return pl.pallas_call(
        functools.partial(_finish_kernel, final_norm=final_norm),
        grid=(n_tiles,),
        in_specs=[pl.BlockSpec((tm * TOP_K,), lambda i: (t0 + i,), memory_space=pltpu.SMEM),
                  pl.BlockSpec((tm * TOP_K,), lambda i: (t0 + jnp.minimum(i + 1, n_tiles - 1),),
                               memory_space=pltpu.SMEM),
                  pl.BlockSpec((tm, d), lambda i: (t0 + i, 0)),
                  pl.BlockSpec((tm, d), lambda i: (i, 0)),
                  pl.BlockSpec((tm, TOP_K), lambda i: (i, 0)),
                  pl.BlockSpec((1, N_MOD, d), lambda i: (group_of_tile(i, tm), 0, 0)),
                  pl.BlockSpec((1, d), lambda i: (0, 0)),
                  pl.BlockSpec(memory_space=pl.ANY)],
        out_specs=pl.BlockSpec((tm, d), lambda i: (i, 0)),
        out_shape=jax.ShapeDtypeStruct((n_rows, d), F32),
        scratch_shapes=[pltpu.VMEM((2, TOP_K * tm * SUBLANES, LANES), jnp.int32),
                        pltpu.VMEM((TOP_K, tm, LANES), F32), pltpu.SemaphoreType.DMA((2,))],
        compiler_params=_params(1),
        name="finish",
    )(dest_flat, dest_flat, shared, x1, gw, mod, final_g, yb)


def _grid_pos_emb(rows, d):
    quarter = d // 4
    omega = 1.0 / (10000.0 ** (jnp.arange(quarter, dtype=F32) / quarter))
    r = jnp.arange(rows, dtype=F32)[:, None] * omega
    cc = jnp.arange(GRID_W, dtype=F32)[:, None] * omega
    by_row = jnp.concatenate([jnp.sin(r), jnp.cos(r)], axis=-1)
    by_col = jnp.concatenate([jnp.sin(cc), jnp.cos(cc)], axis=-1)
    full = jnp.concatenate([jnp.broadcast_to(by_row[:, None, :], (rows, GRID_W, d // 2)),
                            jnp.broadcast_to(by_col[None, :, :], (rows, GRID_W, d // 2))], axis=-1)
    return full.reshape(rows * GRID_W, d)


def _mixer_path(x2d, pos, n_b, seq_len, h0, mod, group_of_tile, p, mats, filt, n_seq):
    d_rg = p['gn_rg'].shape[1]
    c = p['gn_hy'].shape[1]
    u_rg, hvc = _in_proj(x2d, pos, mod, p['norm1_g'], p['w_in'], p['hy_conv_w'], p['hy_conv_b'],
                         seq_len, group_of_tile)
    u3 = u_rg.reshape(n_b, seq_len, u_rg.shape[1])
    y_rg, st = _rglru(u3, h0, p['rg_conv_w'], p['rg_conv_b'], p['rg_w4'], p['rg_b4'], p['rg_lam'], n_seq)
    tm = min(seq_len // 2, 512)
    tc = 512 if seq_len > 512 else c
    z1 = _hyena_order(0, hvc, 0, hvc, 1, mats, filt, p['hy_bias'], tm, tc, False)
    y_hy = _hyena_order(1, z1, 0, hvc, 2, mats, filt, p['hy_bias'], tm, tc, True)
    x1, h2, idx, gw = _post_mixer(y_rg.reshape(-1, d_rg), y_hy.reshape(-1, c), x2d, pos, mod,
                                  p['gn_rg'], p['gn_hy'], p['w_out'], p['norm2_g'],
                                  p['router_w'], p['router_b'], group_of_tile)
    return x1, h2, idx, gw, st


def kernel(x_prompt, x_sample, state_rglru, c, c_ctx, ada_w, ada_b, norm1_g, norm2_g, w_in, rg_conv_w, rg_conv_b, rg_wa, rg_ba, rg_wx, rg_bx, rg_lam, hy_conv_w, hy_conv_b, hy_w1, hy_b1, hy_freq, hy_w2, hy_b2, hy_w3, hy_decay, hy_bias, gn_rg, gn_hy, w_out, router_w, router_b, exp_w_gate, exp_w_up, exp_w_down, sh_w_gate, sh_w_up, sh_w_down, final_g):
    n_cb, seq_c, d = x_prompt.shape
    n_lb, seq_l, _ = x_sample.shape
    depth = ada_w.shape[0]
    d_rg = gn_rg.shape[1]
    hd = d_rg // RG_HEADS
    t_c = n_cb * seq_c
    t_l = n_lb * seq_l
    assert n_lb + 1 <= SUBLANES
    assert d == 2 * SUBLANES * LANES

    pos = _grid_pos_emb(seq_l // GRID_W, d)
    cvec = jnp.zeros((SUBLANES, d), F32).at[0].set(c_ctx).at[1:1 + n_lb].set(c)
    mats_c = _dft_matrices(seq_c)
    mats_l = _dft_matrices(seq_l)

    def group_ctx(i, tm):
        return 0

    def group_lat(i, tm):
        return 1 + (i * tm) // seq_l

    xc = x_prompt.reshape(t_c, d)
    xs = x_sample.reshape(t_l, d)
    ctx_states = []
    for l in range(depth):
        last = l == depth - 1
        mod = _modulation(cvec, ada_w[l], ada_b[l][None]).reshape(SUBLANES, N_MOD, d)

        w4 = jnp.concatenate([rg_wa[l, 0], rg_wx[l, 0], rg_wa[l, 1], rg_wx[l, 1]], axis=-1).astype(BF16)
        b4 = jnp.concatenate([rg_ba[l, 0].reshape(RG_HEADS, 1, hd), rg_bx[l, 0].reshape(RG_HEADS, 1, hd),
                              rg_ba[l, 1].reshape(RG_HEADS, 1, hd), rg_bx[l, 1].reshape(RG_HEADS, 1, hd)], axis=-1)
        p = {
            'norm1_g': norm1_g[l][None], 'norm2_g': norm2_g[l][None], 'w_in': w_in[l].astype(BF16),
            'rg_conv_w': rg_conv_w[l], 'rg_conv_b': rg_conv_b[l][None], 'rg_w4': w4, 'rg_b4': b4,
            'rg_lam': rg_lam[l], 'hy_conv_w': hy_conv_w[l], 'hy_conv_b': hy_conv_b[l][None],
            'hy_bias': hy_bias[l], 'gn_rg': gn_rg[l][None], 'gn_hy': gn_hy[l][None],
            'w_out': w_out[l].astype(BF16), 'router_w': router_w[l].astype(BF16), 'router_b': router_b[l][None],
        }
        filt_args = (hy_w1[l], hy_b1[l][None], hy_freq[l][None], hy_w2[l], hy_b2[l][None], hy_w3[l],
                     hy_decay[l].reshape(1, -1))
        filt_c = _hyena_filters(seq_c, mats_c[:4], *filt_args)
        filt_l = _hyena_filters(seq_l, mats_l[:4], *filt_args)

        h0_c = jnp.zeros((n_cb, 2, d_rg), F32)
        x1_c, h2_c, idx_c, gw_c, st_c = _mixer_path(xc, None, n_cb, seq_c, h0_c, mod, group_ctx, p,
                                                    mats_c, filt_c, n_seq=min(8, n_cb))
        ctx_states.append(st_c)
        x1_l, h2_l, idx_l, gw_l, _ = _mixer_path(xs, pos if l == 0 else None, n_lb, seq_l,
                                                 state_rglru[:, l], mod, group_lat, p,
                                                 mats_l, filt_l, n_seq=1)

        idx_all = jnp.concatenate([idx_c, idx_l], axis=0)
        dest, blk_e, blk_valid, n_used = _route(idx_all, router_w.shape[-1], EXPERT_ROWS)
        dest_flat = dest.reshape(-1)
        sh = (sh_w_gate[l].astype(BF16), sh_w_up[l].astype(BF16), sh_w_down[l].astype(BF16))
        xb, shared = _dispatch(dest_flat, h2_c, h2_l, blk_e.shape[0] * EXPERT_ROWS, *sh)
        yb = _experts(blk_e, blk_valid, n_used, xb, exp_w_gate[l], exp_w_up[l], exp_w_down[l])
        xc = _finish(dest_flat, shared, x1_c, gw_c, yb, mod, final_g[None], 0, group_ctx, last)
        xs = _finish(dest_flat, shared, x1_l, gw_l, yb, mod, final_g[None], t_c, group_lat, last)

    new_state = jnp.stack(ctx_states, axis=1).astype(x_prompt.dtype)
    return (xc.reshape(n_cb, seq_c, d), xs.reshape(n_lb, seq_l, d), new_state)
```

```python
import functools
import math

import jax
import jax.numpy as jnp
from jax import lax
from jax.experimental import pallas as pl
from jax.experimental.pallas import tpu as pltpu

F32 = jnp.float32
BF16 = jnp.bfloat16

GRID_W = 64
RG_HEADS = 8
RG_CONV_W = 4
RG_C = 8.0
HY_CONV_W = 3
HY_BANDS = 16
TOP_K = 8
ROUTED_SCALE = 2.5
N_MOD = 6
EPS = 1e-6

LANES = 128
SUBLANES = 8
VMEM_LIMIT_BYTES = 56 * 1024 * 1024

EXPERT_ROWS = 288


def _params(n_axes, vmem=VMEM_LIMIT_BYTES):
    return pltpu.CompilerParams(dimension_semantics=("arbitrary",) * n_axes, vmem_limit_bytes=vmem)


def _rms(x, g):
    return x * lax.rsqrt(jnp.mean(x * x, axis=-1, keepdims=True) + EPS) * g


def _sigmoid(x):
    return 0.5 * jnp.tanh(0.5 * x) + 0.5


HI_HALF = -65536


def _pack_pair(lo, hi):
    lo_b = lax.bitcast_convert_type(lo.astype(BF16).astype(F32), jnp.int32)
    hi_b = lax.bitcast_convert_type(hi.astype(BF16).astype(F32), jnp.int32)
    return hi_b | lax.shift_right_logical(lo_b, 16)


def _unpack_pair(p):
    lo = lax.bitcast_convert_type(lax.shift_left(p, 16), F32)
    hi = lax.bitcast_convert_type(p & HI_HALF, F32)
    return lo, hi


def _store_token_tiles(ref, packed):
    m = packed.shape[0]
    for j in range(SUBLANES):
        ref[pl.ds(j, m, stride=SUBLANES), :] = packed[:, j * LANES:(j + 1) * LANES]


def _load_token_tiles(ref, row0, m):
    return jnp.concatenate(
        [ref[pl.ds(row0 * SUBLANES + j, m, stride=SUBLANES), :] for j in range(SUBLANES)], axis=1)


def _mod_kernel(c_ref, w_ref, b_ref, o_ref):
    c = c_ref[...]
    s = (c * jax.nn.sigmoid(c)).astype(BF16)
    o_ref[...] = jnp.dot(s, w_ref[...].astype(BF16), preferred_element_type=F32) + b_ref[...]


def _modulation(cvec, ada_w, ada_b):
    d, n = ada_w.shape
    tn = 1536
    return pl.pallas_call(
        _mod_kernel,
        grid=(n // tn,),
        in_specs=[pl.BlockSpec((SUBLANES, d), lambda j: (0, 0)),
                  pl.BlockSpec((d, tn), lambda j: (0, j)),
                  pl.BlockSpec((1, tn), lambda j: (0, j))],
        out_specs=pl.BlockSpec((SUBLANES, tn), lambda j: (0, j)),
        out_shape=jax.ShapeDtypeStruct((SUBLANES, n), F32),
        compiler_params=_params(1),
        name="modulation",
    )(cvec, ada_w, ada_b)


HALO = 16


def _inproj_kernel(*refs, has_pos, tn, n_rg, seq_len):
    if has_pos:
        (x_ref, xp_ref, xn_ref, pos_ref, pp_ref, pn_ref, mod_ref, g_ref, w_hbm, cw_ref, cb_ref,
         u_ref, hv_ref, w_vmem, h_scr, par_scr, sem) = refs
    else:
        (x_ref, xp_ref, xn_ref, mod_ref, g_ref, w_hbm, cw_ref, cb_ref,
         u_ref, hv_ref, w_vmem, h_scr, par_scr, sem) = refs
    i = pl.program_id(0)
    j = pl.program_id(1)
    tm = x_ref.shape[0]

    @pl.when(jnp.logical_and(i == 0, j == 0))
    def _():
        cp = pltpu.make_async_copy(w_hbm, w_vmem, sem)
        cp.start()
        cp.wait()

    @pl.when(j == 0)
    def _():
        gain = g_ref[...] * (1.0 + mod_ref[0, 1:2, :])

        def normed(x_r, p_r):
            x = x_r[...]
            if has_pos:
                x = x + p_r[...]
            return (_rms(x, gain) + mod_ref[0, 0:1, :]).astype(BF16)

        h_scr[0:HALO, :] = normed(xp_ref, pp_ref if has_pos else None)
        h_scr[HALO:HALO + tm, :] = normed(x_ref, pos_ref if has_pos else None)
        h_scr[HALO + tm:, :] = normed(xn_ref, pn_ref if has_pos else None)

    w = w_vmem[:, pl.ds(pl.multiple_of(j * tn, tn), tn)]

    @pl.when(j < n_rg)
    def _():
        u_ref[...] = jnp.dot(h_scr[HALO:HALO + tm, :], w, preferred_element_type=F32)

    @pl.when(j >= n_rg)
    def _():
        ue = jnp.dot(h_scr[...], w, preferred_element_type=F32)
        t_in_seq = (i * tm + lax.broadcasted_iota(jnp.int32, (tm, tn), 0)) & (seq_len - 1)
        cw = cw_ref[...]
        prev = jnp.where(t_in_seq == 0, 0.0, _shift_rows(ue, -1)[HALO:HALO + tm])
        nxt = jnp.where(t_in_seq == seq_len - 1, 0.0, _shift_rows(ue, 1)[HALO:HALO + tm])
        acc = cb_ref[...] + cw[0:1, :] * prev + cw[1:2, :] * ue[HALO:HALO + tm] + cw[2:3, :] * nxt
        n_seq, _, rows, _ = hv_ref.shape
        for c0 in range(0, tn, LANES):
            par_scr[c0 // LANES] = acc[:, c0:c0 + LANES]
        for s in range(n_seq):
            for par in range(2):
                for c0 in range(0, tn, LANES):
                    picked = par_scr[c0 // LANES, pl.ds(s * 2 * rows + par, rows, stride=2), :]
                    hv_ref[s, par, :, c0:c0 + LANES] = picked.astype(BF16)


def _in_proj(x2d, pos, mod, norm_g, w_bf, conv_w, conv_b, seq_len, group_of_tile, tm=512, tn=1024):
    t, d = x2d.shape
    n = w_bf.shape[1]
    n_hv = conv_w.shape[1]
    n_rg = (n - n_hv) // tn
    has_pos = pos is not None
    hb = tm // HALO
    last_h = t // HALO - 1
    seqs = max(1, tm // seq_len)
    tiles_per_seq = max(1, seq_len // tm)
    rows = min(tm, seq_len) // 2

    def prev_blk(i):
        return jnp.maximum(i * hb - 1, 0)

    def next_blk(i):
        return jnp.minimum((i + 1) * hb, last_h)

    in_specs = [pl.BlockSpec((tm, d), lambda i, j: (i, 0)),
                pl.BlockSpec((HALO, d), lambda i, j: (prev_blk(i), 0)),
                pl.BlockSpec((HALO, d), lambda i, j: (next_blk(i), 0))]
    args = [x2d, x2d, x2d]
    if has_pos:
        pos_tiles = pos.shape[0] // tm
        last_p = pos.shape[0] // HALO - 1
        in_specs += [pl.BlockSpec((tm, d), lambda i, j: (i % pos_tiles, 0)),
                     pl.BlockSpec((HALO, d), lambda i, j: (jnp.maximum((i % pos_tiles) * hb - 1, 0), 0)),
                     pl.BlockSpec((HALO, d), lambda i, j: (jnp.minimum((i % pos_tiles + 1) * hb, last_p), 0))]
        args += [pos, pos, pos]
    in_specs += [pl.BlockSpec((1, N_MOD, d), lambda i, j: (group_of_tile(i, tm), 0, 0)),
                 pl.BlockSpec((1, d), lambda i, j: (0, 0)),
                 pl.BlockSpec(memory_space=pl.ANY),
                 pl.BlockSpec((HY_CONV_W, tn), lambda i, j: (0, jnp.maximum(j - n_rg, 0))),
                 pl.BlockSpec((1, tn), lambda i, j: (0, jnp.maximum(j - n_rg, 0)))]
    args += [mod, norm_g, w_bf, conv_w, conv_b]
    return pl.pallas_call(
        functools.partial(_inproj_kernel, has_pos=has_pos, tn=tn, n_rg=n_rg, seq_len=seq_len),
        grid=(t // tm, n // tn),
        in_specs=in_specs,
        out_specs=[pl.BlockSpec((tm, tn), lambda i, j: (i, jnp.minimum(j, n_rg - 1))),
                   pl.BlockSpec((seqs, 2, rows, tn),
                                lambda i, j: (i // tiles_per_seq, 0, i % tiles_per_seq,
                                              jnp.maximum(j - n_rg, 0)))],
        out_shape=[jax.ShapeDtypeStruct((t, n - n_hv), F32),
                   jax.ShapeDtypeStruct((t // seq_len, 2, seq_len // 2, n_hv), BF16)],
        scratch_shapes=[pltpu.VMEM((d, n), BF16), pltpu.VMEM((tm + 2 * HALO, d), BF16),
                        pltpu.VMEM((tn // LANES, tm, LANES), F32), pltpu.SemaphoreType.DMA(())],
        compiler_params=_params(2),
        name="in_proj",
    )(*args)


def _shift_rows(win, off):
    if off == 0:
        return win
    n = win.shape[0]
    return pltpu.roll(win, (-off) % n, axis=0)


def _scan_chunk(a, b, reverse):
    n = a.shape[0]
    row = lax.broadcasted_iota(jnp.int32, a.shape, 0)
    dist = 1
    while dist < n:
        if reverse:
            a_s = pltpu.roll(a, n - dist, axis=0)
            b_s = pltpu.roll(b, n - dist, axis=0)
            m = row < n - dist
        else:
            a_s = pltpu.roll(a, dist, axis=0)
            b_s = pltpu.roll(b, dist, axis=0)
            m = row >= dist
        b = jnp.where(m, a * b_s + b, b)
        a = jnp.where(m, a * a_s, a)
        dist *= 2
    return a, b


def _rglru_kernel(xr_ref, gr_ref, cw_ref, cb_ref, w4_ref, b4_ref, lam_ref, h0_ref,
                  y_ref, st_ref, af, bf, ab, bb, hf, *, seq_len, n_seq, t1, tc):
    hd = xr_ref.shape[-1]
    nlam = -lam_ref[...]
    sp = jnp.maximum(nlam, 0.0) + jnp.log1p(jnp.exp(-jnp.abs(nlam)))
    cw = cw_ref[...]
    cb = cb_ref[...]
    b4 = b4_ref[0]
    nc1 = seq_len // t1
    ncs = seq_len // tc

    def per_seq(s, carry0):
        def gates(c, carry):
            r0 = pl.multiple_of(c * t1, t1)
            cur = xr_ref[s, pl.ds(r0, t1), :]
            p0 = pl.multiple_of(jnp.maximum(r0 - SUBLANES, 0), SUBLANES)
            n0 = pl.multiple_of(jnp.minimum(r0 + t1, seq_len - SUBLANES), SUBLANES)
            prev = jnp.where(c > 0, xr_ref[s, pl.ds(p0, SUBLANES), :], 0.0)
            nxt = jnp.where(c < nc1 - 1, xr_ref[s, pl.ds(n0, SUBLANES), :], 0.0)
            win = jnp.concatenate([prev, cur, nxt], axis=0)
            xr = cb
            for k in range(RG_CONV_W):
                xr = xr + cw[k:k + 1, :] * _shift_rows(win, k - RG_CONV_W // 2)[SUBLANES:SUBLANES + t1]
            z = jnp.dot(xr.astype(BF16), w4_ref[0], preferred_element_type=F32) + b4
            for d_i, (a_scr, b_scr) in enumerate(((af, bf), (ab, bb))):
                r = _sigmoid(z[:, (2 * d_i) * hd:(2 * d_i + 1) * hd])
                gi = _sigmoid(z[:, (2 * d_i + 1) * hd:(2 * d_i + 2) * hd])
                log_a = (-RG_C) * r * sp[d_i:d_i + 1, :]
                a = jnp.exp(log_a)
                a_scr[pl.ds(r0, t1), :] = a
                b_scr[pl.ds(r0, t1), :] = jnp.sqrt(-jnp.tanh(log_a) * (a * a + 1.0)) * (gi * xr)
            return carry

        lax.fori_loop(0, nc1, gates, 0)
        h0 = h0_ref[s]

        def fwd(c, carry):
            r0 = pl.multiple_of(c * tc, tc)
            a, h = _scan_chunk(af[pl.ds(r0, tc), :], bf[pl.ds(r0, tc), :], False)
            h = a * carry + h
            hf[pl.ds(r0, tc), :] = h
            return h[tc - 1:tc, :]

        s_f = lax.fori_loop(0, ncs, fwd, h0[0:1, :])

        def bwd(cc, carry):
            r0 = pl.multiple_of((ncs - 1 - cc) * tc, tc)
            a, h = _scan_chunk(ab[pl.ds(r0, tc), :], bb[pl.ds(r0, tc), :], True)
            h = a * carry + h
            g = gr_ref[s, pl.ds(r0, tc), :]
            y_ref[s, pl.ds(r0, tc), :] = jax.nn.gelu(g) * (hf[pl.ds(r0, tc), :] + h)
            return h[0:1, :]

        s_b = lax.fori_loop(0, ncs, bwd, h0[1:2, :])
        st_ref[s] = jnp.concatenate([s_f, s_b], axis=0)
        return carry0

    lax.fori_loop(0, n_seq, per_seq, 0)


def _rglru(u3, h0, conv_w, conv_b, w4, b4, lam, n_seq):
    b, seq_len, _ = u3.shape
    hd = w4.shape[1]
    d_rg = hd * RG_HEADS
    t1 = min(seq_len, 256)
    tc = 64
    kern = functools.partial(_rglru_kernel, seq_len=seq_len, n_seq=n_seq, t1=t1, tc=tc)
    return pl.pallas_call(
        kern,
        grid=(b // n_seq, RG_HEADS),
        in_specs=[pl.BlockSpec((n_seq, seq_len, hd), lambda i, h: (i, 0, h)),
                  pl.BlockSpec((n_seq, seq_len, hd), lambda i, h: (i, 0, RG_HEADS + h)),
                  pl.BlockSpec((RG_CONV_W, hd), lambda i, h: (0, h)),
                  pl.BlockSpec((1, hd), lambda i, h: (0, h)),
                  pl.BlockSpec((1, hd, 4 * hd), lambda i, h: (h, 0, 0)),
                  pl.BlockSpec((1, 1, 4 * hd), lambda i, h: (h, 0, 0)),
                  pl.BlockSpec((2, hd), lambda i, h: (0, h)),
                  pl.BlockSpec((n_seq, 2, hd), lambda i, h: (i, 0, h))],
        out_specs=[pl.BlockSpec((n_seq, seq_len, hd), lambda i, h: (i, 0, h)),
                   pl.BlockSpec((n_seq, 2, hd), lambda i, h: (i, 0, h))],
        out_shape=[jax.ShapeDtypeStruct((b, seq_len, d_rg), F32),
                   jax.ShapeDtypeStruct((b, 2, d_rg), F32)],
        scratch_shapes=[pltpu.VMEM((seq_len, hd), F32)] * 5,
        compiler_params=_params(2),
        name="rglru",
    )(u3, u3, conv_w, conv_b, w4, b4, lam, h0)


def _filt_time_kernel(w1_ref, b1_ref, fr_ref, w2_ref, b2_ref, w3_ref, dec_ref,
                      g_ref, d_ref, st_ref, *, seq_len, tl):
    i = pl.program_id(0)
    hi = lax.Precision.HIGHEST
    half = seq_len // 2
    r = i * tl + lax.broadcasted_iota(jnp.int32, (tl, LANES), 0)
    posi = 2 * (r & (half - 1)) + jnp.where(r >= half, 1, 0)
    pos = posi.astype(F32)
    lane = lax.broadcasted_iota(jnp.int32, (tl, LANES), 1)
    band = jnp.where(lane <= HY_BANDS, lane, lane - HY_BANDS).astype(F32)
    ang = (2.0 * math.pi) * pos / seq_len * band
    t = pos / seq_len
    feats = jnp.where(lane == 0, t,
                      jnp.where(lane <= HY_BANDS, jnp.cos(ang),
                                jnp.where(lane <= 2 * HY_BANDS, -jnp.sin(ang), 0.0)))
    fr = fr_ref[...]
    hid = jnp.sin(fr * (jnp.dot(feats, w1_ref[...], precision=hi, preferred_element_type=F32) + b1_ref[...]))
    hid = jnp.sin(fr * (jnp.dot(hid, w2_ref[...], precision=hi, preferred_element_type=F32) + b2_ref[...]))
    k = jnp.dot(hid.astype(BF16), w3_ref[...].astype(BF16), preferred_element_type=F32)
    k = k * jnp.exp(-t[:, 0:1] * jnp.abs(dec_ref[...]))
    c = k.shape[1] // 4
    first = posi[:, 0:1] == 0
    p4 = posi[:, 0:1] & 3
    sign = jnp.where(p4 == 0, 1.0, jnp.where(p4 == 2, -1.0, 0.0))

    @pl.when(i == 0)
    def _():
        st_ref[...] = jnp.zeros_like(st_ref)

    for o in range(2):
        kf = k[:, (2 * o) * c:(2 * o + 1) * c]
        kb = jnp.where(first, 0.0, k[:, (2 * o + 1) * c:(2 * o + 2) * c])
        g = kf + kb
        g_ref[:, o * c:(o + 1) * c] = g.astype(BF16)
        d_ref[:, o * c:(o + 1) * c] = (kf - kb).astype(BF16)
        st_ref[0:1, o * c:(o + 1) * c] += jnp.sum(kf * kf + kb * kb, axis=0, keepdims=True)
        st_ref[1:2, o * c:(o + 1) * c] += jnp.sum(sign * g, axis=0, keepdims=True)


def _half_dfts(tabs, cos_even, sin_even, cos_odd, sin_odd):
    ce, se, co, so = tabs
    ae = jnp.dot(ce[...], cos_even, preferred_element_type=F32)
    ao = jnp.dot(co[...], cos_odd, preferred_element_type=F32)
    be = jnp.dot(se[...], sin_even, preferred_element_type=F32)
    bo = jnp.dot(so[...], sin_odd, preferred_element_type=F32)
    return ae, ao, be, bo


def _filt_dft_kernel(ce_ref, se_ref, co_ref, so_ref, g_ref, d_ref, st_ref, kf_ref, ks_ref, qf_ref, qs_ref):
    m = pl.program_id(1)
    half = g_ref.shape[0] // 2
    ae, ao, be, bo = _half_dfts((ce_ref, se_ref, co_ref, so_ref),
                                g_ref[0:half, :], d_ref[0:half, :], g_ref[half:, :], d_ref[half:, :])
    scale = lax.rsqrt(st_ref[0:1, :] + EPS)
    is0 = (m * ae.shape[0] + lax.broadcasted_iota(jnp.int32, ae.shape, 0)) == 0
    kf_ref[...] = (ae + ao) * scale
    ks_ref[...] = (ae - ao) * scale
    qf_ref[...] = jnp.where(is0, st_ref[1:2, :], be + bo) * scale
    qs_ref[...] = jnp.where(is0, bo, bo - be) * scale


def _hyena_filters(seq_len, tabs, w1, b1, freq, w2, b2, w3, decay):
    n_hid = w1.shape[1]
    n_out = w3.shape[1]
    c2 = n_out // 2
    tl = min(seq_len, 512)
    w1p = jnp.zeros((LANES, n_hid), F32).at[:w1.shape[0]].set(w1)
    g, d, stats = pl.pallas_call(
        functools.partial(_filt_time_kernel, seq_len=seq_len, tl=tl),
        grid=(seq_len // tl,),
        in_specs=[pl.BlockSpec((LANES, n_hid), lambda i: (0, 0)),
                  pl.BlockSpec((1, n_hid), lambda i: (0, 0)),
                  pl.BlockSpec((1, n_hid), lambda i: (0, 0)),
                  pl.BlockSpec((n_hid, n_hid), lambda i: (0, 0)),
                  pl.BlockSpec((1, n_hid), lambda i: (0, 0)),
                  pl.BlockSpec((n_hid, n_out), lambda i: (0, 0)),
                  pl.BlockSpec((1, n_out), lambda i: (0, 0))],
        out_specs=[pl.BlockSpec((tl, c2), lambda i: (i, 0)),
                   pl.BlockSpec((tl, c2), lambda i: (i, 0)),
                   pl.BlockSpec((SUBLANES, c2), lambda i: (0, 0))],
        out_shape=[jax.ShapeDtypeStruct((seq_len, c2), BF16),
                   jax.ShapeDtypeStruct((seq_len, c2), BF16),
                   jax.ShapeDtypeStruct((SUBLANES, c2), F32)],
        compiler_params=_params(1),
        name="hyena_filter_taps",
    )(w1p, b1, freq, w2, b2, w3, decay)
    half = seq_len // 2
    tm = min(half, 512)
    tn = 512
    tab = pl.BlockSpec((tm, half), lambda n, m: (m, 0))
    return pl.pallas_call(
        _filt_dft_kernel,
        grid=(c2 // tn, half // tm),
        in_specs=[tab, tab, tab, tab,
                  pl.BlockSpec((seq_len, tn), lambda n, m: (0, n)),
                  pl.BlockSpec((seq_len, tn), lambda n, m: (0, n)),
                  pl.BlockSpec((SUBLANES, tn), lambda n, m: (0, n))],
        out_specs=[pl.BlockSpec((tm, tn), lambda n, m: (m, n))] * 4,
        out_shape=[jax.ShapeDtypeStruct((half, c2), F32)] * 4,
        compiler_params=_params(2),
        name="hyena_filter_dft",
    )(tabs[0], tabs[1], tabs[2], tabs[3], g, d, stats)


def _dft_matrices(seq_len):
    assert seq_len & (seq_len - 1) == 0
    half = seq_len // 2
    tm = min(half, 256)
    out = jax.ShapeDtypeStruct((half, half), BF16)
    return pl.pallas_call(
        functools.partial(_dft_table_kernel, seq_len=seq_len),
        grid=(half // tm,),
        in_specs=[],
        out_specs=[pl.BlockSpec((tm, half), lambda i: (i, 0))] * 7,
        out_shape=[out] * 7,
        scratch_shapes=[pltpu.VMEM((tm, half), F32)] * 4,
        compiler_params=_params(1),
        name="dft_tables",
    )()


def _dft_table_kernel(ce_ref, se_ref, co_ref, so_ref, set_ref, cot_ref, sot_ref, c0e, s0e, c0o, s0o, *, seq_len):
    i = pl.program_id(0)
    tm, half = ce_ref.shape
    wrap = 2 * seq_len - 1
    unit = math.pi / seq_len
    row = lax.broadcasted_iota(jnp.int32, (tm, half), 0)
    col = lax.broadcasted_iota(jnp.int32, (tm, half), 1)
    col1 = col[0:1, :]

    def trig(n):
        ang = (n & wrap).astype(F32) * unit
        return jnp.cos(ang), jnp.sin(ang)

    @pl.when(i == 0)
    def _():
        c0e[...], s0e[...] = trig(2 * row * col)
        c0o[...], s0o[...] = trig(row * (2 * col + 1))

    def rotate(c0, s0, n):
        cn, sn = trig(n)
        return c0[...] * cn - s0[...] * sn, s0[...] * cn + c0[...] * sn

    r0 = i * tm
    ce, se = rotate(c0e, s0e, 2 * r0 * col1)
    co, so = rotate(c0o, s0o, r0 * (2 * col1 + 1))
    cot, sot = rotate(c0e, s0e, (2 * r0 + 1) * col1)
    alt_col = jnp.where((col & 1) == 0, 1.0, -1.0)
    alt_row = jnp.where(((row + r0) & 1) == 0, 1.0, -1.0)
    first_row = row + r0 == 0
    ce_ref[...] = ce.astype(BF16)
    se_ref[...] = jnp.where(first_row, alt_col, se).astype(BF16)
    co_ref[...] = co.astype(BF16)
    so_ref[...] = jnp.where(first_row, alt_col, so).astype(BF16)
    set_ref[...] = jnp.where(col == 0, alt_row, se).astype(BF16)
    cot_ref[...] = cot.astype(BF16)
    sot_ref[...] = jnp.where(col == 0, alt_row, sot).astype(BF16)


def _hy_fwd_kernel(ce_ref, se_ref, co_ref, so_ref, u_ref, kf_ref, ks_ref, qf_ref, qs_ref,
                   ee_ref, eo_ref, de_ref, do_ref, *, seq_len):
    m = pl.program_id(2)
    kf, ks, qf, qs = kf_ref[...], ks_ref[...], qf_ref[...], qs_ref[...]
    is0 = (m * kf.shape[0] + lax.broadcasted_iota(jnp.int32, kf.shape, 0)) == 0
    inv_n = 0.5 / seq_len
    w = jnp.where(is0, inv_n, 2.0 * inv_n)
    for s in range(u_ref.shape[0]):
        ue, uo = u_ref[s, 0], u_ref[s, 1]
        ae, ao, be, bo = _half_dfts((ce_ref, se_ref, co_ref, so_ref), ue, ue, uo, uo)
        a_f, a_s = ae + ao, ae - ao
        b_f, b_s = be + bo, bo - be
        pre_f = jnp.where(is0, a_f * kf, a_f * kf - b_f * qf)
        pre_s = jnp.where(is0, a_s * ks, a_s * ks - b_s * qs)
        pm_f = a_f * qf + b_f * kf
        pm_s = a_s * qs + b_s * ks
        mid_re = be * qf - bo * qs
        mid_mim = be * qs + bo * qf
        ee_ref[s] = ((pre_f + pre_s) * w).astype(BF16)
        eo_ref[s] = ((pre_f - pre_s) * w).astype(BF16)
        de_ref[s] = (jnp.where(is0, mid_re, pm_f - pm_s) * (2.0 * inv_n)).astype(BF16)
        do_ref[s] = (jnp.where(is0, mid_mim, pm_f + pm_s) * (2.0 * inv_n)).astype(BF16)


def _hy_inv_kernel(*refs, natural_out):
    if natural_out:
        ce_ref, set_ref, cot_ref, sot_ref, ee_ref, eo_ref, de_ref, do_ref, u_ref, x_ref, bias_ref, z_ref, scr = refs
    else:
        ce_ref, set_ref, cot_ref, sot_ref, ee_ref, eo_ref, de_ref, do_ref, u_ref, x_ref, bias_ref, z_ref = refs
    bias = bias_ref[...]
    for s in range(u_ref.shape[0]):
        y_e = jnp.dot(ce_ref[...], ee_ref[s], preferred_element_type=F32)
        y_e = y_e + jnp.dot(set_ref[...], de_ref[s], preferred_element_type=F32)
        y_o = jnp.dot(cot_ref[...], eo_ref[s], preferred_element_type=F32)
        y_o = y_o + jnp.dot(sot_ref[...], do_ref[s], preferred_element_type=F32)
        z_e = (y_e + u_ref[s, 0].astype(F32) * bias) * x_ref[s, 0].astype(F32)
        z_o = (y_o + u_ref[s, 1].astype(F32) * bias) * x_ref[s, 1].astype(F32)
        if natural_out:
            th = z_e.shape[0]
            for j in range(z_e.shape[1] // LANES):
                scr[j, pl.ds(0, th, stride=2), :] = z_e[:, j * LANES:(j + 1) * LANES]
                scr[j, pl.ds(1, th, stride=2), :] = z_o[:, j * LANES:(j + 1) * LANES]
            for j in range(z_e.shape[1] // LANES):
                z_ref[s, :, j * LANES:(j + 1) * LANES] = scr[j].astype(BF16)
        else:
            z_ref[s, 0] = z_e.astype(BF16)
            z_ref[s, 1] = z_o.astype(BF16)


def _hyena_order(order, u_arr, u_cb, hvc, gate_cb, tabs, filt, bias, tm, tc, natural_out):
    ce, se, co, so, se_t, co_t, so_t = tabs
    kf, ks, qf, qs = filt
    b, _, half, _ = hvc.shape
    seq_len = 2 * half
    c = kf.shape[1] // 2
    nct = c // tc
    sb = max(1, min(b, 1024 // seq_len))
    grid = (b // sb, nct, half // tm)
    tab = pl.BlockSpec((tm, half), lambda i, n, m: (m, 0))
    spec = pl.BlockSpec((tm, tc), lambda i, n, m: (m, order * nct + n))
    freq = pl.BlockSpec((sb, tm, tc), lambda i, n, m: (i, m, n))
    ee, eo, de, do = pl.pallas_call(
        functools.partial(_hy_fwd_kernel, seq_len=seq_len),
        grid=grid,
        in_specs=[tab, tab, tab, tab,
                  pl.BlockSpec((sb, 2, half, tc), lambda i, n, m: (i, 0, 0, u_cb * nct + n)),
                  spec, spec, spec, spec],
        out_specs=[freq] * 4,
        out_shape=[jax.ShapeDtypeStruct((b, half, c), BF16)] * 4,
        compiler_params=_params(3),
        name="hyena_fwd_dft",
    )(ce, se, co, so, u_arr, kf, ks, qf, qs)
    whole = pl.BlockSpec((sb, half, tc), lambda i, n, m: (i, 0, n))
    if natural_out:
        out_spec = pl.BlockSpec((sb, 2 * tm, tc), lambda i, n, m: (i, m, n))
        out_shape = jax.ShapeDtypeStruct((b, seq_len, c), BF16)
        scratch = [pltpu.VMEM((tc // LANES, 2 * tm, LANES), F32)]
    else:
        out_spec = pl.BlockSpec((sb, 2, tm, tc), lambda i, n, m: (i, 0, m, n))
        out_shape = jax.ShapeDtypeStruct((b, 2, half, c), BF16)
        scratch = []
    return pl.pallas_call(
        functools.partial(_hy_inv_kernel, natural_out=natural_out),
        grid=grid,
        in_specs=[tab, tab, tab, tab, whole, whole, whole, whole,
                  pl.BlockSpec((sb, 2, tm, tc), lambda i, n, m: (i, 0, m, u_cb * nct + n)),
                  pl.BlockSpec((sb, 2, tm, tc), lambda i, n, m: (i, 0, m, gate_cb * nct + n)),
                  pl.BlockSpec((1, tc), lambda i, n, m: (0, n))],
        out_specs=out_spec,
        out_shape=out_shape,
        scratch_shapes=scratch,
        compiler_params=_params(3),
        name="hyena_inv_dft",
    )(ce, se_t, co_t, so_t, ee, eo, de, do, u_arr, hvc, bias[order][None])


def _post_mixer_kernel(*refs, has_pos):
    if has_pos:
        (yrg_ref, yhy_ref, x_ref, pos_ref, mod_ref, gnr_ref, gnh_ref, wo_ref, n2_ref,
         rw_ref, rb_ref, x1_ref, h2_ref, idx_ref, gw_ref) = refs
    else:
        (yrg_ref, yhy_ref, x_ref, mod_ref, gnr_ref, gnh_ref, wo_ref, n2_ref,
         rw_ref, rb_ref, x1_ref, h2_ref, idx_ref, gw_ref) = refs
    d_rg = yrg_ref.shape[1]
    na = _rms(yrg_ref[...], gnr_ref[...]).astype(BF16)
    nb = _rms(yhy_ref[...].astype(F32), gnh_ref[...]).astype(BF16)
    y = jnp.dot(na, wo_ref[0:d_rg, :], preferred_element_type=F32)
    y = y + jnp.dot(nb, wo_ref[d_rg:, :], preferred_element_type=F32)
    x = x_ref[...]
    if has_pos:
        x = x + pos_ref[...]
    x1 = x + mod_ref[0, 2:3, :] * y
    x1_ref[...] = x1
    h2f = _rms(x1, n2_ref[...] * (1.0 + mod_ref[0, 4:5, :])) + mod_ref[0, 3:4, :]
    half = h2f.shape[1] // 2
    _store_token_tiles(h2_ref, _pack_pair(h2f[:, :half], h2f[:, half:]))
    h2 = h2f.astype(BF16)
    scores = jax.nn.sigmoid(jnp.dot(h2, rw_ref[...], preferred_element_type=F32))
    sel = scores + rb_ref[...]
    n_exp = scores.shape[1]
    lane = lax.broadcasted_iota(jnp.int32, scores.shape, 1).astype(F32)
    col = lax.broadcasted_iota(jnp.int32, idx_ref.shape, 1)
    idx_acc = jnp.zeros(idx_ref.shape, F32)
    gw_acc = jnp.zeros(gw_ref.shape, F32)
    for k in range(TOP_K):
        mx = jnp.max(sel, axis=1, keepdims=True)
        pick = jnp.min(jnp.where(sel == mx, lane, float(n_exp)), axis=1, keepdims=True)
        hit = lane == pick
        val = jnp.sum(jnp.where(hit, scores, 0.0), axis=1, keepdims=True)
        sel = jnp.where(hit, -jnp.inf, sel)
        idx_acc = jnp.where(col == k, pick, idx_acc)
        gw_acc = jnp.where(col == k, val, gw_acc)
    idx_ref[...] = idx_acc.astype(jnp.int32)
    gw_ref[...] = gw_acc / jnp.sum(gw_acc, axis=1, keepdims=True) * ROUTED_SCALE


def _post_mixer(y_rg, y_hy, x2d, pos, mod, gn_rg, gn_hy, w_out_bf, norm2_g, router_w_bf, router_b,
                group_of_tile, tm=256):
    t, d = x2d.shape
    d_rg = y_rg.shape[1]
    d_hy = y_hy.shape[1]
    n_exp = router_w_bf.shape[1]
    has_pos = pos is not None
    in_specs = [pl.BlockSpec((tm, d_rg), lambda i: (i, 0)),
                pl.BlockSpec((tm, d_hy), lambda i: (i, 0)),
                pl.BlockSpec((tm, d), lambda i: (i, 0))]
    args = [y_rg, y_hy, x2d]
    if has_pos:
        pos_tiles = pos.shape[0] // tm
        in_specs.append(pl.BlockSpec((tm, d), lambda i: (i % pos_tiles, 0)))
        args.append(pos)
    in_specs += [pl.BlockSpec((1, N_MOD, d), lambda i: (group_of_tile(i, tm), 0, 0)),
                 pl.BlockSpec((1, d_rg), lambda i: (0, 0)),
                 pl.BlockSpec((1, d_hy), lambda i: (0, 0)),
                 pl.BlockSpec((d_rg + d_hy, d), lambda i: (0, 0)),
                 pl.BlockSpec((1, d), lambda i: (0, 0)),
                 pl.BlockSpec((d, n_exp), lambda i: (0, 0)),
                 pl.BlockSpec((1, n_exp), lambda i: (0, 0))]
    args += [mod, gn_rg, gn_hy, w_out_bf, norm2_g, router_w_bf, router_b]
    return pl.pallas_call(
        functools.partial(_post_mixer_kernel, has_pos=has_pos),
        grid=(t // tm,),
        in_specs=in_specs,
        out_specs=[pl.BlockSpec((tm, d), lambda i: (i, 0)),
                   pl.BlockSpec((tm * SUBLANES, LANES), lambda i: (i, 0)),
                   pl.BlockSpec((tm, TOP_K), lambda i: (i, 0)),
                   pl.BlockSpec((tm, TOP_K), lambda i: (i, 0))],
        out_shape=[jax.ShapeDtypeStruct((t, d), F32),
                   jax.ShapeDtypeStruct((t * SUBLANES, LANES), jnp.int32),
                   jax.ShapeDtypeStruct((t, TOP_K), jnp.int32),
                   jax.ShapeDtypeStruct((t, TOP_K), F32)],
        compiler_params=_params(1),
        name="post_mixer",
    )(*args)


def _route_kernel(idx_ref, dest_ref, be_ref, bv_ref, nu_ref, tri, cnt, base, pst, *, blk, n_blk):
    p = pl.program_id(0)
    i = pl.program_id(1)
    tm, top_k = idx_ref.shape
    ne = cnt.shape[1]

    def div_blk(n):
        return jnp.floor((n + 0.5) / blk)

    @pl.when(jnp.logical_and(p == 0, i == 0))
    def _():
        r = lax.broadcasted_iota(jnp.int32, (tm, tm), 0)
        c = lax.broadcasted_iota(jnp.int32, (tm, tm), 1)
        tri[...] = jnp.where(r > c, 1.0, 0.0).astype(BF16)
        cnt[...] = jnp.zeros_like(cnt)

    idx = idx_ref[...]
    lane = lax.broadcasted_iota(jnp.int32, (tm, ne), 1)
    hits = [lane == idx[:, k:k + 1] for k in range(top_k)]
    occ = jnp.zeros((tm, ne), F32)
    for h in hits:
        occ = occ + jnp.where(h, 1.0, 0.0)
    col_sum = jnp.sum(occ, axis=0, keepdims=True)

    @pl.when(p == 0)
    def _():
        cnt[0:1, :] += col_sum

    @pl.when(jnp.logical_and(p == 1, i == 0))
    def _():
        counts = cnt[...]
        padded = div_blk(counts + (blk - 1.0)) * blk
        r = lax.broadcasted_iota(jnp.int32, (ne, ne), 0)
        c = lax.broadcasted_iota(jnp.int32, (ne, ne), 1)
        upper = jnp.where(r <= c, 1.0, 0.0)
        pend = jnp.dot(padded, upper, precision=lax.Precision.HIGHEST, preferred_element_type=F32)
        pstart = pend - padded
        pst[...] = pstart
        base[...] = jnp.zeros_like(base)
        b0 = (lax.broadcasted_iota(jnp.int32, (n_blk, ne), 0) * blk).astype(F32)
        be = jnp.sum(jnp.where(pend[0:1, :] <= b0, 1.0, 0.0), axis=1, keepdims=True)
        be = jnp.minimum(be, ne - 1.0)
        own = lax.broadcasted_iota(jnp.int32, (n_blk, ne), 1).astype(F32) == be
        pst_b = jnp.sum(jnp.where(own, pstart[0:1, :], 0.0), axis=1, keepdims=True)
        cnt_b = jnp.sum(jnp.where(own, counts[0:1, :], 0.0), axis=1, keepdims=True)
        valid = jnp.clip(cnt_b - (b0[:, 0:1] - pst_b), 0.0, float(blk))
        be_ref[...] = be.astype(jnp.int32)
        bv_ref[...] = valid.astype(jnp.int32)
        total = jnp.max(pend[0:1, :], axis=1, keepdims=True)
        nu_ref[...] = jnp.broadcast_to(div_blk(total).astype(jnp.int32), nu_ref.shape)

    @pl.when(p == 1)
    def _():
        cum = jnp.dot(tri[...], occ.astype(BF16), preferred_element_type=F32) + base[0:1, :] + pst[0:1, :]
        col = lax.broadcasted_iota(jnp.int32, (tm, top_k), 1)
        acc = jnp.zeros((tm, top_k), F32)
        for k in range(top_k):
            v = jnp.sum(jnp.where(hits[k], cum, 0.0), axis=1, keepdims=True)
            acc = jnp.where(col == k, v, acc)
        dest_ref[...] = acc.astype(jnp.int32)
        base[0:1, :] += col_sum


def _route(idx, n_exp, blk, tm=512):
    n_tok, top_k = idx.shape
    n_blk = -(-n_tok * top_k // blk) + n_exp
    dest, be, bv, nu = pl.pallas_call(
        functools.partial(_route_kernel, blk=blk, n_blk=n_blk),
        grid=(2, n_tok // tm),
        in_specs=[pl.BlockSpec((tm, top_k), lambda p, i: (i, 0))],
        out_specs=[pl.BlockSpec((tm, top_k), lambda p, i: (i * p, 0)),
                   pl.BlockSpec((n_blk, 1), lambda p, i: (0, 0)),
                   pl.BlockSpec((n_blk, 1), lambda p, i: (0, 0)),
                   pl.BlockSpec((SUBLANES, LANES), lambda p, i: (0, 0))],
        out_shape=[jax.ShapeDtypeStruct((n_tok, top_k), jnp.int32),
                   jax.ShapeDtypeStruct((n_blk, 1), jnp.int32),
                   jax.ShapeDtypeStruct((n_blk, 1), jnp.int32),
                   jax.ShapeDtypeStruct((SUBLANES, LANES), jnp.int32)],
        scratch_shapes=[pltpu.VMEM((tm, tm), BF16), pltpu.VMEM((SUBLANES, n_exp), F32),
                        pltpu.VMEM((SUBLANES, n_exp), F32), pltpu.VMEM((SUBLANES, n_exp), F32)],
        compiler_params=_params(2),
        name="route",
    )(idx)
    return dest, be.reshape(n_blk), bv.reshape(n_blk), nu[0, 0:1]


def _dispatch_kernel(dest_ref, xa_ref, xl_ref, sg_ref, su_ref, sd_ref, out_ref, sh_ref, sem, *, n_a):
    tm = xa_ref.shape[0] // SUBLANES
    dh = SUBLANES * LANES

    def tile(ref, r):
        return ref.at[pl.ds(pl.multiple_of(r * SUBLANES, SUBLANES), SUBLANES), :]

    def run(src):
        def issue(t, c):
            for k in range(TOP_K):
                pltpu.make_async_copy(tile(src, t), tile(out_ref, dest_ref[t * TOP_K + k]), sem).start(priority=k % 2)
            return c

        lax.fori_loop(0, tm, issue, 0)

        lo, hi = _unpack_pair(_load_token_tiles(src, 0, tm))
        lo = lo.astype(BF16)
        hi = hi.astype(BF16)
        g = jnp.dot(lo, sg_ref[0:dh, :], preferred_element_type=F32)
        g = g + jnp.dot(hi, sg_ref[dh:, :], preferred_element_type=F32)
        u = jnp.dot(lo, su_ref[0:dh, :], preferred_element_type=F32)
        u = u + jnp.dot(hi, su_ref[dh:, :], preferred_element_type=F32)
        hmid = (g * _sigmoid(g) * u).astype(BF16)
        sh_ref[...] = jnp.dot(hmid, sd_ref[...], preferred_element_type=F32).astype(BF16)

        def drain(t, c):
            for k in range(TOP_K):
                pltpu.make_async_copy(tile(src, 0), tile(out_ref, 0), sem).wait()
            return c

        lax.fori_loop(0, tm, drain, 0, unroll=8)

    @pl.when(pl.program_id(0) < n_a)
    def _():
        run(xa_ref)

    @pl.when(pl.program_id(0) >= n_a)
    def _():
        run(xl_ref)


def _dispatch(dest_flat, h2p_a, h2p_l, n_rows, sg_bf, su_bf, sd_bf, tm=256):
    n_a = h2p_a.shape[0] // (tm * SUBLANES)
    n_l = h2p_l.shape[0] // (tm * SUBLANES)
    d, ds_ = sg_bf.shape
    return pl.pallas_call(
        functools.partial(_dispatch_kernel, n_a=n_a),
        grid=(n_a + n_l,),
        in_specs=[pl.BlockSpec((tm * TOP_K,), lambda i: (i,), memory_space=pltpu.SMEM),
                  pl.BlockSpec((tm * SUBLANES, LANES), lambda i: (jnp.minimum(i, n_a - 1), 0)),
                  pl.BlockSpec((tm * SUBLANES, LANES), lambda i: (jnp.maximum(i - n_a, 0), 0)),
                  pl.BlockSpec((d, ds_), lambda i: (0, 0)),
                  pl.BlockSpec((d, ds_), lambda i: (0, 0)),
                  pl.BlockSpec((ds_, d), lambda i: (0, 0))],
        out_specs=[pl.BlockSpec(memory_space=pl.ANY),
                   pl.BlockSpec((tm, d), lambda i: (i, 0))],
        out_shape=[jax.ShapeDtypeStruct((n_rows * SUBLANES, LANES), jnp.int32),
                   jax.ShapeDtypeStruct(((n_a + n_l) * tm, d), BF16)],
        scratch_shapes=[pltpu.SemaphoreType.DMA(())],
        compiler_params=_params(1),
        name="dispatch",
    )(dest_flat, h2p_a, h2p_l, sg_bf, su_bf, sd_bf)


WEIGHT_DMA_PRIORITY = (1, 1, 0)


def _expert_kernel(be_ref, bv_ref, nu_ref, x_ref, wg_hbm, wu_hbm, wd_hbm, o_ref,
                   wg_f, wu_f, wd_f, wg_b, wu_b, wd_b, sem, grp, *, n_blk):
    i = pl.program_id(0)
    n_used = nu_ref[0]
    e = be_ref[i]
    active = i < n_used
    changed = jnp.logical_or(i == 0, e != be_ref[jnp.maximum(i - 1, 0)])

    def expert_at(j):
        return be_ref[jnp.minimum(j, n_blk - 1)]

    def next_group(j):
        ej = expert_at(j)
        return lax.while_loop(lambda q: jnp.logical_and(q < n_used, expert_at(q) == ej), lambda q: q + 1, j + 1)

    def copies(ex, slot):
        return (pltpu.make_async_copy(wg_hbm.at[ex], wg_f.at[slot], sem.at[slot, 0]),
                pltpu.make_async_copy(wu_hbm.at[ex], wu_f.at[slot], sem.at[slot, 1]),
                pltpu.make_async_copy(wd_hbm.at[ex], wd_f.at[slot], sem.at[slot, 2]))

    @pl.when(jnp.logical_and(active, i == 0))
    def _():
        grp[0] = 0
        for cp, queue in zip(copies(e, 0), WEIGHT_DMA_PRIORITY):
            cp.start(priority=queue)
        n1 = next_group(i)

        @pl.when(n1 < n_used)
        def _():
            for cp, queue in zip(copies(expert_at(n1), 1), WEIGHT_DMA_PRIORITY):
                cp.start(priority=queue)

    @pl.when(jnp.logical_and(active, changed))
    def _():
        slot = grp[0] % 2
        for cp in copies(e, slot):
            cp.wait()
        wg_b[...] = wg_f[slot].astype(BF16)
        wu_b[...] = wu_f[slot].astype(BF16)
        wd_b[...] = wd_f[slot].astype(BF16)
        n2 = next_group(next_group(i))

        @pl.when(n2 < n_used)
        def _():
            for cp, queue in zip(copies(expert_at(n2), slot), WEIGHT_DMA_PRIORITY):
                cp.start(priority=queue)

        grp[0] = grp[0] + 1

    @pl.when(active)
    def _():
        tm = x_ref.shape[0] // SUBLANES
        dh = SUBLANES * LANES
        live = lax.broadcasted_iota(jnp.int32, (tm, dh), 0) < bv_ref[i]
        lo, hi = _unpack_pair(_load_token_tiles(x_ref, 0, tm))
        lo = jnp.where(live, lo, 0.0).astype(BF16)
        hi = jnp.where(live, hi, 0.0).astype(BF16)
        g = jnp.dot(lo, wg_b[0:dh, :], preferred_element_type=F32)
        g = g + jnp.dot(hi, wg_b[dh:, :], preferred_element_type=F32)
        u = jnp.dot(lo, wu_b[0:dh, :], preferred_element_type=F32)
        u = u + jnp.dot(hi, wu_b[dh:, :], preferred_element_type=F32)
        hmid = (g * _sigmoid(g) * u).astype(BF16)
        y = jnp.dot(hmid, wd_b[...], preferred_element_type=F32)
        _store_token_tiles(o_ref, _pack_pair(y[:, :dh], y[:, dh:]))


def _experts(blk_e, blk_valid, n_used, xb, wg, wu, wd):
    n_exp, d, de = wg.shape
    tm = EXPERT_ROWS
    n_blk = xb.shape[0] // (tm * SUBLANES)
    grid_spec = pltpu.PrefetchScalarGridSpec(
        num_scalar_prefetch=3,
        grid=(n_blk,),
        in_specs=[pl.BlockSpec((tm * SUBLANES, LANES), lambda i, be, bv, nu: (jnp.minimum(i, nu[0] - 1), 0)),
                  pl.BlockSpec(memory_space=pl.ANY),
                  pl.BlockSpec(memory_space=pl.ANY),
                  pl.BlockSpec(memory_space=pl.ANY)],
        out_specs=pl.BlockSpec((tm * SUBLANES, LANES), lambda i, be, bv, nu: (jnp.minimum(i, nu[0] - 1), 0)),
        scratch_shapes=[pltpu.VMEM((2, d, de), F32), pltpu.VMEM((2, d, de), F32), pltpu.VMEM((2, de, d), F32),
                        pltpu.VMEM((d, de), BF16), pltpu.VMEM((d, de), BF16), pltpu.VMEM((de, d), BF16),
                        pltpu.SemaphoreType.DMA((2, 3)), pltpu.SMEM((1,), jnp.int32)],
    )
    return pl.pallas_call(
        functools.partial(_expert_kernel, n_blk=n_blk),
        grid_spec=grid_spec,
        out_shape=jax.ShapeDtypeStruct(xb.shape, jnp.int32),
        compiler_params=_params(1),
        name="routed_experts",
    )(blk_e, blk_valid, n_used, xb, wg, wu, wd)


FINISH_ROWS = 64


def _finish_kernel(dcur_ref, dnxt_ref, sh_ref, x1_ref, gw_ref, mod_ref, fg_ref,
                   yb_hbm, o_ref, gbuf, gwb, sem, *, final_norm):
    i = pl.program_id(0)
    n_tiles = pl.num_programs(0)
    tm, d = x1_ref.shape
    dh = SUBLANES * LANES
    slot = i % 2

    def row_copy(row, t, k, s):
        src = yb_hbm.at[pl.ds(pl.multiple_of(row * SUBLANES, SUBLANES), SUBLANES), :]
        dst = gbuf.at[s, pl.ds(pl.multiple_of((k * tm + t) * SUBLANES, SUBLANES), SUBLANES), :]
        return pltpu.make_async_copy(src, dst, sem.at[s])

    def gather(d_ref, s):
        def issue(t, c):
            for k in range(TOP_K):
                row_copy(d_ref[t * TOP_K + k], t, k, s).start(priority=k % 2)
            return c

        lax.fori_loop(0, tm, issue, 0)

    @pl.when(i == 0)
    def _():
        gather(dcur_ref, 0)

    @pl.when(i + 1 < n_tiles)
    def _():
        gather(dnxt_ref, 1 - slot)

    gw = gw_ref[...]
    for k in range(TOP_K):
        gwb[k] = jnp.broadcast_to(gw[:, k:k + 1], (tm, LANES))

    def drain(t, c):
        for k in range(TOP_K):
            row_copy(0, 0, 0, slot).wait()
        return c

    lax.fori_loop(0, tm, drain, 0, unroll=8)

    rows_g = gbuf.at[slot]
    for r0 in range(0, tm, FINISH_ROWS):
        rs = slice(r0, r0 + FINISH_ROWS)
        ssq = jnp.zeros((FINISH_ROWS, 1), F32)
        for j in range(SUBLANES):
            cols = (slice(j * LANES, (j + 1) * LANES), slice(dh + j * LANES, dh + (j + 1) * LANES))
            acc = [sh_ref[rs, c].astype(F32) for c in cols]
            for k in range(TOP_K):
                halves = _unpack_pair(rows_g[pl.ds((k * tm + r0) * SUBLANES + j, FINISH_ROWS, stride=SUBLANES), :])
                w = gwb[k, rs, :]
                acc = [a + w * h for a, h in zip(acc, halves)]
            for c, a in zip(cols, acc):
                x2 = x1_ref[rs, c] + mod_ref[0, 5:6, c] * a
                o_ref[rs, c] = x2
                ssq = ssq + jnp.sum(x2 * x2, axis=-1, keepdims=True)
        if final_norm:
            inv = lax.rsqrt(ssq / d + EPS)
            o_ref[rs, :] = o_ref[rs, :] * inv * fg_ref[...]


def _finish(dest_flat, shared, x1, gw, yb, mod, final_g, row0, group_of_tile, final_norm, tm=256):
    n_rows, d = x1.shape
    t0 = row0 // tm
    n_tiles = n_rows // tm
    return pl.pallas_call(
        functools.partial(_finish_kernel, final_norm=final_norm),
        grid=(n_tiles,),
        in_specs=[pl.BlockSpec((tm * TOP_K,), lambda i: (t0 + i,), memory_space=pltpu.SMEM),
                  pl.BlockSpec((tm * TOP_K,), lambda i: (t0 + jnp.minimum(i + 1, n_tiles - 1),),
                               memory_space=pltpu.SMEM),
                  pl.BlockSpec((tm, d), lambda i: (t0 + i, 0)),
                  pl.BlockSpec((tm, d), lambda i: (i, 0)),
                  pl.BlockSpec((tm, TOP_K), lambda i: (i, 0)),
                  pl.BlockSpec((1, N_MOD, d), lambda i: (group_of_tile(i, tm), 0, 0)),
                  pl.BlockSpec((1, d), lambda i: (0, 0)),
                  pl.BlockSpec(memory_space=pl.ANY)],
        out_specs=pl.BlockSpec((tm, d), lambda i: (i, 0)),
        out_shape=jax.ShapeDtypeStruct((n_rows, d), F32),
        scratch_shapes=[pltpu.VMEM((2, TOP_K * tm * SUBLANES, LANES), jnp.int32),
                        pltpu.VMEM((TOP_K, tm, LANES), F32), pltpu.SemaphoreType.DMA((2,))],
        compiler_params=_params(1),
        name="finish",
    )(dest_flat, dest_flat, shared, x1, gw, mod, final_g, yb)


def _grid_pos_emb(rows, d):
    quarter = d // 4
    omega = 1.0 / (10000.0 ** (jnp.arange(quarter, dtype=F32) / quarter))
    r = jnp.arange(rows, dtype=F32)[:, None] * omega
    cc = jnp.arange(GRID_W, dtype=F32)[:, None] * omega
    by_row = jnp.concatenate([jnp.sin(r), jnp.cos(r)], axis=-1)
    by_col = jnp.concatenate([jnp.sin(cc), jnp.cos(cc)], axis=-1)
    full = jnp.concatenate([jnp.broadcast_to(by_row[:, None, :], (rows, GRID_W, d // 2)),
                            jnp.broadcast_to(by_col[None, :, :], (rows, GRID_W, d // 2))], axis=-1)
    return full.reshape(rows * GRID_W, d)


def _mixer_path(x2d, pos, n_b, seq_len, h0, mod, group_of_tile, p, mats, filt, n_seq):
    d_rg = p['gn_rg'].shape[1]
    c = p['gn_hy'].shape[1]
    u_rg, hvc = _in_proj(x2d, pos, mod, p['norm1_g'], p['w_in'], p['hy_conv_w'], p['hy_conv_b'],
                         seq_len, group_of_tile)
    u3 = u_rg.reshape(n_b, seq_len, u_rg.shape[1])
    y_rg, st = _rglru(u3, h0, p['rg_conv_w'], p['rg_conv_b'], p['rg_w4'], p['rg_b4'], p['rg_lam'], n_seq)
    tm = min(seq_len // 2, 512)
    tc = 512 if seq_len > 512 else c
    z1 = _hyena_order(0, hvc, 0, hvc, 1, mats, filt, p['hy_bias'], tm, tc, False)
    y_hy = _hyena_order(1, z1, 0, hvc, 2, mats, filt, p['hy_bias'], tm, tc, True)
    x1, h2, idx, gw = _post_mixer(y_rg.reshape(-1, d_rg), y_hy.reshape(-1, c), x2d, pos, mod,
                                  p['gn_rg'], p['gn_hy'], p['w_out'], p['norm2_g'],
                                  p['router_w'], p['router_b'], group_of_tile)
    return x1, h2, idx, gw, st


def kernel(x_prompt, x_sample, state_rglru, c, c_ctx, ada_w, ada_b, norm1_g, norm2_g, w_in, rg_conv_w, rg_conv_b, rg_wa, rg_ba, rg_wx, rg_bx, rg_lam, hy_conv_w, hy_conv_b, hy_w1, hy_b1, hy_freq, hy_w2, hy_b2, hy_w3, hy_decay, hy_bias, gn_rg, gn_hy, w_out, router_w, router_b, exp_w_gate, exp_w_up, exp_w_down, sh_w_gate, sh_w_up, sh_w_down, final_g):
    n_cb, seq_c, d = x_prompt.shape
    n_lb, seq_l, _ = x_sample.shape
    depth = ada_w.shape[0]
    d_rg = gn_rg.shape[1]
    hd = d_rg // RG_HEADS
    t_c = n_cb * seq_c
    t_l = n_lb * seq_l
    assert n_lb + 1 <= SUBLANES
    assert d == 2 * SUBLANES * LANES

    pos = _grid_pos_emb(seq_l // GRID_W, d)
    cvec = jnp.zeros((SUBLANES, d), F32).at[0].set(c_ctx).at[1:1 + n_lb].set(c)
    mats_c = _dft_matrices(seq_c)
    mats_l = _dft_matrices(seq_l)

    def group_ctx(i, tm):
        return 0

    def group_lat(i, tm):
        return 1 + (i * tm) // seq_l

    xc = x_prompt.reshape(t_c, d)
    xs = x_sample.reshape(t_l, d)
    ctx_states = []
    for l in range(depth):
        last = l == depth - 1
        mod = _modulation(cvec, ada_w[l], ada_b[l][None]).reshape(SUBLANES, N_MOD, d)

        w4 = jnp.concatenate([rg_wa[l, 0], rg_wx[l, 0], rg_wa[l, 1], rg_wx[l, 1]], axis=-1).astype(BF16)
        b4 = jnp.concatenate([rg_ba[l, 0].reshape(RG_HEADS, 1, hd), rg_bx[l, 0].reshape(RG_HEADS, 1, hd),
                              rg_ba[l, 1].reshape(RG_HEADS, 1, hd), rg_bx[l, 1].reshape(RG_HEADS, 1, hd)], axis=-1)
        p = {
            'norm1_g': norm1_g[l][None], 'norm2_g': norm2_g[l][None], 'w_in': w_in[l].astype(BF16),
            'rg_conv_w': rg_conv_w[l], 'rg_conv_b': rg_conv_b[l][None], 'rg_w4': w4, 'rg_b4': b4,
            'rg_lam': rg_lam[l], 'hy_conv_w': hy_conv_w[l], 'hy_conv_b': hy_conv_b[l][None],
            'hy_bias': hy_bias[l], 'gn_rg': gn_rg[l][None], 'gn_hy': gn_hy[l][None],
            'w_out': w_out[l].astype(BF16), 'router_w': router_w[l].astype(BF16), 'router_b': router_b[l][None],
        }
        filt_args = (hy_w1[l], hy_b1[l][None], hy_freq[l][None], hy_w2[l], hy_b2[l][None], hy_w3[l],
                     hy_decay[l].reshape(1, -1))
        filt_c = _hyena_filters(seq_c, mats_c[:4], *filt_args)
        filt_l = _hyena_filters(seq_l, mats_l[:4], *filt_args)

        h0_c = jnp.zeros((n_cb, 2, d_rg), F32)
        x1_c, h2_c, idx_c, gw_c, st_c = _mixer_path(xc, None, n_cb, seq_c, h0_c, mod, group_ctx, p,
                                                    mats_c, filt_c, n_seq=min(8, n_cb))
        ctx_states.append(st_c)
        x1_l, h2_l, idx_l, gw_l, _ = _mixer_path(xs, pos if l == 0 else None, n_lb, seq_l,
                                                 state_rglru[:, l], mod, group_lat, p,
                                                 mats_l, filt_l, n_seq=1)

        idx_all = jnp.concatenate([idx_c, idx_l], axis=0)
        dest, blk_e, blk_valid, n_used = _route(idx_all, router_w.shape[-1], EXPERT_ROWS)
        dest_flat = dest.reshape(-1)
        sh = (sh_w_gate[l].astype(BF16), sh_w_up[l].astype(BF16), sh_w_down[l].astype(BF16))
        xb, shared = _dispatch(dest_flat, h2_c, h2_l, blk_e.shape[0] * EXPERT_ROWS, *sh)
        yb = _experts(blk_e, blk_valid, n_used, xb, exp_w_gate[l], exp_w_up[l], exp_w_down[l])
        xc = _finish(dest_flat, shared, x1_c, gw_c, yb, mod, final_g[None], 0, group_ctx, last)
        xs = _finish(dest_flat, shared, x1_l, gw_l, yb, mod, final_g[None], t_c, group_lat, last)

    new_state = jnp.stack(ctx_states, axis=1).astype(x_prompt.dtype)
    return (xc.reshape(n_cb, seq_c, d), xs.reshape(n_lb, seq_l, d), new_state)
```

```python
import functools
import math

import jax
import jax.numpy as jnp
from jax import lax
from jax.experimental import pallas as pl
from jax.experimental.pallas import tpu as pltpu

F32 = jnp.float32
BF16 = jnp.bfloat16

GRID_W = 64
RG_HEADS = 8
RG_CONV_W = 4
RG_C = 8.0
HY_CONV_W = 3
HY_BANDS = 16
TOP_K = 8
ROUTED_SCALE = 2.5
N_MOD = 6
EPS = 1e-6

LANES = 128
SUBLANES = 8
VMEM_LIMIT_BYTES = 56 * 1024 * 1024

EXPERT_ROWS = 288


def _params(n_axes, vmem=VMEM_LIMIT_BYTES):
    return pltpu.CompilerParams(dimension_semantics=("arbitrary",) * n_axes, vmem_limit_bytes=vmem)


def _rms(x, g):
    return x * lax.rsqrt(jnp.mean(x * x, axis=-1, keepdims=True) + EPS) * g


def _sigmoid(x):
    return 0.5 * jnp.tanh(0.5 * x) + 0.5


HI_HALF = -65536


def _pack_pair(lo, hi):
    lo_b = lax.bitcast_convert_type(lo.astype(BF16).astype(F32), jnp.int32)
    hi_b = lax.bitcast_convert_type(hi.astype(BF16).astype(F32), jnp.int32)
    return hi_b | lax.shift_right_logical(lo_b, 16)


def _unpack_pair(p):
    lo = lax.bitcast_convert_type(lax.shift_left(p, 16), F32)
    hi = lax.bitcast_convert_type(p & HI_HALF, F32)
    return lo, hi


def _store_token_tiles(ref, packed):
    m = packed.shape[0]
    for j in range(SUBLANES):
        ref[pl.ds(j, m, stride=SUBLANES), :] = packed[:, j * LANES:(j + 1) * LANES]


def _load_token_tiles(ref, row0, m):
    return jnp.concatenate(
        [ref[pl.ds(row0 * SUBLANES + j, m, stride=SUBLANES), :] for j in range(SUBLANES)], axis=1)


def _mod_kernel(c_ref, w_ref, b_ref, o_ref):
    c = c_ref[...]
    s = (c * jax.nn.sigmoid(c)).astype(BF16)
    o_ref[...] = jnp.dot(s, w_ref[...].astype(BF16), preferred_element_type=F32) + b_ref[...]


def _modulation(cvec, ada_w, ada_b):
    d, n = ada_w.shape
    tn = 1536
    return pl.pallas_call(
        _mod_kernel,
        grid=(n // tn,),
        in_specs=[pl.BlockSpec((SUBLANES, d), lambda j: (0, 0)),
                  pl.BlockSpec((d, tn), lambda j: (0, j)),
                  pl.BlockSpec((1, tn), lambda j: (0, j))],
        out_specs=pl.BlockSpec((SUBLANES, tn), lambda j: (0, j)),
        out_shape=jax.ShapeDtypeStruct((SUBLANES, n), F32),
        compiler_params=_params(1),
        name="modulation",
    )(cvec, ada_w, ada_b)


HALO = 16


def _inproj_kernel(*refs, has_pos, tn, n_rg, seq_len):
    if has_pos:
        (x_ref, xp_ref, xn_ref, pos_ref, pp_ref, pn_ref, mod_ref, g_ref, w_hbm, cw_ref, cb_ref,
         u_ref, hv_ref, w_vmem, h_scr, par_scr, sem) = refs
    else:
        (x_ref, xp_ref, xn_ref, mod_ref, g_ref, w_hbm, cw_ref, cb_ref,
         u_ref, hv_ref, w_vmem, h_scr, par_scr, sem) = refs
    i = pl.program_id(0)
    j = pl.program_id(1)
    tm = x_ref.shape[0]

    @pl.when(jnp.logical_and(i == 0, j == 0))
    def _():
        cp = pltpu.make_async_copy(w_hbm, w_vmem, sem)
        cp.start()
        cp.wait()

    @pl.when(j == 0)
    def _():
        gain = g_ref[...] * (1.0 + mod_ref[0, 1:2, :])

        def normed(x_r, p_r):
            x = x_r[...]
            if has_pos:
                x = x + p_r[...]
            return (_rms(x, gain) + mod_ref[0, 0:1, :]).astype(BF16)

        h_scr[0:HALO, :] = normed(xp_ref, pp_ref if has_pos else None)
        h_scr[HALO:HALO + tm, :] = normed(x_ref, pos_ref if has_pos else None)
        h_scr[HALO + tm:, :] = normed(xn_ref, pn_ref if has_pos else None)

    w = w_vmem[:, pl.ds(pl.multiple_of(j * tn, tn), tn)]

    @pl.when(j < n_rg)
    def _():
        u_ref[...] = jnp.dot(h_scr[HALO:HALO + tm, :], w, preferred_element_type=F32)

    @pl.when(j >= n_rg)
    def _():
        ue = jnp.dot(h_scr[...], w, preferred_element_type=F32)
        t_in_seq = (i * tm + lax.broadcasted_iota(jnp.int32, (tm, tn), 0)) & (seq_len - 1)
        cw = cw_ref[...]
        prev = jnp.where(t_in_seq == 0, 0.0, _shift_rows(ue, -1)[HALO:HALO + tm])
        nxt = jnp.where(t_in_seq == seq_len - 1, 0.0, _shift_rows(ue, 1)[HALO:HALO + tm])
        acc = cb_ref[...] + cw[0:1, :] * prev + cw[1:2, :] * ue[HALO:HALO + tm] + cw[2:3, :] * nxt
        n_seq, _, rows, _ = hv_ref.shape
        for c0 in range(0, tn, LANES):
            par_scr[c0 // LANES] = acc[:, c0:c0 + LANES]
        for s in range(n_seq):
            for par in range(2):
                for c0 in range(0, tn, LANES):
                    picked = par_scr[c0 // LANES, pl.ds(s * 2 * rows + par, rows, stride=2), :]
                    hv_ref[s, par, :, c0:c0 + LANES] = picked.astype(BF16)


def _in_proj(x2d, pos, mod, norm_g, w_bf, conv_w, conv_b, seq_len, group_of_tile, tm=512, tn=1024):
    t, d = x2d.shape
    n = w_bf.shape[1]
    n_hv = conv_w.shape[1]
    n_rg = (n - n_hv) // tn
    has_pos = pos is not None
    hb = tm // HALO
    last_h = t // HALO - 1
    seqs = max(1, tm // seq_len)
    tiles_per_seq = max(1, seq_len // tm)
    rows = min(tm, seq_len) // 2

    def prev_blk(i):
        return jnp.maximum(i * hb - 1, 0)

    def next_blk(i):
        return jnp.minimum((i + 1) * hb, last_h)

    in_specs = [pl.BlockSpec((tm, d), lambda i, j: (i, 0)),
                pl.BlockSpec((HALO, d), lambda i, j: (prev_blk(i), 0)),
                pl.BlockSpec((HALO, d), lambda i, j: (next_blk(i), 0))]
    args = [x2d, x2d, x2d]
    if has_pos:
        pos_tiles = pos.shape[0] // tm
        last_p = pos.shape[0] // HALO - 1
        in_specs += [pl.BlockSpec((tm, d), lambda i, j: (i % pos_tiles, 0)),
                     pl.BlockSpec((HALO, d), lambda i, j: (jnp.maximum((i % pos_tiles) * hb - 1, 0), 0)),
                     pl.BlockSpec((HALO, d), lambda i, j: (jnp.minimum((i % pos_tiles + 1) * hb, last_p), 0))]
        args += [pos, pos, pos]
    in_specs += [pl.BlockSpec((1, N_MOD, d), lambda i, j: (group_of_tile(i, tm), 0, 0)),
                 pl.BlockSpec((1, d), lambda i, j: (0, 0)),
                 pl.BlockSpec(memory_space=pl.ANY),
                 pl.BlockSpec((HY_CONV_W, tn), lambda i, j: (0, jnp.maximum(j - n_rg, 0))),
                 pl.BlockSpec((1, tn), lambda i, j: (0, jnp.maximum(j - n_rg, 0)))]
    args += [mod, norm_g, w_bf, conv_w, conv_b]
    return pl.pallas_call(
        functools.partial(_inproj_kernel, has_pos=has_pos, tn=tn, n_rg=n_rg, seq_len=seq_len),
        grid=(t // tm, n // tn),
        in_specs=in_specs,
        out_specs=[pl.BlockSpec((tm, tn), lambda i, j: (i, jnp.minimum(j, n_rg - 1))),
                   pl.BlockSpec((seqs, 2, rows, tn),
                                lambda i, j: (i // tiles_per_seq, 0, i % tiles_per_seq,
                                              jnp.maximum(j - n_rg, 0)))],
        out_shape=[jax.ShapeDtypeStruct((t, n - n_hv), F32),
                   jax.ShapeDtypeStruct((t // seq_len, 2, seq_len // 2, n_hv), BF16)],
        scratch_shapes=[pltpu.VMEM((d, n), BF16), pltpu.VMEM((tm + 2 * HALO, d), BF16),
                        pltpu.VMEM((tn // LANES, tm, LANES), F32), pltpu.SemaphoreType.DMA(())],
        compiler_params=_params(2),
        name="in_proj",
    )(*args)


def _shift_rows(win, off):
    if off == 0:
        return win
    n = win.shape[0]
    return pltpu.roll(win, (-off) % n, axis=0)


def _scan_chunk(a, b, reverse):
    n = a.shape[0]
    row = lax.broadcasted_iota(jnp.int32, a.shape, 0)
    dist = 1
    while dist < n:
        if reverse:
            a_s = pltpu.roll(a, n - dist, axis=0)
            b_s = pltpu.roll(b, n - dist, axis=0)
            m = row < n - dist
        else:
            a_s = pltpu.roll(a, dist, axis=0)
            b_s = pltpu.roll(b, dist, axis=0)
            m = row >= dist
        b = jnp.where(m, a * b_s + b, b)
        a = jnp.where(m, a * a_s, a)
        dist *= 2
    return a, b


def _rglru_kernel(xr_ref, gr_ref, cw_ref, cb_ref, w4_ref, b4_ref, lam_ref, h0_ref,
                  y_ref, st_ref, af, bf, ab, bb, hf, *, seq_len, n_seq, t1, tc):
    hd = xr_ref.shape[-1]
    nlam = -lam_ref[...]
    sp = jnp.maximum(nlam, 0.0) + jnp.log1p(jnp.exp(-jnp.abs(nlam)))
    cw = cw_ref[...]
    cb = cb_ref[...]
    b4 = b4_ref[0]
    nc1 = seq_len // t1
    ncs = seq_len // tc

    def per_seq(s, carry0):
        def gates(c, carry):
            r0 = pl.multiple_of(c * t1, t1)
            cur = xr_ref[s, pl.ds(r0, t1), :]
            p0 = pl.multiple_of(jnp.maximum(r0 - SUBLANES, 0), SUBLANES)
            n0 = pl.multiple_of(jnp.minimum(r0 + t1, seq_len - SUBLANES), SUBLANES)
            prev = jnp.where(c > 0, xr_ref[s, pl.ds(p0, SUBLANES), :], 0.0)
            nxt = jnp.where(c < nc1 - 1, xr_ref[s, pl.ds(n0, SUBLANES), :], 0.0)
            win = jnp.concatenate([prev, cur, nxt], axis=0)
            xr = cb
            for k in range(RG_CONV_W):
                xr = xr + cw[k:k + 1, :] * _shift_rows(win, k - RG_CONV_W // 2)[SUBLANES:SUBLANES + t1]
            z = jnp.dot(xr.astype(BF16), w4_ref[0], preferred_element_type=F32) + b4
            for d_i, (a_scr, b_scr) in enumerate(((af, bf), (ab, bb))):
                r = _sigmoid(z[:, (2 * d_i) * hd:(2 * d_i + 1) * hd])
                gi = _sigmoid(z[:, (2 * d_i + 1) * hd:(2 * d_i + 2) * hd])
                log_a = (-RG_C) * r * sp[d_i:d_i + 1, :]
                a = jnp.exp(log_a)
                a_scr[pl.ds(r0, t1), :] = a
                b_scr[pl.ds(r0, t1), :] = jnp.sqrt(-jnp.tanh(log_a) * (a * a + 1.0)) * (gi * xr)
            return carry

        lax.fori_loop(0, nc1, gates, 0)
        h0 = h0_ref[s]

        def fwd(c, carry):
            r0 = pl.multiple_of(c * tc, tc)
            a, h = _scan_chunk(af[pl.ds(r0, tc), :], bf[pl.ds(r0, tc), :], False)
            h = a * carry + h
            hf[pl.ds(r0, tc), :] = h
            return h[tc - 1:tc, :]

        s_f = lax.fori_loop(0, ncs, fwd, h0[0:1, :])

        def bwd(cc, carry):
            r0 = pl.multiple_of((ncs - 1 - cc) * tc, tc)
            a, h = _scan_chunk(ab[pl.ds(r0, tc), :], bb[pl.ds(r0, tc), :], True)
            h = a * carry + h
            g = gr_ref[s, pl.ds(r0, tc), :]
            y_ref[s, pl.ds(r0, tc), :] = jax.nn.gelu(g) * (hf[pl.ds(r0, tc), :] + h)
            return h[0:1, :]

        s_b = lax.fori_loop(0, ncs, bwd, h0[1:2, :])
        st_ref[s] = jnp.concatenate([s_f, s_b], axis=0)
        return carry0

    lax.fori_loop(0, n_seq, per_seq, 0)


def _rglru(u3, h0, conv_w, conv_b, w4, b4, lam, n_seq):
    b, seq_len, _ = u3.shape
    hd = w4.shape[1]
    d_rg = hd * RG_HEADS
    t1 = min(seq_len, 256)
    tc = 64
    kern = functools.partial(_rglru_kernel, seq_len=seq_len, n_seq=n_seq, t1=t1, tc=tc)
    return pl.pallas_call(
        kern,
        grid=(b // n_seq, RG_HEADS),
        in_specs=[pl.BlockSpec((n_seq, seq_len, hd), lambda i, h: (i, 0, h)),
                  pl.BlockSpec((n_seq, seq_len, hd), lambda i, h: (i, 0, RG_HEADS + h)),
                  pl.BlockSpec((RG_CONV_W, hd), lambda i, h: (0, h)),
                  pl.BlockSpec((1, hd), lambda i, h: (0, h)),
                  pl.BlockSpec((1, hd, 4 * hd), lambda i, h: (h, 0, 0)),
                  pl.BlockSpec((1, 1, 4 * hd), lambda i, h: (h, 0, 0)),
                  pl.BlockSpec((2, hd), lambda i, h: (0, h)),
                  pl.BlockSpec((n_seq, 2, hd), lambda i, h: (i, 0, h))],
        out_specs=[pl.BlockSpec((n_seq, seq_len, hd), lambda i, h: (i, 0, h)),
                   pl.BlockSpec((n_seq, 2, hd), lambda i, h: (i, 0, h))],
        out_shape=[jax.ShapeDtypeStruct((b, seq_len, d_rg), F32),
                   jax.ShapeDtypeStruct((b, 2, d_rg), F32)],
        scratch_shapes=[pltpu.VMEM((seq_len, hd), F32)] * 5,
        compiler_params=_params(2),
        name="rglru",
    )(u3, u3, conv_w, conv_b, w4, b4, lam, h0)


def _filt_time_kernel(w1_ref, b1_ref, fr_ref, w2_ref, b2_ref, w3_ref, dec_ref,
                      g_ref, d_ref, st_ref, *, seq_len, tl):
    i = pl.program_id(0)
    hi = lax.Precision.HIGHEST
    half = seq_len // 2
    r = i * tl + lax.broadcasted_iota(jnp.int32, (tl, LANES), 0)
    posi = 2 * (r & (half - 1)) + jnp.where(r >= half, 1, 0)
    pos = posi.astype(F32)
    lane = lax.broadcasted_iota(jnp.int32, (tl, LANES), 1)
    band = jnp.where(lane <= HY_BANDS, lane, lane - HY_BANDS).astype(F32)
    ang = (2.0 * math.pi) * pos / seq_len * band
    t = pos / seq_len
    feats = jnp.where(lane == 0, t,
                      jnp.where(lane <= HY_BANDS, jnp.cos(ang),
                                jnp.where(lane <= 2 * HY_BANDS, -jnp.sin(ang), 0.0)))
    fr = fr_ref[...]
    hid = jnp.sin(fr * (jnp.dot(feats, w1_ref[...], precision=hi, preferred_element_type=F32) + b1_ref[...]))
    hid = jnp.sin(fr * (jnp.dot(hid, w2_ref[...], precision=hi, preferred_element_type=F32) + b2_ref[...]))
    k = jnp.dot(hid.astype(BF16), w3_ref[...].astype(BF16), preferred_element_type=F32)
    k = k * jnp.exp(-t[:, 0:1] * jnp.abs(dec_ref[...]))
    c = k.shape[1] // 4
    first = posi[:, 0:1] == 0
    p4 = posi[:, 0:1] & 3
    sign = jnp.where(p4 == 0, 1.0, jnp.where(p4 == 2, -1.0, 0.0))

    @pl.when(i == 0)
    def _():
        st_ref[...] = jnp.zeros_like(st_ref)

    for o in range(2):
        kf = k[:, (2 * o) * c:(2 * o + 1) * c]
        kb = jnp.where(first, 0.0, k[:, (2 * o + 1) * c:(2 * o + 2) * c])
        g = kf + kb
        g_ref[:, o * c:(o + 1) * c] = g.astype(BF16)
        d_ref[:, o * c:(o + 1) * c] = (kf - kb).astype(BF16)
        st_ref[0:1, o * c:(o + 1) * c] += jnp.sum(kf * kf + kb * kb, axis=0, keepdims=True)
        st_ref[1:2, o * c:(o + 1) * c] += jnp.sum(sign * g, axis=0, keepdims=True)


def _half_dfts(tabs, cos_even, sin_even, cos_odd, sin_odd):
    ce, se, co, so = tabs
    ae = jnp.dot(ce[...], cos_even, preferred_element_type=F32)
    ao = jnp.dot(co[...], cos_odd, preferred_element_type=F32)
    be = jnp.dot(se[...], sin_even, preferred_element_type=F32)
    bo = jnp.dot(so[...], sin_odd, preferred_element_type=F32)
    return ae, ao, be, bo


def _filt_dft_kernel(ce_ref, se_ref, co_ref, so_ref, g_ref, d_ref, st_ref, kf_ref, ks_ref, qf_ref, qs_ref):
    m = pl.program_id(1)
    half = g_ref.shape[0] // 2
    ae, ao, be, bo = _half_dfts((ce_ref, se_ref, co_ref, so_ref),
                                g_ref[0:half, :], d_ref[0:half, :], g_ref[half:, :], d_ref[half:, :])
    scale = lax.rsqrt(st_ref[0:1, :] + EPS)
    is0 = (m * ae.shape[0] + lax.broadcasted_iota(jnp.int32, ae.shape, 0)) == 0
    kf_ref[...] = (ae + ao) * scale
    ks_ref[...] = (ae - ao) * scale
    qf_ref[...] = jnp.where(is0, st_ref[1:2, :], be + bo) * scale
    qs_ref[...] = jnp.where(is0, bo, bo - be) * scale


def _hyena_filters(seq_len, tabs, w1, b1, freq, w2, b2, w3, decay):
    n_hid = w1.shape[1]
    n_out = w3.shape[1]
    c2 = n_out // 2
    tl = min(seq_len, 512)
    w1p = jnp.zeros((LANES, n_hid), F32).at[:w1.shape[0]].set(w1)
    g, d, stats = pl.pallas_call(
        functools.partial(_filt_time_kernel, seq_len=seq_len, tl=tl),
        grid=(seq_len // tl,),
        in_specs=[pl.BlockSpec((LANES, n_hid), lambda i: (0, 0)),
                  pl.BlockSpec((1, n_hid), lambda i: (0, 0)),
                  pl.BlockSpec((1, n_hid), lambda i: (0, 0)),
                  pl.BlockSpec((n_hid, n_hid), lambda i: (0, 0)),
                  pl.BlockSpec((1, n_hid), lambda i: (0, 0)),
                  pl.BlockSpec((n_hid, n_out), lambda i: (0, 0)),
                  pl.BlockSpec((1, n_out), lambda i: (0, 0))],
        out_specs=[pl.BlockSpec((tl, c2), lambda i: (i, 0)),
                   pl.BlockSpec((tl, c2), lambda i: (i, 0)),
                   pl.BlockSpec((SUBLANES, c2), lambda i: (0, 0))],
        out_shape=[jax.ShapeDtypeStruct((seq_len, c2), BF16),
                   jax.ShapeDtypeStruct((seq_len, c2), BF16),
                   jax.ShapeDtypeStruct((SUBLANES, c2), F32)],
        compiler_params=_params(1),
        name="hyena_filter_taps",
    )(w1p, b1, freq, w2, b2, w3, decay)
    half = seq_len // 2
    tm = min(half, 512)
    tn = 512
    tab = pl.BlockSpec((tm, half), lambda n, m: (m, 0))
    return pl.pallas_call(
        _filt_dft_kernel,
        grid=(c2 // tn, half // tm),
        in_specs=[tab, tab, tab, tab,
                  pl.BlockSpec((seq_len, tn), lambda n, m: (0, n)),
                  pl.BlockSpec((seq_len, tn), lambda n, m: (0, n)),
                  pl.BlockSpec((SUBLANES, tn), lambda n, m: (0, n))],
        out_specs=[pl.BlockSpec((tm, tn), lambda n, m: (m, n))] * 4,
        out_shape=[jax.ShapeDtypeStruct((half, c2), F32)] * 4,
        compiler_params=_params(2),
        name="hyena_filter_dft",
    )(tabs[0], tabs[1], tabs[2], tabs[3], g, d, stats)


def _dft_matrices(seq_len):
    assert seq_len & (seq_len - 1) == 0
    half = seq_len // 2
    tm = min(half, 256)
    out = jax.ShapeDtypeStruct((half, half), BF16)
    return pl.pallas_call(
        functools.partial(_dft_table_kernel, seq_len=seq_len),
        grid=(half // tm,),
        in_specs=[],
        out_specs=[pl.BlockSpec((tm, half), lambda i: (i, 0))] * 7,
        out_shape=[out] * 7,
        scratch_shapes=[pltpu.VMEM((tm, half), F32)] * 4,
        compiler_params=_params(1),
        name="dft_tables",
    )()


def _dft_table_kernel(ce_ref, se_ref, co_ref, so_ref, set_ref, cot_ref, sot_ref, c0e, s0e, c0o, s0o, *, seq_len):
    i = pl.program_id(0)
    tm, half = ce_ref.shape
    wrap = 2 * seq_len - 1
    unit = math.pi / seq_len
    row = lax.broadcasted_iota(jnp.int32, (tm, half), 0)
    col = lax.broadcasted_iota(jnp.int32, (tm, half), 1)
    col1 = col[0:1, :]

    def trig(n):
        ang = (n & wrap).astype(F32) * unit
        return jnp.cos(ang), jnp.sin(ang)

    @pl.when(i == 0)
    def _():
        c0e[...], s0e[...] = trig(2 * row * col)
        c0o[...], s0o[...] = trig(row * (2 * col + 1))

    def rotate(c0, s0, n):
        cn, sn = trig(n)
        return c0[...] * cn - s0[...] * sn, s0[...] * cn + c0[...] * sn

    r0 = i * tm
    ce, se = rotate(c0e, s0e, 2 * r0 * col1)
    co, so = rotate(c0o, s0o, r0 * (2 * col1 + 1))
    cot, sot = rotate(c0e, s0e, (2 * r0 + 1) * col1)
    alt_col = jnp.where((col & 1) == 0, 1.0, -1.0)
    alt_row = jnp.where(((row + r0) & 1) == 0, 1.0, -1.0)
    first_row = row + r0 == 0
    ce_ref[...] = ce.astype(BF16)
    se_ref[...] = jnp.where(first_row, alt_col, se).astype(BF16)
    co_ref[...] = co.astype(BF16)
    so_ref[...] = jnp.where(first_row, alt_col, so).astype(BF16)
    set_ref[...] = jnp.where(col == 0, alt_row, se).astype(BF16)
    cot_ref[...] = cot.astype(BF16)
    sot_ref[...] = jnp.where(col == 0, alt_row, sot).astype(BF16)


def _hy_fwd_kernel(ce_ref, se_ref, co_ref, so_ref, u_ref, kf_ref, ks_ref, qf_ref, qs_ref,
                   ee_ref, eo_ref, de_ref, do_ref, *, seq_len):
    m = pl.program_id(2)
    kf, ks, qf, qs = kf_ref[...], ks_ref[...], qf_ref[...], qs_ref[...]
    is0 = (m * kf.shape[0] + lax.broadcasted_iota(jnp.int32, kf.shape, 0)) == 0
    inv_n = 0.5 / seq_len
    w = jnp.where(is0, inv_n, 2.0 * inv_n)
    for s in range(u_ref.shape[0]):
        ue, uo = u_ref[s, 0], u_ref[s, 1]
        ae, ao, be, bo = _half_dfts((ce_ref, se_ref, co_ref, so_ref), ue, ue, uo, uo)
        a_f, a_s = ae + ao, ae - ao
        b_f, b_s = be + bo, bo - be
        pre_f = jnp.where(is0, a_f * kf, a_f * kf - b_f * qf)
        pre_s = jnp.where(is0, a_s * ks, a_s * ks - b_s * qs)
        pm_f = a_f * qf + b_f * kf
        pm_s = a_s * qs + b_s * ks
        mid_re = be * qf - bo * qs
        mid_mim = be * qs + bo * qf
        ee_ref[s] = ((pre_f + pre_s) * w).astype(BF16)
        eo_ref[s] = ((pre_f - pre_s) * w).astype(BF16)
        de_ref[s] = (jnp.where(is0, mid_re, pm_f - pm_s) * (2.0 * inv_n)).astype(BF16)
        do_ref[s] = (jnp.where(is0, mid_mim, pm_f + pm_s) * (2.0 * inv_n)).astype(BF16)


def _hy_inv_kernel(*refs, natural_out):
    if natural_out:
        ce_ref, set_ref, cot_ref, sot_ref, ee_ref, eo_ref, de_ref, do_ref, u_ref, x_ref, bias_ref, z_ref, scr = refs
    else:
        ce_ref, set_ref, cot_ref, sot_ref, ee_ref, eo_ref, de_ref, do_ref, u_ref, x_ref, bias_ref, z_ref = refs
    bias = bias_ref[...]
    for s in range(u_ref.shape[0]):
        y_e = jnp.dot(ce_ref[...], ee_ref[s], preferred_element_type=F32)
        y_e = y_e + jnp.dot(set_ref[...], de_ref[s], preferred_element_type=F32)
        y_o = jnp.dot(cot_ref[...], eo_ref[s], preferred_element_type=F32)
        y_o = y_o + jnp.dot(sot_ref[...], do_ref[s], preferred_element_type=F32)
        z_e = (y_e + u_ref[s, 0].astype(F32) * bias) * x_ref[s, 0].astype(F32)
        z_o = (y_o + u_ref[s, 1].astype(F32) * bias) * x_ref[s, 1].astype(F32)
        if natural_out:
            th = z_e.shape[0]
            for j in range(z_e.shape[1] // LANES):
                scr[j, pl.ds(0, th, stride=2), :] = z_e[:, j * LANES:(j + 1) * LANES]
                scr[j, pl.ds(1, th, stride=2), :] = z_o[:, j * LANES:(j + 1) * LANES]
            for j in range(z_e.shape[1] // LANES):
                z_ref[s, :, j * LANES:(j + 1) * LANES] = scr[j].astype(BF16)
        else:
            z_ref[s, 0] = z_e.astype(BF16)
            z_ref[s, 1] = z_o.astype(BF16)


def _hyena_order(order, u_arr, u_cb, hvc, gate_cb, tabs, filt, bias, tm, tc, natural_out):
    ce, se, co, so, se_t, co_t, so_t = tabs
    kf, ks, qf, qs = filt
    b, _, half, _ = hvc.shape
    seq_len = 2 * half
    c = kf.shape[1] // 2
    nct = c // tc
    sb = max(1, min(b, 1024 // seq_len))
    grid = (b // sb, nct, half // tm)
    tab = pl.BlockSpec((tm, half), lambda i, n, m: (m, 0))
    spec = pl.BlockSpec((tm, tc), lambda i, n, m: (m, order * nct + n))
    freq = pl.BlockSpec((sb, tm, tc), lambda i, n, m: (i, m, n))
    ee, eo, de, do = pl.pallas_call(
        functools.partial(_hy_fwd_kernel, seq_len=seq_len),
        grid=grid,
        in_specs=[tab, tab, tab, tab,
                  pl.BlockSpec((sb, 2, half, tc), lambda i, n, m: (i, 0, 0, u_cb * nct + n)),
                  spec, spec, spec, spec],
        out_specs=[freq] * 4,
        out_shape=[jax.ShapeDtypeStruct((b, half, c), BF16)] * 4,
        compiler_params=_params(3),
        name="hyena_fwd_dft",
    )(ce, se, co, so, u_arr, kf, ks, qf, qs)
    whole = pl.BlockSpec((sb, half, tc), lambda i, n, m: (i, 0, n))
    if natural_out:
        out_spec = pl.BlockSpec((sb, 2 * tm, tc), lambda i, n, m: (i, m, n))
        out_shape = jax.ShapeDtypeStruct((b, seq_len, c), BF16)
        scratch = [pltpu.VMEM((tc // LANES, 2 * tm, LANES), F32)]
    else:
        out_spec = pl.BlockSpec((sb, 2, tm, tc), lambda i, n, m: (i, 0, m, n))
        out_shape = jax.ShapeDtypeStruct((b, 2, half, c), BF16)
        scratch = []
    return pl.pallas_call(
        functools.partial(_hy_inv_kernel, natural_out=natural_out),
        grid=grid,
        in_specs=[tab, tab, tab, tab, whole, whole, whole, whole,
                  pl.BlockSpec((sb, 2, tm, tc), lambda i, n, m: (i, 0, m, u_cb * nct + n)),
                  pl.BlockSpec((sb, 2, tm, tc), lambda i, n, m: (i, 0, m, gate_cb * nct + n)),
                  pl.BlockSpec((1, tc), lambda i, n, m: (0, n))],
        out_specs=out_spec,
        out_shape=out_shape,
        scratch_shapes=scratch,
        compiler_params=_params(3),
        name="hyena_inv_dft",
    )(ce, se_t, co_t, so_t, ee, eo, de, do, u_arr, hvc, bias[order][None])


def _post_mixer_kernel(*refs, has_pos):
    if has_pos:
        (yrg_ref, yhy_ref, x_ref, pos_ref, mod_ref, gnr_ref, gnh_ref, wo_ref, n2_ref,
         rw_ref, rb_ref, x1_ref, h2_ref, idx_ref, gw_ref) = refs
    else:
        (yrg_ref, yhy_ref, x_ref, mod_ref, gnr_ref, gnh_ref, wo_ref, n2_ref,
         rw_ref, rb_ref, x1_ref, h2_ref, idx_ref, gw_ref) = refs
    d_rg = yrg_ref.shape[1]
    na = _rms(yrg_ref[...], gnr_ref[...]).astype(BF16)
    nb = _rms(yhy_ref[...].astype(F32), gnh_ref[...]).astype(BF16)
    y = jnp.dot(na, wo_ref[0:d_rg, :], preferred_element_type=F32)
    y = y + jnp.dot(nb, wo_ref[d_rg:, :], preferred_element_type=F32)
    x = x_ref[...]
    if has_pos:
        x = x + pos_ref[...]
    x1 = x + mod_ref[0, 2:3, :] * y
    x1_ref[...] = x1
    h2f = _rms(x1, n2_ref[...] * (1.0 + mod_ref[0, 4:5, :])) + mod_ref[0, 3:4, :]
    half = h2f.shape[1] // 2
    _store_token_tiles(h2_ref, _pack_pair(h2f[:, :half], h2f[:, half:]))
    h2 = h2f.astype(BF16)
    scores = jax.nn.sigmoid(jnp.dot(h2, rw_ref[...], preferred_element_type=F32))
    sel = scores + rb_ref[...]
    n_exp = scores.shape[1]
    lane = lax.broadcasted_iota(jnp.int32, scores.shape, 1).astype(F32)
    col = lax.broadcasted_iota(jnp.int32, idx_ref.shape, 1)
    idx_acc = jnp.zeros(idx_ref.shape, F32)
    gw_acc = jnp.zeros(gw_ref.shape, F32)
    for k in range(TOP_K):
        mx = jnp.max(sel, axis=1, keepdims=True)
        pick = jnp.min(jnp.where(sel == mx, lane, float(n_exp)), axis=1, keepdims=True)
        hit = lane == pick
        val = jnp.sum(jnp.where(hit, scores, 0.0), axis=1, keepdims=True)
        sel = jnp.where(hit, -jnp.inf, sel)
        idx_acc = jnp.where(col == k, pick, idx_acc)
        gw_acc = jnp.where(col == k, val, gw_acc)
    idx_ref[...] = idx_acc.astype(jnp.int32)
    gw_ref[...] = gw_acc / jnp.sum(gw_acc, axis=1, keepdims=True) * ROUTED_SCALE


def _post_mixer(y_rg, y_hy, x2d, pos, mod, gn_rg, gn_hy, w_out_bf, norm2_g, router_w_bf, router_b,
                group_of_tile, tm=512):
    t, d = x2d.shape
    d_rg = y_rg.shape[1]
    d_hy = y_hy.shape[1]
    n_exp = router_w_bf.shape[1]
    has_pos = pos is not None
    in_specs = [pl.BlockSpec((tm, d_rg), lambda i: (i, 0)),
                pl.BlockSpec((tm, d_hy), lambda i: (i, 0)),
                pl.BlockSpec((tm, d), lambda i: (i, 0))]
    args = [y_rg, y_hy, x2d]
    if has_pos:
        pos_tiles = pos.shape[0] // tm
        in_specs.append(pl.BlockSpec((tm, d), lambda i: (i % pos_tiles, 0)))
        args.append(pos)
    in_specs += [pl.BlockSpec((1, N_MOD, d), lambda i: (group_of_tile(i, tm), 0, 0)),
                 pl.BlockSpec((1, d_rg), lambda i: (0, 0)),
                 pl.BlockSpec((1, d_hy), lambda i: (0, 0)),
                 pl.BlockSpec((d_rg + d_hy, d), lambda i: (0, 0), pipeline_mode=pl.Buffered(1)),
                 pl.BlockSpec((1, d), lambda i: (0, 0)),
                 pl.BlockSpec((d, n_exp), lambda i: (0, 0), pipeline_mode=pl.Buffered(1)),
                 pl.BlockSpec((1, n_exp), lambda i: (0, 0))]
    args += [mod, gn_rg, gn_hy, w_out_bf, norm2_g, router_w_bf, router_b]
    return pl.pallas_call(
        functools.partial(_post_mixer_kernel, has_pos=has_pos),
        grid=(t // tm,),
        in_specs=in_specs,
        out_specs=[pl.BlockSpec((tm, d), lambda i: (i, 0)),
                   pl.BlockSpec((tm * SUBLANES, LANES), lambda i: (i, 0)),
                   pl.BlockSpec((tm, TOP_K), lambda i: (i, 0)),
                   pl.BlockSpec((tm, TOP_K), lambda i: (i, 0))],
        out_shape=[jax.ShapeDtypeStruct((t, d), F32),
                   jax.ShapeDtypeStruct((t * SUBLANES, LANES), jnp.int32),
                   jax.ShapeDtypeStruct((t, TOP_K), jnp.int32),
                   jax.ShapeDtypeStruct((t, TOP_K), F32)],
        compiler_params=_params(1),
        name="post_mixer",
    )(*args)


def _route_kernel(idx_ref, dest_ref, be_ref, bv_ref, nu_ref, tri, cnt, base, pst, *, blk, n_blk):
    p = pl.program_id(0)
    i = pl.program_id(1)
    tm, top_k = idx_ref.shape
    ne = cnt.shape[1]

    def div_blk(n):
        return jnp.floor((n + 0.5) / blk)

    @pl.when(jnp.logical_and(p == 0, i == 0))
    def _():
        r = lax.broadcasted_iota(jnp.int32, (tm, tm), 0)
        c = lax.broadcasted_iota(jnp.int32, (tm, tm), 1)
        tri[...] = jnp.where(r > c, 1.0, 0.0).astype(BF16)
        cnt[...] = jnp.zeros_like(cnt)

    idx = idx_ref[...]
    lane = lax.broadcasted_iota(jnp.int32, (tm, ne), 1)
    hits = [lane == idx[:, k:k + 1] for k in range(top_k)]
    occ = jnp.zeros((tm, ne), F32)
    for h in hits:
        occ = occ + jnp.where(h, 1.0, 0.0)
    col_sum = jnp.sum(occ, axis=0, keepdims=True)

    @pl.when(p == 0)
    def _():
        cnt[0:1, :] += col_sum

    @pl.when(jnp.logical_and(p == 1, i == 0))
    def _():
        counts = cnt[...]
        padded = div_blk(counts + (blk - 1.0)) * blk
        r = lax.broadcasted_iota(jnp.int32, (ne, ne), 0)
        c = lax.broadcasted_iota(jnp.int32, (ne, ne), 1)
        upper = jnp.where(r <= c, 1.0, 0.0)
        pend = jnp.dot(padded, upper, precision=lax.Precision.HIGHEST, preferred_element_type=F32)
        pstart = pend - padded
        pst[...] = pstart
        base[...] = jnp.zeros_like(base)
        b0 = (lax.broadcasted_iota(jnp.int32, (n_blk, ne), 0) * blk).astype(F32)
        be = jnp.sum(jnp.where(pend[0:1, :] <= b0, 1.0, 0.0), axis=1, keepdims=True)
        be = jnp.minimum(be, ne - 1.0)
        own = lax.broadcasted_iota(jnp.int32, (n_blk, ne), 1).astype(F32) == be
        pst_b = jnp.sum(jnp.where(own, pstart[0:1, :], 0.0), axis=1, keepdims=True)
        cnt_b = jnp.sum(jnp.where(own, counts[0:1, :], 0.0), axis=1, keepdims=True)
        valid = jnp.clip(cnt_b - (b0[:, 0:1] - pst_b), 0.0, float(blk))
        be_ref[...] = be.astype(jnp.int32)
        bv_ref[...] = valid.astype(jnp.int32)
        total = jnp.max(pend[0:1, :], axis=1, keepdims=True)
        nu_ref[...] = jnp.broadcast_to(div_blk(total).astype(jnp.int32), nu_ref.shape)

    @pl.when(p == 1)
    def _():
        cum = jnp.dot(tri[...], occ.astype(BF16), preferred_element_type=F32) + base[0:1, :] + pst[0:1, :]
        col = lax.broadcasted_iota(jnp.int32, (tm, top_k), 1)
        acc = jnp.zeros((tm, top_k), F32)
        for k in range(top_k):
            v = jnp.sum(jnp.where(hits[k], cum, 0.0), axis=1, keepdims=True)
            acc = jnp.where(col == k, v, acc)
        dest_ref[...] = acc.astype(jnp.int32)
        base[0:1, :] += col_sum


def _route(idx, n_exp, blk, tm=512):
    n_tok, top_k = idx.shape
    n_blk = -(-n_tok * top_k // blk) + n_exp
    dest, be, bv, nu = pl.pallas_call(
        functools.partial(_route_kernel, blk=blk, n_blk=n_blk),
        grid=(2, n_tok // tm),
        in_specs=[pl.BlockSpec((tm, top_k), lambda p, i: (i, 0))],
        out_specs=[pl.BlockSpec((tm, top_k), lambda p, i: (i * p, 0)),
                   pl.BlockSpec((n_blk, 1), lambda p, i: (0, 0)),
                   pl.BlockSpec((n_blk, 1), lambda p, i: (0, 0)),
                   pl.BlockSpec((SUBLANES, LANES), lambda p, i: (0, 0))],
        out_shape=[jax.ShapeDtypeStruct((n_tok, top_k), jnp.int32),
                   jax.ShapeDtypeStruct((n_blk, 1), jnp.int32),
                   jax.ShapeDtypeStruct((n_blk, 1), jnp.int32),
                   jax.ShapeDtypeStruct((SUBLANES, LANES), jnp.int32)],
        scratch_shapes=[pltpu.VMEM((tm, tm), BF16), pltpu.VMEM((SUBLANES, n_exp), F32),
                        pltpu.VMEM((SUBLANES, n_exp), F32), pltpu.VMEM((SUBLANES, n_exp), F32)],
        compiler_params=_params(2),
        name="route",
    )(idx)
    return dest, be.reshape(n_blk), bv.reshape(n_blk), nu[0, 0:1]


def _dispatch_kernel(dest_ref, xa_ref, xl_ref, sg_ref, su_ref, sd_ref, out_ref, sh_ref, sem, *, n_a):
    tm = xa_ref.shape[0] // SUBLANES
    dh = SUBLANES * LANES

    def tile(ref, r):
        return ref.at[pl.ds(pl.multiple_of(r * SUBLANES, SUBLANES), SUBLANES), :]

    def run(src):
        def issue(t, c):
            for k in range(TOP_K):
                pltpu.make_async_copy(tile(src, t), tile(out_ref, dest_ref[t * TOP_K + k]), sem).start(priority=k % 2)
            return c

        lax.fori_loop(0, tm, issue, 0)

        lo, hi = _unpack_pair(_load_token_tiles(src, 0, tm))
        lo = lo.astype(BF16)
        hi = hi.astype(BF16)
        g = jnp.dot(lo, sg_ref[0:dh, :], preferred_element_type=F32)
        g = g + jnp.dot(hi, sg_ref[dh:, :], preferred_element_type=F32)
        u = jnp.dot(lo, su_ref[0:dh, :], preferred_element_type=F32)
        u = u + jnp.dot(hi, su_ref[dh:, :], preferred_element_type=F32)
        hmid = (g * _sigmoid(g) * u).astype(BF16)
        sh_ref[...] = jnp.dot(hmid, sd_ref[...], preferred_element_type=F32).astype(BF16)

        def drain(t, c):
            for k in range(TOP_K):
                pltpu.make_async_copy(tile(src, 0), tile(out_ref, 0), sem).wait()
            return c

        lax.fori_loop(0, tm, drain, 0, unroll=8)

    @pl.when(pl.program_id(0) < n_a)
    def _():
        run(xa_ref)

    @pl.when(pl.program_id(0) >= n_a)
    def _():
        run(xl_ref)


def _dispatch(dest_flat, h2p_a, h2p_l, n_rows, sg_bf, su_bf, sd_bf, tm=256):
    n_a = h2p_a.shape[0] // (tm * SUBLANES)
    n_l = h2p_l.shape[0] // (tm * SUBLANES)
    d, ds_ = sg_bf.shape
    return pl.pallas_call(
        functools.partial(_dispatch_kernel, n_a=n_a),
        grid=(n_a + n_l,),
        in_specs=[pl.BlockSpec((tm * TOP_K,), lambda i: (i,), memory_space=pltpu.SMEM),
                  pl.BlockSpec((tm * SUBLANES, LANES), lambda i: (jnp.minimum(i, n_a - 1), 0)),
                  pl.BlockSpec((tm * SUBLANES, LANES), lambda i: (jnp.maximum(i - n_a, 0), 0)),
                  pl.BlockSpec((d, ds_), lambda i: (0, 0)),
                  pl.BlockSpec((d, ds_), lambda i: (0, 0)),
                  pl.BlockSpec((ds_, d), lambda i: (0, 0))],
        out_specs=[pl.BlockSpec(memory_space=pl.ANY),
                   pl.BlockSpec((tm, d), lambda i: (i, 0))],
        out_shape=[jax.ShapeDtypeStruct((n_rows * SUBLANES, LANES), jnp.int32),
                   jax.ShapeDtypeStruct(((n_a + n_l) * tm, d), BF16)],
        scratch_shapes=[pltpu.SemaphoreType.DMA(())],
        compiler_params=_params(1),
        name="dispatch",
    )(dest_flat, h2p_a, h2p_l, sg_bf, su_bf, sd_bf)


WEIGHT_DMA_PRIORITY = 1


def _expert_kernel(be_ref, bv_ref, nu_ref, x_ref, wg_hbm, wu_hbm, wd_hbm, o_ref,
                   wg_f, wu_f, wd_f, wg_b, wu_b, wd_b, sem, grp, *, n_blk):
    i = pl.program_id(0)
    n_used = nu_ref[0]
    e = be_ref[i]
    active = i < n_used
    changed = jnp.logical_or(i == 0, e != be_ref[jnp.maximum(i - 1, 0)])

    def expert_at(j):
        return be_ref[jnp.minimum(j, n_blk - 1)]

    def next_group(j):
        ej = expert_at(j)
        return lax.while_loop(lambda q: jnp.logical_and(q < n_used, expert_at(q) == ej), lambda q: q + 1, j + 1)

    def copies(ex, slot):
        return (pltpu.make_async_copy(wg_hbm.at[ex], wg_f.at[slot], sem.at[slot, 0]),
                pltpu.make_async_copy(wu_hbm.at[ex], wu_f.at[slot], sem.at[slot, 1]),
                pltpu.make_async_copy(wd_hbm.at[ex], wd_f.at[slot], sem.at[slot, 2]))

    @pl.when(jnp.logical_and(active, i == 0))
    def _():
        grp[0] = 0
        for cp in copies(e, 0):
            cp.start(priority=WEIGHT_DMA_PRIORITY)
        n1 = next_group(i)

        @pl.when(n1 < n_used)
        def _():
            for cp in copies(expert_at(n1), 1):
                cp.start(priority=WEIGHT_DMA_PRIORITY)

    @pl.when(jnp.logical_and(active, changed))
    def _():
        slot = grp[0] % 2
        for cp in copies(e, slot):
            cp.wait()
        wg_b[...] = wg_f[slot].astype(BF16)
        wu_b[...] = wu_f[slot].astype(BF16)
        wd_b[...] = wd_f[slot].astype(BF16)
        n2 = next_group(next_group(i))

        @pl.when(n2 < n_used)
        def _():
            for cp in copies(expert_at(n2), slot):
                cp.start(priority=WEIGHT_DMA_PRIORITY)

        grp[0] = grp[0] + 1

    @pl.when(active)
    def _():
        tm = x_ref.shape[0] // SUBLANES
        dh = SUBLANES * LANES
        live = lax.broadcasted_iota(jnp.int32, (tm, dh), 0) < bv_ref[i]
        lo, hi = _unpack_pair(_load_token_tiles(x_ref, 0, tm))
        lo = jnp.where(live, lo, 0.0).astype(BF16)
        hi = jnp.where(live, hi, 0.0).astype(BF16)
        g = jnp.dot(lo, wg_b[0:dh, :], preferred_element_type=F32)
        g = g + jnp.dot(hi, wg_b[dh:, :], preferred_element_type=F32)
        u = jnp.dot(lo, wu_b[0:dh, :], preferred_element_type=F32)
        u = u + jnp.dot(hi, wu_b[dh:, :], preferred_element_type=F32)
        hmid = (g * _sigmoid(g) * u).astype(BF16)
        y = jnp.dot(hmid, wd_b[...], preferred_element_type=F32)
        _store_token_tiles(o_ref, _pack_pair(y[:, :dh], y[:, dh:]))


def _experts(blk_e, blk_valid, n_used, xb, wg, wu, wd):
    n_exp, d, de = wg.shape
    tm = EXPERT_ROWS
    n_blk = xb.shape[0] // (tm * SUBLANES)
    grid_spec = pltpu.PrefetchScalarGridSpec(
        num_scalar_prefetch=3,
        grid=(n_blk,),
        in_specs=[pl.BlockSpec((tm * SUBLANES, LANES), lambda i, be, bv, nu: (jnp.minimum(i, nu[0] - 1), 0)),
                  pl.BlockSpec(memory_space=pl.ANY),
                  pl.BlockSpec(memory_space=pl.ANY),
                  pl.BlockSpec(memory_space=pl.ANY)],
        out_specs=pl.BlockSpec((tm * SUBLANES, LANES), lambda i, be, bv, nu: (jnp.minimum(i, nu[0] - 1), 0)),
        scratch_shapes=[pltpu.VMEM((2, d, de), F32), pltpu.VMEM((2, d, de), F32), pltpu.VMEM((2, de, d), F32),
                        pltpu.VMEM((d, de), BF16), pltpu.VMEM((d, de), BF16), pltpu.VMEM((de, d), BF16),
                        pltpu.SemaphoreType.DMA((2, 3)), pltpu.SMEM((1,), jnp.int32)],
    )
    return pl.pallas_call(
        functools.partial(_expert_kernel, n_blk=n_blk),
        grid_spec=grid_spec,
        out_shape=jax.ShapeDtypeStruct(xb.shape, jnp.int32),
        compiler_params=_params(1),
        name="routed_experts",
    )(blk_e, blk_valid, n_used, xb, wg, wu, wd)


FINISH_ROWS = 64


def _finish_kernel(dcur_ref, dnxt_ref, sh_ref, x1_ref, gw_ref, mod_ref, fg_ref,
                   yb_hbm, o_ref, gbuf, gwb, sem, *, final_norm):
    i = pl.program_id(0)
    n_tiles = pl.num_programs(0)
    tm, d = x1_ref.shape
    dh = SUBLANES * LANES
    slot = i % 2

    def row_copy(row, t, k, s):
        src = yb_hbm.at[pl.ds(pl.multiple_of(row * SUBLANES, SUBLANES), SUBLANES), :]
        dst = gbuf.at[s, pl.ds(pl.multiple_of((k * tm + t) * SUBLANES, SUBLANES), SUBLANES), :]
        return pltpu.make_async_copy(src, dst, sem.at[s])

    def gather(d_ref, s):
        def issue(t, c):
            for k in range(TOP_K):
                row_copy(d_ref[t * TOP_K + k], t, k, s).start(priority=k % 2)
            return c

        lax.fori_loop(0, tm, issue, 0)

    @pl.when(i == 0)
    def _():
        gather(dcur_ref, 0)

    @pl.when(i + 1 < n_tiles)
    def _():
        gather(dnxt_ref, 1 - slot)

    gw = gw_ref[...]
    for k in range(TOP_K):
        gwb[k] = jnp.broadcast_to(gw[:, k:k + 1], (tm, LANES))

    def drain(t, c):
        for k in range(TOP_K):
            row_copy(0, 0, 0, slot).wait()
        return c

    lax.fori_loop(0, tm, drain, 0, unroll=8)

    rows_g = gbuf.at[slot]
    for r0 in range(0, tm, FINISH_ROWS):
        rs = slice(r0, r0 + FINISH_ROWS)
        ssq = jnp.zeros((FINISH_ROWS, 1), F32)
        for j in range(SUBLANES):
            cols = (slice(j * LANES, (j + 1) * LANES), slice(dh + j * LANES, dh + (j + 1) * LANES))
            acc = [sh_ref[rs, c].astype(F32) for c in cols]
            for k in range(TOP_K):
                halves = _unpack_pair(rows_g[pl.ds((k * tm + r0) * SUBLANES + j, FINISH_ROWS, stride=SUBLANES), :])
                w = gwb[k, rs, :]
                acc = [a + w * h for a, h in zip(acc, halves)]
            for c, a in zip(cols, acc):
                x2 = x1_ref[rs, c] + mod_ref[0, 5:6, c] * a
                o_ref[rs, c] = x2
                ssq = ssq + jnp.sum(x2 * x2, axis=-1, keepdims=True)
        if final_norm:
            inv = lax.rsqrt(ssq / d + EPS)
            o_ref[rs, :] = o_ref[rs, :] * inv * fg_ref[...]


def _finish(dest_flat, shared, x1, gw, yb, mod, final_g, row0, group_of_tile, final_norm, tm=256):
    n_rows, d = x1.shape
    t0 = row0 // tm
    n_tiles = n_rows // tm
    return pl.pallas_call(
        functools.partial(_finish_kernel, final_norm=final_norm),
        grid=(n_tiles,),
        in_specs=[pl.BlockSpec((tm * TOP_K,), lambda i: (t0 + i,), memory_space=pltpu.SMEM),
                  pl.BlockSpec((tm * TOP_K,), lambda i: (t0 + jnp.minimum(i + 1, n_tiles - 1),),
                               memory_space=pltpu.SMEM),
                  pl.BlockSpec((tm, d), lambda i: (t0 + i, 0)),
                  pl.BlockSpec((tm, d), lambda i: (i, 0)),
                  pl.BlockSpec((tm, TOP_K), lambda i: (i, 0)),
                  pl.BlockSpec((1, N_MOD, d), lambda i: (group_of_tile(i, tm), 0, 0)),
                  pl.BlockSpec((1, d), lambda i: (0, 0)),
                  pl.BlockSpec(memory_space=pl.ANY)],
        out_specs=pl.BlockSpec((tm, d), lambda i: (i, 0)),
        out_shape=jax.ShapeDtypeStruct((n_rows, d), F32),
        scratch_shapes=[pltpu.VMEM((2, TOP_K * tm * SUBLANES, LANES), jnp.int32),
                        pltpu.VMEM((TOP_K, tm, LANES), F32), pltpu.SemaphoreType.DMA((2,))],
        compiler_params=_params(1),
        name="finish",
    )(dest_flat, dest_flat, shared, x1, gw, mod, final_g, yb)


def _grid_pos_emb(rows, d):
    quarter = d // 4
    omega = 1.0 / (10000.0 ** (jnp.arange(quarter, dtype=F32) / quarter))
    r = jnp.arange(rows, dtype=F32)[:, None] * omega
    cc = jnp.arange(GRID_W, dtype=F32)[:, None] * omega
    by_row = jnp.concatenate([jnp.sin(r), jnp.cos(r)], axis=-1)
    by_col = jnp.concatenate([jnp.sin(cc), jnp.cos(cc)], axis=-1)
    full = jnp.concatenate([jnp.broadcast_to(by_row[:, None, :], (rows, GRID_W, d // 2)),
                            jnp.broadcast_to(by_col[None, :, :], (rows, GRID_W, d // 2))], axis=-1)
    return full.reshape(rows * GRID_W, d)


def _mixer_path(x2d, pos, n_b, seq_len, h0, mod, group_of_tile, p, mats, filt, n_seq):
    d_rg = p['gn_rg'].shape[1]
    c = p['gn_hy'].shape[1]
    u_rg, hvc = _in_proj(x2d, pos, mod, p['norm1_g'], p['w_in'], p['hy_conv_w'], p['hy_conv_b'],
                         seq_len, group_of_tile)
    u3 = u_rg.reshape(n_b, seq_len, u_rg.shape[1])
    y_rg, st = _rglru(u3, h0, p['rg_conv_w'], p['rg_conv_b'], p['rg_w4'], p['rg_b4'], p['rg_lam'], n_seq)
    tm = min(seq_len // 2, 512)
    tc = 512 if seq_len > 512 else c
    z1 = _hyena_order(0, hvc, 0, hvc, 1, mats, filt, p['hy_bias'], tm, tc, False)
    y_hy = _hyena_order(1, z1, 0, hvc, 2, mats, filt, p['hy_bias'], tm, tc, True)
    x1, h2, idx, gw = _post_mixer(y_rg.reshape(-1, d_rg), y_hy.reshape(-1, c), x2d, pos, mod,
                                  p['gn_rg'], p['gn_hy'], p['w_out'], p['norm2_g'],
                                  p['router_w'], p['router_b'], group_of_tile)
    return x1, h2, idx, gw, st


def kernel(x_prompt, x_sample, state_rglru, c, c_ctx, ada_w, ada_b, norm1_g, norm2_g, w_in, rg_conv_w, rg_conv_b, rg_wa, rg_ba, rg_wx, rg_bx, rg_lam, hy_conv_w, hy_conv_b, hy_w1, hy_b1, hy_freq, hy_w2, hy_b2, hy_w3, hy_decay, hy_bias, gn_rg, gn_hy, w_out, router_w, router_b, exp_w_gate, exp_w_up, exp_w_down, sh_w_gate, sh_w_up, sh_w_down, final_g):
    n_cb, seq_c, d = x_prompt.shape
    n_lb, seq_l, _ = x_sample.shape
    depth = ada_w.shape[0]
    d_rg = gn_rg.shape[1]
    hd = d_rg // RG_HEADS
    t_c = n_cb * seq_c
    t_l = n_lb * seq_l
    assert n_lb + 1 <= SUBLANES
    assert d == 2 * SUBLANES * LANES

    pos = _grid_pos_emb(seq_l // GRID_W, d)
    cvec = jnp.zeros((SUBLANES, d), F32).at[0].set(c_ctx).at[1:1 + n_lb].set(c)
    mats_c = _dft_matrices(seq_c)
    mats_l = _dft_matrices(seq_l)

    def group_ctx(i, tm):
        return 0

    def group_lat(i, tm):
        return 1 + (i * tm) // seq_l

    xc = x_prompt.reshape(t_c, d)
    xs = x_sample.reshape(t_l, d)
    ctx_states = []
    for l in range(depth):
        last = l == depth - 1
        mod = _modulation(cvec, ada_w[l], ada_b[l][None]).reshape(SUBLANES, N_MOD, d)

        w4 = jnp.concatenate([rg_wa[l, 0], rg_wx[l, 0], rg_wa[l, 1], rg_wx[l, 1]], axis=-1).astype(BF16)
        b4 = jnp.concatenate([rg_ba[l, 0].reshape(RG_HEADS, 1, hd), rg_bx[l, 0].reshape(RG_HEADS, 1, hd),
                              rg_ba[l, 1].reshape(RG_HEADS, 1, hd), rg_bx[l, 1].reshape(RG_HEADS, 1, hd)], axis=-1)
        p = {
            'norm1_g': norm1_g[l][None], 'norm2_g': norm2_g[l][None], 'w_in': w_in[l].astype(BF16),
            'rg_conv_w': rg_conv_w[l], 'rg_conv_b': rg_conv_b[l][None], 'rg_w4': w4, 'rg_b4': b4,
            'rg_lam': rg_lam[l], 'hy_conv_w': hy_conv_w[l], 'hy_conv_b': hy_conv_b[l][None],
            'hy_bias': hy_bias[l], 'gn_rg': gn_rg[l][None], 'gn_hy': gn_hy[l][None],
            'w_out': w_out[l].astype(BF16), 'router_w': router_w[l].astype(BF16), 'router_b': router_b[l][None],
        }
        filt_args = (hy_w1[l], hy_b1[l][None], hy_freq[l][None], hy_w2[l], hy_b2[l][None], hy_w3[l],
                     hy_decay[l].reshape(1, -1))
        filt_c = _hyena_filters(seq_c, mats_c[:4], *filt_args)
        filt_l = _hyena_filters(seq_l, mats_l[:4], *filt_args)

        h0_c = jnp.zeros((n_cb, 2, d_rg), F32)
        x1_c, h2_c, idx_c, gw_c, st_c = _mixer_path(xc, None, n_cb, seq_c, h0_c, mod, group_ctx, p,
                                                    mats_c, filt_c, n_seq=min(8, n_cb))
        ctx_states.append(st_c)
        x1_l, h2_l, idx_l, gw_l, _ = _mixer_path(xs, pos if l == 0 else None, n_lb, seq_l,
                                                 state_rglru[:, l], mod, group_lat, p,
                                                 mats_l, filt_l, n_seq=1)

        idx_all = jnp.concatenate([idx_c, idx_l], axis=0)
        dest, blk_e, blk_valid, n_used = _route(idx_all, router_w.shape[-1], EXPERT_ROWS)
        dest_flat = dest.reshape(-1)
        sh = (sh_w_gate[l].astype(BF16), sh_w_up[l].astype(BF16), sh_w_down[l].astype(BF16))
        xb, shared = _dispatch(dest_flat, h2_c, h2_l, blk_e.shape[0] * EXPERT_ROWS, *sh)
        yb = _experts(blk_e, blk_valid, n_used, xb, exp_w_gate[l], exp_w_up[l], exp_w_down[l])
        xc = _finish(dest_flat, shared, x1_c, gw_c, yb, mod, final_g[None], 0, group_ctx, last)
        xs = _finish(dest_flat, shared, x1_l, gw_l, yb, mod, final_g[None], t_c, group_lat, last)

    new_state = jnp.stack(ctx_states, axis=1).astype(x_prompt.dtype)
    return (xc.reshape(n_cb, seq_c, d), xs.reshape(n_lb, seq_l, d), new_state)
```
